```python
import math
import jax, jax.numpy as jnp
from jax import lax
import numpy as np

D_MODEL = 1024
BATCH = 2
SEQ = 8192
DEPTH = 1

HEAD_DIM = 64
NSA_HEADS = 8
NSA_KV_GROUPS = 2
NSA_HPG = NSA_HEADS // NSA_KV_GROUPS
NSA_WIDTH = NSA_HEADS * HEAD_DIM
CMP_BLOCK = 32
CMP_STRIDE = 16
SLC_BLOCK = 64
SLC_TOPK = 16
WIN = 512
FORCE_SCORE = 1.0e4
DIL_HEADS = 8
DIL_WIDTH = DIL_HEADS * HEAD_DIM
DIL_PATTERNS = ((128, 1), (512, 4), (2048, 16))
Q_BLOCK = 128
RMS_EPS = 1e-6
N_ALIBI_HEADS = NSA_HEADS + DIL_HEADS
C_NSA_Q = NSA_WIDTH
C_NSA_KV = 6 * NSA_KV_GROUPS * HEAD_DIM
C_NSA_GATE = 3 * NSA_HEADS
C_NSA_Z = NSA_WIDTH
C_DIL_QKV = 3 * DIL_WIDTH
C_DIL_Z = DIL_WIDTH
C_MERGE = 2 * D_MODEL
C_IN = C_NSA_Q + C_NSA_KV + C_NSA_GATE + C_NSA_Z + C_DIL_QKV + C_DIL_Z + C_MERGE

kernel_name = 'nsa_dilated_hybrid_block'


def rmsnorm(x, g):
    xf = x.astype(jnp.float32)
    y = xf * lax.rsqrt(jnp.mean(xf * xf, axis=-1, keepdims=True) + RMS_EPS)
    return (y * g.astype(jnp.float32)).astype(x.dtype)


def masked_softmax(s, mask):
    s = jnp.where(mask, s, -jnp.inf)
    m = jnp.max(s, axis=-1, keepdims=True)
    m = jnp.where(jnp.isfinite(m), m, 0.0)
    e = jnp.exp(s - m)
    l = jnp.sum(e, axis=-1, keepdims=True)
    p = e / jnp.maximum(l, 1e-30)
    lse = m[..., 0] + jnp.log(l[..., 0])
    return p, lse


def alibi_slopes():
    s = 2.0 ** (-8.0 * np.arange(1, N_ALIBI_HEADS + 1) / N_ALIBI_HEADS)
    return jnp.asarray(s[0::2], jnp.float32), jnp.asarray(s[1::2], jnp.float32)


def overlap_matrix(n_cmp, n_sel):
    i = np.arange(n_cmp)[:, None]
    j = np.arange(n_sel)[None, :]
    lo = np.maximum(i * CMP_STRIDE, j * SLC_BLOCK)
    hi = np.minimum(i * CMP_STRIDE + CMP_BLOCK, (j + 1) * SLC_BLOCK)
    return (np.clip(hi - lo, 0, None) / CMP_BLOCK).astype(np.float32)


def nsa_compress(src, pos, w1, w2):
    S = src.shape[1]
    n_cmp = (S - CMP_BLOCK) // CMP_STRIDE + 1
    idx = np.arange(n_cmp)[:, None] * CMP_STRIDE + np.arange(CMP_BLOCK)[None, :]
    blocks = src[:, idx] + pos[None, None, :, None, :]
    hid = jax.nn.gelu(jnp.einsum('bnlgd,lde->bgne', blocks, w1))
    return jnp.einsum('bgne,ef->bgnf', hid, w2)


def nsa_attention(q, kv, gates, pos_k, w1_k, w2_k, pos_v, w1_v, w2_v, slopes):
    B, S, H, dh = q.shape
    G, hpg = NSA_KV_GROUPS, NSA_HPG
    qg = q.reshape(B, S, G, hpg, dh).transpose(0, 2, 3, 1, 4)
    k_cmp = nsa_compress(kv[:, :, 0], pos_k, w1_k, w2_k)
    v_cmp = nsa_compress(kv[:, :, 1], pos_v, w1_v, w2_v)
    n_cmp = k_cmp.shape[2]
    n_sel = S // SLC_BLOCK
    top_k = min(SLC_TOPK, n_sel)
    k_slc = kv[:, :, 2].transpose(0, 2, 1, 3).reshape(B, G, n_sel, SLC_BLOCK, dh)
    v_slc = kv[:, :, 3].transpose(0, 2, 1, 3).reshape(B, G, n_sel, SLC_BLOCK, dh)
    pad = ((0, 0), (0, 0), (WIN, 0), (0, 0))
    k_win = jnp.pad(kv[:, :, 4].transpose(0, 2, 1, 3), pad)
    v_win = jnp.pad(kv[:, :, 5].transpose(0, 2, 1, 3), pad)
    cmp_end = jnp.asarray(np.arange(n_cmp) * CMP_STRIDE + CMP_BLOCK - 1, jnp.int32)
    overlap = jnp.asarray(overlap_matrix(n_cmp, n_sel))
    slope = slopes.reshape(G, hpg)
    b_idx = jnp.arange(B)[:, None, None, None]
    g_idx = jnp.arange(G)[None, :, None, None]
    within = jnp.arange(SLC_BLOCK)
    win_off = jnp.arange(Q_BLOCK + WIN) - WIN
    blk_ids = jnp.arange(n_sel)

    def block(n):
        start = n * Q_BLOCK
        t = start + jnp.arange(Q_BLOCK)
        qb = lax.dynamic_slice_in_dim(qg, start, Q_BLOCK, axis=3)
        dist_c = t[:, None] - cmp_end[None, :]
        s_c = (jnp.einsum('bghqd,bgnd->bghqn', qb, k_cmp).astype(jnp.float32)
               - slope[None, :, :, None, None] * dist_c.astype(jnp.float32))
        p_c, _ = masked_softmax(s_c, dist_c >= 0)
        o_c = jnp.einsum('bghqn,bgnd->bghqd', p_c.astype(v_cmp.dtype), v_cmp)
        imp = jnp.einsum('bghqn,nj->bgqj', p_c, overlap)
        cur = t // SLC_BLOCK
        forced = (blk_ids[None] == 0) | (blk_ids[None] == cur[:, None]) | (blk_ids[None] == cur[:, None] - 1)
        valid = blk_ids[None] <= cur[:, None]
        score = jnp.where(valid, jnp.where(forced, FORCE_SCORE, imp), -1.0)
        top_val, top_idx = lax.top_k(score, top_k)
        kb = k_slc[b_idx, g_idx, top_idx]
        vb = v_slc[b_idx, g_idx, top_idx]
        pos_s = top_idx[..., None] * SLC_BLOCK + within
        dist_s = t[None, None, :, None, None] - pos_s
        mask_s = (top_val >= 0.0)[..., None] & (dist_s >= 0)
        s_s = (jnp.einsum('bghqd,bgqkld->bghqkl', qb, kb).astype(jnp.float32)
               - slope[None, :, :, None, None, None] * dist_s[:, :, None].astype(jnp.float32))
        shp = s_s.shape
        p_s, _ = masked_softmax(s_s.reshape(shp[:4] + (-1,)),
                                mask_s[:, :, None].reshape(B, G, 1, Q_BLOCK, -1))
        o_s = jnp.einsum('bghqkl,bgqkld->bghqd', p_s.reshape(shp).astype(vb.dtype), vb)
        kw = lax.dynamic_slice_in_dim(k_win, start, Q_BLOCK + WIN, axis=2)
        vw = lax.dynamic_slice_in_dim(v_win, start, Q_BLOCK + WIN, axis=2)
        pos_w = start + win_off
        dist_w = t[:, None] - pos_w[None, :]
        mask_w = (dist_w >= 0) & (dist_w < WIN) & (pos_w[None, :] >= 0)
        s_w = (jnp.einsum('bghqd,bgkd->bghqk', qb, kw).astype(jnp.float32)
               - slope[None, :, :, None, None] * dist_w.astype(jnp.float32))
        p_w, _ = masked_softmax(s_w, mask_w)
        o_w = jnp.einsum('bghqk,bgkd->bghqd', p_w.astype(vw.dtype), vw)
        return o_c, o_s, o_w

    o_c, o_s, o_w = lax.map(block, jnp.arange(S // Q_BLOCK))

    def to_seq(o):
        return o.transpose(1, 0, 4, 2, 3, 5).reshape(B, S, H, dh)

    return (gates[:, :, 0, :, None] * to_seq(o_c) + gates[:, :, 1, :, None] * to_seq(o_s)
            + gates[:, :, 2, :, None] * to_seq(o_w))


def dilated_branch(q, k, v, slopes, window, dilation):
    B, S, H, dh = q.shape
    L = S // dilation
    wd = window // dilation
    c = Q_BLOCK
    nbd = -(-L // c)
    Lp = nbd * c

    def phase(a):
        a = a.reshape(B, L, dilation, H, dh).transpose(0, 3, 2, 1, 4)
        return jnp.pad(a, ((0, 0), (0, 0), (0, 0), (c, Lp - L), (0, 0)))

    def band(a):
        prev = a[..., :Lp, :].reshape(B, H, dilation, nbd, c, dh)
        cur = a[..., c:, :].reshape(B, H, dilation, nbd, c, dh)
        return jnp.concatenate([prev, cur], axis=-2)

    qp = phase(q)[..., c:, :].reshape(B, H, dilation, nbd, c, dh)
    kb = band(phase(k))
    vb = band(phase(v))
    i = np.arange(c)[:, None]
    j = np.arange(2 * c)[None, :]
    kk = c + i - j
    key_l = (np.arange(nbd)[:, None, None] - 1) * c + j[None]
    mask = jnp.asarray((kk >= 0)[None] & (kk <= wd)[None] & (key_l >= 0))
    dist = jnp.asarray((kk * dilation).astype(np.float32))
    s = (jnp.einsum('bhpnqd,bhpnkd->bhpnqk', qp, kb).astype(jnp.float32)
         - slopes[None, :, None, None, None, None] * dist)
    p, lse = masked_softmax(s, mask[None, None, None])
    o = jnp.einsum('bhpnqk,bhpnkd->bhpnqd', p.astype(vb.dtype), vb)
    o = o.reshape(B, H, dilation, Lp, dh)[..., :L, :].transpose(0, 3, 2, 1, 4).reshape(B, S, H, dh)
    lse = lse.reshape(B, H, dilation, Lp)[..., :L].transpose(0, 3, 2, 1).reshape(B, S, H)
    return o, lse


def dilated_attention(q, k, v, slopes):
    outs, lses = [], []
    for window, dilation in DIL_PATTERNS:
        o, lse = dilated_branch(q, k, v, slopes, window, dilation)
        outs.append(o)
        lses.append(lse)
    w = jax.nn.softmax(jnp.stack(lses, axis=0), axis=0)
    return jnp.sum(w[..., None].astype(q.dtype) * jnp.stack(outs, axis=0), axis=0)


def hybrid_layer(xn, w_in, cmp_pos_k, cmp_w1_k, cmp_w2_k, cmp_pos_v, cmp_w1_v, cmp_w2_v,
                 w_br_nsa, w_br_dil, w_out):
    B, S, _ = xn.shape
    proj = jnp.einsum('bsd,dc->bsc', xn, w_in)
    offs = np.cumsum([C_NSA_Q, C_NSA_KV, C_NSA_GATE, C_NSA_Z, C_DIL_QKV, C_DIL_Z]).tolist()
    q_n, kv_n, g_n, z_n, qkv_d, z_d, mg = jnp.split(proj, offs, axis=-1)
    scale = HEAD_DIM ** -0.5
    slopes_nsa, slopes_dil = alibi_slopes()
    o_nsa = nsa_attention(q_n.reshape(B, S, NSA_HEADS, HEAD_DIM) * scale,
                          kv_n.reshape(B, S, 6, NSA_KV_GROUPS, HEAD_DIM),
                          jax.nn.sigmoid(g_n.reshape(B, S, 3, NSA_HEADS)),
                          cmp_pos_k, cmp_w1_k, cmp_w2_k, cmp_pos_v, cmp_w1_v, cmp_w2_v, slopes_nsa)
    o_nsa = o_nsa.reshape(B, S, NSA_WIDTH) * jax.nn.silu(z_n)
    qkv_d = qkv_d.reshape(B, S, 3, DIL_HEADS, HEAD_DIM)
    o_dil = dilated_attention(qkv_d[:, :, 0] * scale, qkv_d[:, :, 1], qkv_d[:, :, 2], slopes_dil)
    o_dil = o_dil.reshape(B, S, DIL_WIDTH) * jax.nn.silu(z_d)
    mg = jax.nn.sigmoid(mg.reshape(B, S, 2, D_MODEL))
    merged = (mg[:, :, 0] * jnp.einsum('bsc,cd->bsd', o_nsa, w_br_nsa)
              + mg[:, :, 1] * jnp.einsum('bsc,cd->bsd', o_dil, w_br_dil))
    return jnp.einsum('bsd,de->bse', merged, w_out)


def setup_inputs(seed: int = 0) -> dict:
    key = jax.random.key(seed)
    ks = jax.random.split(key, 13)
    hd = HEAD_DIM

    def nrm(k, shape, scale):
        return jax.random.normal(k, shape, jnp.float32) * scale

    return {
        'x': nrm(ks[0], (BATCH, SEQ, D_MODEL), 1.0),
        'norm_g': 1.0 + nrm(ks[1], (DEPTH, D_MODEL), 0.05),
        'w_in': nrm(ks[2], (DEPTH, D_MODEL, C_IN), D_MODEL ** -0.5),
        'cmp_pos_k': nrm(ks[3], (DEPTH, CMP_BLOCK, hd), 0.1),
        'cmp_w1_k': nrm(ks[4], (DEPTH, CMP_BLOCK, hd, hd), (CMP_BLOCK * hd) ** -0.5),
        'cmp_w2_k': nrm(ks[5], (DEPTH, hd, hd), hd ** -0.5),
        'cmp_pos_v': nrm(ks[6], (DEPTH, CMP_BLOCK, hd), 0.1),
        'cmp_w1_v': nrm(ks[7], (DEPTH, CMP_BLOCK, hd, hd), (CMP_BLOCK * hd) ** -0.5),
        'cmp_w2_v': nrm(ks[8], (DEPTH, hd, hd), hd ** -0.5),
        'w_br_nsa': nrm(ks[9], (DEPTH, NSA_WIDTH, D_MODEL), NSA_WIDTH ** -0.5),
        'w_br_dil': nrm(ks[10], (DEPTH, DIL_WIDTH, D_MODEL), DIL_WIDTH ** -0.5),
        'w_out': nrm(ks[11], (DEPTH, D_MODEL, D_MODEL), D_MODEL ** -0.5),
        'final_g': 1.0 + nrm(ks[12], (D_MODEL,), 0.05),
    }


def reference(x, norm_g, w_in, cmp_pos_k, cmp_w1_k, cmp_w2_k, cmp_pos_v, cmp_w1_v, cmp_w2_v,
              w_br_nsa, w_br_dil, w_out, final_g):
    h = x
    for layer in range(DEPTH):
        xn = rmsnorm(h, norm_g[layer])
        h = h + hybrid_layer(xn, w_in[layer], cmp_pos_k[layer], cmp_w1_k[layer], cmp_w2_k[layer],
                             cmp_pos_v[layer], cmp_w1_v[layer], cmp_w2_v[layer],
                             w_br_nsa[layer], w_br_dil[layer], w_out[layer])
    return rmsnorm(h, final_g)
```

```python
import functools

import numpy as np
import jax
import jax.numpy as jnp
from jax import lax
from jax.experimental import pallas as pl
from jax.experimental.pallas import tpu as pltpu

F32 = jnp.float32
BF16 = jnp.bfloat16

HEAD_DIM = 64
LANES = 128
NSA_HEADS = 8
NSA_GROUPS = 2
NSA_HPG = NSA_HEADS // NSA_GROUPS
CMP_BLOCK = 32
CMP_STRIDE = 16
SLC_BLOCK = 64
SLC_TOPK = 16
WIN = 512
FORCE_SCORE = 1.0e4
DIL_HEADS = 8
DIL_PATTERNS = ((128, 1), (512, 4), (2048, 16))
Q_BLOCK = 128
RMS_EPS = 1e-6
SLC_CHUNK = 256
NEG_INIT = -1.0e30
VMEM_LIMIT = 56 * 1024 * 1024


def _alibi_slopes():
    n = NSA_HEADS + DIL_HEADS
    s = 2.0 ** (-8.0 * np.arange(1, n + 1) / n)
    return s[0::2].astype(np.float32), s[1::2].astype(np.float32)


def _nt(a, b):
    return lax.dot_general(a, b, (((1,), (1,)), ((), ())), preferred_element_type=F32)


def _rms(x, g):
    return (x * lax.rsqrt(jnp.mean(x * x, axis=-1, keepdims=True) + RMS_EPS)) * g


def _pad_heads(q_pairs, n_heads):
    rows = q_pairs.shape[0]
    lane = lax.broadcasted_iota(jnp.int32, (rows, LANES), 1)
    out = []
    for h in range(n_heads):
        tile = q_pairs[:, (h // 2) * LANES:(h // 2 + 1) * LANES]
        if h % 2:
            tile = pltpu.roll(tile, HEAD_DIM, 1)
        out.append(jnp.where(lane < HEAD_DIM, tile, 0.0))
    return out


def _compact_pair(a_pad, b_pad):
    lane = lax.broadcasted_iota(jnp.int32, a_pad.shape, 1)
    return jnp.where(lane < HEAD_DIM, pltpu.roll(a_pad, HEAD_DIM, 1), b_pad)


C_QN, C_CMP, C_KVS, C_KVW, C_GATE, C_QD, C_KVD = 512, 256, 256, 256, 256, 512, 1024
_PROJ_OFFS = np.cumsum([0, C_QN, C_CMP, C_KVS, C_KVW, C_GATE, C_QD, C_KVD]).tolist()


def _proj_kernel(x_ref, g_ref, w_ref, qn_ref, cmp_ref, kvs_ref, kvw_ref, gate_ref, qd_ref, kvd_ref):
    xb = _rms(x_ref[...], g_ref[...]).astype(BF16)
    scale = HEAD_DIM ** -0.5

    def seg(i):
        return jnp.dot(xb, w_ref[:, _PROJ_OFFS[i]:_PROJ_OFFS[i + 1]], preferred_element_type=F32)

    qn_ref[...] = (seg(0) * scale).astype(BF16)
    cmp_ref[...] = seg(1)
    kvs = seg(2).astype(BF16)
    kvw = seg(3).astype(BF16)
    gate = jax.nn.sigmoid(seg(4))
    for g in range(NSA_GROUPS):
        kvs_ref[g] = kvs[:, g * LANES:(g + 1) * LANES]
        kvw_ref[g] = kvw[:, g * LANES:(g + 1) * LANES]
        gate_ref[g] = gate[:, g * LANES:(g + 1) * LANES]
    qd_ref[...] = (seg(5) * scale).astype(BF16)
    kvd_ref[...] = seg(6).astype(BF16)


def _proj_call(x, g, w, tm):
    B, S, D = x.shape
    row = lambda c: pl.BlockSpec((None, tm, c), lambda b, i: (b, i, 0))
    grp = pl.BlockSpec((None, NSA_GROUPS, tm, LANES), lambda b, i: (b, 0, i, 0))
    out_shape = (
        jax.ShapeDtypeStruct((B, S, C_QN), BF16),
        jax.ShapeDtypeStruct((B, S, C_CMP), F32),
        jax.ShapeDtypeStruct((B, NSA_GROUPS, S, LANES), BF16),
        jax.ShapeDtypeStruct((B, NSA_GROUPS, S, LANES), BF16),
        jax.ShapeDtypeStruct((B, NSA_GROUPS, S, LANES), F32),
        jax.ShapeDtypeStruct((B, S, C_QD), BF16),
        jax.ShapeDtypeStruct((B, S, C_KVD), BF16),
    )
    return pl.pallas_call(
        _proj_kernel,
        grid=(B, S // tm),
        in_specs=[row(D), pl.BlockSpec((1, D), lambda b, i: (0, 0)),
                  pl.BlockSpec(w.shape, lambda b, i: (0, 0))],
        out_specs=(row(C_QN), row(C_CMP), grp, grp, grp, row(C_QD), row(C_KVD)),
        out_shape=out_shape,
        compiler_params=pltpu.CompilerParams(
            dimension_semantics=("parallel", "parallel"), vmem_limit_bytes=VMEM_LIMIT),
        name="proj",
    )(x, g, w)


def _compress_kernel(r_ref, posa_ref, posb_ref, wa_ref, wb_ref, w2_ref, o_ref):
    r = r_ref[...]
    ha = jnp.dot((r + posa_ref[...]).astype(BF16), wa_ref[...], preferred_element_type=F32)
    hb = jnp.dot((r + posb_ref[...]).astype(BF16), wb_ref[...], preferred_element_type=F32)
    hid = jax.nn.gelu(ha + pltpu.roll(hb, hb.shape[0] - 1, 0))
    out = jnp.dot(hid.astype(BF16), w2_ref[...], preferred_element_type=F32).astype(BF16)
    for g in range(NSA_GROUPS):
        o_ref[g] = out[:, g * LANES:(g + 1) * LANES]


def _compress_call(r, posa, posb, wa, wb, w2):
    B, R, C = r.shape
    full = lambda a: pl.BlockSpec(a.shape, lambda b: (0,) * a.ndim)
    return pl.pallas_call(
        _compress_kernel,
        grid=(B,),
        in_specs=[pl.BlockSpec((None, R, C), lambda b: (b, 0, 0)),
                  full(posa), full(posb), full(wa), full(wb), full(w2)],
        out_specs=pl.BlockSpec((None, NSA_GROUPS, R, LANES), lambda b: (b, 0, 0, 0)),
        out_shape=jax.ShapeDtypeStruct((B, NSA_GROUPS, R, LANES), BF16),
        compiler_params=pltpu.CompilerParams(
            dimension_semantics=("parallel",), vmem_limit_bytes=VMEM_LIMIT),
        name="compress",
    )(r, posa, posb, wa, wb, w2)


def _online_update(h, s, kv, m_ref, l_ref, acc_ref):
    m_old = m_ref[h]
    m_new = jnp.maximum(m_old, jnp.max(s, axis=1, keepdims=True))
    alpha = jnp.exp(m_old - m_new)
    p = jnp.exp(s - m_new)
    l_ref[h] = alpha * l_ref[h] + jnp.sum(p, axis=1, keepdims=True)
    acc_ref[h] = alpha * acc_ref[h] + jnp.dot(p.astype(BF16), kv, preferred_element_type=F32)
    m_ref[h] = m_new


def _reset_state(m_ref, l_ref, acc_ref):
    m_ref[...] = jnp.full(m_ref.shape, NEG_INIT, F32)
    l_ref[...] = jnp.zeros(l_ref.shape, F32)
    acc_ref[...] = jnp.zeros(acc_ref.shape, F32)


def _nsa_kernel(slope_ref, q_ref, kvc_ref, kvs_ref, kvw_ref, gate_ref, ovt_ref, exp_ref,
                o_ref, m_ref, l_ref, acc_ref):
    g = pl.program_id(1)
    n = pl.program_id(2)
    t0 = n * Q_BLOCK
    qb = Q_BLOCK
    slopes = [slope_ref[g, h] for h in range(NSA_HPG)]
    q = jnp.concatenate(_pad_heads(q_ref[...].astype(F32), NSA_HPG), axis=0).astype(BF16)

    kvc = kvc_ref[...]
    n_cmp = kvc.shape[0]
    s_all = _nt(q, kvc)
    tq = t0 + lax.broadcasted_iota(jnp.int32, (qb, n_cmp), 0)
    cmp_end = lax.broadcasted_iota(jnp.int32, (qb, n_cmp), 1) * CMP_STRIDE + (CMP_BLOCK - 1)
    dist_c = tq - cmp_end
    ok_c = dist_c >= 0
    dist_cf = dist_c.astype(F32)
    ovt = ovt_ref[...]
    n_sel = ovt.shape[0]
    imp_t = jnp.zeros((n_sel, qb), F32)
    o_cmp = []
    for h in range(NSA_HPG):
        s = s_all[h * qb:(h + 1) * qb] - slopes[h] * dist_cf
        s = jnp.where(ok_c, s, -jnp.inf)
        m = jnp.max(s, axis=1, keepdims=True)
        m = jnp.where(m == -jnp.inf, 0.0, m)
        e = jnp.exp(s - m)
        l = jnp.sum(e, axis=1, keepdims=True)
        p = (e / jnp.maximum(l, 1e-30)).astype(BF16)
        o_cmp.append(jnp.dot(p, kvc, preferred_element_type=F32))
        imp_t = imp_t + _nt(ovt, p)

    blk = lax.broadcasted_iota(jnp.int32, (n_sel, qb), 0).astype(F32)
    cur = ((t0 + lax.broadcasted_iota(jnp.int32, (n_sel, qb), 1)) // SLC_BLOCK).astype(F32)
    forced = (blk == 0) | (blk == cur) | (blk == cur - 1)
    valid = blk <= cur
    score = jnp.where(valid, jnp.where(forced, FORCE_SCORE, imp_t), -1.0)
    picked = jnp.zeros((n_sel, qb), F32)
    for _ in range(min(SLC_TOPK, n_sel)):
        mx = jnp.max(score, axis=0, keepdims=True)
        idx = jnp.min(jnp.where(score == mx, blk, float(n_sel)), axis=0, keepdims=True)
        hit = blk == idx
        picked = jnp.where(hit, 1.0, picked)
        score = jnp.where(hit, -jnp.inf, score)
    sel = jnp.where(valid, picked, 0.0).T.astype(BF16)

    _reset_state(m_ref, l_ref, acc_ref)

    def slc_body(c, carry):
        k0 = pl.multiple_of(c * SLC_CHUNK, SLC_CHUNK)
        kv = kvs_ref[pl.ds(k0, SLC_CHUNK), :]
        s_all = _nt(q, kv)
        chosen = jnp.dot(sel, exp_ref[c], preferred_element_type=F32)
        dist = (t0 - k0) + (lax.broadcasted_iota(jnp.int32, (qb, SLC_CHUNK), 0)
                            - lax.broadcasted_iota(jnp.int32, (qb, SLC_CHUNK), 1))
        ok = (chosen > 0.5) & (dist >= 0)
        distf = dist.astype(F32)
        for h in range(NSA_HPG):
            s = s_all[h * qb:(h + 1) * qb] - slopes[h] * distf
            _online_update(h, jnp.where(ok, s, -jnp.inf), kv, m_ref, l_ref, acc_ref)
        return carry

    lax.fori_loop(0, (t0 + qb + SLC_CHUNK - 1) // SLC_CHUNK, slc_body, 0)
    o_slc = [acc_ref[h] / jnp.maximum(l_ref[h], 1e-30) for h in range(NSA_HPG)]

    _reset_state(m_ref, l_ref, acc_ref)

    def win_body(c, carry):
        k0 = pl.multiple_of(c * qb, qb)
        kv = kvw_ref[pl.ds(k0, qb), :]
        s_all = _nt(q, kv)
        dist = (t0 - k0) + (lax.broadcasted_iota(jnp.int32, (qb, qb), 0)
                            - lax.broadcasted_iota(jnp.int32, (qb, qb), 1))
        ok = (dist >= 0) & (dist < WIN)
        distf = dist.astype(F32)
        for h in range(NSA_HPG):
            s = s_all[h * qb:(h + 1) * qb] - slopes[h] * distf
            _online_update(h, jnp.where(ok, s, -jnp.inf), kv, m_ref, l_ref, acc_ref)
        return carry

    lax.fori_loop(jnp.maximum(n - WIN // qb, 0), n + 1, win_body, 0)

    gate = gate_ref[...]
    outs = []
    for h in range(NSA_HPG):
        o_win = acc_ref[h] / jnp.maximum(l_ref[h], 1e-30)
        gc = [gate[:, j * NSA_HPG + h:j * NSA_HPG + h + 1] for j in range(3)]
        outs.append(gc[0] * o_cmp[h] + gc[1] * o_slc[h] + gc[2] * o_win)
    for p in range(NSA_HPG // 2):
        o_ref[:, p * LANES:(p + 1) * LANES] = _compact_pair(outs[2 * p], outs[2 * p + 1])


def _nsa_call(slopes, qn, kvc, kvs, kvw, gates, ovt, expand):
    B, S, _ = qn.shape
    G = NSA_GROUPS
    nb = S // Q_BLOCK
    n_cmp = kvc.shape[2]
    whole = lambda r: pl.BlockSpec((None, None, r, LANES), lambda b, g, n: (b, g, 0, 0))
    return pl.pallas_call(
        _nsa_kernel,
        grid=(B, G, nb),
        in_specs=[
            pl.BlockSpec(memory_space=pltpu.SMEM),
            pl.BlockSpec((None, Q_BLOCK, NSA_HPG * HEAD_DIM), lambda b, g, n: (b, n, g)),
            whole(n_cmp), whole(S), whole(S),
            pl.BlockSpec((None, None, Q_BLOCK, LANES), lambda b, g, n: (b, g, n, 0)),
            pl.BlockSpec(ovt.shape, lambda b, g, n: (0, 0)),
            pl.BlockSpec(expand.shape, lambda b, g, n: (0, 0, 0)),
        ],
        out_specs=pl.BlockSpec((None, Q_BLOCK, NSA_HPG * HEAD_DIM), lambda b, g, n: (b, n, g)),
        out_shape=jax.ShapeDtypeStruct((B, S, NSA_HEADS * HEAD_DIM), F32),
        scratch_shapes=[pltpu.VMEM((NSA_HPG, Q_BLOCK, 1), F32),
                        pltpu.VMEM((NSA_HPG, Q_BLOCK, 1), F32),
                        pltpu.VMEM((NSA_HPG, Q_BLOCK, LANES), F32)],
        compiler_params=pltpu.CompilerParams(
            dimension_semantics=("parallel", "parallel", "arbitrary"), vmem_limit_bytes=VMEM_LIMIT),
        name="nsa",
    )(slopes, qn, kvc, kvs, kvw, gates, ovt, expand)


def _dilated_kernel(q_ref, kvp_ref, kvc_ref, o_ref, lse_ref, *, slopes, dilation, wd):
    c = Q_BLOCK
    first = pl.program_id(2) == 0
    qs = _pad_heads(q_ref[...].astype(F32), DIL_HEADS)
    back = (lax.broadcasted_iota(jnp.int32, (c, c), 0)
            - lax.broadcasted_iota(jnp.int32, (c, c), 1))
    kk_prev = back + c
    ok_prev = kk_prev <= jnp.where(first, 0, wd)
    ok_cur = back >= 0
    dist_prev = (kk_prev * dilation).astype(F32)
    dist_cur = (back * dilation).astype(F32)
    outs, lses = [], []
    for h in range(DIL_HEADS):
        qh = qs[h].astype(BF16)
        kvp = kvp_ref[:, h * LANES:(h + 1) * LANES]
        kvc = kvc_ref[:, h * LANES:(h + 1) * LANES]
        sp = jnp.where(ok_prev, _nt(qh, kvp) - slopes[h] * dist_prev, -jnp.inf)
        sc = jnp.where(ok_cur, _nt(qh, kvc) - slopes[h] * dist_cur, -jnp.inf)
        m = jnp.maximum(jnp.max(sp, axis=1, keepdims=True), jnp.max(sc, axis=1, keepdims=True))
        ep = jnp.exp(sp - m)
        ec = jnp.exp(sc - m)
        l = jnp.sum(ep, axis=1, keepdims=True) + jnp.sum(ec, axis=1, keepdims=True)
        acc = (jnp.dot(ep.astype(BF16), kvp, preferred_element_type=F32)
               + jnp.dot(ec.astype(BF16), kvc, preferred_element_type=F32))
        outs.append(acc / l)
        lses.append(jnp.broadcast_to(m + jnp.log(l), (c, LANES)))
    lane = lax.broadcasted_iota(jnp.int32, (c, LANES), 1)
    for p in range(DIL_HEADS // 2):
        o_ref[:, p * LANES:(p + 1) * LANES] = _compact_pair(outs[2 * p], outs[2 * p + 1])
        lse_ref[:, p * LANES:(p + 1) * LANES] = jnp.where(lane < HEAD_DIM, lses[2 * p], lses[2 * p + 1])


def _dilated_call(qd, kvd, slopes, window, dilation):
    B, S, _ = qd.shape
    L = S // dilation
    nblk = L // Q_BLOCK
    cq, ckv = DIL_HEADS * HEAD_DIM, DIL_HEADS * LANES
    q_v = qd.reshape(B, L, dilation * cq)
    kv_v = kvd.reshape(B, L, dilation * ckv)
    kern = functools.partial(_dilated_kernel, slopes=tuple(float(s) for s in slopes),
                             dilation=dilation, wd=window // dilation)
    blk = lambda cw: pl.BlockSpec((None, Q_BLOCK, cw), lambda b, p, i: (b, i, p))
    o, lse = pl.pallas_call(
        kern,
        grid=(B, dilation, nblk),
        in_specs=[blk(cq),
                  pl.BlockSpec((None, Q_BLOCK, ckv), lambda b, p, i: (b, jnp.maximum(i - 1, 0), p)),
                  blk(ckv)],
        out_specs=(blk(cq), blk(cq)),
        out_shape=(jax.ShapeDtypeStruct((B, L, dilation * cq), F32),
                   jax.ShapeDtypeStruct((B, L, dilation * cq), F32)),
        compiler_params=pltpu.CompilerParams(
            dimension_semantics=("parallel", "parallel", "arbitrary"), vmem_limit_bytes=VMEM_LIMIT),
        name=f"dilated_d{dilation}",
    )(q_v, kv_v, kv_v)
    return o.reshape(B, S, cq), lse.reshape(B, S, cq)


def _merge_kernel(x_ref, g_ref, wg_ref, onsa_ref, o1_ref, o2_ref, o3_ref, l1_ref, l2_ref, l3_ref,
                  wbn_ref, wbd_ref, wo_ref, fg_ref, out_ref, *, final):
    x = x_ref[...]
    xb = _rms(x, g_ref[...]).astype(BF16)
    cz = NSA_HEADS * HEAD_DIM
    d = x.shape[-1]

    def gate_seg(c0, c1):
        return jnp.dot(xb, wg_ref[:, c0:c1], preferred_element_type=F32)

    o_nsa = onsa_ref[...] * jax.nn.silu(gate_seg(0, cz))
    l1, l2, l3 = l1_ref[...], l2_ref[...], l3_ref[...]
    mx = jnp.maximum(jnp.maximum(l1, l2), l3)
    e1, e2, e3 = jnp.exp(l1 - mx), jnp.exp(l2 - mx), jnp.exp(l3 - mx)
    den = e1 + e2 + e3
    o_dil = (e1 / den) * o1_ref[...] + (e2 / den) * o2_ref[...] + (e3 / den) * o3_ref[...]
    o_dil = o_dil * jax.nn.silu(gate_seg(cz, 2 * cz))
    a = jnp.dot(o_nsa.astype(BF16), wbn_ref[...], preferred_element_type=F32)
    bd = jnp.dot(o_dil.astype(BF16), wbd_ref[...], preferred_element_type=F32)
    merged = (jax.nn.sigmoid(gate_seg(2 * cz, 2 * cz + d)) * a
              + jax.nn.sigmoid(gate_seg(2 * cz + d, 2 * cz + 2 * d)) * bd)
    y = jnp.dot(merged.astype(BF16), wo_ref[...], preferred_element_type=F32)
    out_ref[...] = _rms(x + y, fg_ref[...]) if final else x + y


def _merge_call(x, g, wg, o_nsa, o_d, lse_d, wbn, wbd, wo, fg, tm, final):
    B, S, D = x.shape
    cz = o_nsa.shape[-1]
    row = lambda c: pl.BlockSpec((None, tm, c), lambda b, i: (b, i, 0))
    full = lambda a: pl.BlockSpec(a.shape, lambda b, i: (0,) * a.ndim)
    return pl.pallas_call(
        functools.partial(_merge_kernel, final=final),
        grid=(B, S // tm),
        in_specs=[row(D), full(g), full(wg), row(cz), row(cz), row(cz), row(cz), row(cz), row(cz),
                  row(cz), full(wbn), full(wbd), full(wo), full(fg)],
        out_specs=row(D),
        out_shape=jax.ShapeDtypeStruct((B, S, D), F32),
        compiler_params=pltpu.CompilerParams(
            dimension_semantics=("parallel", "parallel"), vmem_limit_bytes=VMEM_LIMIT),
        name="merge",
    )(x, g, wg, o_nsa, *o_d, *lse_d, wbn, wbd, wo, fg)


def _proj_columns(d_model):
    hd, G, H = HEAD_DIM, NSA_GROUPS, NSA_HEADS
    nsa_w = H * hd
    o_q, o_kv = 0, nsa_w
    o_g = o_kv + 6 * G * hd
    o_zn = o_g + 3 * H
    o_qkvd = o_zn + nsa_w
    dil_w = DIL_HEADS * hd
    o_zd = o_qkvd + 3 * dil_w
    o_mg = o_zd + dil_w
    kv = lambda j, g: o_kv + (j * G + g) * hd + np.arange(hd)
    cols = [o_q + np.arange(nsa_w)]
    cols += [kv(j, g) for j in (0, 1) for g in range(G)]
    cols += [kv(j, g) for g in range(G) for j in (2, 3)]
    cols += [kv(j, g) for g in range(G) for j in (4, 5)]
    for g in range(G):
        lanes = np.full(LANES, -1)
        for j in range(3):
            lanes[j * NSA_HPG:(j + 1) * NSA_HPG] = o_g + j * H + g * NSA_HPG + np.arange(NSA_HPG)
        cols.append(lanes)
    cols.append(o_qkvd + np.arange(dil_w))
    for h in range(DIL_HEADS):
        cols += [o_qkvd + j * dil_w + h * hd + np.arange(hd) for j in (1, 2)]
    proj_cols = np.concatenate(cols)
    gate_cols = np.concatenate([o_zn + np.arange(nsa_w), o_zd + np.arange(dil_w),
                                o_mg + np.arange(2 * d_model)])
    return proj_cols, gate_cols


def _compress_weights(pos_k, w1_k, w2_k, pos_v, w1_v, w2_v):
    hd, G, half = HEAD_DIM, NSA_GROUPS, CMP_BLOCK // 2
    eye = jnp.eye(2 * G, dtype=F32)
    w1 = jnp.stack([w1_k] * G + [w1_v] * G)
    pos = jnp.stack([pos_k] * G + [pos_v] * G)

    def first_layer(lo):
        w = jnp.einsum('slde,st->lsdte', w1[:, lo:lo + half], eye)
        p = pos[:, lo:lo + half].transpose(1, 0, 2)
        return w.reshape(half * 2 * G * hd, 2 * G * hd).astype(BF16), p.reshape(1, half * 2 * G * hd)

    wa, posa = first_layer(0)
    wb, posb = first_layer(half)
    w2 = jnp.zeros((2 * G, hd, 2 * G, hd), F32)
    for g in range(G):
        w2 = w2.at[g, :, 2 * g].set(w2_k).at[G + g, :, 2 * g + 1].set(w2_v)
    return posa, posb, wa, wb, w2.reshape(2 * G * hd, 2 * G * hd).astype(BF16)


def _overlap_t(n_cmp_pad, n_sel):
    i = np.arange(n_cmp_pad)[None, :]
    j = np.arange(n_sel)[:, None]
    lo = np.maximum(i * CMP_STRIDE, j * SLC_BLOCK)
    hi = np.minimum(i * CMP_STRIDE + CMP_BLOCK, (j + 1) * SLC_BLOCK)
    return (np.clip(hi - lo, 0, None) / CMP_BLOCK).astype(np.float32)


def _block_expand(seq, n_sel):
    key_blk = (np.arange(seq) // SLC_BLOCK).reshape(seq // SLC_CHUNK, 1, SLC_CHUNK)
    return (key_blk == np.arange(n_sel)[None, :, None]).astype(np.float32)


def kernel(x, norm_g, w_in, cmp_pos_k, cmp_w1_k, cmp_w2_k, cmp_pos_v, cmp_w1_v, cmp_w2_v,
           w_br_nsa, w_br_dil, w_out, final_g):
    B, S, D = x.shape
    assert S % (Q_BLOCK * max(d for _, d in DIL_PATTERNS)) == 0 and S % SLC_CHUNK == 0
    n_sel = S // SLC_BLOCK
    n_cmp_pad = S // CMP_STRIDE
    slopes_nsa, slopes_dil = _alibi_slopes()
    proj_cols, gate_cols = _proj_columns(D)
    ovt = jnp.asarray(_overlap_t(n_cmp_pad, n_sel), BF16)
    expand = jnp.asarray(_block_expand(S, n_sel), BF16)
    slopes = jnp.asarray(slopes_nsa.reshape(NSA_GROUPS, NSA_HPG))
    tm = 512

    h = x
    for layer in range(w_in.shape[0]):
        w_l = jnp.concatenate([w_in[layer], jnp.zeros((D, 1), w_in.dtype)], axis=1)
        w_proj = jnp.take(w_l, jnp.asarray(np.where(proj_cols < 0, w_l.shape[1] - 1, proj_cols)),
                          axis=1).astype(BF16)
        w_gate = jnp.take(w_l, jnp.asarray(gate_cols), axis=1).astype(BF16)
        g_l = norm_g[layer].reshape(1, D)

        qn, cmp_src, kvs, kvw, gates, qd, kvd = _proj_call(h, g_l, w_proj, tm)
        cw = _compress_weights(cmp_pos_k[layer], cmp_w1_k[layer], cmp_w2_k[layer],
                               cmp_pos_v[layer], cmp_w1_v[layer], cmp_w2_v[layer])
        kvc = _compress_call(cmp_src.reshape(B, n_cmp_pad, CMP_STRIDE * C_CMP), *cw)
        o_nsa = _nsa_call(slopes, qn, kvc, kvs, kvw, gates, ovt, expand)
        o_d, lse_d = [], []
        for window, dilation in DIL_PATTERNS:
            o, lse = _dilated_call(qd, kvd, slopes_dil, window, dilation)
            o_d.append(o)
            lse_d.append(lse)
        h = _merge_call(h, g_l, w_gate, o_nsa, o_d, lse_d, w_br_nsa[layer].astype(BF16),
                        w_br_dil[layer].astype(BF16), w_out[layer].astype(BF16),
                        final_g.reshape(1, D), tm, final=layer == w_in.shape[0] - 1)
    return h
```

```python
import functools
import math

import numpy as np
import jax
import jax.numpy as jnp
from jax import lax
from jax.experimental import pallas as pl
from jax.experimental.pallas import tpu as pltpu

F32 = jnp.float32
BF16 = jnp.bfloat16

HEAD_DIM = 64
LANES = 128
NSA_HEADS = 8
NSA_GROUPS = 2
NSA_HPG = NSA_HEADS // NSA_GROUPS
CMP_BLOCK = 32
CMP_STRIDE = 16
SLC_BLOCK = 64
SLC_TOPK = 16
WIN = 512
FORCE_SCORE = 1.0e4
DIL_HEADS = 8
DIL_PATTERNS = ((128, 1), (512, 4), (2048, 16))
Q_BLOCK = 128
RMS_EPS = 1e-6
KEY_CHUNK = 128
VT_ROWS = 80
GATE_ROWS = 16
FEAT_SPLIT = 3
FLAG_BITS = 16
NEG_INIT = -1.0e30
LOG2E = math.log2(math.e)
VMEM_LIMIT = 56 * 1024 * 1024


def _alibi_slopes():
    n = NSA_HEADS + DIL_HEADS
    s = 2.0 ** (-8.0 * np.arange(1, n + 1) / n)
    return s[0::2].astype(np.float32), s[1::2].astype(np.float32)


def _nt(a, b):
    return lax.dot_general(a, b, (((1,), (1,)), ((), ())), preferred_element_type=F32)


def _rms(x, g):
    return (x * lax.rsqrt(jnp.mean(x * x, axis=-1, keepdims=True) + RMS_EPS)) * g


def _head_tile(q_pairs, h):
    tile = q_pairs[:, (h // 2) * LANES:(h // 2 + 1) * LANES]
    return pltpu.roll(tile, HEAD_DIM, 1) if h % 2 else tile


def _pad_heads(q_pairs, n_heads):
    lane = lax.broadcasted_iota(jnp.int32, (q_pairs.shape[0], LANES), 1)
    return [jnp.where(lane < HEAD_DIM, _head_tile(q_pairs, h), 0.0) for h in range(n_heads)]


def _compact_pair(a_pad, b_pad):
    lane = lax.broadcasted_iota(jnp.int32, a_pad.shape, 1)
    return jnp.where(lane < HEAD_DIM, pltpu.roll(a_pad, HEAD_DIM, 1), b_pad)


def _pos_features(pos):
    lane = lax.broadcasted_iota(jnp.int32, pos.shape, 1) - HEAD_DIM
    hi = (pos // LANES).astype(F32)
    lo = (pos % LANES).astype(F32)
    return jnp.where(lane < 0, 0.0, jnp.where(lane < FEAT_SPLIT, hi, jnp.where(lane < 2 * FEAT_SPLIT, lo, 0.0)))


def _slope_features(slopes):
    out = np.zeros((len(slopes), LANES), np.float32)
    for h, s in enumerate(slopes):
        rest = np.float32(s) * np.float32(LOG2E)
        for i in range(FEAT_SPLIT):
            piece = np.float32(np.asarray(rest, np.float32).astype(BF16))
            out[h, HEAD_DIM + FEAT_SPLIT + i] = piece
            out[h, HEAD_DIM + i] = piece * LANES
            rest = np.float32(rest - piece)
    return out


_PROJ_SEGS = (("qn", 512), ("cmp", 256), ("ks", 256), ("kw", 256), ("vs", 128), ("vw", 128),
              ("gate", 128), ("qd", 512), ("kvd", 1024))
_PROJ_OFFS = dict(zip((n for n, _ in _PROJ_SEGS),
                      zip(np.cumsum([0] + [w for _, w in _PROJ_SEGS])[:-1].tolist(),
                          np.cumsum([w for _, w in _PROJ_SEGS]).tolist())))


def _proj_kernel(x_ref, g_ref, w_ref, qn_ref, cmp_ref, ks_ref, kw_ref, vst_ref, vwt_ref, gate_ref,
                 qd_ref, kvd_ref):
    tm = x_ref.shape[0]
    xb = _rms(x_ref[...], g_ref[...]).astype(BF16)
    scale = HEAD_DIM ** -0.5

    def seg(name):
        c0, c1 = _PROJ_OFFS[name]
        return jnp.dot(xb, w_ref[:, c0:c1], preferred_element_type=F32)

    qn_ref[...] = (seg("qn") * (scale * LOG2E)).astype(BF16)
    cmp_ref[...] = seg("cmp")
    feat = _pos_features(pl.program_id(1) * tm + lax.broadcasted_iota(jnp.int32, (tm, LANES), 0))
    ks, kw = seg("ks"), seg("kw")
    vst, vwt = seg("vs").T, seg("vw").T
    gate_t = jax.nn.sigmoid(seg("gate")).T
    tail_row = lax.broadcasted_iota(jnp.int32, (VT_ROWS - HEAD_DIM, tm), 0)
    tail = jnp.where(tail_row == 0, 1.0, 0.0)
    for g in range(NSA_GROUPS):
        ks_ref[g] = (ks[:, g * LANES:(g + 1) * LANES] + feat).astype(BF16)
        kw_ref[g] = (kw[:, g * LANES:(g + 1) * LANES] + feat).astype(BF16)
        gate_ref[g] = gate_t[g * GATE_ROWS:(g + 1) * GATE_ROWS]
        for src, dst in ((vst, vst_ref), (vwt, vwt_ref)):
            vt = jnp.concatenate([src[g * HEAD_DIM:(g + 1) * HEAD_DIM], tail], axis=0).astype(BF16)
            for j in range(tm // KEY_CHUNK):
                dst[g, j] = vt[:, j * KEY_CHUNK:(j + 1) * KEY_CHUNK]
    qd_ref[...] = (seg("qd") * scale).astype(BF16)
    kvd_ref[...] = seg("kvd").astype(BF16)


def _proj_call(x, g, w, tm):
    B, S, D = x.shape
    G = NSA_GROUPS
    row = lambda c: pl.BlockSpec((None, tm, c), lambda b, i: (b, i, 0))
    k_spec = pl.BlockSpec((None, G, tm, LANES), lambda b, i: (b, 0, i, 0))
    vt_spec = pl.BlockSpec((None, G, tm // KEY_CHUNK, VT_ROWS, KEY_CHUNK), lambda b, i: (b, 0, i, 0, 0))
    out_shape = (
        jax.ShapeDtypeStruct((B, S, 512), BF16),
        jax.ShapeDtypeStruct((B, S, 256), F32),
        jax.ShapeDtypeStruct((B, G, S, LANES), BF16),
        jax.ShapeDtypeStruct((B, G, S, LANES), BF16),
        jax.ShapeDtypeStruct((B, G, S // KEY_CHUNK, VT_ROWS, KEY_CHUNK), BF16),
        jax.ShapeDtypeStruct((B, G, S // KEY_CHUNK, VT_ROWS, KEY_CHUNK), BF16),
        jax.ShapeDtypeStruct((B, G, GATE_ROWS, S), F32),
        jax.ShapeDtypeStruct((B, S, 512), BF16),
        jax.ShapeDtypeStruct((B, S, 1024), BF16),
    )
    return pl.pallas_call(
        _proj_kernel,
        grid=(B, S // tm),
        in_specs=[row(D), pl.BlockSpec((1, D), lambda b, i: (0, 0)),
                  pl.BlockSpec(w.shape, lambda b, i: (0, 0))],
        out_specs=(row(512), row(256), k_spec, k_spec, vt_spec, vt_spec,
                   pl.BlockSpec((None, G, GATE_ROWS, tm), lambda b, i: (b, 0, 0, i)),
                   row(512), row(1024)),
        out_shape=out_shape,
        compiler_params=pltpu.CompilerParams(
            dimension_semantics=("parallel", "parallel"), vmem_limit_bytes=VMEM_LIMIT),
        name="proj",
    )(x, g, w)


def _compress_kernel(r_ref, posa_ref, posb_ref, wa_ref, wb_ref, w2_ref, ovt_ref, kc_ref, cvo_ref):
    r = r_ref[...]
    n = r.shape[0]
    ha = jnp.dot((r + posa_ref[...]).astype(BF16), wa_ref[...], preferred_element_type=F32)
    hb = jnp.dot((r + posb_ref[...]).astype(BF16), wb_ref[...], preferred_element_type=F32)
    hid = jax.nn.gelu(ha + pltpu.roll(hb, n - 1, 0))
    out = jnp.dot(hid.astype(BF16), w2_ref[...], preferred_element_type=F32)
    cmp_end = lax.broadcasted_iota(jnp.int32, (n, LANES), 0) * CMP_STRIDE + (CMP_BLOCK - 1)
    feat = _pos_features(cmp_end)
    v_t = out[:, NSA_GROUPS * LANES:].T
    for g in range(NSA_GROUPS):
        kc_ref[g] = (out[:, g * LANES:(g + 1) * LANES] + feat).astype(BF16)
        cvo_ref[g] = jnp.concatenate([v_t[g * HEAD_DIM:(g + 1) * HEAD_DIM], ovt_ref[...]],
                                     axis=0).astype(BF16)


def _compress_call(r, posa, posb, wa, wb, w2, ovt):
    B, R, C = r.shape
    G = NSA_GROUPS
    rows = HEAD_DIM + ovt.shape[0]
    full = lambda a: pl.BlockSpec(a.shape, lambda b: (0,) * a.ndim)
    return pl.pallas_call(
        _compress_kernel,
        grid=(B,),
        in_specs=[pl.BlockSpec((None, R, C), lambda b: (b, 0, 0)),
                  full(posa), full(posb), full(wa), full(wb), full(w2), full(ovt)],
        out_specs=(pl.BlockSpec((None, G, R, LANES), lambda b: (b, 0, 0, 0)),
                   pl.BlockSpec((None, G, rows, R), lambda b: (b, 0, 0, 0))),
        out_shape=(jax.ShapeDtypeStruct((B, G, R, LANES), BF16),
                   jax.ShapeDtypeStruct((B, G, rows, R), BF16)),
        compiler_params=pltpu.CompilerParams(
            dimension_semantics=("parallel",), vmem_limit_bytes=VMEM_LIMIT),
        name="compress",
    )(r, posa, posb, wa, wb, w2, ovt)


def _flash_step(s, ok, v_t, m_ref, acc_ref):
    ps, alphas = [], []
    for h in range(NSA_HPG):
        cols = slice(h * LANES, (h + 1) * LANES)
        sh = jnp.where(ok, s[:, cols], -jnp.inf)
        m_old = m_ref[:, cols]
        m_new = jnp.maximum(m_old, jnp.max(sh, axis=0, keepdims=True))
        alphas.append(jnp.exp2(m_old - m_new))
        ps.append(jnp.exp2(sh - m_new).astype(BF16))
        m_ref[:, cols] = m_new
    pv = jnp.dot(v_t, jnp.concatenate(ps, axis=1), preferred_element_type=F32)
    acc_ref[...] = jnp.concatenate(alphas, axis=1) * acc_ref[...] + pv


def _flash_result(acc_ref):
    l = acc_ref[HEAD_DIM:HEAD_DIM + 1, :]
    return acc_ref[0:HEAD_DIM, :] * (1.0 / jnp.maximum(l, 1e-30))


def _nsa_kernel(qfeat_ref, q_ref, kc_ref, cvo_ref, ks_ref, vst_ref, kw_ref, vwt_ref, gate_ref,
                o_ref, sel_ref, ms_ref, accs_ref, mw_ref, accw_ref, flag_ref):
    n = pl.program_id(2)
    qb, kc = Q_BLOCK, KEY_CHUNK
    t0 = n * qb
    lane = lax.broadcasted_iota(jnp.int32, (qb, LANES), 1)
    qf = q_ref[...].astype(F32)
    q = jnp.concatenate(
        [jnp.where(lane < HEAD_DIM, _head_tile(qf, h), qfeat_ref[h][0:1, :]) for h in range(NSA_HPG)],
        axis=0).astype(BF16)

    n_cmp = kc_ref.shape[0]
    n_sel = sel_ref.shape[0]
    s = _nt(kc_ref[...], q)
    cmp_end = lax.broadcasted_iota(jnp.int32, (n_cmp, qb), 0) * CMP_STRIDE + (CMP_BLOCK - 1)
    ok_c = cmp_end <= t0 + lax.broadcasted_iota(jnp.int32, (n_cmp, qb), 1)
    ps = []
    for h in range(NSA_HPG):
        sh = jnp.where(ok_c, s[:, h * qb:(h + 1) * qb], -jnp.inf)
        m = jnp.max(sh, axis=0, keepdims=True)
        m = jnp.where(m == -jnp.inf, 0.0, m)
        e = jnp.exp2(sh - m)
        l = jnp.sum(e, axis=0, keepdims=True)
        ps.append((e * (1.0 / jnp.maximum(l, 1e-30))).astype(BF16))
    r = jnp.dot(cvo_ref[...], jnp.concatenate(ps, axis=1), preferred_element_type=F32)
    o_cmp = r[0:HEAD_DIM]
    imp_t = r[HEAD_DIM:, 0:qb]
    for h in range(1, NSA_HPG):
        imp_t = imp_t + r[HEAD_DIM:, h * qb:(h + 1) * qb]

    blk = lax.broadcasted_iota(jnp.int32, (n_sel, qb), 0).astype(F32)
    cur = ((t0 + lax.broadcasted_iota(jnp.int32, (n_sel, qb), 1)) // SLC_BLOCK).astype(F32)
    forced = (blk == 0) | (blk == cur) | (blk == cur - 1)
    valid = blk <= cur
    score = jnp.where(valid, jnp.where(forced, FORCE_SCORE, imp_t), -1.0)
    picked = jnp.zeros((n_sel, qb), F32)
    for _ in range(min(SLC_TOPK, n_sel)):
        mx = jnp.max(score, axis=0, keepdims=True)
        idx = jnp.min(jnp.where(score == mx, blk, float(n_sel)), axis=0, keepdims=True)
        hit = blk == idx
        picked = jnp.where(hit, 1.0, picked)
        score = jnp.where(hit, -jnp.inf, score)
    sel_t = jnp.where(valid, picked, 0.0)
    sel_ref[...] = sel_t
    any_q = jnp.max(sel_t, axis=1, keepdims=True)
    bit = jnp.left_shift(1, lax.broadcasted_iota(jnp.int32, (n_sel, 1), 0) % FLAG_BITS).astype(F32)
    packed = any_q * bit
    for i in range(n_sel // FLAG_BITS):
        word = jnp.sum(packed[i * FLAG_BITS:(i + 1) * FLAG_BITS], axis=0, keepdims=True)
        flag_ref[i] = word.astype(jnp.int32)[0, 0]

    row = lax.broadcasted_iota(jnp.int32, (kc, qb), 0)
    back = lax.broadcasted_iota(jnp.int32, (kc, qb), 1) - row

    ms_ref[...] = jnp.full(ms_ref.shape, NEG_INIT, F32)
    accs_ref[...] = jnp.zeros(accs_ref.shape, F32)
    blocks_per_chunk = kc // SLC_BLOCK
    chunks_per_word = FLAG_BITS // blocks_per_chunk

    def slc_body(c, carry):
        bits = (flag_ref[c // chunks_per_word] >> ((c % chunks_per_word) * blocks_per_chunk)) \
            & ((1 << blocks_per_chunk) - 1)

        @pl.when(bits != 0)
        def _():
            k0 = pl.multiple_of(c * kc, kc)
            chosen = sel_ref[pl.ds(c * blocks_per_chunk, 1), :]
            for j in range(1, blocks_per_chunk):
                chosen = jnp.where(row < j * SLC_BLOCK, chosen,
                                   sel_ref[pl.ds(c * blocks_per_chunk + j, 1), :])
            ok = jnp.where(back + (t0 - k0) >= 0, chosen, 0.0) > 0.5
            _flash_step(_nt(ks_ref[pl.ds(k0, kc), :], q), ok, vst_ref[c], ms_ref, accs_ref)

        return carry

    lax.fori_loop(0, n * (qb // kc) + qb // kc, slc_body, 0)

    mw_ref[...] = jnp.full(mw_ref.shape, NEG_INIT, F32)
    accw_ref[...] = jnp.zeros(accw_ref.shape, F32)

    def win_body(c, carry):
        k0 = pl.multiple_of(c * kc, kc)
        dist = back + (t0 - k0)
        ok = (dist >= 0) & (dist < WIN)
        _flash_step(_nt(kw_ref[pl.ds(k0, kc), :], q), ok, vwt_ref[c], mw_ref, accw_ref)
        return carry

    lax.fori_loop(jnp.maximum(t0 - WIN, 0) // kc, (t0 + qb) // kc, win_body, 0)

    o_slc = _flash_result(accs_ref)
    o_win = _flash_result(accw_ref)
    gate = gate_ref[...]
    outs = []
    for h in range(NSA_HPG):
        cols = slice(h * qb, (h + 1) * qb)
        gc = [gate[j * NSA_HPG + h:j * NSA_HPG + h + 1, :] for j in range(3)]
        outs.append(gc[0] * o_cmp[:, cols] + gc[1] * o_slc[:, cols] + gc[2] * o_win[:, cols])
    o_ref[...] = jnp.concatenate(outs, axis=0).T


def _nsa_call(qfeat, qn, kc, cvo, ks, vst, kw, vwt, gates):
    B, S, _ = qn.shape
    G = NSA_GROUPS
    nb = S // Q_BLOCK
    n_cmp = kc.shape[2]
    n_sel = S // SLC_BLOCK
    width = NSA_HPG * Q_BLOCK
    per_group = lambda a: pl.BlockSpec((None, None) + a.shape[2:],
                                       lambda b, g, n: (b, g) + (0,) * (a.ndim - 2))
    return pl.pallas_call(
        _nsa_kernel,
        grid=(B, G, nb),
        in_specs=[
            pl.BlockSpec((None,) + qfeat.shape[1:], lambda b, g, n: (g, 0, 0, 0)),
            pl.BlockSpec((None, Q_BLOCK, NSA_HPG * HEAD_DIM), lambda b, g, n: (b, n, g)),
            per_group(kc), per_group(cvo), per_group(ks), per_group(vst), per_group(kw), per_group(vwt),
            pl.BlockSpec((None, None, GATE_ROWS, Q_BLOCK), lambda b, g, n: (b, g, 0, n)),
        ],
        out_specs=pl.BlockSpec((None, Q_BLOCK, NSA_HPG * HEAD_DIM), lambda b, g, n: (b, n, g)),
        out_shape=jax.ShapeDtypeStruct((B, S, NSA_HEADS * HEAD_DIM), F32),
        scratch_shapes=[pltpu.VMEM((n_sel, Q_BLOCK), F32),
                        pltpu.VMEM((1, width), F32), pltpu.VMEM((VT_ROWS, width), F32),
                        pltpu.VMEM((1, width), F32), pltpu.VMEM((VT_ROWS, width), F32),
                        pltpu.SMEM((n_sel // FLAG_BITS,), jnp.int32)],
        compiler_params=pltpu.CompilerParams(
            dimension_semantics=("parallel", "parallel", "arbitrary"), vmem_limit_bytes=VMEM_LIMIT),
        name="nsa",
    )(qfeat, qn, kc, cvo, ks, vst, kw, vwt, gates)


def _dilated_kernel(q_ref, kvp_ref, kvc_ref, o_ref, lse_ref, *, slopes, dilation, wd):
    c = Q_BLOCK
    first = pl.program_id(2) == 0
    qs = _pad_heads(q_ref[...].astype(F32), DIL_HEADS)
    back = (lax.broadcasted_iota(jnp.int32, (c, c), 0)
            - lax.broadcasted_iota(jnp.int32, (c, c), 1))
    kk_prev = back + c
    ok_prev = kk_prev <= jnp.where(first, 0, wd)
    ok_cur = back >= 0
    dist_prev = (kk_prev * dilation).astype(F32)
    dist_cur = (back * dilation).astype(F32)
    outs, lses = [], []
    for h in range(DIL_HEADS):
        qh = qs[h].astype(BF16)
        kvp = kvp_ref[:, h * LANES:(h + 1) * LANES]
        kvc = kvc_ref[:, h * LANES:(h + 1) * LANES]
        sp = jnp.where(ok_prev, _nt(qh, kvp) - slopes[h] * dist_prev, -jnp.inf)
        sc = jnp.where(ok_cur, _nt(qh, kvc) - slopes[h] * dist_cur, -jnp.inf)
        m = jnp.maximum(jnp.max(sp, axis=1, keepdims=True), jnp.max(sc, axis=1, keepdims=True))
        ep = jnp.exp(sp - m)
        ec = jnp.exp(sc - m)
        l = jnp.sum(ep, axis=1, keepdims=True) + jnp.sum(ec, axis=1, keepdims=True)
        acc = (jnp.dot(ep.astype(BF16), kvp, preferred_element_type=F32)
               + jnp.dot(ec.astype(BF16), kvc, preferred_element_type=F32))
        outs.append(acc / l)
        lses.append(jnp.broadcast_to(m + jnp.log(l), (c, LANES)))
    lane = lax.broadcasted_iota(jnp.int32, (c, LANES), 1)
    for p in range(DIL_HEADS // 2):
        o_ref[:, p * LANES:(p + 1) * LANES] = _compact_pair(outs[2 * p], outs[2 * p + 1])
        lse_ref[:, p * LANES:(p + 1) * LANES] = jnp.where(lane < HEAD_DIM, lses[2 * p], lses[2 * p + 1])


def _dilated_call(qd, kvd, slopes, window, dilation):
    B, S, _ = qd.shape
    L = S // dilation
    nblk = L // Q_BLOCK
    cq, ckv = DIL_HEADS * HEAD_DIM, DIL_HEADS * LANES
    q_v = qd.reshape(B, L, dilation * cq)
    kv_v = kvd.reshape(B, L, dilation * ckv)
    kern = functools.partial(_dilated_kernel, slopes=tuple(float(s) for s in slopes),
                             dilation=dilation, wd=window // dilation)
    blk = lambda cw: pl.BlockSpec((None, Q_BLOCK, cw), lambda b, p, i: (b, i, p))
    o, lse = pl.pallas_call(
        kern,
        grid=(B, dilation, nblk),
        in_specs=[blk(cq),
                  pl.BlockSpec((None, Q_BLOCK, ckv), lambda b, p, i: (b, jnp.maximum(i - 1, 0), p)),
                  blk(ckv)],
        out_specs=(blk(cq), blk(cq)),
        out_shape=(jax.ShapeDtypeStruct((B, L, dilation * cq), F32),
                   jax.ShapeDtypeStruct((B, L, dilation * cq), F32)),
        compiler_params=pltpu.CompilerParams(
            dimension_semantics=("parallel", "parallel", "arbitrary"), vmem_limit_bytes=VMEM_LIMIT),
        name=f"dilated_d{dilation}",
    )(q_v, kv_v, kv_v)
    return o.reshape(B, S, cq), lse.reshape(B, S, cq)


def _merge_kernel(x_ref, g_ref, wg_ref, onsa_ref, o1_ref, o2_ref, o3_ref, l1_ref, l2_ref, l3_ref,
                  wbn_ref, wbd_ref, wo_ref, fg_ref, out_ref, *, final):
    x = x_ref[...]
    xb = _rms(x, g_ref[...]).astype(BF16)
    cz = NSA_HEADS * HEAD_DIM
    d = x.shape[-1]

    def gate_seg(c0, c1):
        return jnp.dot(xb, wg_ref[:, c0:c1], preferred_element_type=F32)

    o_nsa = onsa_ref[...] * jax.nn.silu(gate_seg(0, cz))
    l1, l2, l3 = l1_ref[...], l2_ref[...], l3_ref[...]
    mx = jnp.maximum(jnp.maximum(l1, l2), l3)
    e1, e2, e3 = jnp.exp(l1 - mx), jnp.exp(l2 - mx), jnp.exp(l3 - mx)
    den = e1 + e2 + e3
    o_dil = (e1 / den) * o1_ref[...] + (e2 / den) * o2_ref[...] + (e3 / den) * o3_ref[...]
    o_dil = o_dil * jax.nn.silu(gate_seg(cz, 2 * cz))
    a = jnp.dot(o_nsa.astype(BF16), wbn_ref[...], preferred_element_type=F32)
    bd = jnp.dot(o_dil.astype(BF16), wbd_ref[...], preferred_element_type=F32)
    merged = (jax.nn.sigmoid(gate_seg(2 * cz, 2 * cz + d)) * a
              + jax.nn.sigmoid(gate_seg(2 * cz + d, 2 * cz + 2 * d)) * bd)
    y = jnp.dot(merged.astype(BF16), wo_ref[...], preferred_element_type=F32)
    out_ref[...] = _rms(x + y, fg_ref[...]) if final else x + y


def _merge_call(x, g, wg, o_nsa, o_d, lse_d, wbn, wbd, wo, fg, tm, final):
    B, S, D = x.shape
    cz = o_nsa.shape[-1]
    row = lambda c: pl.BlockSpec((None, tm, c), lambda b, i: (b, i, 0))
    full = lambda a: pl.BlockSpec(a.shape, lambda b, i: (0,) * a.ndim)
    return pl.pallas_call(
        functools.partial(_merge_kernel, final=final),
        grid=(B, S // tm),
        in_specs=[row(D), full(g), full(wg), row(cz), row(cz), row(cz), row(cz), row(cz), row(cz),
                  row(cz), full(wbn), full(wbd), full(wo), full(fg)],
        out_specs=row(D),
        out_shape=jax.ShapeDtypeStruct((B, S, D), F32),
        compiler_params=pltpu.CompilerParams(
            dimension_semantics=("parallel", "parallel"), vmem_limit_bytes=VMEM_LIMIT),
        name="merge",
    )(x, g, wg, o_nsa, *o_d, *lse_d, wbn, wbd, wo, fg)


def _proj_columns(d_model):
    hd, G, H = HEAD_DIM, NSA_GROUPS, NSA_HEADS
    nsa_w = H * hd
    o_q, o_kv = 0, nsa_w
    o_g = o_kv + 6 * G * hd
    o_zn = o_g + 3 * H
    o_qkvd = o_zn + nsa_w
    dil_w = DIL_HEADS * hd
    o_zd = o_qkvd + 3 * dil_w
    o_mg = o_zd + dil_w
    kv = lambda j, g: o_kv + (j * G + g) * hd + np.arange(hd)
    pad = np.full(hd, -1)
    cols = [o_q + np.arange(nsa_w)]
    cols += [kv(j, g) for j in (0, 1) for g in range(G)]
    for j in (2, 4):
        for g in range(G):
            cols += [kv(j, g), pad]
    for j in (3, 5):
        cols += [kv(j, g) for g in range(G)]
    lanes = np.full(LANES, -1)
    for g in range(G):
        for j in range(3):
            lanes[g * GATE_ROWS + j * NSA_HPG + np.arange(NSA_HPG)] = o_g + j * H + g * NSA_HPG + np.arange(NSA_HPG)
    cols.append(lanes)
    cols.append(o_qkvd + np.arange(dil_w))
    for h in range(DIL_HEADS):
        cols += [o_qkvd + j * dil_w + h * hd + np.arange(hd) for j in (1, 2)]
    proj_cols = np.concatenate(cols)
    assert proj_cols.size == _PROJ_OFFS["kvd"][1]
    gate_cols = np.concatenate([o_zn + np.arange(nsa_w), o_zd + np.arange(dil_w),
                                o_mg + np.arange(2 * d_model)])
    return proj_cols, gate_cols


def _compress_weights(pos_k, w1_k, w2_k, pos_v, w1_v, w2_v):
    hd, G, half = HEAD_DIM, NSA_GROUPS, CMP_BLOCK // 2
    eye = jnp.eye(2 * G, dtype=F32)
    w1 = jnp.stack([w1_k] * G + [w1_v] * G)
    pos = jnp.stack([pos_k] * G + [pos_v] * G)

    def first_layer(lo):
        w = jnp.einsum('slde,st->lsdte', w1[:, lo:lo + half], eye)
        p = pos[:, lo:lo + half].transpose(1, 0, 2)
        return w.reshape(half * 2 * G * hd, 2 * G * hd).astype(BF16), p.reshape(1, half * 2 * G * hd)

    wa, posa = first_layer(0)
    wb, posb = first_layer(half)
    w2 = jnp.zeros((2 * G, hd, 3 * G, hd), F32)
    for g in range(G):
        w2 = w2.at[g, :, 2 * g].set(w2_k).at[G + g, :, 2 * G + g].set(w2_v)
    return posa, posb, wa, wb, w2.reshape(2 * G * hd, 3 * G * hd).astype(BF16)


def _overlap_t(n_cmp_pad, n_sel):
    i = np.arange(n_cmp_pad)[None, :]
    j = np.arange(n_sel)[:, None]
    lo = np.maximum(i * CMP_STRIDE, j * SLC_BLOCK)
    hi = np.minimum(i * CMP_STRIDE + CMP_BLOCK, (j + 1) * SLC_BLOCK)
    return (np.clip(hi - lo, 0, None) / CMP_BLOCK).astype(np.float32)


def kernel(x, norm_g, w_in, cmp_pos_k, cmp_w1_k, cmp_w2_k, cmp_pos_v, cmp_w1_v, cmp_w2_v,
           w_br_nsa, w_br_dil, w_out, final_g):
    B, S, D = x.shape
    n_sel = S // SLC_BLOCK
    assert S % (Q_BLOCK * max(d for _, d in DIL_PATTERNS)) == 0 and n_sel % FLAG_BITS == 0
    n_cmp_pad = S // CMP_STRIDE
    slopes_nsa, slopes_dil = _alibi_slopes()
    proj_cols, gate_cols = _proj_columns(D)
    ovt = jnp.asarray(_overlap_t(n_cmp_pad, n_sel))
    qfeat = jnp.asarray(np.broadcast_to(
        _slope_features(slopes_nsa).reshape(NSA_GROUPS, NSA_HPG, 1, LANES),
        (NSA_GROUPS, NSA_HPG, 8, LANES)))
    tm = 512

    h = x
    for layer in range(w_in.shape[0]):
        w_l = jnp.concatenate([w_in[layer], jnp.zeros((D, 1), w_in.dtype)], axis=1)
        w_proj = jnp.take(w_l, jnp.asarray(np.where(proj_cols < 0, w_l.shape[1] - 1, proj_cols)),
                          axis=1).astype(BF16)
        w_gate = jnp.take(w_l, jnp.asarray(gate_cols), axis=1).astype(BF16)
        g_l = norm_g[layer].reshape(1, D)

        qn, cmp_src, ks, kw, vst, vwt, gates, qd, kvd = _proj_call(h, g_l, w_proj, tm)
        cw = _compress_weights(cmp_pos_k[layer], cmp_w1_k[layer], cmp_w2_k[layer],
                               cmp_pos_v[layer], cmp_w1_v[layer], cmp_w2_v[layer])
        kc, cvo = _compress_call(cmp_src.reshape(B, n_cmp_pad, CMP_STRIDE * cmp_src.shape[-1]), *cw, ovt)
        o_nsa = _nsa_call(qfeat, qn, kc, cvo, ks, vst, kw, vwt, gates)
        o_d, lse_d = [], []
        for window, dilation in DIL_PATTERNS:
            o, lse = _dilated_call(qd, kvd, slopes_dil, window, dilation)
            o_d.append(o)
            lse_d.append(lse)
        h = _merge_call(h, g_l, w_gate, o_nsa, o_d, lse_d, w_br_nsa[layer].astype(BF16),
                        w_br_dil[layer].astype(BF16), w_out[layer].astype(BF16),
                        final_g.reshape(1, D), tm, final=layer == w_in.shape[0] - 1)
    return h
```

```python
import functools
import math

import numpy as np
import jax
import jax.numpy as jnp
from jax import lax
from jax.experimental import pallas as pl
from jax.experimental.pallas import tpu as pltpu

F32 = jnp.float32
BF16 = jnp.bfloat16

HEAD_DIM = 64
LANES = 128
NSA_HEADS = 8
NSA_GROUPS = 2
NSA_HPG = NSA_HEADS // NSA_GROUPS
CMP_BLOCK = 32
CMP_STRIDE = 16
SLC_BLOCK = 64
SLC_TOPK = 16
WIN = 512
FORCE_SCORE = 1.0e4
DIL_HEADS = 8
DIL_PATTERNS = ((128, 1), (512, 4), (2048, 16))
Q_BLOCK = 128
RMS_EPS = 1e-6
KEY_CHUNK = 128
VT_ROWS = 80
GATE_ROWS = 16
FEAT_SPLIT = 3
FLAG_BITS = 16
SLC_GROUP = 4
SEL_PAD_ROWS = 8
NEG_INIT = -1.0e30
LOG2E = math.log2(math.e)
VMEM_LIMIT = 56 * 1024 * 1024


def _alibi_slopes():
    n = NSA_HEADS + DIL_HEADS
    s = 2.0 ** (-8.0 * np.arange(1, n + 1) / n)
    return s[0::2].astype(np.float32), s[1::2].astype(np.float32)


def _nt(a, b):
    return lax.dot_general(a, b, (((1,), (1,)), ((), ())), preferred_element_type=F32)


def _rms(x, g):
    return (x * lax.rsqrt(jnp.mean(x * x, axis=-1, keepdims=True) + RMS_EPS)) * g


def _head_tile(q_pairs, h):
    tile = q_pairs[:, (h // 2) * LANES:(h // 2 + 1) * LANES]
    return pltpu.roll(tile, HEAD_DIM, 1) if h % 2 else tile


def _pad_heads(q_pairs, n_heads):
    lane = lax.broadcasted_iota(jnp.int32, (q_pairs.shape[0], LANES), 1)
    return [jnp.where(lane < HEAD_DIM, _head_tile(q_pairs, h), 0.0) for h in range(n_heads)]


def _compact_pair(a_pad, b_pad):
    lane = lax.broadcasted_iota(jnp.int32, a_pad.shape, 1)
    return jnp.where(lane < HEAD_DIM, pltpu.roll(a_pad, HEAD_DIM, 1), b_pad)


def _pos_features(pos):
    lane = lax.broadcasted_iota(jnp.int32, pos.shape, 1) - HEAD_DIM
    hi = (pos // LANES).astype(F32)
    lo = (pos % LANES).astype(F32)
    return jnp.where(lane < 0, 0.0, jnp.where(lane < FEAT_SPLIT, hi, jnp.where(lane < 2 * FEAT_SPLIT, lo, 0.0)))


def _slope_features(slopes):
    out = np.zeros((len(slopes), LANES), np.float32)
    for h, s in enumerate(slopes):
        rest = np.float32(s) * np.float32(LOG2E)
        for i in range(FEAT_SPLIT):
            piece = np.float32(np.asarray(rest, np.float32).astype(BF16))
            out[h, HEAD_DIM + FEAT_SPLIT + i] = piece
            out[h, HEAD_DIM + i] = piece * LANES
            rest = np.float32(rest - piece)
    return out


_PROJ_SEGS = (("qn", 512), ("cmp", 256), ("ks", 256), ("kw", 256), ("vs", 128), ("vw", 128),
              ("gate", 128), ("qd", 512), ("kvd", 1024))
_PROJ_OFFS = dict(zip((n for n, _ in _PROJ_SEGS),
                      zip(np.cumsum([0] + [w for _, w in _PROJ_SEGS])[:-1].tolist(),
                          np.cumsum([w for _, w in _PROJ_SEGS]).tolist())))


def _proj_kernel(x_ref, g_ref, w_ref, qn_ref, cmp_ref, ks_ref, kw_ref, vst_ref, vwt_ref, gate_ref,
                 qd_ref, kvd_ref):
    tm = x_ref.shape[0]
    xb = _rms(x_ref[...], g_ref[...]).astype(BF16)
    scale = HEAD_DIM ** -0.5

    def seg(name):
        c0, c1 = _PROJ_OFFS[name]
        return jnp.dot(xb, w_ref[:, c0:c1], preferred_element_type=F32)

    qn_ref[...] = (seg("qn") * (scale * LOG2E)).astype(BF16)
    cmp_ref[...] = seg("cmp")
    feat = _pos_features(pl.program_id(1) * tm + lax.broadcasted_iota(jnp.int32, (tm, LANES), 0))
    ks, kw = seg("ks"), seg("kw")
    vst, vwt = seg("vs").T, seg("vw").T
    gate_t = jax.nn.sigmoid(seg("gate")).T
    tail_row = lax.broadcasted_iota(jnp.int32, (VT_ROWS - HEAD_DIM, tm), 0)
    tail = jnp.where(tail_row == 0, 1.0, 0.0)
    for g in range(NSA_GROUPS):
        ks_ref[g] = (ks[:, g * LANES:(g + 1) * LANES] + feat).astype(BF16)
        kw_ref[g] = (kw[:, g * LANES:(g + 1) * LANES] + feat).astype(BF16)
        gate_ref[g] = gate_t[g * GATE_ROWS:(g + 1) * GATE_ROWS]
        for src, dst in ((vst, vst_ref), (vwt, vwt_ref)):
            vt = jnp.concatenate([src[g * HEAD_DIM:(g + 1) * HEAD_DIM], tail], axis=0).astype(BF16)
            for j in range(tm // KEY_CHUNK):
                dst[g, j] = vt[:, j * KEY_CHUNK:(j + 1) * KEY_CHUNK]
    qd_ref[...] = (seg("qd") * scale).astype(BF16)
    kvd_ref[...] = seg("kvd").astype(BF16)


def _proj_call(x, g, w, tm):
    B, S, D = x.shape
    G = NSA_GROUPS
    row = lambda c: pl.BlockSpec((None, tm, c), lambda b, i: (b, i, 0))
    k_spec = pl.BlockSpec((None, G, tm, LANES), lambda b, i: (b, 0, i, 0))
    vt_spec = pl.BlockSpec((None, G, tm // KEY_CHUNK, VT_ROWS, KEY_CHUNK), lambda b, i: (b, 0, i, 0, 0))
    out_shape = (
        jax.ShapeDtypeStruct((B, S, 512), BF16),
        jax.ShapeDtypeStruct((B, S, 256), F32),
        jax.ShapeDtypeStruct((B, G, S, LANES), BF16),
        jax.ShapeDtypeStruct((B, G, S, LANES), BF16),
        jax.ShapeDtypeStruct((B, G, S // KEY_CHUNK, VT_ROWS, KEY_CHUNK), BF16),
        jax.ShapeDtypeStruct((B, G, S // KEY_CHUNK, VT_ROWS, KEY_CHUNK), BF16),
        jax.ShapeDtypeStruct((B, G, GATE_ROWS, S), F32),
        jax.ShapeDtypeStruct((B, S, 512), BF16),
        jax.ShapeDtypeStruct((B, S, 1024), BF16),
    )
    return pl.pallas_call(
        _proj_kernel,
        grid=(B, S // tm),
        in_specs=[row(D), pl.BlockSpec((1, D), lambda b, i: (0, 0)),
                  pl.BlockSpec(w.shape, lambda b, i: (0, 0))],
        out_specs=(row(512), row(256), k_spec, k_spec, vt_spec, vt_spec,
                   pl.BlockSpec((None, G, GATE_ROWS, tm), lambda b, i: (b, 0, 0, i)),
                   row(512), row(1024)),
        out_shape=out_shape,
        compiler_params=pltpu.CompilerParams(
            dimension_semantics=("parallel", "parallel"), vmem_limit_bytes=VMEM_LIMIT),
        name="proj",
    )(x, g, w)


def _compress_kernel(r_ref, posa_ref, posb_ref, wa_ref, wb_ref, w2_ref, ovt_ref, kc_ref, cvo_ref):
    r = r_ref[...]
    n = r.shape[0]
    ha = jnp.dot((r + posa_ref[...]).astype(BF16), wa_ref[...], preferred_element_type=F32)
    hb = jnp.dot((r + posb_ref[...]).astype(BF16), wb_ref[...], preferred_element_type=F32)
    hid = jax.nn.gelu(ha + pltpu.roll(hb, n - 1, 0))
    out = jnp.dot(hid.astype(BF16), w2_ref[...], preferred_element_type=F32)
    cmp_end = lax.broadcasted_iota(jnp.int32, (n, LANES), 0) * CMP_STRIDE + (CMP_BLOCK - 1)
    feat = _pos_features(cmp_end)
    v_t = out[:, NSA_GROUPS * LANES:].T
    for g in range(NSA_GROUPS):
        kc_ref[g] = (out[:, g * LANES:(g + 1) * LANES] + feat).astype(BF16)
        cvo_ref[g] = jnp.concatenate([v_t[g * HEAD_DIM:(g + 1) * HEAD_DIM], ovt_ref[...]],
                                     axis=0).astype(BF16)


def _compress_call(r, posa, posb, wa, wb, w2, ovt):
    B, R, C = r.shape
    G = NSA_GROUPS
    rows = HEAD_DIM + ovt.shape[0]
    full = lambda a: pl.BlockSpec(a.shape, lambda b: (0,) * a.ndim)
    return pl.pallas_call(
        _compress_kernel,
        grid=(B,),
        in_specs=[pl.BlockSpec((None, R, C), lambda b: (b, 0, 0)),
                  full(posa), full(posb), full(wa), full(wb), full(w2), full(ovt)],
        out_specs=(pl.BlockSpec((None, G, R, LANES), lambda b: (b, 0, 0, 0)),
                   pl.BlockSpec((None, G, rows, R), lambda b: (b, 0, 0, 0))),
        out_shape=(jax.ShapeDtypeStruct((B, G, R, LANES), BF16),
                   jax.ShapeDtypeStruct((B, G, rows, R), BF16)),
        compiler_params=pltpu.CompilerParams(
            dimension_semantics=("parallel",), vmem_limit_bytes=VMEM_LIMIT),
        name="compress",
    )(r, posa, posb, wa, wb, w2, ovt)


def _flash_step(s, ok, v_t, m_ref, acc_ref):
    ps, alphas = [], []
    for h in range(NSA_HPG):
        cols = slice(h * LANES, (h + 1) * LANES)
        sh = jnp.where(ok, s[:, cols], -jnp.inf)
        m_old = m_ref[:, cols]
        m_new = jnp.maximum(m_old, jnp.max(sh, axis=0, keepdims=True))
        alphas.append(jnp.exp2(m_old - m_new))
        ps.append(jnp.exp2(sh - m_new).astype(BF16))
        m_ref[:, cols] = m_new
    pv = jnp.dot(v_t, jnp.concatenate(ps, axis=1), preferred_element_type=F32)
    acc_ref[...] = jnp.concatenate(alphas, axis=1) * acc_ref[...] + pv


def _flash_result(acc_ref):
    l = acc_ref[HEAD_DIM:HEAD_DIM + 1, :]
    return acc_ref[0:HEAD_DIM, :] * (1.0 / jnp.maximum(l, 1e-30))


def _softmax_pv(s, ok, v_t):
    ps = []
    for h in range(NSA_HPG):
        sh = jnp.where(ok, s[:, h * LANES:(h + 1) * LANES], -jnp.inf)
        ps.append(jnp.exp2(sh - jnp.max(sh, axis=0, keepdims=True)).astype(BF16))
    r = jnp.dot(v_t, jnp.concatenate(ps, axis=1), preferred_element_type=F32)
    return r[0:HEAD_DIM] * (1.0 / jnp.maximum(r[HEAD_DIM:HEAD_DIM + 1], 1e-30))


def _nsa_kernel(qfeat_ref, q_ref, kc_ref, cvo_ref, ks_ref, vst_ref, kw_ref, vwt_ref, gate_ref,
                o_ref, sel_ref, ms_ref, accs_ref, flag_ref, list_ref):
    n = pl.program_id(2)
    qb, kc = Q_BLOCK, KEY_CHUNK
    t0 = n * qb
    lane = lax.broadcasted_iota(jnp.int32, (qb, LANES), 1)
    qf = q_ref[...].astype(F32)
    q = jnp.concatenate(
        [jnp.where(lane < HEAD_DIM, _head_tile(qf, h), qfeat_ref[h][0:1, :]) for h in range(NSA_HPG)],
        axis=0).astype(BF16)

    span = WIN + qb
    k_lo = pl.multiple_of(jnp.maximum(t0 - WIN, 0), kc)
    dist_w = ((t0 - k_lo) + lax.broadcasted_iota(jnp.int32, (span, qb), 1)
              - lax.broadcasted_iota(jnp.int32, (span, qb), 0))
    vt_w = jnp.concatenate([vwt_ref[k_lo // kc + u] for u in range(span // kc)], axis=1)
    o_win = _softmax_pv(_nt(kw_ref[pl.ds(k_lo, span), :], q), (dist_w >= 0) & (dist_w < WIN), vt_w)

    n_cmp = kc_ref.shape[0]
    n_sel = sel_ref.shape[0] - SEL_PAD_ROWS
    s = _nt(kc_ref[...], q)
    cmp_end = lax.broadcasted_iota(jnp.int32, (n_cmp, qb), 0) * CMP_STRIDE + (CMP_BLOCK - 1)
    ok_c = cmp_end <= t0 + lax.broadcasted_iota(jnp.int32, (n_cmp, qb), 1)
    ps = []
    for h in range(NSA_HPG):
        sh = jnp.where(ok_c, s[:, h * qb:(h + 1) * qb], -jnp.inf)
        m = jnp.max(sh, axis=0, keepdims=True)
        m = jnp.where(m == -jnp.inf, 0.0, m)
        e = jnp.exp2(sh - m)
        l = jnp.sum(e, axis=0, keepdims=True)
        ps.append((e * (1.0 / jnp.maximum(l, 1e-30))).astype(BF16))
    r = jnp.dot(cvo_ref[...], jnp.concatenate(ps, axis=1), preferred_element_type=F32)
    o_cmp = r[0:HEAD_DIM]
    imp_t = r[HEAD_DIM:, 0:qb]
    for h in range(1, NSA_HPG):
        imp_t = imp_t + r[HEAD_DIM:, h * qb:(h + 1) * qb]

    blk = lax.broadcasted_iota(jnp.int32, (n_sel, qb), 0).astype(F32)
    cur = ((t0 + lax.broadcasted_iota(jnp.int32, (n_sel, qb), 1)) // SLC_BLOCK).astype(F32)
    forced = (blk == 0) | (blk == cur) | (blk == cur - 1)
    valid = blk <= cur
    score = jnp.where(valid, jnp.where(forced, FORCE_SCORE, imp_t), -1.0)
    picked = jnp.zeros((n_sel, qb), F32)
    for _ in range(min(SLC_TOPK, n_sel)):
        mx = jnp.max(score, axis=0, keepdims=True)
        idx = jnp.min(jnp.where(score == mx, blk, float(n_sel)), axis=0, keepdims=True)
        hit = blk == idx
        picked = jnp.where(hit, 1.0, picked)
        score = jnp.where(hit, -jnp.inf, score)
    sel_t = jnp.where(valid, picked, 0.0)
    sel_ref[0:n_sel, :] = sel_t
    sel_ref[n_sel:, :] = jnp.zeros((SEL_PAD_ROWS, qb), F32)
    any_q = jnp.max(sel_t, axis=1, keepdims=True)
    bit = jnp.left_shift(1, lax.broadcasted_iota(jnp.int32, (n_sel, 1), 0) % FLAG_BITS).astype(F32)
    packed = any_q * bit
    for i in range(n_sel // FLAG_BITS):
        word = jnp.sum(packed[i * FLAG_BITS:(i + 1) * FLAG_BITS], axis=0, keepdims=True)
        flag_ref[i] = word.astype(jnp.int32)[0, 0]

    blocks_per_chunk = kc // SLC_BLOCK
    chunks_per_word = FLAG_BITS // blocks_per_chunk

    def scan_body(w, cnt):
        word = flag_ref[w]
        for j in range(chunks_per_word):
            bits = (word >> (j * blocks_per_chunk)) & ((1 << blocks_per_chunk) - 1)
            list_ref[cnt] = w * chunks_per_word + j
            cnt = cnt + (bits != 0).astype(jnp.int32)
        return cnt

    cnt = lax.fori_loop(0, (t0 + qb - 1) // (FLAG_BITS * SLC_BLOCK) + 1, scan_body, 0)
    for u in range(SLC_GROUP):
        list_ref[cnt + u] = -1

    row = lax.broadcasted_iota(jnp.int32, (kc, qb), 0)
    back = lax.broadcasted_iota(jnp.int32, (kc, qb), 1) - row
    ms_ref[...] = jnp.full(ms_ref.shape, NEG_INIT, F32)
    accs_ref[...] = jnp.zeros(accs_ref.shape, F32)

    def slc_body(gi, carry):
        keys, oks, vts = [], [], []
        for u in range(SLC_GROUP):
            c = list_ref[gi * SLC_GROUP + u]
            cc = jnp.maximum(c, 0)
            k0 = pl.multiple_of(cc * kc, kc)
            first = jnp.where(c >= 0, cc * blocks_per_chunk, n_sel)
            chosen = sel_ref[pl.ds(first, 1), :]
            for j in range(1, blocks_per_chunk):
                chosen = jnp.where(row < j * SLC_BLOCK, chosen, sel_ref[pl.ds(first + j, 1), :])
            oks.append(jnp.where(back + (t0 - k0) >= 0, chosen, 0.0))
            keys.append(ks_ref[pl.ds(k0, kc), :])
            vts.append(vst_ref[cc])
        _flash_step(_nt(jnp.concatenate(keys, axis=0), q), jnp.concatenate(oks, axis=0) > 0.5,
                    jnp.concatenate(vts, axis=1), ms_ref, accs_ref)
        return carry

    lax.fori_loop(0, (cnt + SLC_GROUP - 1) // SLC_GROUP, slc_body, 0)

    o_slc = _flash_result(accs_ref)
    gate = gate_ref[...]
    outs = []
    for h in range(NSA_HPG):
        cols = slice(h * qb, (h + 1) * qb)
        gc = [gate[j * NSA_HPG + h:j * NSA_HPG + h + 1, :] for j in range(3)]
        outs.append(gc[0] * o_cmp[:, cols] + gc[1] * o_slc[:, cols] + gc[2] * o_win[:, cols])
    o_ref[...] = jnp.concatenate(outs, axis=0).T


def _nsa_call(qfeat, qn, kc, cvo, ks, vst, kw, vwt, gates):
    B, S, _ = qn.shape
    G = NSA_GROUPS
    nb = S // Q_BLOCK
    n_cmp = kc.shape[2]
    n_sel = S // SLC_BLOCK
    width = NSA_HPG * Q_BLOCK
    per_group = lambda a: pl.BlockSpec((None, None) + a.shape[2:],
                                       lambda b, g, n: (b, g) + (0,) * (a.ndim - 2))
    return pl.pallas_call(
        _nsa_kernel,
        grid=(B, G, nb),
        in_specs=[
            pl.BlockSpec((None,) + qfeat.shape[1:], lambda b, g, n: (g, 0, 0, 0)),
            pl.BlockSpec((None, Q_BLOCK, NSA_HPG * HEAD_DIM), lambda b, g, n: (b, n, g)),
            per_group(kc), per_group(cvo), per_group(ks), per_group(vst), per_group(kw), per_group(vwt),
            pl.BlockSpec((None, None, GATE_ROWS, Q_BLOCK), lambda b, g, n: (b, g, 0, n)),
        ],
        out_specs=pl.BlockSpec((None, Q_BLOCK, NSA_HPG * HEAD_DIM), lambda b, g, n: (b, n, g)),
        out_shape=jax.ShapeDtypeStruct((B, S, NSA_HEADS * HEAD_DIM), F32),
        scratch_shapes=[pltpu.VMEM((n_sel + SEL_PAD_ROWS, Q_BLOCK), F32),
                        pltpu.VMEM((1, width), F32), pltpu.VMEM((VT_ROWS, width), F32),
                        pltpu.SMEM((n_sel // FLAG_BITS,), jnp.int32),
                        pltpu.SMEM((S // KEY_CHUNK + SLC_GROUP,), jnp.int32)],
        compiler_params=pltpu.CompilerParams(
            dimension_semantics=("parallel", "parallel", "arbitrary"), vmem_limit_bytes=VMEM_LIMIT),
        name="nsa",
    )(qfeat, qn, kc, cvo, ks, vst, kw, vwt, gates)


def _dilated_kernel(q_ref, kvp_ref, kvc_ref, o_ref, lse_ref, *, slopes, dilation, wd):
    c = Q_BLOCK
    first = pl.program_id(2) == 0
    qs = _pad_heads(q_ref[...].astype(F32), DIL_HEADS)
    back = (lax.broadcasted_iota(jnp.int32, (c, c), 0)
            - lax.broadcasted_iota(jnp.int32, (c, c), 1))
    kk_prev = back + c
    ok_prev = kk_prev <= jnp.where(first, 0, wd)
    ok_cur = back >= 0
    dist_prev = (kk_prev * dilation).astype(F32)
    dist_cur = (back * dilation).astype(F32)
    outs, lses = [], []
    for h in range(DIL_HEADS):
        qh = qs[h].astype(BF16)
        kvp = kvp_ref[:, h * LANES:(h + 1) * LANES]
        kvc = kvc_ref[:, h * LANES:(h + 1) * LANES]
        sp = jnp.where(ok_prev, _nt(qh, kvp) - slopes[h] * dist_prev, -jnp.inf)
        sc = jnp.where(ok_cur, _nt(qh, kvc) - slopes[h] * dist_cur, -jnp.inf)
        m = jnp.maximum(jnp.max(sp, axis=1, keepdims=True), jnp.max(sc, axis=1, keepdims=True))
        ep = jnp.exp(sp - m)
        ec = jnp.exp(sc - m)
        l = jnp.sum(ep, axis=1, keepdims=True) + jnp.sum(ec, axis=1, keepdims=True)
        acc = (jnp.dot(ep.astype(BF16), kvp, preferred_element_type=F32)
               + jnp.dot(ec.astype(BF16), kvc, preferred_element_type=F32))
        outs.append(acc / l)
        lses.append(jnp.broadcast_to(m + jnp.log(l), (c, LANES)))
    lane = lax.broadcasted_iota(jnp.int32, (c, LANES), 1)
    for p in range(DIL_HEADS // 2):
        o_ref[:, p * LANES:(p + 1) * LANES] = _compact_pair(outs[2 * p], outs[2 * p + 1])
        lse_ref[:, p * LANES:(p + 1) * LANES] = jnp.where(lane < HEAD_DIM, lses[2 * p], lses[2 * p + 1])


def _dilated_call(qd, kvd, slopes, window, dilation):
    B, S, _ = qd.shape
    L = S // dilation
    nblk = L // Q_BLOCK
    cq, ckv = DIL_HEADS * HEAD_DIM, DIL_HEADS * LANES
    q_v = qd.reshape(B, L, dilation * cq)
    kv_v = kvd.reshape(B, L, dilation * ckv)
    kern = functools.partial(_dilated_kernel, slopes=tuple(float(s) for s in slopes),
                             dilation=dilation, wd=window // dilation)
    blk = lambda cw: pl.BlockSpec((None, Q_BLOCK, cw), lambda b, p, i: (b, i, p))
    o, lse = pl.pallas_call(
        kern,
        grid=(B, dilation, nblk),
        in_specs=[blk(cq),
                  pl.BlockSpec((None, Q_BLOCK, ckv), lambda b, p, i: (b, jnp.maximum(i - 1, 0), p)),
                  blk(ckv)],
        out_specs=(blk(cq), blk(cq)),
        out_shape=(jax.ShapeDtypeStruct((B, L, dilation * cq), F32),
                   jax.ShapeDtypeStruct((B, L, dilation * cq), F32)),
        compiler_params=pltpu.CompilerParams(
            dimension_semantics=("parallel", "parallel", "arbitrary"), vmem_limit_bytes=VMEM_LIMIT),
        name=f"dilated_d{dilation}",
    )(q_v, kv_v, kv_v)
    return o.reshape(B, S, cq), lse.reshape(B, S, cq)


def _merge_kernel(x_ref, g_ref, wg_ref, onsa_ref, o1_ref, o2_ref, o3_ref, l1_ref, l2_ref, l3_ref,
                  wbn_ref, wbd_ref, wo_ref, fg_ref, out_ref, *, final):
    x = x_ref[...]
    xb = _rms(x, g_ref[...]).astype(BF16)
    cz = NSA_HEADS * HEAD_DIM
    d = x.shape[-1]

    def gate_seg(c0, c1):
        return jnp.dot(xb, wg_ref[:, c0:c1], preferred_element_type=F32)

    o_nsa = onsa_ref[...] * jax.nn.silu(gate_seg(0, cz))
    l1, l2, l3 = l1_ref[...], l2_ref[...], l3_ref[...]
    mx = jnp.maximum(jnp.maximum(l1, l2), l3)
    e1, e2, e3 = jnp.exp(l1 - mx), jnp.exp(l2 - mx), jnp.exp(l3 - mx)
    den = e1 + e2 + e3
    o_dil = (e1 / den) * o1_ref[...] + (e2 / den) * o2_ref[...] + (e3 / den) * o3_ref[...]
    o_dil = o_dil * jax.nn.silu(gate_seg(cz, 2 * cz))
    a = jnp.dot(o_nsa.astype(BF16), wbn_ref[...], preferred_element_type=F32)
    bd = jnp.dot(o_dil.astype(BF16), wbd_ref[...], preferred_element_type=F32)
    merged = (jax.nn.sigmoid(gate_seg(2 * cz, 2 * cz + d)) * a
              + jax.nn.sigmoid(gate_seg(2 * cz + d, 2 * cz + 2 * d)) * bd)
    y = jnp.dot(merged.astype(BF16), wo_ref[...], preferred_element_type=F32)
    out_ref[...] = _rms(x + y, fg_ref[...]) if final else x + y


def _merge_call(x, g, wg, o_nsa, o_d, lse_d, wbn, wbd, wo, fg, tm, final):
    B, S, D = x.shape
    cz = o_nsa.shape[-1]
    row = lambda c: pl.BlockSpec((None, tm, c), lambda b, i: (b, i, 0))
    full = lambda a: pl.BlockSpec(a.shape, lambda b, i: (0,) * a.ndim)
    return pl.pallas_call(
        functools.partial(_merge_kernel, final=final),
        grid=(B, S // tm),
        in_specs=[row(D), full(g), full(wg), row(cz), row(cz), row(cz), row(cz), row(cz), row(cz),
                  row(cz), full(wbn), full(wbd), full(wo), full(fg)],
        out_specs=row(D),
        out_shape=jax.ShapeDtypeStruct((B, S, D), F32),
        compiler_params=pltpu.CompilerParams(
            dimension_semantics=("parallel", "parallel"), vmem_limit_bytes=VMEM_LIMIT),
        name="merge",
    )(x, g, wg, o_nsa, *o_d, *lse_d, wbn, wbd, wo, fg)


def _proj_columns(d_model):
    hd, G, H = HEAD_DIM, NSA_GROUPS, NSA_HEADS
    nsa_w = H * hd
    o_q, o_kv = 0, nsa_w
    o_g = o_kv + 6 * G * hd
    o_zn = o_g + 3 * H
    o_qkvd = o_zn + nsa_w
    dil_w = DIL_HEADS * hd
    o_zd = o_qkvd + 3 * dil_w
    o_mg = o_zd + dil_w
    kv = lambda j, g: o_kv + (j * G + g) * hd + np.arange(hd)
    pad = np.full(hd, -1)
    cols = [o_q + np.arange(nsa_w)]
    cols += [kv(j, g) for j in (0, 1) for g in range(G)]
    for j in (2, 4):
        for g in range(G):
            cols += [kv(j, g), pad]
    for j in (3, 5):
        cols += [kv(j, g) for g in range(G)]
    lanes = np.full(LANES, -1)
    for g in range(G):
        for j in range(3):
            lanes[g * GATE_ROWS + j * NSA_HPG + np.arange(NSA_HPG)] = o_g + j * H + g * NSA_HPG + np.arange(NSA_HPG)
    cols.append(lanes)
    cols.append(o_qkvd + np.arange(dil_w))
    for h in range(DIL_HEADS):
        cols += [o_qkvd + j * dil_w + h * hd + np.arange(hd) for j in (1, 2)]
    proj_cols = np.concatenate(cols)
    assert proj_cols.size == _PROJ_OFFS["kvd"][1]
    gate_cols = np.concatenate([o_zn + np.arange(nsa_w), o_zd + np.arange(dil_w),
                                o_mg + np.arange(2 * d_model)])
    return proj_cols, gate_cols


def _compress_weights(pos_k, w1_k, w2_k, pos_v, w1_v, w2_v):
    hd, G, half = HEAD_DIM, NSA_GROUPS, CMP_BLOCK // 2
    eye = jnp.eye(2 * G, dtype=F32)
    w1 = jnp.stack([w1_k] * G + [w1_v] * G)
    pos = jnp.stack([pos_k] * G + [pos_v] * G)

    def first_layer(lo):
        w = jnp.einsum('slde,st->lsdte', w1[:, lo:lo + half], eye)
        p = pos[:, lo:lo + half].transpose(1, 0, 2)
        return w.reshape(half * 2 * G * hd, 2 * G * hd).astype(BF16), p.reshape(1, half * 2 * G * hd)

    wa, posa = first_layer(0)
    wb, posb = first_layer(half)
    w2 = jnp.zeros((2 * G, hd, 3 * G, hd), F32)
    for g in range(G):
        w2 = w2.at[g, :, 2 * g].set(w2_k).at[G + g, :, 2 * G + g].set(w2_v)
    return posa, posb, wa, wb, w2.reshape(2 * G * hd, 3 * G * hd).astype(BF16)


def _overlap_t(n_cmp_pad, n_sel):
    i = np.arange(n_cmp_pad)[None, :]
    j = np.arange(n_sel)[:, None]
    lo = np.maximum(i * CMP_STRIDE, j * SLC_BLOCK)
    hi = np.minimum(i * CMP_STRIDE + CMP_BLOCK, (j + 1) * SLC_BLOCK)
    return (np.clip(hi - lo, 0, None) / CMP_BLOCK).astype(np.float32)


def kernel(x, norm_g, w_in, cmp_pos_k, cmp_w1_k, cmp_w2_k, cmp_pos_v, cmp_w1_v, cmp_w2_v,
           w_br_nsa, w_br_dil, w_out, final_g):
    B, S, D = x.shape
    n_sel = S // SLC_BLOCK
    assert S % (Q_BLOCK * max(d for _, d in DIL_PATTERNS)) == 0 and n_sel % FLAG_BITS == 0
    n_cmp_pad = S // CMP_STRIDE
    slopes_nsa, slopes_dil = _alibi_slopes()
    proj_cols, gate_cols = _proj_columns(D)
    ovt = jnp.asarray(_overlap_t(n_cmp_pad, n_sel))
    qfeat = jnp.asarray(np.broadcast_to(
        _slope_features(slopes_nsa).reshape(NSA_GROUPS, NSA_HPG, 1, LANES),
        (NSA_GROUPS, NSA_HPG, 8, LANES)))
    tm = 512

    h = x
    for layer in range(w_in.shape[0]):
        w_l = jnp.concatenate([w_in[layer], jnp.zeros((D, 1), w_in.dtype)], axis=1)
        w_proj = jnp.take(w_l, jnp.asarray(np.where(proj_cols < 0, w_l.shape[1] - 1, proj_cols)),
                          axis=1).astype(BF16)
        w_gate = jnp.take(w_l, jnp.asarray(gate_cols), axis=1).astype(BF16)
        g_l = norm_g[layer].reshape(1, D)

        qn, cmp_src, ks, kw, vst, vwt, gates, qd, kvd = _proj_call(h, g_l, w_proj, tm)
        cw = _compress_weights(cmp_pos_k[layer], cmp_w1_k[layer], cmp_w2_k[layer],
                               cmp_pos_v[layer], cmp_w1_v[layer], cmp_w2_v[layer])
        kc, cvo = _compress_call(cmp_src.reshape(B, n_cmp_pad, CMP_STRIDE * cmp_src.shape[-1]), *cw, ovt)
        o_nsa = _nsa_call(qfeat, qn, kc, cvo, ks, vst, kw, vwt, gates)
        o_d, lse_d = [], []
        for window, dilation in DIL_PATTERNS:
            o, lse = _dilated_call(qd, kvd, slopes_dil, window, dilation)
            o_d.append(o)
            lse_d.append(lse)
        h = _merge_call(h, g_l, w_gate, o_nsa, o_d, lse_d, w_br_nsa[layer].astype(BF16),
                        w_br_dil[layer].astype(BF16), w_out[layer].astype(BF16),
                        final_g.reshape(1, D), tm, final=layer == w_in.shape[0] - 1)
    return h
```

```python
import functools
import math

import numpy as np
import jax
import jax.numpy as jnp
from jax import lax
from jax.experimental import pallas as pl
from jax.experimental.pallas import tpu as pltpu

F32 = jnp.float32
BF16 = jnp.bfloat16

HEAD_DIM = 64
LANES = 128
NSA_HEADS = 8
NSA_GROUPS = 2
NSA_HPG = NSA_HEADS // NSA_GROUPS
CMP_BLOCK = 32
CMP_STRIDE = 16
SLC_BLOCK = 64
SLC_TOPK = 16
WIN = 512
FORCE_SCORE = 1.0e4
DIL_HEADS = 8
DIL_PATTERNS = ((128, 1), (512, 4), (2048, 16))
Q_BLOCK = 128
RMS_EPS = 1e-6
KEY_CHUNK = 128
VT_ROWS = 80
GATE_ROWS = 16
FEAT_SPLIT = 3
FLAG_BITS = 16
SLC_GROUP = 4
SEL_PAD_ROWS = 8
NEG_INIT = -1.0e30
LOG2E = math.log2(math.e)
VMEM_LIMIT = 56 * 1024 * 1024


def _alibi_slopes():
    n = NSA_HEADS + DIL_HEADS
    s = 2.0 ** (-8.0 * np.arange(1, n + 1) / n)
    return s[0::2].astype(np.float32), s[1::2].astype(np.float32)


def _nt(a, b):
    return lax.dot_general(a, b, (((1,), (1,)), ((), ())), preferred_element_type=F32)


def _rms(x, g):
    return (x * lax.rsqrt(jnp.mean(x * x, axis=-1, keepdims=True) + RMS_EPS)) * g


def _head_tile(q_pairs, h):
    tile = q_pairs[:, (h // 2) * LANES:(h // 2 + 1) * LANES]
    return pltpu.roll(tile, HEAD_DIM, 1) if h % 2 else tile


def _pad_heads(q_pairs, n_heads):
    lane = lax.broadcasted_iota(jnp.int32, (q_pairs.shape[0], LANES), 1)
    return [jnp.where(lane < HEAD_DIM, _head_tile(q_pairs, h), 0.0) for h in range(n_heads)]


def _compact_pair(a_pad, b_pad):
    lane = lax.broadcasted_iota(jnp.int32, a_pad.shape, 1)
    return jnp.where(lane < HEAD_DIM, pltpu.roll(a_pad, HEAD_DIM, 1), b_pad)


def _pos_features(pos):
    lane = lax.broadcasted_iota(jnp.int32, pos.shape, 1) - HEAD_DIM
    hi = (pos // LANES).astype(F32)
    lo = (pos % LANES).astype(F32)
    return jnp.where(lane < 0, 0.0, jnp.where(lane < FEAT_SPLIT, hi, jnp.where(lane < 2 * FEAT_SPLIT, lo, 0.0)))


def _slope_features(slopes):
    out = np.zeros((len(slopes), LANES), np.float32)
    for h, s in enumerate(slopes):
        rest = np.float32(s) * np.float32(LOG2E)
        for i in range(FEAT_SPLIT):
            piece = np.float32(np.asarray(rest, np.float32).astype(BF16))
            out[h, HEAD_DIM + FEAT_SPLIT + i] = piece
            out[h, HEAD_DIM + i] = piece * LANES
            rest = np.float32(rest - piece)
    return out


_PROJ_SEGS = (("qn", 512), ("cmp", 256), ("ks", 256), ("kw", 256), ("vs", 128), ("vw", 128),
              ("gate", 128), ("qd", 512), ("kd", 1024), ("vd", 1024))
DIL_Q_W, DIL_KV_W = DIL_HEADS * HEAD_DIM, DIL_HEADS * LANES
_PROJ_OFFS = dict(zip((n for n, _ in _PROJ_SEGS),
                      zip(np.cumsum([0] + [w for _, w in _PROJ_SEGS])[:-1].tolist(),
                          np.cumsum([w for _, w in _PROJ_SEGS]).tolist())))


def _to_lane_tiles(scr, first, val):
    for j in range(val.shape[1] // LANES):
        scr[first + j] = val[:, j * LANES:(j + 1) * LANES]


def _regroup_rows(scr, lo, hi, out_ref, d, dtype):
    rows, width = scr.shape[1], (hi - lo) * LANES
    for p in range(d):
        for j in range(hi - lo):
            tile = scr[lo + j, pl.ds(p, rows // d, stride=d), :] if d > 1 else scr[lo + j]
            out_ref[:, p * width + j * LANES:p * width + (j + 1) * LANES] = tile.astype(dtype)


def _proj_kernel(x_ref, g_ref, w_ref, qn_ref, cmp_ref, ks_ref, kw_ref, vst_ref, vwt_ref, gate_ref,
                 *dil_refs):
    cmp_scr, dil_scr = dil_refs[-2:]
    tm = x_ref.shape[0]
    xb = _rms(x_ref[...], g_ref[...]).astype(BF16)
    scale = HEAD_DIM ** -0.5

    def seg(name):
        c0, c1 = _PROJ_OFFS[name]
        return jnp.dot(xb, w_ref[:, c0:c1], preferred_element_type=F32)

    qn_ref[...] = (seg("qn") * (scale * LOG2E)).astype(BF16)
    _to_lane_tiles(cmp_scr, 0, seg("cmp"))
    _regroup_rows(cmp_scr, 0, cmp_scr.shape[0], cmp_ref, CMP_STRIDE, F32)
    feat = _pos_features(pl.program_id(1) * tm + lax.broadcasted_iota(jnp.int32, (tm, LANES), 0))
    ks, kw = seg("ks"), seg("kw")
    vst, vwt = seg("vs").T, seg("vw").T
    gate_t = jax.nn.sigmoid(seg("gate")).T
    tail_row = lax.broadcasted_iota(jnp.int32, (VT_ROWS - HEAD_DIM, tm), 0)
    tail = jnp.where(tail_row == 0, 1.0, 0.0)
    for g in range(NSA_GROUPS):
        ks_ref[g] = (ks[:, g * LANES:(g + 1) * LANES] + feat).astype(BF16)
        kw_ref[g] = (kw[:, g * LANES:(g + 1) * LANES] + feat).astype(BF16)
        gate_ref[g] = gate_t[g * GATE_ROWS:(g + 1) * GATE_ROWS]
        for src, dst in ((vst, vst_ref), (vwt, vwt_ref)):
            vt = jnp.concatenate([src[g * HEAD_DIM:(g + 1) * HEAD_DIM], tail], axis=0).astype(BF16)
            for j in range(tm // KEY_CHUNK):
                dst[g, j] = vt[:, j * KEY_CHUNK:(j + 1) * KEY_CHUNK]
    head_lane = lax.broadcasted_iota(jnp.int32, (tm, LANES), 1)
    one_col = jnp.where(head_lane == HEAD_DIM, 1.0, 0.0)
    tiles = (0, DIL_Q_W // LANES, (DIL_Q_W + DIL_KV_W) // LANES, (DIL_Q_W + 2 * DIL_KV_W) // LANES)
    _to_lane_tiles(dil_scr, tiles[0], seg("qd") * (scale * LOG2E))
    _to_lane_tiles(dil_scr, tiles[1], seg("kd") + jnp.concatenate([feat] * DIL_HEADS, axis=1))
    _to_lane_tiles(dil_scr, tiles[2], seg("vd") + jnp.concatenate([one_col] * DIL_HEADS, axis=1))
    for i, (_, d) in enumerate(DIL_PATTERNS):
        for j, ref in enumerate(dil_refs[3 * i:3 * i + 3]):
            _regroup_rows(dil_scr, tiles[j], tiles[j + 1], ref, d, BF16)


def _proj_call(x, g, w, tm):
    B, S, D = x.shape
    G = NSA_GROUPS
    cmp_w = _PROJ_OFFS["cmp"][1] - _PROJ_OFFS["cmp"][0]
    row = lambda c: pl.BlockSpec((None, tm, c), lambda b, i: (b, i, 0))
    k_spec =pl.BlockSpec((None, G, tm, LANES), lambda b, i: (b, 0, i, 0))
    vt_spec = pl.BlockSpec((None, G, tm // KEY_CHUNK, VT_ROWS, KEY_CHUNK), lambda b, i: (b, 0, i, 0, 0))
    out_shape = (
        jax.ShapeDtypeStruct((B, S, 512), BF16),
        jax.ShapeDtypeStruct((B, S // CMP_STRIDE, CMP_STRIDE * cmp_w), F32),
        jax.ShapeDtypeStruct((B, G, S, LANES), BF16),
        jax.ShapeDtypeStruct((B, G, S, LANES), BF16),
        jax.ShapeDtypeStruct((B, G, S // KEY_CHUNK, VT_ROWS, KEY_CHUNK), BF16),
        jax.ShapeDtypeStruct((B, G, S // KEY_CHUNK, VT_ROWS, KEY_CHUNK), BF16),
        jax.ShapeDtypeStruct((B, G, GATE_ROWS, S), F32),
    )
    dil_specs = ()
    for _, d in DIL_PATTERNS:
        for width in (DIL_Q_W, DIL_KV_W, DIL_KV_W):
            out_shape += (jax.ShapeDtypeStruct((B, S // d, d * width), BF16),)
            dil_specs += (pl.BlockSpec((None, tm // d, d * width), lambda b, i: (b, i, 0)),)
    return pl.pallas_call(
        _proj_kernel,
        grid=(B, S // tm),
        in_specs=[row(D), pl.BlockSpec((1, D), lambda b, i: (0, 0)),
                  pl.BlockSpec(w.shape, lambda b, i: (0, 0))],
        out_specs=(row(512),
                   pl.BlockSpec((None, tm // CMP_STRIDE, CMP_STRIDE * cmp_w), lambda b, i: (b, i, 0)),
                   k_spec, k_spec, vt_spec, vt_spec,
                   pl.BlockSpec((None, G, GATE_ROWS, tm), lambda b, i: (b, 0, 0, i))) + dil_specs,
        out_shape=out_shape,
        scratch_shapes=[pltpu.VMEM((cmp_w // LANES, tm, LANES), F32),
                        pltpu.VMEM(((DIL_Q_W + 2 * DIL_KV_W) // LANES, tm, LANES), F32)],
        compiler_params=pltpu.CompilerParams(
            dimension_semantics=("parallel", "parallel"), vmem_limit_bytes=VMEM_LIMIT),
        name="proj",
    )(x, g, w)


def _compress_kernel(r_ref, posa_ref, posb_ref, wa_ref, wb_ref, w2_ref, ovt_ref, kc_ref, cvo_ref):
    r = r_ref[...]
    n = r.shape[0]
    ha = jnp.dot((r + posa_ref[...]).astype(BF16), wa_ref[...], preferred_element_type=F32)
    hb = jnp.dot((r + posb_ref[...]).astype(BF16), wb_ref[...], preferred_element_type=F32)
    hid = jax.nn.gelu(ha + pltpu.roll(hb, n - 1, 0))
    out = jnp.dot(hid.astype(BF16), w2_ref[...], preferred_element_type=F32)
    cmp_end = lax.broadcasted_iota(jnp.int32, (n, LANES), 0) * CMP_STRIDE + (CMP_BLOCK - 1)
    feat = _pos_features(cmp_end)
    v_t = out[:, NSA_GROUPS * LANES:].T
    for g in range(NSA_GROUPS):
        kc_ref[g] = (out[:, g * LANES:(g + 1) * LANES] + feat).astype(BF16)
        cvo_ref[g] = jnp.concatenate([v_t[g * HEAD_DIM:(g + 1) * HEAD_DIM], ovt_ref[...]],
                                     axis=0).astype(BF16)


def _compress_call(r, posa, posb, wa, wb, w2, ovt):
    B, R, C = r.shape
    G = NSA_GROUPS
    rows = HEAD_DIM + ovt.shape[0]
    full = lambda a: pl.BlockSpec(a.shape, lambda b: (0,) * a.ndim)
    return pl.pallas_call(
        _compress_kernel,
        grid=(B,),
        in_specs=[pl.BlockSpec((None, R, C), lambda b: (b, 0, 0)),
                  full(posa), full(posb), full(wa), full(wb), full(w2), full(ovt)],
        out_specs=(pl.BlockSpec((None, G, R, LANES), lambda b: (b, 0, 0, 0)),
                   pl.BlockSpec((None, G, rows, R), lambda b: (b, 0, 0, 0))),
        out_shape=(jax.ShapeDtypeStruct((B, G, R, LANES), BF16),
                   jax.ShapeDtypeStruct((B, G, rows, R), BF16)),
        compiler_params=pltpu.CompilerParams(
            dimension_semantics=("parallel",), vmem_limit_bytes=VMEM_LIMIT),
        name="compress",
    )(r, posa, posb, wa, wb, w2, ovt)


def _flash_step(s, ok, v_t, m_ref, acc_ref):
    ps, alphas = [], []
    for h in range(NSA_HPG):
        cols = slice(h * LANES, (h + 1) * LANES)
        sh = jnp.where(ok, s[:, cols], -jnp.inf)
        m_old = m_ref[:, cols]
        m_new = jnp.maximum(m_old, jnp.max(sh, axis=0, keepdims=True))
        alphas.append(jnp.exp2(m_old - m_new))
        ps.append(jnp.exp2(sh - m_new).astype(BF16))
        m_ref[:, cols] = m_new
    pv = jnp.dot(v_t, jnp.concatenate(ps, axis=1), preferred_element_type=F32)
    acc_ref[...] = jnp.concatenate(alphas, axis=1) * acc_ref[...] + pv


def _flash_result(acc_ref):
    l = acc_ref[HEAD_DIM:HEAD_DIM + 1, :]
    return acc_ref[0:HEAD_DIM, :] * (1.0 / jnp.maximum(l, 1e-30))


def _softmax_pv(s, ok, v_t):
    ps = []
    for h in range(NSA_HPG):
        sh = jnp.where(ok, s[:, h * LANES:(h + 1) * LANES], -jnp.inf)
        ps.append(jnp.exp2(sh - jnp.max(sh, axis=0, keepdims=True)).astype(BF16))
    r = jnp.dot(v_t, jnp.concatenate(ps, axis=1), preferred_element_type=F32)
    return r[0:HEAD_DIM] * (1.0 / jnp.maximum(r[HEAD_DIM:HEAD_DIM + 1], 1e-30))


def _nsa_kernel(qfeat_ref, q_ref, kc_ref, cvo_ref, ks_ref, vst_ref, kw_ref, vwt_ref, gate_ref,
                o_ref, sel_ref, ms_ref, accs_ref, flag_ref, list_ref):
    n = pl.program_id(2)
    qb, kc = Q_BLOCK, KEY_CHUNK
    t0 = n * qb
    lane = lax.broadcasted_iota(jnp.int32, (qb, LANES), 1)
    qf = q_ref[...].astype(F32)
    q = jnp.concatenate(
        [jnp.where(lane < HEAD_DIM, _head_tile(qf, h), qfeat_ref[h][0:1, :]) for h in range(NSA_HPG)],
        axis=0).astype(BF16)

    span = WIN + qb
    k_lo = pl.multiple_of(jnp.maximum(t0 - WIN, 0), kc)
    dist_w = ((t0 - k_lo) + lax.broadcasted_iota(jnp.int32, (span, qb), 1)
              - lax.broadcasted_iota(jnp.int32, (span, qb), 0))
    vt_w = jnp.concatenate([vwt_ref[k_lo // kc + u] for u in range(span // kc)], axis=1)
    o_win = _softmax_pv(_nt(kw_ref[pl.ds(k_lo, span), :], q), (dist_w >= 0) & (dist_w < WIN), vt_w)

    n_cmp = kc_ref.shape[0]
    n_sel = sel_ref.shape[0] - SEL_PAD_ROWS
    s = _nt(kc_ref[...], q)
    cmp_end = lax.broadcasted_iota(jnp.int32, (n_cmp, qb), 0) * CMP_STRIDE + (CMP_BLOCK - 1)
    ok_c = cmp_end <= t0 + lax.broadcasted_iota(jnp.int32, (n_cmp, qb), 1)
    ps = []
    for h in range(NSA_HPG):
        sh = jnp.where(ok_c, s[:, h * qb:(h + 1) * qb], -jnp.inf)
        m = jnp.max(sh, axis=0, keepdims=True)
        m = jnp.where(m == -jnp.inf, 0.0, m)
        e = jnp.exp2(sh - m)
        l = jnp.sum(e, axis=0, keepdims=True)
        ps.append((e * (1.0 / jnp.maximum(l, 1e-30))).astype(BF16))
    r = jnp.dot(cvo_ref[...], jnp.concatenate(ps, axis=1), preferred_element_type=F32)
    o_cmp = r[0:HEAD_DIM]
    imp_t = r[HEAD_DIM:, 0:qb]
    for h in range(1, NSA_HPG):
        imp_t = imp_t + r[HEAD_DIM:, h * qb:(h + 1) * qb]

    blk = lax.broadcasted_iota(jnp.int32, (n_sel, qb), 0).astype(F32)
    cur = ((t0 + lax.broadcasted_iota(jnp.int32, (n_sel, qb), 1)) // SLC_BLOCK).astype(F32)
    forced = (blk == 0) | (blk == cur) | (blk == cur - 1)
    valid = blk <= cur
    score = jnp.where(valid, jnp.where(forced, FORCE_SCORE, imp_t), -1.0)
    picked = jnp.zeros((n_sel, qb), F32)
    for _ in range(min(SLC_TOPK, n_sel)):
        mx = jnp.max(score, axis=0, keepdims=True)
        idx = jnp.min(jnp.where(score == mx, blk, float(n_sel)), axis=0, keepdims=True)
        hit = blk == idx
        picked = jnp.where(hit, 1.0, picked)
        score = jnp.where(hit, -jnp.inf, score)
    sel_t = jnp.where(valid, picked, 0.0)
    sel_ref[0:n_sel, :] = sel_t
    sel_ref[n_sel:, :] = jnp.zeros((SEL_PAD_ROWS, qb), F32)
    any_q = jnp.max(sel_t, axis=1, keepdims=True)
    bit = jnp.left_shift(1, lax.broadcasted_iota(jnp.int32, (n_sel, 1), 0) % FLAG_BITS).astype(F32)
    packed = any_q * bit
    for i in range(n_sel // FLAG_BITS):
        word = jnp.sum(packed[i * FLAG_BITS:(i + 1) * FLAG_BITS], axis=0, keepdims=True)
        flag_ref[i] = word.astype(jnp.int32)[0, 0]

    blocks_per_chunk = kc // SLC_BLOCK
    chunks_per_word = FLAG_BITS // blocks_per_chunk

    def scan_body(w, cnt):
        word = flag_ref[w]
        for j in range(chunks_per_word):
            bits = (word >> (j * blocks_per_chunk)) & ((1 << blocks_per_chunk) - 1)
            list_ref[cnt] = w * chunks_per_word + j
            cnt = cnt + (bits != 0).astype(jnp.int32)
        return cnt

    cnt = lax.fori_loop(0, (t0 + qb - 1) // (FLAG_BITS * SLC_BLOCK) + 1, scan_body, 0)
    for u in range(SLC_GROUP):
        list_ref[cnt + u] = -1

    row = lax.broadcasted_iota(jnp.int32, (kc, qb), 0)
    back = lax.broadcasted_iota(jnp.int32, (kc, qb), 1) - row
    ms_ref[...] = jnp.full(ms_ref.shape, NEG_INIT, F32)
    accs_ref[...] = jnp.zeros(accs_ref.shape, F32)

    def slc_body(gi, carry):
        keys, oks, vts = [], [], []
        for u in range(SLC_GROUP):
            c = list_ref[gi * SLC_GROUP + u]
            cc = jnp.maximum(c, 0)
            k0 = pl.multiple_of(cc * kc, kc)
            first = jnp.where(c >= 0, cc * blocks_per_chunk, n_sel)
            chosen = sel_ref[pl.ds(first, 1), :]
            for j in range(1, blocks_per_chunk):
                chosen = jnp.where(row < j * SLC_BLOCK, chosen, sel_ref[pl.ds(first + j, 1), :])
            oks.append(jnp.where(back + (t0 - k0) >= 0, chosen, 0.0))
            keys.append(ks_ref[pl.ds(k0, kc), :])
            vts.append(vst_ref[cc])
        _flash_step(_nt(jnp.concatenate(keys, axis=0), q), jnp.concatenate(oks, axis=0) > 0.5,
                    jnp.concatenate(vts, axis=1), ms_ref, accs_ref)
        return carry

    lax.fori_loop(0, (cnt + SLC_GROUP - 1) // SLC_GROUP, slc_body, 0)

    o_slc = _flash_result(accs_ref)
    gate = gate_ref[...]
    outs = []
    for h in range(NSA_HPG):
        cols = slice(h * qb, (h + 1) * qb)
        gc = [gate[j * NSA_HPG + h:j * NSA_HPG + h + 1, :] for j in range(3)]
        outs.append(gc[0] * o_cmp[:, cols] + gc[1] * o_slc[:, cols] + gc[2] * o_win[:, cols])
    o_ref[...] = jnp.concatenate(outs, axis=0).T


def _nsa_call(qfeat, qn, kc, cvo, ks, vst, kw, vwt, gates):
    B, S, _ = qn.shape
    G = NSA_GROUPS
    nb = S // Q_BLOCK
    n_cmp = kc.shape[2]
    n_sel = S // SLC_BLOCK
    width = NSA_HPG * Q_BLOCK
    per_group = lambda a: pl.BlockSpec((None, None) + a.shape[2:],
                                       lambda b, g, n: (b, g) + (0,) * (a.ndim - 2))
    return pl.pallas_call(
        _nsa_kernel,
        grid=(B, G, nb),
        in_specs=[
            pl.BlockSpec((None,) + qfeat.shape[1:], lambda b, g, n: (g, 0, 0, 0)),
            pl.BlockSpec((None, Q_BLOCK, NSA_HPG * HEAD_DIM), lambda b, g, n: (b, n, g)),
            per_group(kc), per_group(cvo), per_group(ks), per_group(vst), per_group(kw), per_group(vwt),
            pl.BlockSpec((None, None, GATE_ROWS, Q_BLOCK), lambda b, g, n: (b, g, 0, n)),
        ],
        out_specs=pl.BlockSpec((None, Q_BLOCK, NSA_HPG * HEAD_DIM), lambda b, g, n: (b, n, g)),
        out_shape=jax.ShapeDtypeStruct((B, S, NSA_HEADS * HEAD_DIM), F32),
        scratch_shapes=[pltpu.VMEM((n_sel + SEL_PAD_ROWS, Q_BLOCK), F32),
                        pltpu.VMEM((1, width), F32), pltpu.VMEM((VT_ROWS, width), F32),
                        pltpu.SMEM((n_sel // FLAG_BITS,), jnp.int32),
                        pltpu.SMEM((S // KEY_CHUNK + SLC_GROUP,), jnp.int32)],
        compiler_params=pltpu.CompilerParams(
            dimension_semantics=("parallel", "parallel", "arbitrary"), vmem_limit_bytes=VMEM_LIMIT),
        name="nsa",
    )(qfeat, qn, kc, cvo, ks, vst, kw, vwt, gates)


def _dilated_kernel(qfeat_ref, q_ref, kp_ref, kc_ref, vp_ref, vc_ref, o_ref, lse_ref, *, wd):
    c = Q_BLOCK
    qf = q_ref[...].astype(F32)
    lane = lax.broadcasted_iota(jnp.int32, (c, LANES), 1)
    key = lax.broadcasted_iota(jnp.int32, (2 * c, c), 0)
    qry = lax.broadcasted_iota(jnp.int32, (2 * c, c), 1)
    first = pl.program_id(2) == 0
    ok = jnp.where(key < c, jnp.where(first, 2 * c, qry) - (key + (wd - c)), (key - c) - qry) <= 0
    outs, lses = [], []
    for h in range(DIL_HEADS):
        hs = slice(h * LANES, (h + 1) * LANES)
        qh = jnp.where(lane < HEAD_DIM, _head_tile(qf, h), qfeat_ref[h][0:1, :]).astype(BF16)
        keys = jnp.concatenate([kp_ref[:, hs], kc_ref[:, hs]], axis=0)
        vals = jnp.concatenate([vp_ref[:, hs], vc_ref[:, hs]], axis=0)
        v_t = vals.astype(F32).T[0:VT_ROWS].astype(BF16)
        s = jnp.where(ok, _nt(keys, qh), -jnp.inf)
        m = jnp.max(s, axis=0, keepdims=True)
        r = jnp.dot(v_t, jnp.exp2(s - m).astype(BF16), preferred_element_type=F32)
        l = r[HEAD_DIM:HEAD_DIM + 1]
        outs.append(r[0:HEAD_DIM] * (1.0 / l))
        lses.append(jnp.broadcast_to(m + jnp.log2(l), (HEAD_DIM, c)))
    for p in range(DIL_HEADS // 2):
        o_ref[:, p * LANES:(p + 1) * LANES] = jnp.concatenate(outs[2 * p:2 * p + 2], axis=0).T
        lse_ref[:, p * LANES:(p + 1) * LANES] = jnp.concatenate(lses[2 * p:2 * p + 2], axis=0).T


def _dilated_call(qfeat, q_v, k_v, v_v, window, dilation):
    B, L, _ = q_v.shape
    nblk = L // Q_BLOCK
    blk = lambda cw: pl.BlockSpec((None, Q_BLOCK, cw), lambda b, p, i: (b, i, p))
    prev = pl.BlockSpec((None, Q_BLOCK, DIL_KV_W), lambda b, p, i: (b, jnp.maximum(i - 1, 0), p))
    return pl.pallas_call(
        functools.partial(_dilated_kernel, wd=window // dilation),
        grid=(B, dilation, nblk),
        in_specs=[pl.BlockSpec(qfeat.shape, lambda b, p, i: (0, 0, 0)),
                  blk(DIL_Q_W), prev, blk(DIL_KV_W), prev, blk(DIL_KV_W)],
        out_specs=(blk(DIL_Q_W), blk(DIL_Q_W)),
        out_shape=(jax.ShapeDtypeStruct((B, L, dilation * DIL_Q_W), F32),
                   jax.ShapeDtypeStruct((B, L, dilation * DIL_Q_W), F32)),
        compiler_params=pltpu.CompilerParams(
            dimension_semantics=("parallel", "parallel", "arbitrary"), vmem_limit_bytes=VMEM_LIMIT),
        name=f"dilated_d{dilation}",
    )(qfeat, q_v, k_v, k_v, v_v, v_v)


def _merge_kernel(x_ref, g_ref, wg_ref, onsa_ref, o1_ref, o2_ref, o3_ref, l1_ref, l2_ref, l3_ref,
                  wbn_ref, wbd_ref, wo_ref, fg_ref, out_ref, tok_scr, *, final):
    x = x_ref[...]
    tm = x.shape[0]
    xb = _rms(x, g_ref[...]).astype(BF16)
    cz = NSA_HEADS * HEAD_DIM
    d = x.shape[-1]

    def gate_seg(c0, c1):
        return jnp.dot(xb, wg_ref[:, c0:c1], preferred_element_type=F32)

    def token_major(ref, slot, dil):
        if dil == 1:
            return ref[...]
        tiles = DIL_Q_W // LANES
        for p in range(dil):
            for j in range(tiles):
                tok_scr[slot * tiles + j, pl.ds(p, tm // dil, stride=dil), :] = \
                    ref[:, p * DIL_Q_W + j * LANES:p * DIL_Q_W + (j + 1) * LANES]
        return jnp.concatenate([tok_scr[slot * tiles + j] for j in range(tiles)], axis=1)

    o_nsa = onsa_ref[...] * jax.nn.silu(gate_seg(0, cz))
    dils = [dil for _, dil in DIL_PATTERNS]
    o1, o2, o3 = (token_major(r, i, dil) for i, (r, dil) in enumerate(zip((o1_ref, o2_ref, o3_ref), dils)))
    l1, l2, l3 = (token_major(r, 3 + i, dil) for i, (r, dil) in enumerate(zip((l1_ref, l2_ref, l3_ref), dils)))
    mx = jnp.maximum(jnp.maximum(l1, l2), l3)
    e1, e2, e3 = jnp.exp2(l1 - mx), jnp.exp2(l2 - mx), jnp.exp2(l3 - mx)
    den = e1 + e2 + e3
    o_dil = (e1 / den) * o1 + (e2 / den) * o2 + (e3 / den) * o3
    o_dil = o_dil * jax.nn.silu(gate_seg(cz, 2 * cz))
    a = jnp.dot(o_nsa.astype(BF16), wbn_ref[...], preferred_element_type=F32)
    bd = jnp.dot(o_dil.astype(BF16), wbd_ref[...], preferred_element_type=F32)
    merged = (jax.nn.sigmoid(gate_seg(2 * cz, 2 * cz + d)) * a
              + jax.nn.sigmoid(gate_seg(2 * cz + d, 2 * cz + 2 * d)) * bd)
    y = jnp.dot(merged.astype(BF16), wo_ref[...], preferred_element_type=F32)
    out_ref[...] = _rms(x + y, fg_ref[...]) if final else x + y


def _merge_call(x, g, wg, o_nsa, o_d, lse_d, wbn, wbd, wo, fg, tm, final):
    B, S, D = x.shape
    cz = o_nsa.shape[-1]
    row = lambda c: pl.BlockSpec((None, tm, c), lambda b, i: (b, i, 0))
    full = lambda a: pl.BlockSpec(a.shape, lambda b, i: (0,) * a.ndim)
    phase = [pl.BlockSpec((None, tm // dil, dil * cz), lambda b, i: (b, i, 0)) for _, dil in DIL_PATTERNS]
    return pl.pallas_call(
        functools.partial(_merge_kernel, final=final),
        grid=(B, S // tm),
        in_specs=[row(D), full(g), full(wg), row(cz)] + phase + phase
                 + [full(wbn), full(wbd), full(wo), full(fg)],
        out_specs=row(D),
        out_shape=jax.ShapeDtypeStruct((B, S, D), F32),
        scratch_shapes=[pltpu.VMEM((2 * len(DIL_PATTERNS) * cz // LANES, tm, LANES), F32)],
        compiler_params=pltpu.CompilerParams(
            dimension_semantics=("parallel", "parallel"), vmem_limit_bytes=VMEM_LIMIT),
        name="merge",
    )(x, g, wg, o_nsa, *o_d, *lse_d, wbn, wbd, wo, fg)


def _proj_columns(d_model):
    hd, G, H = HEAD_DIM, NSA_GROUPS, NSA_HEADS
    nsa_w = H * hd
    o_q, o_kv = 0, nsa_w
    o_g = o_kv + 6 * G * hd
    o_zn = o_g + 3 * H
    o_qkvd = o_zn + nsa_w
    dil_w = DIL_HEADS * hd
    o_zd = o_qkvd + 3 * dil_w
    o_mg = o_zd + dil_w
    kv = lambda j, g: o_kv + (j * G + g) * hd + np.arange(hd)
    pad = np.full(hd, -1)
    cols = [o_q + np.arange(nsa_w)]
    cols += [kv(j, g) for j in (0, 1) for g in range(G)]
    for j in (2, 4):
        for g in range(G):
            cols += [kv(j, g), pad]
    for j in (3, 5):
        cols += [kv(j, g) for g in range(G)]
    lanes = np.full(LANES, -1)
    for g in range(G):
        for j in range(3):
            lanes[g * GATE_ROWS + j * NSA_HPG + np.arange(NSA_HPG)] = o_g + j * H + g * NSA_HPG + np.arange(NSA_HPG)
    cols.append(lanes)
    cols.append(o_qkvd + np.arange(dil_w))
    for j in (1, 2):
        for h in range(DIL_HEADS):
            cols += [o_qkvd + j * dil_w + h * hd + np.arange(hd), pad]
    proj_cols = np.concatenate(cols)
    assert proj_cols.size == _PROJ_OFFS["vd"][1]
    gate_cols = np.concatenate([o_zn + np.arange(nsa_w), o_zd + np.arange(dil_w),
                                o_mg + np.arange(2 * d_model)])
    return proj_cols, gate_cols


def _compress_weights(pos_k, w1_k, w2_k, pos_v, w1_v, w2_v):
    hd, G, half = HEAD_DIM, NSA_GROUPS, CMP_BLOCK // 2
    eye = jnp.eye(2 * G, dtype=F32)
    w1 = jnp.stack([w1_k] * G + [w1_v] * G)
    pos = jnp.stack([pos_k] * G + [pos_v] * G)

    def first_layer(lo):
        w = jnp.einsum('slde,st->lsdte', w1[:, lo:lo + half], eye)
        p = pos[:, lo:lo + half].transpose(1, 0, 2)
        return w.reshape(half * 2 * G * hd, 2 * G * hd).astype(BF16), p.reshape(1, half * 2 * G * hd)

    wa, posa = first_layer(0)
    wb, posb = first_layer(half)
    w2 = jnp.zeros((2 * G, hd, 3 * G, hd), F32)
    for g in range(G):
        w2 = w2.at[g, :, 2 * g].set(w2_k).at[G + g, :, 2 * G + g].set(w2_v)
    return posa, posb, wa, wb, w2.reshape(2 * G * hd, 3 * G * hd).astype(BF16)


def _overlap_t(n_cmp_pad, n_sel):
    i = np.arange(n_cmp_pad)[None, :]
    j = np.arange(n_sel)[:, None]
    lo = np.maximum(i * CMP_STRIDE, j * SLC_BLOCK)
    hi = np.minimum(i * CMP_STRIDE + CMP_BLOCK, (j + 1) * SLC_BLOCK)
    return (np.clip(hi - lo, 0, None) / CMP_BLOCK).astype(np.float32)


def kernel(x, norm_g, w_in, cmp_pos_k, cmp_w1_k, cmp_w2_k, cmp_pos_v, cmp_w1_v, cmp_w2_v,
           w_br_nsa, w_br_dil, w_out, final_g):
    B, S, D = x.shape
    n_sel = S // SLC_BLOCK
    assert S % (Q_BLOCK * max(d for _, d in DIL_PATTERNS)) == 0 and n_sel % FLAG_BITS == 0
    n_cmp_pad = S // CMP_STRIDE
    slopes_nsa, slopes_dil = _alibi_slopes()
    proj_cols, gate_cols = _proj_columns(D)
    ovt = jnp.asarray(_overlap_t(n_cmp_pad, n_sel))
    qfeat = jnp.asarray(np.broadcast_to(
        _slope_features(slopes_nsa).reshape(NSA_GROUPS, NSA_HPG, 1, LANES),
        (NSA_GROUPS, NSA_HPG, 8, LANES)))
    qfeat_dil = jnp.asarray(np.broadcast_to(
        _slope_features(slopes_dil).reshape(DIL_HEADS, 1, LANES), (DIL_HEADS, 8, LANES)))
    tm = 512

    h = x
    for layer in range(w_in.shape[0]):
        w_l = jnp.concatenate([w_in[layer], jnp.zeros((D, 1), w_in.dtype)], axis=1)
        w_proj = jnp.take(w_l, jnp.asarray(np.where(proj_cols < 0, w_l.shape[1] - 1, proj_cols)),
                          axis=1).astype(BF16)
        w_gate = jnp.take(w_l, jnp.asarray(gate_cols), axis=1).astype(BF16)
        g_l = norm_g[layer].reshape(1, D)

        qn, cmp_src, ks, kw, vst, vwt, gates, *dil = _proj_call(h, g_l, w_proj, tm)
        cw = _compress_weights(cmp_pos_k[layer], cmp_w1_k[layer], cmp_w2_k[layer],
                               cmp_pos_v[layer], cmp_w1_v[layer], cmp_w2_v[layer])
        kc, cvo = _compress_call(cmp_src, *cw, ovt)
        o_nsa = _nsa_call(qfeat, qn, kc, cvo, ks, vst, kw, vwt, gates)
        o_d, lse_d = [], []
        for i, (window, dilation) in enumerate(DIL_PATTERNS):
            assert Q_BLOCK - 1 <= window // dilation <= Q_BLOCK
            o, lse = _dilated_call(qfeat_dil, *dil[3 * i:3 * i + 3], window, dilation)
            o_d.append(o)
            lse_d.append(lse)
        h = _merge_call(h, g_l, w_gate, o_nsa, o_d, lse_d, w_br_nsa[layer].astype(BF16),
                        w_br_dil[layer].astype(BF16), w_out[layer].astype(BF16),
                        final_g.reshape(1, D), tm, final=layer == w_in.shape[0] - 1)
    return h
```

```python
import functools
import math

import numpy as np
import jax
import jax.numpy as jnp
from jax import lax
from jax.experimental import pallas as pl
from jax.experimental.pallas import tpu as pltpu

F32 = jnp.float32
BF16 = jnp.bfloat16

HEAD_DIM = 64
LANES = 128
NSA_HEADS = 8
NSA_GROUPS = 2
NSA_HPG = NSA_HEADS // NSA_GROUPS
CMP_BLOCK = 32
CMP_STRIDE = 16
SLC_BLOCK = 64
SLC_TOPK = 16
WIN = 512
FORCE_SCORE = 1.0e4
DIL_HEADS = 8
DIL_PATTERNS = ((128, 1), (512, 4), (2048, 16))
Q_BLOCK = 128
RMS_EPS = 1e-6
KEY_CHUNK = 128
VT_ROWS = 80
GATE_ROWS = 16
FEAT_SPLIT = 3
FLAG_BITS = 16
SLC_GROUP = 4
NSA_QB = 256
N_FORCED = 3
assert FORCE_SCORE > NSA_HPG + 1
SEL_PAD_ROWS = 8
NEG_INIT = -1.0e30
LOG2E = math.log2(math.e)
VMEM_LIMIT = 56 * 1024 * 1024


def _alibi_slopes():
    n = NSA_HEADS + DIL_HEADS
    s = 2.0 ** (-8.0 * np.arange(1, n + 1) / n)
    return s[0::2].astype(np.float32), s[1::2].astype(np.float32)


def _nt(a, b):
    return lax.dot_general(a, b, (((1,), (1,)), ((), ())), preferred_element_type=F32)


def _rms(x, g):
    return (x * lax.rsqrt(jnp.mean(x * x, axis=-1, keepdims=True) + RMS_EPS)) * g


def _head_tile(q_pairs, h):
    tile = q_pairs[:, (h // 2) * LANES:(h // 2 + 1) * LANES]
    return pltpu.roll(tile, HEAD_DIM, 1) if h % 2 else tile


def _pad_heads(q_pairs, n_heads):
    lane = lax.broadcasted_iota(jnp.int32, (q_pairs.shape[0], LANES), 1)
    return [jnp.where(lane < HEAD_DIM, _head_tile(q_pairs, h), 0.0) for h in range(n_heads)]


def _compact_pair(a_pad, b_pad):
    lane = lax.broadcasted_iota(jnp.int32, a_pad.shape, 1)
    return jnp.where(lane < HEAD_DIM, pltpu.roll(a_pad, HEAD_DIM, 1), b_pad)


def _pos_features(pos):
    lane = lax.broadcasted_iota(jnp.int32, pos.shape, 1) - HEAD_DIM
    hi = (pos // LANES).astype(F32)
    lo = (pos % LANES).astype(F32)
    return jnp.where(lane < 0, 0.0, jnp.where(lane < FEAT_SPLIT, hi, jnp.where(lane < 2 * FEAT_SPLIT, lo, 0.0)))


def _slope_features(slopes):
    out = np.zeros((len(slopes), LANES), np.float32)
    for h, s in enumerate(slopes):
        rest = np.float32(s) * np.float32(LOG2E)
        for i in range(FEAT_SPLIT):
            piece = np.float32(np.asarray(rest, np.float32).astype(BF16))
            out[h, HEAD_DIM + FEAT_SPLIT + i] = piece
            out[h, HEAD_DIM + i] = piece * LANES
            rest = np.float32(rest - piece)
    return out


_PROJ_SEGS = (("qn", 512), ("cmp", 256), ("ks", 256), ("kw", 256), ("vs", 128), ("vw", 128),
              ("gate", 128), ("qd", 512), ("kd", 1024), ("vd", 1024))
DIL_Q_W, DIL_KV_W = DIL_HEADS * HEAD_DIM, DIL_HEADS * LANES
_PROJ_OFFS = dict(zip((n for n, _ in _PROJ_SEGS),
                      zip(np.cumsum([0] + [w for _, w in _PROJ_SEGS])[:-1].tolist(),
                          np.cumsum([w for _, w in _PROJ_SEGS]).tolist())))


def _to_lane_tiles(scr, first, val):
    for j in range(val.shape[1] // LANES):
        scr[first + j] = val[:, j * LANES:(j + 1) * LANES]


def _regroup_rows(scr, lo, hi, out_ref, d, dtype):
    rows, width = scr.shape[1], (hi - lo) * LANES
    for p in range(d):
        for j in range(hi - lo):
            tile = scr[lo + j, pl.ds(p, rows // d, stride=d), :] if d > 1 else scr[lo + j]
            out_ref[:, p * width + j * LANES:p * width + (j + 1) * LANES] = tile.astype(dtype)


def _proj_kernel(x_ref, g_ref, w_ref, qn_ref, cmp_ref, ks_ref, kw_ref, vst_ref, vwt_ref, gate_ref,
                 *dil_refs):
    cmp_scr, dil_scr = dil_refs[-2:]
    tm = x_ref.shape[0]
    xb = _rms(x_ref[...], g_ref[...]).astype(BF16)
    scale = HEAD_DIM ** -0.5

    def seg(name):
        c0, c1 = _PROJ_OFFS[name]
        return jnp.dot(xb, w_ref[:, c0:c1], preferred_element_type=F32)

    qn_ref[...] = (seg("qn") * (scale * LOG2E)).astype(BF16)
    _to_lane_tiles(cmp_scr, 0, seg("cmp"))
    _regroup_rows(cmp_scr, 0, cmp_scr.shape[0], cmp_ref, CMP_STRIDE, F32)
    feat = _pos_features(pl.program_id(1) * tm + lax.broadcasted_iota(jnp.int32, (tm, LANES), 0))
    ks, kw = seg("ks"), seg("kw")
    vst, vwt = seg("vs").T, seg("vw").T
    gate_t = jax.nn.sigmoid(seg("gate")).T
    tail_row = lax.broadcasted_iota(jnp.int32, (VT_ROWS - HEAD_DIM, tm), 0)
    tail = jnp.where(tail_row == 0, 1.0, 0.0)
    for g in range(NSA_GROUPS):
        ks_ref[g] = (ks[:, g * LANES:(g + 1) * LANES] + feat).astype(BF16)
        kw_ref[g] = (kw[:, g * LANES:(g + 1) * LANES] + feat).astype(BF16)
        gate_ref[g] = gate_t[g * GATE_ROWS:(g + 1) * GATE_ROWS]
        for src, dst in ((vst, vst_ref), (vwt, vwt_ref)):
            vt = jnp.concatenate([src[g * HEAD_DIM:(g + 1) * HEAD_DIM], tail], axis=0).astype(BF16)
            for j in range(tm // KEY_CHUNK):
                dst[g, j] = vt[:, j * KEY_CHUNK:(j + 1) * KEY_CHUNK]
    head_lane = lax.broadcasted_iota(jnp.int32, (tm, LANES), 1)
    one_col = jnp.where(head_lane == HEAD_DIM, 1.0, 0.0)
    tiles = (0, DIL_Q_W // LANES, (DIL_Q_W + DIL_KV_W) // LANES, (DIL_Q_W + 2 * DIL_KV_W) // LANES)
    _to_lane_tiles(dil_scr, tiles[0], seg("qd") * (scale * LOG2E))
    _to_lane_tiles(dil_scr, tiles[1], seg("kd") + jnp.concatenate([feat] * DIL_HEADS, axis=1))
    _to_lane_tiles(dil_scr, tiles[2], seg("vd") + jnp.concatenate([one_col] * DIL_HEADS, axis=1))
    for i, (_, d) in enumerate(DIL_PATTERNS):
        for j, ref in enumerate(dil_refs[3 * i:3 * i + 3]):
            _regroup_rows(dil_scr, tiles[j], tiles[j + 1], ref, d, BF16)


def _proj_call(x, g, w, tm):
    B, S, D = x.shape
    G = NSA_GROUPS
    cmp_w = _PROJ_OFFS["cmp"][1] - _PROJ_OFFS["cmp"][0]
    row = lambda c: pl.BlockSpec((None, tm, c), lambda b, i: (b, i, 0))
    k_spec =pl.BlockSpec((None, G, tm, LANES), lambda b, i: (b, 0, i, 0))
    vt_spec = pl.BlockSpec((None, G, tm // KEY_CHUNK, VT_ROWS, KEY_CHUNK), lambda b, i: (b, 0, i, 0, 0))
    out_shape = (
        jax.ShapeDtypeStruct((B, S, 512), BF16),
        jax.ShapeDtypeStruct((B, S // CMP_STRIDE, CMP_STRIDE * cmp_w), F32),
        jax.ShapeDtypeStruct((B, G, S, LANES), BF16),
        jax.ShapeDtypeStruct((B, G, S, LANES), BF16),
        jax.ShapeDtypeStruct((B, G, S // KEY_CHUNK, VT_ROWS, KEY_CHUNK), BF16),
        jax.ShapeDtypeStruct((B, G, S // KEY_CHUNK, VT_ROWS, KEY_CHUNK), BF16),
        jax.ShapeDtypeStruct((B, G, GATE_ROWS, S), F32),
    )
    dil_specs = ()
    for _, d in DIL_PATTERNS:
        for width in (DIL_Q_W, DIL_KV_W, DIL_KV_W):
            out_shape += (jax.ShapeDtypeStruct((B, S // d, d * width), BF16),)
            dil_specs += (pl.BlockSpec((None, tm // d, d * width), lambda b, i: (b, i, 0)),)
    return pl.pallas_call(
        _proj_kernel,
        grid=(B, S // tm),
        in_specs=[row(D), pl.BlockSpec((1, D), lambda b, i: (0, 0)),
                  pl.BlockSpec(w.shape, lambda b, i: (0, 0))],
        out_specs=(row(512),
                   pl.BlockSpec((None, tm // CMP_STRIDE, CMP_STRIDE * cmp_w), lambda b, i: (b, i, 0)),
                   k_spec, k_spec, vt_spec, vt_spec,
                   pl.BlockSpec((None, G, GATE_ROWS, tm), lambda b, i: (b, 0, 0, i))) + dil_specs,
        out_shape=out_shape,
        scratch_shapes=[pltpu.VMEM((cmp_w // LANES, tm, LANES), F32),
                        pltpu.VMEM(((DIL_Q_W + 2 * DIL_KV_W) // LANES, tm, LANES), F32)],
        compiler_params=pltpu.CompilerParams(
            dimension_semantics=("parallel", "parallel"), vmem_limit_bytes=VMEM_LIMIT),
        name="proj",
    )(x, g, w)


def _compress_kernel(r_ref, posa_ref, posb_ref, wa_ref, wb_ref, w2_ref, ovt_ref, kc_ref, cvo_ref):
    r = r_ref[...]
    n = r.shape[0]
    ha = jnp.dot((r + posa_ref[...]).astype(BF16), wa_ref[...], preferred_element_type=F32)
    hb = jnp.dot((r + posb_ref[...]).astype(BF16), wb_ref[...], preferred_element_type=F32)
    hid = jax.nn.gelu(ha + pltpu.roll(hb, n - 1, 0))
    out = jnp.dot(hid.astype(BF16), w2_ref[...], preferred_element_type=F32)
    cmp_end = lax.broadcasted_iota(jnp.int32, (n, LANES), 0) * CMP_STRIDE + (CMP_BLOCK - 1)
    feat = _pos_features(cmp_end)
    v_t = out[:, NSA_GROUPS * LANES:].T
    for g in range(NSA_GROUPS):
        kc_ref[g] = (out[:, g * LANES:(g + 1) * LANES] + feat).astype(BF16)
        cvo_ref[g] = jnp.concatenate([v_t[g * HEAD_DIM:(g + 1) * HEAD_DIM], ovt_ref[...]],
                                     axis=0).astype(BF16)


def _compress_call(r, posa, posb, wa, wb, w2, ovt):
    B, R, C = r.shape
    G = NSA_GROUPS
    rows = HEAD_DIM + ovt.shape[0]
    full = lambda a: pl.BlockSpec(a.shape, lambda b: (0,) * a.ndim)
    return pl.pallas_call(
        _compress_kernel,
        grid=(B,),
        in_specs=[pl.BlockSpec((None, R, C), lambda b: (b, 0, 0)),
                  full(posa), full(posb), full(wa), full(wb), full(w2), full(ovt)],
        out_specs=(pl.BlockSpec((None, G, R, LANES), lambda b: (b, 0, 0, 0)),
                   pl.BlockSpec((None, G, rows, R), lambda b: (b, 0, 0, 0))),
        out_shape=(jax.ShapeDtypeStruct((B, G, R, LANES), BF16),
                   jax.ShapeDtypeStruct((B, G, rows, R), BF16)),
        compiler_params=pltpu.CompilerParams(
            dimension_semantics=("parallel",), vmem_limit_bytes=VMEM_LIMIT),
        name="compress",
    )(r, posa, posb, wa, wb, w2, ovt)


def _flash_step(s, ok, v_t, m_ref, acc_ref):
    ps, alphas = [], []
    qb = ok.shape[1]
    for h in range(NSA_HPG):
        cols = slice(h * qb, (h + 1) * qb)
        sh = jnp.where(ok, s[:, cols], -jnp.inf)
        m_old = m_ref[:, cols]
        m_new = jnp.maximum(m_old, jnp.max(sh, axis=0, keepdims=True))
        alphas.append(jnp.exp2(m_old - m_new))
        ps.append(jnp.exp2(sh - m_new).astype(BF16))
        m_ref[:, cols] = m_new
    pv = jnp.dot(v_t, jnp.concatenate(ps, axis=1), preferred_element_type=F32)
    acc_ref[...] = jnp.concatenate(alphas, axis=1) * acc_ref[...] + pv


def _flash_result(acc_ref):
    l = acc_ref[HEAD_DIM:HEAD_DIM + 1, :]
    return acc_ref[0:HEAD_DIM, :] * (1.0 / jnp.maximum(l, 1e-30))


def _softmax_pv(s, ok, v_t):
    ps = []
    qb = ok.shape[1]
    for h in range(NSA_HPG):
        sh = jnp.where(ok, s[:, h * qb:(h + 1) * qb], -jnp.inf)
        ps.append(jnp.exp2(sh - jnp.max(sh, axis=0, keepdims=True)).astype(BF16))
    r = jnp.dot(v_t, jnp.concatenate(ps, axis=1), preferred_element_type=F32)
    return r[0:HEAD_DIM] * (1.0 / jnp.maximum(r[HEAD_DIM:HEAD_DIM + 1], 1e-30))


def _nsa_kernel(qfeat_ref, q_ref, kc_ref, cvo_ref, ks_ref, vst_ref, kw_ref, vwt_ref, gate_ref,
                o_ref, sel_ref, ms_ref, accs_ref, flag_ref, list_ref):
    n = pl.program_id(2)
    qb, kc = NSA_QB, KEY_CHUNK
    t0 = n * qb
    lane = lax.broadcasted_iota(jnp.int32, (qb, LANES), 1)
    qf = q_ref[...].astype(F32)
    q = jnp.concatenate(
        [jnp.where(lane < HEAD_DIM, _head_tile(qf, h), qfeat_ref[h][0:1, :]) for h in range(NSA_HPG)],
        axis=0).astype(BF16)

    span = WIN + qb
    k_lo = pl.multiple_of(jnp.maximum(t0 - WIN, 0), kc)
    dist_w = ((t0 - k_lo) + lax.broadcasted_iota(jnp.int32, (span, qb), 1)
              - lax.broadcasted_iota(jnp.int32, (span, qb), 0))
    vt_w = jnp.concatenate([vwt_ref[k_lo // kc + u] for u in range(span // kc)], axis=1)
    o_win = _softmax_pv(_nt(kw_ref[pl.ds(k_lo, span), :], q), (dist_w >= 0) & (dist_w < WIN), vt_w)

    n_cmp = kc_ref.shape[0]
    n_sel = sel_ref.shape[0] - SEL_PAD_ROWS
    s = _nt(kc_ref[...], q)
    cmp_end = lax.broadcasted_iota(jnp.int32, (n_cmp, qb), 0) * CMP_STRIDE + (CMP_BLOCK - 1)
    ok_c = cmp_end <= t0 + lax.broadcasted_iota(jnp.int32, (n_cmp, qb), 1)
    ps = []
    for h in range(NSA_HPG):
        sh = jnp.where(ok_c, s[:, h * qb:(h + 1) * qb], -jnp.inf)
        m = jnp.max(sh, axis=0, keepdims=True)
        m = jnp.where(m == -jnp.inf, 0.0, m)
        e = jnp.exp2(sh - m)
        l = jnp.sum(e, axis=0, keepdims=True)
        ps.append((e * (1.0 / jnp.maximum(l, 1e-30))).astype(BF16))
    r = jnp.dot(cvo_ref[...], jnp.concatenate(ps, axis=1), preferred_element_type=F32)
    o_cmp = r[0:HEAD_DIM]
    imp_t = r[HEAD_DIM:, 0:qb]
    for h in range(1, NSA_HPG):
        imp_t = imp_t + r[HEAD_DIM:, h * qb:(h + 1) * qb]

    blk = lax.broadcasted_iota(jnp.int32, (n_sel, qb), 0).astype(F32)
    cur = ((t0 + lax.broadcasted_iota(jnp.int32, (n_sel, qb), 1)) // SLC_BLOCK).astype(F32)
    ago = cur - blk
    forced = jnp.where(blk == 0, 1.0, jnp.where(ago == 0, 1.0, jnp.where(ago == 1, 1.0, 0.0)))
    valid = ago >= 0
    score = jnp.where(valid, jnp.where(forced > 0.5, -1.0, imp_t), -1.0)
    for _ in range(max(min(SLC_TOPK, n_sel) - N_FORCED, 0)):
        mx = jnp.max(score, axis=0, keepdims=True)
        idx = jnp.min(jnp.where(score == mx, blk, float(n_sel)), axis=0, keepdims=True)
        score = jnp.where(blk == idx, -jnp.inf, score)
    sel_t = jnp.where(valid, jnp.where(score == -jnp.inf, 1.0, forced), 0.0)
    sel_ref[0:n_sel, :] = sel_t
    sel_ref[n_sel:, :] = jnp.zeros((SEL_PAD_ROWS, qb), F32)
    any_q = jnp.max(sel_t, axis=1, keepdims=True)
    bit = jnp.left_shift(1, lax.broadcasted_iota(jnp.int32, (n_sel, 1), 0) % FLAG_BITS).astype(F32)
    packed = any_q * bit
    for i in range(n_sel // FLAG_BITS):
        word = jnp.sum(packed[i * FLAG_BITS:(i + 1) * FLAG_BITS], axis=0, keepdims=True)
        flag_ref[i] = word.astype(jnp.int32)[0, 0]

    blocks_per_chunk = kc // SLC_BLOCK
    chunks_per_word = FLAG_BITS // blocks_per_chunk

    def scan_body(w, cnt):
        word = flag_ref[w]
        for j in range(chunks_per_word):
            bits = (word >> (j * blocks_per_chunk)) & ((1 << blocks_per_chunk) - 1)
            list_ref[cnt] = w * chunks_per_word + j
            cnt = cnt + (bits != 0).astype(jnp.int32)
        return cnt

    cnt = lax.fori_loop(0, (t0 + qb - 1) // (FLAG_BITS * SLC_BLOCK) + 1, scan_body, 0)
    for u in range(SLC_GROUP):
        list_ref[cnt + u] = -1

    row = lax.broadcasted_iota(jnp.int32, (kc, qb), 0)
    back = lax.broadcasted_iota(jnp.int32, (kc, qb), 1) - row
    ms_ref[...] = jnp.full(ms_ref.shape, NEG_INIT, F32)
    accs_ref[...] = jnp.zeros(accs_ref.shape, F32)

    def slc_body(gi, carry):
        keys, oks, vts = [], [], []
        for u in range(SLC_GROUP):
            c = list_ref[gi * SLC_GROUP + u]
            cc = jnp.maximum(c, 0)
            k0 = pl.multiple_of(cc * kc, kc)
            first = jnp.where(c >= 0, cc * blocks_per_chunk, n_sel)
            chosen = sel_ref[pl.ds(first, 1), :]
            for j in range(1, blocks_per_chunk):
                chosen = jnp.where(row < j * SLC_BLOCK, chosen, sel_ref[pl.ds(first + j, 1), :])
            oks.append(jnp.where(back + (t0 - k0) >= 0, chosen, 0.0))
            keys.append(ks_ref[pl.ds(k0, kc), :])
            vts.append(vst_ref[cc])
        _flash_step(_nt(jnp.concatenate(keys, axis=0), q), jnp.concatenate(oks, axis=0) > 0.5,
                    jnp.concatenate(vts, axis=1), ms_ref, accs_ref)
        return carry

    lax.fori_loop(0, (cnt + SLC_GROUP - 1) // SLC_GROUP, slc_body, 0)

    o_slc = _flash_result(accs_ref)
    gate = gate_ref[...]
    outs = []
    for h in range(NSA_HPG):
        cols = slice(h * qb, (h + 1) * qb)
        gc = [gate[j * NSA_HPG + h:j * NSA_HPG + h + 1, :] for j in range(3)]
        outs.append(gc[0] * o_cmp[:, cols] + gc[1] * o_slc[:, cols] + gc[2] * o_win[:, cols])
    o_ref[...] = jnp.concatenate(outs, axis=0).T


def _nsa_call(qfeat, qn, kc, cvo, ks, vst, kw, vwt, gates):
    B, S, _ = qn.shape
    G = NSA_GROUPS
    qb = NSA_QB
    nb = S // qb
    n_sel = S // SLC_BLOCK
    width = NSA_HPG * qb
    per_group = lambda a: pl.BlockSpec((None, None) + a.shape[2:],
                                       lambda b, g, n: (b, g) + (0,) * (a.ndim - 2))
    return pl.pallas_call(
        _nsa_kernel,
        grid=(B, G, nb),
        in_specs=[
            pl.BlockSpec((None,) + qfeat.shape[1:], lambda b, g, n: (g, 0, 0, 0)),
            pl.BlockSpec((None, qb, NSA_HPG * HEAD_DIM), lambda b, g, n: (b, n, g)),
            per_group(kc), per_group(cvo), per_group(ks), per_group(vst), per_group(kw), per_group(vwt),
            pl.BlockSpec((None, None, GATE_ROWS, qb), lambda b, g, n: (b, g, 0, n)),
        ],
        out_specs=pl.BlockSpec((None, qb, NSA_HPG * HEAD_DIM), lambda b, g, n: (b, n, g)),
        out_shape=jax.ShapeDtypeStruct((B, S, NSA_HEADS * HEAD_DIM), F32),
        scratch_shapes=[pltpu.VMEM((n_sel + SEL_PAD_ROWS, qb), F32),
                        pltpu.VMEM((1, width), F32), pltpu.VMEM((VT_ROWS, width), F32),
                        pltpu.SMEM((n_sel // FLAG_BITS,), jnp.int32),
                        pltpu.SMEM((S // KEY_CHUNK + SLC_GROUP,), jnp.int32)],
        compiler_params=pltpu.CompilerParams(
            dimension_semantics=("parallel", "parallel", "arbitrary"), vmem_limit_bytes=VMEM_LIMIT),
        name="nsa",
    )(qfeat, qn, kc, cvo, ks, vst, kw, vwt, gates)


def _dilated_kernel(qfeat_ref, q_ref, kp_ref, kc_ref, vp_ref, vc_ref, o_ref, lse_ref, *, wd):
    c = Q_BLOCK
    qf = q_ref[...].astype(F32)
    lane = lax.broadcasted_iota(jnp.int32, (c, LANES), 1)
    key = lax.broadcasted_iota(jnp.int32, (2 * c, c), 0)
    qry = lax.broadcasted_iota(jnp.int32, (2 * c, c), 1)
    first = pl.program_id(2) == 0
    ok = jnp.where(key < c, jnp.where(first, 2 * c, qry) - (key + (wd - c)), (key - c) - qry) <= 0
    outs, lses = [], []
    for h in range(DIL_HEADS):
        hs = slice(h * LANES, (h + 1) * LANES)
        qh = jnp.where(lane < HEAD_DIM, _head_tile(qf, h), qfeat_ref[h][0:1, :]).astype(BF16)
        keys = jnp.concatenate([kp_ref[:, hs], kc_ref[:, hs]], axis=0)
        vals = jnp.concatenate([vp_ref[:, hs], vc_ref[:, hs]], axis=0)
        v_t = vals.astype(F32).T[0:VT_ROWS].astype(BF16)
        s = jnp.where(ok, _nt(keys, qh), -jnp.inf)
        m = jnp.max(s, axis=0, keepdims=True)
        r = jnp.dot(v_t, jnp.exp2(s - m).astype(BF16), preferred_element_type=F32)
        l = r[HEAD_DIM:HEAD_DIM + 1]
        outs.append(r[0:HEAD_DIM] * (1.0 / l))
        lses.append(jnp.broadcast_to(m + jnp.log2(l), (HEAD_DIM, c)))
    for p in range(DIL_HEADS // 2):
        o_ref[:, p * LANES:(p + 1) * LANES] = jnp.concatenate(outs[2 * p:2 * p + 2], axis=0).T
        lse_ref[:, p * LANES:(p + 1) * LANES] = jnp.concatenate(lses[2 * p:2 * p + 2], axis=0).T


def _dilated_call(qfeat, q_v, k_v, v_v, window, dilation):
    B, L, _ = q_v.shape
    nblk = L // Q_BLOCK
    blk = lambda cw: pl.BlockSpec((None, Q_BLOCK, cw), lambda b, p, i: (b, i, p))
    prev = pl.BlockSpec((None, Q_BLOCK, DIL_KV_W), lambda b, p, i: (b, jnp.maximum(i - 1, 0), p))
    return pl.pallas_call(
        functools.partial(_dilated_kernel, wd=window // dilation),
        grid=(B, dilation, nblk),
        in_specs=[pl.BlockSpec(qfeat.shape, lambda b, p, i: (0, 0, 0)),
                  blk(DIL_Q_W), prev, blk(DIL_KV_W), prev, blk(DIL_KV_W)],
        out_specs=(blk(DIL_Q_W), blk(DIL_Q_W)),
        out_shape=(jax.ShapeDtypeStruct((B, L, dilation * DIL_Q_W), F32),
                   jax.ShapeDtypeStruct((B, L, dilation * DIL_Q_W), F32)),
        compiler_params=pltpu.CompilerParams(
            dimension_semantics=("parallel", "parallel", "arbitrary"), vmem_limit_bytes=VMEM_LIMIT),
        name=f"dilated_d{dilation}",
    )(qfeat, q_v, k_v, k_v, v_v, v_v)


def _merge_kernel(x_ref, g_ref, wg_ref, onsa_ref, o1_ref, o2_ref, o3_ref, l1_ref, l2_ref, l3_ref,
                  wbn_ref, wbd_ref, wo_ref, fg_ref, out_ref, tok_scr, *, final):
    x = x_ref[...]
    tm = x.shape[0]
    xb = _rms(x, g_ref[...]).astype(BF16)
    cz = NSA_HEADS * HEAD_DIM
    d = x.shape[-1]

    def gate_seg(c0, c1):
        return jnp.dot(xb, wg_ref[:, c0:c1], preferred_element_type=F32)

    def token_major(ref, slot, dil):
        if dil == 1:
            return ref[...]
        tiles = DIL_Q_W // LANES
        for p in range(dil):
            for j in range(tiles):
                tok_scr[slot * tiles + j, pl.ds(p, tm // dil, stride=dil), :] = \
                    ref[:, p * DIL_Q_W + j * LANES:p * DIL_Q_W + (j + 1) * LANES]
        return jnp.concatenate([tok_scr[slot * tiles + j] for j in range(tiles)], axis=1)

    o_nsa = onsa_ref[...] * jax.nn.silu(gate_seg(0, cz))
    dils = [dil for _, dil in DIL_PATTERNS]
    o1, o2, o3 = (token_major(r, i, dil) for i, (r, dil) in enumerate(zip((o1_ref, o2_ref, o3_ref), dils)))
    l1, l2, l3 = (token_major(r, 3 + i, dil) for i, (r, dil) in enumerate(zip((l1_ref, l2_ref, l3_ref), dils)))
    mx = jnp.maximum(jnp.maximum(l1, l2), l3)
    e1, e2, e3 = jnp.exp2(l1 - mx), jnp.exp2(l2 - mx), jnp.exp2(l3 - mx)
    den = e1 + e2 + e3
    o_dil = (e1 / den) * o1 + (e2 / den) * o2 + (e3 / den) * o3
    o_dil = o_dil * jax.nn.silu(gate_seg(cz, 2 * cz))
    a = jnp.dot(o_nsa.astype(BF16), wbn_ref[...], preferred_element_type=F32)
    bd = jnp.dot(o_dil.astype(BF16), wbd_ref[...], preferred_element_type=F32)
    merged = (jax.nn.sigmoid(gate_seg(2 * cz, 2 * cz + d)) * a
              + jax.nn.sigmoid(gate_seg(2 * cz + d, 2 * cz + 2 * d)) * bd)
    y = jnp.dot(merged.astype(BF16), wo_ref[...], preferred_element_type=F32)
    out_ref[...] = _rms(x + y, fg_ref[...]) if final else x + y


def _merge_call(x, g, wg, o_nsa, o_d, lse_d, wbn, wbd, wo, fg, tm, final):
    B, S, D = x.shape
    cz = o_nsa.shape[-1]
    row = lambda c: pl.BlockSpec((None, tm, c), lambda b, i: (b, i, 0))
    full = lambda a: pl.BlockSpec(a.shape, lambda b, i: (0,) * a.ndim)
    phase = [pl.BlockSpec((None, tm // dil, dil * cz), lambda b, i: (b, i, 0)) for _, dil in DIL_PATTERNS]
    return pl.pallas_call(
        functools.partial(_merge_kernel, final=final),
        grid=(B, S // tm),
        in_specs=[row(D), full(g), full(wg), row(cz)] + phase + phase
                 + [full(wbn), full(wbd), full(wo), full(fg)],
        out_specs=row(D),
        out_shape=jax.ShapeDtypeStruct((B, S, D), F32),
        scratch_shapes=[pltpu.VMEM((2 * len(DIL_PATTERNS) * cz // LANES, tm, LANES), F32)],
        compiler_params=pltpu.CompilerParams(
            dimension_semantics=("parallel", "parallel"), vmem_limit_bytes=VMEM_LIMIT),
        name="merge",
    )(x, g, wg, o_nsa, *o_d, *lse_d, wbn, wbd, wo, fg)


def _proj_columns(d_model):
    hd, G, H = HEAD_DIM, NSA_GROUPS, NSA_HEADS
    nsa_w = H * hd
    o_q, o_kv = 0, nsa_w
    o_g = o_kv + 6 * G * hd
    o_zn = o_g + 3 * H
    o_qkvd = o_zn + nsa_w
    dil_w = DIL_HEADS * hd
    o_zd = o_qkvd + 3 * dil_w
    o_mg = o_zd + dil_w
    kv = lambda j, g: o_kv + (j * G + g) * hd + np.arange(hd)
    pad = np.full(hd, -1)
    cols = [o_q + np.arange(nsa_w)]
    cols += [kv(j, g) for j in (0, 1) for g in range(G)]
    for j in (2, 4):
        for g in range(G):
            cols += [kv(j, g), pad]
    for j in (3, 5):
        cols += [kv(j, g) for g in range(G)]
    lanes = np.full(LANES, -1)
    for g in range(G):
        for j in range(3):
            lanes[g * GATE_ROWS + j * NSA_HPG + np.arange(NSA_HPG)] = o_g + j * H + g * NSA_HPG + np.arange(NSA_HPG)
    cols.append(lanes)
    cols.append(o_qkvd + np.arange(dil_w))
    for j in (1, 2):
        for h in range(DIL_HEADS):
            cols += [o_qkvd + j * dil_w + h * hd + np.arange(hd), pad]
    proj_cols = np.concatenate(cols)
    assert proj_cols.size == _PROJ_OFFS["vd"][1]
    gate_cols = np.concatenate([o_zn + np.arange(nsa_w), o_zd + np.arange(dil_w),
                                o_mg + np.arange(2 * d_model)])
    return proj_cols, gate_cols


def _compress_weights(pos_k, w1_k, w2_k, pos_v, w1_v, w2_v):
    hd, G, half = HEAD_DIM, NSA_GROUPS, CMP_BLOCK // 2
    eye = jnp.eye(2 * G, dtype=F32)
    w1 = jnp.stack([w1_k] * G + [w1_v] * G)
    pos = jnp.stack([pos_k] * G + [pos_v] * G)

    def first_layer(lo):
        w = jnp.einsum('slde,st->lsdte', w1[:, lo:lo + half], eye)
        p = pos[:, lo:lo + half].transpose(1, 0, 2)
        return w.reshape(half * 2 * G * hd, 2 * G * hd).astype(BF16), p.reshape(1, half * 2 * G * hd)

    wa, posa = first_layer(0)
    wb, posb = first_layer(half)
    w2 = jnp.zeros((2 * G, hd, 3 * G, hd), F32)
    for g in range(G):
        w2 = w2.at[g, :, 2 * g].set(w2_k).at[G + g, :, 2 * G + g].set(w2_v)
    return posa, posb, wa, wb, w2.reshape(2 * G * hd, 3 * G * hd).astype(BF16)


def _overlap_t(n_cmp_pad, n_sel):
    i = np.arange(n_cmp_pad)[None, :]
    j = np.arange(n_sel)[:, None]
    lo = np.maximum(i * CMP_STRIDE, j * SLC_BLOCK)
    hi = np.minimum(i * CMP_STRIDE + CMP_BLOCK, (j + 1) * SLC_BLOCK)
    return (np.clip(hi - lo, 0, None) / CMP_BLOCK).astype(np.float32)


def kernel(x, norm_g, w_in, cmp_pos_k, cmp_w1_k, cmp_w2_k, cmp_pos_v, cmp_w1_v, cmp_w2_v,
           w_br_nsa, w_br_dil, w_out, final_g):
    B, S, D = x.shape
    n_sel = S // SLC_BLOCK
    assert S % (Q_BLOCK * max(d for _, d in DIL_PATTERNS)) == 0 and n_sel % FLAG_BITS == 0
    n_cmp_pad = S // CMP_STRIDE
    slopes_nsa, slopes_dil = _alibi_slopes()
    proj_cols, gate_cols = _proj_columns(D)
    ovt = jnp.asarray(_overlap_t(n_cmp_pad, n_sel))
    qfeat = jnp.asarray(np.broadcast_to(
        _slope_features(slopes_nsa).reshape(NSA_GROUPS, NSA_HPG, 1, LANES),
        (NSA_GROUPS, NSA_HPG, 8, LANES)))
    qfeat_dil = jnp.asarray(np.broadcast_to(
        _slope_features(slopes_dil).reshape(DIL_HEADS, 1, LANES), (DIL_HEADS, 8, LANES)))
    tm = 512

    h = x
    for layer in range(w_in.shape[0]):
        w_l = jnp.concatenate([w_in[layer], jnp.zeros((D, 1), w_in.dtype)], axis=1)
        w_proj = jnp.take(w_l, jnp.asarray(np.where(proj_cols < 0, w_l.shape[1] - 1, proj_cols)),
                          axis=1).astype(BF16)
        w_gate = jnp.take(w_l, jnp.asarray(gate_cols), axis=1).astype(BF16)
        g_l = norm_g[layer].reshape(1, D)

        qn, cmp_src, ks, kw, vst, vwt, gates, *dil = _proj_call(h, g_l, w_proj, tm)
        cw = _compress_weights(cmp_pos_k[layer], cmp_w1_k[layer], cmp_w2_k[layer],
                               cmp_pos_v[layer], cmp_w1_v[layer], cmp_w2_v[layer])
        kc, cvo = _compress_call(cmp_src, *cw, ovt)
        o_nsa = _nsa_call(qfeat, qn, kc, cvo, ks, vst, kw, vwt, gates)
        o_d, lse_d = [], []
        for i, (window, dilation) in enumerate(DIL_PATTERNS):
            assert Q_BLOCK - 1 <= window // dilation <= Q_BLOCK
            o, lse = _dilated_call(qfeat_dil, *dil[3 * i:3 * i + 3], window, dilation)
            o_d.append(o)
            lse_d.append(lse)
        h = _merge_call(h, g_l, w_gate, o_nsa, o_d, lse_d, w_br_nsa[layer].astype(BF16),
                        w_br_dil[layer].astype(BF16), w_out[layer].astype(BF16),
                        final_g.reshape(1, D), tm, final=layer == w_in.shape[0] - 1)
    return h
```

```python
import functools
import math

import numpy as np
import jax
import jax.numpy as jnp
from jax import lax
from jax.experimental import pallas as pl
from jax.experimental.pallas import tpu as pltpu

F32 = jnp.float32
BF16 = jnp.bfloat16

HEAD_DIM = 64
LANES = 128
NSA_HEADS = 8
NSA_GROUPS = 2
NSA_HPG = NSA_HEADS // NSA_GROUPS
CMP_BLOCK = 32
CMP_STRIDE = 16
SLC_BLOCK = 64
SLC_TOPK = 16
WIN = 512
FORCE_SCORE = 1.0e4
DIL_HEADS = 8
DIL_PATTERNS = ((128, 1), (512, 4), (2048, 16))
Q_BLOCK = 128
RMS_EPS = 1e-6
KEY_CHUNK = 128
VT_ROWS = 80
GATE_ROWS = 16
FEAT_SPLIT = 3
FLAG_BITS = 16
SLC_GROUP = 4
NSA_QB = 256
DIL_BLOCKS_PER_STEP = 2
N_FORCED = 3
assert FORCE_SCORE > NSA_HPG + 1
SEL_PAD_ROWS = 8
NEG_INIT = -1.0e30
LOG2E = math.log2(math.e)
VMEM_LIMIT = 56 * 1024 * 1024


def _alibi_slopes():
    n = NSA_HEADS + DIL_HEADS
    s = 2.0 ** (-8.0 * np.arange(1, n + 1) / n)
    return s[0::2].astype(np.float32), s[1::2].astype(np.float32)


def _nt(a, b):
    return lax.dot_general(a, b, (((1,), (1,)), ((), ())), preferred_element_type=F32)


def _rms(x, g):
    return (x * lax.rsqrt(jnp.mean(x * x, axis=-1, keepdims=True) + RMS_EPS)) * g


def _head_tile(q_pairs, h):
    tile = q_pairs[:, (h // 2) * LANES:(h // 2 + 1) * LANES]
    return pltpu.roll(tile, HEAD_DIM, 1) if h % 2 else tile


def _pad_heads(q_pairs, n_heads):
    lane = lax.broadcasted_iota(jnp.int32, (q_pairs.shape[0], LANES), 1)
    return [jnp.where(lane < HEAD_DIM, _head_tile(q_pairs, h), 0.0) for h in range(n_heads)]


def _compact_pair(a_pad, b_pad):
    lane = lax.broadcasted_iota(jnp.int32, a_pad.shape, 1)
    return jnp.where(lane < HEAD_DIM, pltpu.roll(a_pad, HEAD_DIM, 1), b_pad)


def _pos_features(pos):
    lane = lax.broadcasted_iota(jnp.int32, pos.shape, 1) - HEAD_DIM
    hi = (pos // LANES).astype(F32)
    lo = (pos % LANES).astype(F32)
    return jnp.where(lane < 0, 0.0, jnp.where(lane < FEAT_SPLIT, hi, jnp.where(lane < 2 * FEAT_SPLIT, lo, 0.0)))


def _slope_features(slopes):
    out = np.zeros((len(slopes), LANES), np.float32)
    for h, s in enumerate(slopes):
        rest = np.float32(s) * np.float32(LOG2E)
        for i in range(FEAT_SPLIT):
            piece = np.float32(np.asarray(rest, np.float32).astype(BF16))
            out[h, HEAD_DIM + FEAT_SPLIT + i] = piece
            out[h, HEAD_DIM + i] = piece * LANES
            rest = np.float32(rest - piece)
    return out


_PROJ_SEGS = (("qn", 512), ("cmp", 256), ("ks", 256), ("kw", 256), ("vs", 128), ("vw", 128),
              ("gate", 128), ("qd", 512), ("kd", 1024), ("vd", 1024))
DIL_Q_W, DIL_KV_W = DIL_HEADS * HEAD_DIM, DIL_HEADS * LANES
_PROJ_OFFS = dict(zip((n for n, _ in _PROJ_SEGS),
                      zip(np.cumsum([0] + [w for _, w in _PROJ_SEGS])[:-1].tolist(),
                          np.cumsum([w for _, w in _PROJ_SEGS]).tolist())))


def _to_lane_tiles(scr, first, val):
    for j in range(val.shape[1] // LANES):
        scr[first + j] = val[:, j * LANES:(j + 1) * LANES]


def _regroup_rows(scr, lo, hi, out_ref, d, dtype):
    rows, width = scr.shape[1], (hi - lo) * LANES
    for p in range(d):
        for j in range(hi - lo):
            tile = scr[lo + j, pl.ds(p, rows // d, stride=d), :] if d > 1 else scr[lo + j]
            out_ref[:, p * width + j * LANES:p * width + (j + 1) * LANES] = tile.astype(dtype)


def _proj_kernel(x_ref, g_ref, w_ref, perm_ref, qn_ref, cmp_ref, ks_ref, kw_ref, vst_ref, vwt_ref,
                 gate_ref, *dil_refs):
    cmp_scr = dil_refs[-1]
    tm = x_ref.shape[0]
    xb = _rms(x_ref[...], g_ref[...]).astype(BF16)
    scale = HEAD_DIM ** -0.5

    def seg(name):
        c0, c1 = _PROJ_OFFS[name]
        return jnp.dot(xb, w_ref[:, c0:c1], preferred_element_type=F32)

    qn_ref[...] = (seg("qn") * (scale * LOG2E)).astype(BF16)
    _to_lane_tiles(cmp_scr, 0, seg("cmp"))
    _regroup_rows(cmp_scr, 0, cmp_scr.shape[0], cmp_ref, CMP_STRIDE, F32)
    feat = _pos_features(pl.program_id(1) * tm + lax.broadcasted_iota(jnp.int32, (tm, LANES), 0))
    ks, kw = seg("ks"), seg("kw")
    vst, vwt = seg("vs").T, seg("vw").T
    gate_t = jax.nn.sigmoid(seg("gate")).T
    tail_row = lax.broadcasted_iota(jnp.int32, (VT_ROWS - HEAD_DIM, tm), 0)
    tail = jnp.where(tail_row == 0, 1.0, 0.0)
    for g in range(NSA_GROUPS):
        ks_ref[g] = (ks[:, g * LANES:(g + 1) * LANES] + feat).astype(BF16)
        kw_ref[g] = (kw[:, g * LANES:(g + 1) * LANES] + feat).astype(BF16)
        gate_ref[g] = gate_t[g * GATE_ROWS:(g + 1) * GATE_ROWS]
        for src, dst in ((vst, vst_ref), (vwt, vwt_ref)):
            vt = jnp.concatenate([src[g * HEAD_DIM:(g + 1) * HEAD_DIM], tail], axis=0).astype(BF16)
            for j in range(tm // KEY_CHUNK):
                dst[g, j] = vt[:, j * KEY_CHUNK:(j + 1) * KEY_CHUNK]
    head_lane = lax.broadcasted_iota(jnp.int32, (tm, LANES), 1)
    one_col = jnp.where(head_lane == HEAD_DIM, 1.0, 0.0)
    qkv = jnp.concatenate(
        [(seg("qd") * (scale * LOG2E)).astype(BF16),
         (seg("kd") + jnp.concatenate([feat] * DIL_HEADS, axis=1)).astype(BF16),
         (seg("vd") + jnp.concatenate([one_col] * DIL_HEADS, axis=1)).astype(BF16)], axis=1)
    bounds = (0, DIL_Q_W, DIL_Q_W + DIL_KV_W, DIL_Q_W + 2 * DIL_KV_W)
    for i, (_, d) in enumerate(DIL_PATTERNS):
        by_phase = qkv if d == 1 else jnp.dot(perm_ref[i], qkv, preferred_element_type=F32).astype(BF16)
        for p in range(d):
            rows = by_phase[p * (tm // d):(p + 1) * (tm // d)]
            for j, ref in enumerate(dil_refs[3 * i:3 * i + 3]):
                width = bounds[j + 1] - bounds[j]
                ref[:, p * width:(p + 1) * width] = rows[:, bounds[j]:bounds[j + 1]]


def _proj_call(x, g, w, tm):
    B, S, D = x.shape
    G = NSA_GROUPS
    cmp_w = _PROJ_OFFS["cmp"][1] - _PROJ_OFFS["cmp"][0]
    row = lambda c: pl.BlockSpec((None, tm, c), lambda b, i: (b, i, 0))
    k_spec =pl.BlockSpec((None, G, tm, LANES), lambda b, i: (b, 0, i, 0))
    vt_spec = pl.BlockSpec((None, G, tm // KEY_CHUNK, VT_ROWS, KEY_CHUNK), lambda b, i: (b, 0, i, 0, 0))
    out_shape = (
        jax.ShapeDtypeStruct((B, S, 512), BF16),
        jax.ShapeDtypeStruct((B, S // CMP_STRIDE, CMP_STRIDE * cmp_w), F32),
        jax.ShapeDtypeStruct((B, G, S, LANES), BF16),
        jax.ShapeDtypeStruct((B, G, S, LANES), BF16),
        jax.ShapeDtypeStruct((B, G, S // KEY_CHUNK, VT_ROWS, KEY_CHUNK), BF16),
        jax.ShapeDtypeStruct((B, G, S // KEY_CHUNK, VT_ROWS, KEY_CHUNK), BF16),
        jax.ShapeDtypeStruct((B, G, GATE_ROWS, S), F32),
    )
    dil_specs = ()
    for _, d in DIL_PATTERNS:
        for width in (DIL_Q_W, DIL_KV_W, DIL_KV_W):
            out_shape += (jax.ShapeDtypeStruct((B, S // d, d * width), BF16),)
            dil_specs += (pl.BlockSpec((None, tm // d, d * width), lambda b, i: (b, i, 0)),)
    r = np.arange(tm)
    perm = np.zeros((len(DIL_PATTERNS), tm, tm), np.float32)
    for i, (_, d) in enumerate(DIL_PATTERNS):
        perm[i, (r % d) * (tm // d) + r // d, r] = 1.0
    perm = jnp.asarray(perm, BF16)
    return pl.pallas_call(
        _proj_kernel,
        grid=(B, S // tm),
        in_specs=[row(D), pl.BlockSpec((1, D), lambda b, i: (0, 0)),
                  pl.BlockSpec(w.shape, lambda b, i: (0, 0)),
                  pl.BlockSpec(perm.shape, lambda b, i: (0, 0, 0))],
        out_specs=(row(512),
                   pl.BlockSpec((None, tm // CMP_STRIDE, CMP_STRIDE * cmp_w), lambda b, i: (b, i, 0)),
                   k_spec, k_spec, vt_spec, vt_spec,
                   pl.BlockSpec((None, G, GATE_ROWS, tm), lambda b, i: (b, 0, 0, i))) + dil_specs,
        out_shape=out_shape,
        scratch_shapes=[pltpu.VMEM((cmp_w // LANES, tm, LANES), F32)],
        compiler_params=pltpu.CompilerParams(
            dimension_semantics=("parallel", "parallel"), vmem_limit_bytes=VMEM_LIMIT),
        name="proj",
    )(x, g, w, perm)


def _compress_kernel(r_ref, posa_ref, posb_ref, wa_ref, wb_ref, w2_ref, ovt_ref, kc_ref, cvo_ref):
    r = r_ref[...]
    n = r.shape[0]
    ha = jnp.dot((r + posa_ref[...]).astype(BF16), wa_ref[...], preferred_element_type=F32)
    hb = jnp.dot((r + posb_ref[...]).astype(BF16), wb_ref[...], preferred_element_type=F32)
    hid = jax.nn.gelu(ha + pltpu.roll(hb, n - 1, 0))
    out = jnp.dot(hid.astype(BF16), w2_ref[...], preferred_element_type=F32)
    cmp_end = lax.broadcasted_iota(jnp.int32, (n, LANES), 0) * CMP_STRIDE + (CMP_BLOCK - 1)
    feat = _pos_features(cmp_end)
    v_t = out[:, NSA_GROUPS * LANES:].T
    for g in range(NSA_GROUPS):
        kc_ref[g] = (out[:, g * LANES:(g + 1) * LANES] + feat).astype(BF16)
        cvo_ref[g] = jnp.concatenate([v_t[g * HEAD_DIM:(g + 1) * HEAD_DIM], ovt_ref[...]],
                                     axis=0).astype(BF16)


def _compress_call(r, posa, posb, wa, wb, w2, ovt):
    B, R, C = r.shape
    G = NSA_GROUPS
    rows = HEAD_DIM + ovt.shape[0]
    full = lambda a: pl.BlockSpec(a.shape, lambda b: (0,) * a.ndim)
    return pl.pallas_call(
        _compress_kernel,
        grid=(B,),
        in_specs=[pl.BlockSpec((None, R, C), lambda b: (b, 0, 0)),
                  full(posa), full(posb), full(wa), full(wb), full(w2), full(ovt)],
        out_specs=(pl.BlockSpec((None, G, R, LANES), lambda b: (b, 0, 0, 0)),
                   pl.BlockSpec((None, G, rows, R), lambda b: (b, 0, 0, 0))),
        out_shape=(jax.ShapeDtypeStruct((B, G, R, LANES), BF16),
                   jax.ShapeDtypeStruct((B, G, rows, R), BF16)),
        compiler_params=pltpu.CompilerParams(
            dimension_semantics=("parallel",), vmem_limit_bytes=VMEM_LIMIT),
        name="compress",
    )(r, posa, posb, wa, wb, w2, ovt)


def _flash_step(s, ok, v_t, m_ref, acc_ref):
    ps, alphas = [], []
    qb = ok.shape[1]
    for h in range(NSA_HPG):
        cols = slice(h * qb, (h + 1) * qb)
        sh = jnp.where(ok, s[:, cols], -jnp.inf)
        m_old = m_ref[:, cols]
        m_new = jnp.maximum(m_old, jnp.max(sh, axis=0, keepdims=True))
        alphas.append(jnp.exp2(m_old - m_new))
        ps.append(jnp.exp2(sh - m_new).astype(BF16))
        m_ref[:, cols] = m_new
    pv = jnp.dot(v_t, jnp.concatenate(ps, axis=1), preferred_element_type=F32)
    acc_ref[...] = jnp.concatenate(alphas, axis=1) * acc_ref[...] + pv


def _flash_result(acc_ref):
    l = acc_ref[HEAD_DIM:HEAD_DIM + 1, :]
    return acc_ref[0:HEAD_DIM, :] * (1.0 / jnp.maximum(l, 1e-30))


def _softmax_pv(s, ok, v_t):
    ps = []
    qb = ok.shape[1]
    for h in range(NSA_HPG):
        sh = jnp.where(ok, s[:, h * qb:(h + 1) * qb], -jnp.inf)
        ps.append(jnp.exp2(sh - jnp.max(sh, axis=0, keepdims=True)).astype(BF16))
    r = jnp.dot(v_t, jnp.concatenate(ps, axis=1), preferred_element_type=F32)
    return r[0:HEAD_DIM] * (1.0 / jnp.maximum(r[HEAD_DIM:HEAD_DIM + 1], 1e-30))


def _nsa_kernel(qfeat_ref, q_ref, kc_ref, cvo_ref, ks_ref, vst_ref, kw_ref, vwt_ref, gate_ref,
                o_ref, sel_ref, ms_ref, accs_ref, flag_ref, list_ref):
    n = pl.program_id(2)
    qb, kc = NSA_QB, KEY_CHUNK
    t0 = n * qb
    lane = lax.broadcasted_iota(jnp.int32, (qb, LANES), 1)
    qf = q_ref[...].astype(F32)
    q = jnp.concatenate(
        [jnp.where(lane < HEAD_DIM, _head_tile(qf, h), qfeat_ref[h][0:1, :]) for h in range(NSA_HPG)],
        axis=0).astype(BF16)

    span = WIN + qb
    k_lo = pl.multiple_of(jnp.maximum(t0 - WIN, 0), kc)
    dist_w = ((t0 - k_lo) + lax.broadcasted_iota(jnp.int32, (span, qb), 1)
              - lax.broadcasted_iota(jnp.int32, (span, qb), 0))
    vt_w = jnp.concatenate([vwt_ref[k_lo // kc + u] for u in range(span // kc)], axis=1)
    o_win = _softmax_pv(_nt(kw_ref[pl.ds(k_lo, span), :], q), (dist_w >= 0) & (dist_w < WIN), vt_w)

    n_cmp = kc_ref.shape[0]
    n_sel = sel_ref.shape[0] - SEL_PAD_ROWS
    s = _nt(kc_ref[...], q)
    cmp_end = lax.broadcasted_iota(jnp.int32, (n_cmp, qb), 0) * CMP_STRIDE + (CMP_BLOCK - 1)
    ok_c = cmp_end <= t0 + lax.broadcasted_iota(jnp.int32, (n_cmp, qb), 1)
    ps = []
    for h in range(NSA_HPG):
        sh = jnp.where(ok_c, s[:, h * qb:(h + 1) * qb], -jnp.inf)
        m = jnp.max(sh, axis=0, keepdims=True)
        m = jnp.where(m == -jnp.inf, 0.0, m)
        e = jnp.exp2(sh - m)
        l = jnp.sum(e, axis=0, keepdims=True)
        ps.append((e * (1.0 / jnp.maximum(l, 1e-30))).astype(BF16))
    r = jnp.dot(cvo_ref[...], jnp.concatenate(ps, axis=1), preferred_element_type=F32)
    o_cmp = r[0:HEAD_DIM]
    imp_t = r[HEAD_DIM:, 0:qb]
    for h in range(1, NSA_HPG):
        imp_t = imp_t + r[HEAD_DIM:, h * qb:(h + 1) * qb]

    blk = lax.broadcasted_iota(jnp.int32, (n_sel, qb), 0).astype(F32)
    cur = ((t0 + lax.broadcasted_iota(jnp.int32, (n_sel, qb), 1)) // SLC_BLOCK).astype(F32)
    ago = cur - blk
    forced = jnp.where(blk == 0, 1.0, jnp.where(ago == 0, 1.0, jnp.where(ago == 1, 1.0, 0.0)))
    valid = ago >= 0
    score = jnp.where(valid, jnp.where(forced > 0.5, -1.0, imp_t), -1.0)
    for _ in range(max(min(SLC_TOPK, n_sel) - N_FORCED, 0)):
        mx = jnp.max(score, axis=0, keepdims=True)
        idx = jnp.min(jnp.where(score == mx, blk, float(n_sel)), axis=0, keepdims=True)
        score = jnp.where(blk == idx, -jnp.inf, score)
    sel_t = jnp.where(valid, jnp.where(score == -jnp.inf, 1.0, forced), 0.0)
    sel_ref[0:n_sel, :] = sel_t
    sel_ref[n_sel:, :] = jnp.zeros((SEL_PAD_ROWS, qb), F32)
    any_q = jnp.max(sel_t, axis=1, keepdims=True)
    bit = jnp.left_shift(1, lax.broadcasted_iota(jnp.int32, (n_sel, 1), 0) % FLAG_BITS).astype(F32)
    packed = any_q * bit
    for i in range(n_sel // FLAG_BITS):
        word = jnp.sum(packed[i * FLAG_BITS:(i + 1) * FLAG_BITS], axis=0, keepdims=True)
        flag_ref[i] = word.astype(jnp.int32)[0, 0]

    blocks_per_chunk = kc // SLC_BLOCK
    chunks_per_word = FLAG_BITS // blocks_per_chunk

    def scan_body(w, cnt):
        word = flag_ref[w]
        for j in range(chunks_per_word):
            bits = (word >> (j * blocks_per_chunk)) & ((1 << blocks_per_chunk) - 1)
            list_ref[cnt] = w * chunks_per_word + j
            cnt = cnt + (bits != 0).astype(jnp.int32)
        return cnt

    cnt = lax.fori_loop(0, (t0 + qb - 1) // (FLAG_BITS * SLC_BLOCK) + 1, scan_body, 0)
    for u in range(SLC_GROUP):
        list_ref[cnt + u] = -1

    row = lax.broadcasted_iota(jnp.int32, (kc, qb), 0)
    back = lax.broadcasted_iota(jnp.int32, (kc, qb), 1) - row
    ms_ref[...] = jnp.full(ms_ref.shape, NEG_INIT, F32)
    accs_ref[...] = jnp.zeros(accs_ref.shape, F32)

    def slc_body(gi, carry):
        keys, oks, vts = [], [], []
        for u in range(SLC_GROUP):
            c = list_ref[gi * SLC_GROUP + u]
            cc = jnp.maximum(c, 0)
            k0 = pl.multiple_of(cc * kc, kc)
            first = jnp.where(c >= 0, cc * blocks_per_chunk, n_sel)
            chosen = sel_ref[pl.ds(first, 1), :]
            for j in range(1, blocks_per_chunk):
                chosen = jnp.where(row < j * SLC_BLOCK, chosen, sel_ref[pl.ds(first + j, 1), :])
            oks.append(jnp.where(back + (t0 - k0) >= 0, chosen, 0.0))
            keys.append(ks_ref[pl.ds(k0, kc), :])
            vts.append(vst_ref[cc])
        _flash_step(_nt(jnp.concatenate(keys, axis=0), q), jnp.concatenate(oks, axis=0) > 0.5,
                    jnp.concatenate(vts, axis=1), ms_ref, accs_ref)
        return carry

    lax.fori_loop(0, (cnt + SLC_GROUP - 1) // SLC_GROUP, slc_body, 0)

    o_slc = _flash_result(accs_ref)
    gate = gate_ref[...]
    outs = []
    for h in range(NSA_HPG):
        cols = slice(h * qb, (h + 1) * qb)
        gc = [gate[j * NSA_HPG + h:j * NSA_HPG + h + 1, :] for j in range(3)]
        outs.append(gc[0] * o_cmp[:, cols] + gc[1] * o_slc[:, cols] + gc[2] * o_win[:, cols])
    o_ref[...] = jnp.concatenate(outs, axis=0).T


def _nsa_call(qfeat, qn, kc, cvo, ks, vst, kw, vwt, gates):
    B, S, _ = qn.shape
    G = NSA_GROUPS
    qb = NSA_QB
    nb = S // qb
    n_sel = S // SLC_BLOCK
    width = NSA_HPG * qb
    per_group = lambda a: pl.BlockSpec((None, None) + a.shape[2:],
                                       lambda b, g, n: (b, g) + (0,) * (a.ndim - 2))
    return pl.pallas_call(
        _nsa_kernel,
        grid=(B, G, nb),
        in_specs=[
            pl.BlockSpec((None,) + qfeat.shape[1:], lambda b, g, n: (g, 0, 0, 0)),
            pl.BlockSpec((None, qb, NSA_HPG * HEAD_DIM), lambda b, g, n: (b, n, g)),
            per_group(kc), per_group(cvo), per_group(ks), per_group(vst), per_group(kw), per_group(vwt),
            pl.BlockSpec((None, None, GATE_ROWS, qb), lambda b, g, n: (b, g, 0, n)),
        ],
        out_specs=pl.BlockSpec((None, qb, NSA_HPG * HEAD_DIM), lambda b, g, n: (b, n, g)),
        out_shape=jax.ShapeDtypeStruct((B, S, NSA_HEADS * HEAD_DIM), F32),
        scratch_shapes=[pltpu.VMEM((n_sel + SEL_PAD_ROWS, qb), F32),
                        pltpu.VMEM((1, width), F32), pltpu.VMEM((VT_ROWS, width), F32),
                        pltpu.SMEM((n_sel // FLAG_BITS,), jnp.int32),
                        pltpu.SMEM((S // KEY_CHUNK + SLC_GROUP,), jnp.int32)],
        compiler_params=pltpu.CompilerParams(
            dimension_semantics=("parallel", "parallel", "arbitrary"), vmem_limit_bytes=VMEM_LIMIT),
        name="nsa",
    )(qfeat, qn, kc, cvo, ks, vst, kw, vwt, gates)


def _dilated_kernel(qfeat_ref, q_ref, kp_ref, kc_ref, vp_ref, vc_ref, o_ref, lse_ref, *, wd):
    c = Q_BLOCK
    blocks = q_ref.shape[0] // c
    qf = q_ref[...].astype(F32)
    lane = lax.broadcasted_iota(jnp.int32, qf.shape[:1] + (LANES,), 1)
    key = lax.broadcasted_iota(jnp.int32, (2 * c, c), 0)
    qry = lax.broadcasted_iota(jnp.int32, (2 * c, c), 1)
    first = pl.program_id(2) == 0
    ok_first = jnp.where(key < c, jnp.where(first, 2 * c, qry) - (key + (wd - c)), (key - c) - qry) <= 0
    ok_later = jnp.where(key < c, qry - (key + (wd - c)), (key - c) - qry) <= 0
    outs = [[None] * DIL_HEADS for _ in range(blocks)]
    lses = [[None] * DIL_HEADS for _ in range(blocks)]
    for h in range(DIL_HEADS):
        hs = slice(h * LANES, (h + 1) * LANES)
        qh = jnp.where(lane < HEAD_DIM, _head_tile(qf, h), qfeat_ref[h][0:1, :]).astype(BF16)
        ks = [kp_ref[:, hs]] + [kc_ref[j * c:(j + 1) * c, hs] for j in range(blocks)]
        vs = [vp_ref[:, hs]] + [vc_ref[j * c:(j + 1) * c, hs] for j in range(blocks)]
        vts = [v.astype(F32).T[0:VT_ROWS].astype(BF16) for v in vs]
        for j in range(blocks):
            s = _nt(jnp.concatenate(ks[j:j + 2], axis=0), qh[j * c:(j + 1) * c])
            s = jnp.where(ok_later if j else ok_first, s, -jnp.inf)
            m = jnp.max(s, axis=0, keepdims=True)
            r = jnp.dot(jnp.concatenate(vts[j:j + 2], axis=1), jnp.exp2(s - m).astype(BF16),
                        preferred_element_type=F32)
            l = r[HEAD_DIM:HEAD_DIM + 1]
            outs[j][h] = r[0:HEAD_DIM] * (1.0 / l)
            lses[j][h] = jnp.broadcast_to(m + jnp.log2(l), (HEAD_DIM, c))
    for j in range(blocks):
        rows = slice(j * c, (j + 1) * c)
        for p in range(DIL_HEADS // 2):
            cols = slice(p * LANES, (p + 1) * LANES)
            o_ref[rows, cols] = jnp.concatenate(outs[j][2 * p:2 * p + 2], axis=0).T
            lse_ref[rows, cols] = jnp.concatenate(lses[j][2 * p:2 * p + 2], axis=0).T


def _dilated_call(qfeat, q_v, k_v, v_v, window, dilation):
    B, L, _ = q_v.shape
    per_step = min(DIL_BLOCKS_PER_STEP, L // Q_BLOCK)
    nblk = L // (Q_BLOCK * per_step)
    blk = lambda cw: pl.BlockSpec((None, Q_BLOCK * per_step, cw), lambda b, p, i: (b, i, p))
    prev = pl.BlockSpec((None, Q_BLOCK, DIL_KV_W),
                        lambda b, p, i: (b, jnp.maximum(i * per_step - 1, 0), p))
    return pl.pallas_call(
        functools.partial(_dilated_kernel, wd=window // dilation),
        grid=(B, dilation, nblk),
        in_specs=[pl.BlockSpec(qfeat.shape, lambda b, p, i: (0, 0, 0)),
                  blk(DIL_Q_W), prev, blk(DIL_KV_W), prev, blk(DIL_KV_W)],
        out_specs=(blk(DIL_Q_W), blk(DIL_Q_W)),
        out_shape=(jax.ShapeDtypeStruct((B, L, dilation * DIL_Q_W), F32),
                   jax.ShapeDtypeStruct((B, L, dilation * DIL_Q_W), F32)),
        compiler_params=pltpu.CompilerParams(
            dimension_semantics=("parallel", "parallel", "arbitrary"), vmem_limit_bytes=VMEM_LIMIT),
        name=f"dilated_d{dilation}",
    )(qfeat, q_v, k_v, k_v, v_v, v_v)


def _merge_kernel(x_ref, g_ref, wg_ref, onsa_ref, o1_ref, o2_ref, o3_ref, l1_ref, l2_ref, l3_ref,
                  wbn_ref, wbd_ref, wo_ref, fg_ref, out_ref, tok_scr, *, final):
    x = x_ref[...]
    tm = x.shape[0]
    xb = _rms(x, g_ref[...]).astype(BF16)
    cz = NSA_HEADS * HEAD_DIM
    d = x.shape[-1]

    def gate_seg(c0, c1):
        return jnp.dot(xb, wg_ref[:, c0:c1], preferred_element_type=F32)

    def token_major(ref, slot, dil):
        if dil == 1:
            return ref[...]
        tiles = DIL_Q_W // LANES
        for p in range(dil):
            for j in range(tiles):
                tok_scr[slot * tiles + j, pl.ds(p, tm // dil, stride=dil), :] = \
                    ref[:, p * DIL_Q_W + j * LANES:p * DIL_Q_W + (j + 1) * LANES]
        return jnp.concatenate([tok_scr[slot * tiles + j] for j in range(tiles)], axis=1)

    o_nsa = onsa_ref[...] * jax.nn.silu(gate_seg(0, cz))
    dils = [dil for _, dil in DIL_PATTERNS]
    o1, o2, o3 = (token_major(r, i, dil) for i, (r, dil) in enumerate(zip((o1_ref, o2_ref, o3_ref), dils)))
    l1, l2, l3 = (token_major(r, 3 + i, dil) for i, (r, dil) in enumerate(zip((l1_ref, l2_ref, l3_ref), dils)))
    mx = jnp.maximum(jnp.maximum(l1, l2), l3)
    e1, e2, e3 = jnp.exp2(l1 - mx), jnp.exp2(l2 - mx), jnp.exp2(l3 - mx)
    den = e1 + e2 + e3
    o_dil = (e1 / den) * o1 + (e2 / den) * o2 + (e3 / den) * o3
    o_dil = o_dil * jax.nn.silu(gate_seg(cz, 2 * cz))
    a = jnp.dot(o_nsa.astype(BF16), wbn_ref[...], preferred_element_type=F32)
    bd = jnp.dot(o_dil.astype(BF16), wbd_ref[...], preferred_element_type=F32)
    merged = (jax.nn.sigmoid(gate_seg(2 * cz, 2 * cz + d)) * a
              + jax.nn.sigmoid(gate_seg(2 * cz + d, 2 * cz + 2 * d)) * bd)
    y = jnp.dot(merged.astype(BF16), wo_ref[...], preferred_element_type=F32)
    out_ref[...] = _rms(x + y, fg_ref[...]) if final else x + y


def _merge_call(x, g, wg, o_nsa, o_d, lse_d, wbn, wbd, wo, fg, tm, final):
    B, S, D = x.shape
    cz = o_nsa.shape[-1]
    row = lambda c: pl.BlockSpec((None, tm, c), lambda b, i: (b, i, 0))
    full = lambda a: pl.BlockSpec(a.shape, lambda b, i: (0,) * a.ndim)
    phase = [pl.BlockSpec((None, tm // dil, dil * cz), lambda b, i: (b, i, 0)) for _, dil in DIL_PATTERNS]
    return pl.pallas_call(
        functools.partial(_merge_kernel, final=final),
        grid=(B, S // tm),
        in_specs=[row(D), full(g), full(wg), row(cz)] + phase + phase
                 + [full(wbn), full(wbd), full(wo), full(fg)],
        out_specs=row(D),
        out_shape=jax.ShapeDtypeStruct((B, S, D), F32),
        scratch_shapes=[pltpu.VMEM((2 * len(DIL_PATTERNS) * cz // LANES, tm, LANES), F32)],
        compiler_params=pltpu.CompilerParams(
            dimension_semantics=("parallel", "parallel"), vmem_limit_bytes=VMEM_LIMIT),
        name="merge",
    )(x, g, wg, o_nsa, *o_d, *lse_d, wbn, wbd, wo, fg)


def _proj_columns(d_model):
    hd, G, H = HEAD_DIM, NSA_GROUPS, NSA_HEADS
    nsa_w = H * hd
    o_q, o_kv = 0, nsa_w
    o_g = o_kv + 6 * G * hd
    o_zn = o_g + 3 * H
    o_qkvd = o_zn + nsa_w
    dil_w = DIL_HEADS * hd
    o_zd = o_qkvd + 3 * dil_w
    o_mg = o_zd + dil_w
    kv = lambda j, g: o_kv + (j * G + g) * hd + np.arange(hd)
    pad = np.full(hd, -1)
    cols = [o_q + np.arange(nsa_w)]
    cols += [kv(j, g) for j in (0, 1) for g in range(G)]
    for j in (2, 4):
        for g in range(G):
            cols += [kv(j, g), pad]
    for j in (3, 5):
        cols += [kv(j, g) for g in range(G)]
    lanes = np.full(LANES, -1)
    for g in range(G):
        for j in range(3):
            lanes[g * GATE_ROWS + j * NSA_HPG + np.arange(NSA_HPG)] = o_g + j * H + g * NSA_HPG + np.arange(NSA_HPG)
    cols.append(lanes)
    cols.append(o_qkvd + np.arange(dil_w))
    for j in (1, 2):
        for h in range(DIL_HEADS):
            cols += [o_qkvd + j * dil_w + h * hd + np.arange(hd), pad]
    proj_cols = np.concatenate(cols)
    assert proj_cols.size == _PROJ_OFFS["vd"][1]
    gate_cols = np.concatenate([o_zn + np.arange(nsa_w), o_zd + np.arange(dil_w),
                                o_mg + np.arange(2 * d_model)])
    return proj_cols, gate_cols


def _take_columns(w, cols, dtype):
    pieces, start = [], 0
    for i in range(1, len(cols) + 1):
        if i < len(cols):
            same_run = (cols[i] < 0 and cols[i - 1] < 0) or (cols[i - 1] >= 0 and cols[i] == cols[i - 1] + 1)
        if i == len(cols) or not same_run:
            n = i - start
            pieces.append(jnp.zeros((w.shape[0], n), dtype) if cols[start] < 0
                          else w[:, int(cols[start]):int(cols[start]) + n].astype(dtype))
            start = i
    return jnp.concatenate(pieces, axis=1)


def _compress_weights(pos_k, w1_k, w2_k, pos_v, w1_v, w2_v):
    hd, G, half = HEAD_DIM, NSA_GROUPS, CMP_BLOCK // 2
    eye = jnp.eye(2 * G, dtype=F32)
    w1 = jnp.stack([w1_k] * G + [w1_v] * G)
    pos = jnp.stack([pos_k] * G + [pos_v] * G)

    def first_layer(lo):
        w = jnp.einsum('slde,st->lsdte', w1[:, lo:lo + half], eye)
        p = pos[:, lo:lo + half].transpose(1, 0, 2)
        return w.reshape(half * 2 * G * hd, 2 * G * hd).astype(BF16), p.reshape(1, half * 2 * G * hd)

    wa, posa = first_layer(0)
    wb, posb = first_layer(half)
    w2 = jnp.zeros((2 * G, hd, 3 * G, hd), F32)
    for g in range(G):
        w2 = w2.at[g, :, 2 * g].set(w2_k).at[G + g, :, 2 * G + g].set(w2_v)
    return posa, posb, wa, wb, w2.reshape(2 * G * hd, 3 * G * hd).astype(BF16)


def _overlap_t(n_cmp_pad, n_sel):
    i = np.arange(n_cmp_pad)[None, :]
    j = np.arange(n_sel)[:, None]
    lo = np.maximum(i * CMP_STRIDE, j * SLC_BLOCK)
    hi = np.minimum(i * CMP_STRIDE + CMP_BLOCK, (j + 1) * SLC_BLOCK)
    return (np.clip(hi - lo, 0, None) / CMP_BLOCK).astype(np.float32)


def kernel(x, norm_g, w_in, cmp_pos_k, cmp_w1_k, cmp_w2_k, cmp_pos_v, cmp_w1_v, cmp_w2_v,
           w_br_nsa, w_br_dil, w_out, final_g):
    B, S, D = x.shape
    n_sel = S // SLC_BLOCK
    assert S % (Q_BLOCK * max(d for _, d in DIL_PATTERNS)) == 0 and n_sel % FLAG_BITS == 0
    n_cmp_pad = S // CMP_STRIDE
    slopes_nsa, slopes_dil = _alibi_slopes()
    proj_cols, gate_cols = _proj_columns(D)
    ovt = jnp.asarray(_overlap_t(n_cmp_pad, n_sel))
    qfeat = jnp.asarray(np.broadcast_to(
        _slope_features(slopes_nsa).reshape(NSA_GROUPS, NSA_HPG, 1, LANES),
        (NSA_GROUPS, NSA_HPG, 8, LANES)))
    qfeat_dil = jnp.asarray(np.broadcast_to(
        _slope_features(slopes_dil).reshape(DIL_HEADS, 1, LANES), (DIL_HEADS, 8, LANES)))
    tm = 512

    h = x
    for layer in range(w_in.shape[0]):
        w_proj = _take_columns(w_in[layer], proj_cols, BF16)
        w_gate = _take_columns(w_in[layer], gate_cols, BF16)
        g_l = norm_g[layer].reshape(1, D)

        qn, cmp_src, ks, kw, vst, vwt, gates, *dil = _proj_call(h, g_l, w_proj, tm)
        cw = _compress_weights(cmp_pos_k[layer], cmp_w1_k[layer], cmp_w2_k[layer],
                               cmp_pos_v[layer], cmp_w1_v[layer], cmp_w2_v[layer])
        kc, cvo = _compress_call(cmp_src, *cw, ovt)
        o_nsa = _nsa_call(qfeat, qn, kc, cvo, ks, vst, kw, vwt, gates)
        o_d, lse_d = [], []
        for i, (window, dilation) in enumerate(DIL_PATTERNS):
            assert Q_BLOCK - 1 <= window // dilation <= Q_BLOCK
            o, lse = _dilated_call(qfeat_dil, *dil[3 * i:3 * i + 3], window, dilation)
            o_d.append(o)
            lse_d.append(lse)
        h = _merge_call(h, g_l, w_gate, o_nsa, o_d, lse_d, w_br_nsa[layer].astype(BF16),
                        w_br_dil[layer].astype(BF16), w_out[layer].astype(BF16),
                        final_g.reshape(1, D), tm, final=layer == w_in.shape[0] - 1)
    return h
```

```python
import functools
import math

import numpy as np
import jax
import jax.numpy as jnp
from jax import lax
from jax.experimental import pallas as pl
from jax.experimental.pallas import tpu as pltpu

F32 = jnp.float32
BF16 = jnp.bfloat16

HEAD_DIM = 64
LANES = 128
NSA_HEADS = 8
NSA_GROUPS = 2
NSA_HPG = NSA_HEADS // NSA_GROUPS
CMP_BLOCK = 32
CMP_STRIDE = 16
SLC_BLOCK = 64
SLC_TOPK = 16
WIN = 512
FORCE_SCORE = 1.0e4
DIL_HEADS = 8
DIL_PATTERNS = ((128, 1), (512, 4), (2048, 16))
Q_BLOCK = 128
RMS_EPS = 1e-6
KEY_CHUNK = 128
VT_ROWS = 80
PAIR_VT_ROWS = 144
GATE_ROWS = 16
FEAT_SPLIT = 3
FLAG_BITS = 16
SLC_GROUP = 4
NSA_QB = 256
DIL_BLOCKS_PER_STEP = 4
N_FORCED = 3
assert FORCE_SCORE > NSA_HPG + 1
SEL_PAD_ROWS = 8
NEG_INIT = -1.0e30
LOG2E = math.log2(math.e)
VMEM_LIMIT = 56 * 1024 * 1024


def _alibi_slopes():
    n = NSA_HEADS + DIL_HEADS
    s = 2.0 ** (-8.0 * np.arange(1, n + 1) / n)
    return s[0::2].astype(np.float32), s[1::2].astype(np.float32)


def _nt(a, b):
    return lax.dot_general(a, b, (((1,), (1,)), ((), ())), preferred_element_type=F32)


def _rms(x, g):
    return (x * lax.rsqrt(jnp.mean(x * x, axis=-1, keepdims=True) + RMS_EPS)) * g


def _head_tile(q_pairs, h):
    tile = q_pairs[:, (h // 2) * LANES:(h // 2 + 1) * LANES]
    return pltpu.roll(tile, HEAD_DIM, 1) if h % 2 else tile


def _pad_heads(q_pairs, n_heads):
    lane = lax.broadcasted_iota(jnp.int32, (q_pairs.shape[0], LANES), 1)
    return [jnp.where(lane < HEAD_DIM, _head_tile(q_pairs, h), 0.0) for h in range(n_heads)]


def _compact_pair(a_pad, b_pad):
    lane = lax.broadcasted_iota(jnp.int32, a_pad.shape, 1)
    return jnp.where(lane < HEAD_DIM, pltpu.roll(a_pad, HEAD_DIM, 1), b_pad)


def _pos_features(pos):
    lane = lax.broadcasted_iota(jnp.int32, pos.shape, 1) - HEAD_DIM
    hi = (pos // LANES).astype(F32)
    lo = (pos % LANES).astype(F32)
    return jnp.where(lane < 0, 0.0, jnp.where(lane < FEAT_SPLIT, hi, jnp.where(lane < 2 * FEAT_SPLIT, lo, 0.0)))


def _slope_features(slopes):
    out = np.zeros((len(slopes), LANES), np.float32)
    for h, s in enumerate(slopes):
        rest = np.float32(s) * np.float32(LOG2E)
        for i in range(FEAT_SPLIT):
            piece = np.float32(np.asarray(rest, np.float32).astype(BF16))
            out[h, HEAD_DIM + FEAT_SPLIT + i] = piece
            out[h, HEAD_DIM + i] = piece * LANES
            rest = np.float32(rest - piece)
    return out


_PROJ_SEGS = (("qn", 512), ("cmp", 256), ("ks", 256), ("kw", 256), ("vs", 128), ("vw", 128),
              ("gate", 128), ("qd", 512), ("kd", 512), ("vd", 512))
DIL_Q_W = DIL_HEADS * HEAD_DIM
_PROJ_OFFS = dict(zip((n for n, _ in _PROJ_SEGS),
                      zip(np.cumsum([0] + [w for _, w in _PROJ_SEGS])[:-1].tolist(),
                          np.cumsum([w for _, w in _PROJ_SEGS]).tolist())))


def _to_lane_tiles(scr, first, val):
    for j in range(val.shape[1] // LANES):
        scr[first + j] = val[:, j * LANES:(j + 1) * LANES]


def _regroup_rows(scr, lo, hi, out_ref, d, dtype):
    rows, width = scr.shape[1], (hi - lo) * LANES
    for p in range(d):
        for j in range(hi - lo):
            tile = scr[lo + j, pl.ds(p, rows // d, stride=d), :] if d > 1 else scr[lo + j]
            out_ref[:, p * width + j * LANES:p * width + (j + 1) * LANES] = tile.astype(dtype)


def _proj_kernel(x_ref, g_ref, w_ref, perm_ref, qn_ref, cmp_ref, ks_ref, kw_ref, vst_ref, vwt_ref,
                 gate_ref, *dil_refs):
    cmp_scr = dil_refs[-1]
    tm = x_ref.shape[0]
    xb = _rms(x_ref[...], g_ref[...]).astype(BF16)
    scale = HEAD_DIM ** -0.5

    def seg(name):
        c0, c1 = _PROJ_OFFS[name]
        return jnp.dot(xb, w_ref[:, c0:c1], preferred_element_type=F32)

    qn_ref[...] = (seg("qn") * (scale * LOG2E)).astype(BF16)
    _to_lane_tiles(cmp_scr, 0, seg("cmp"))
    _regroup_rows(cmp_scr, 0, cmp_scr.shape[0], cmp_ref, CMP_STRIDE, F32)
    feat = _pos_features(pl.program_id(1) * tm + lax.broadcasted_iota(jnp.int32, (tm, LANES), 0))
    ks, kw = seg("ks"), seg("kw")
    vst, vwt = seg("vs").T, seg("vw").T
    gate_t = jax.nn.sigmoid(seg("gate")).T
    tail_row = lax.broadcasted_iota(jnp.int32, (VT_ROWS - HEAD_DIM, tm), 0)
    tail = jnp.where(tail_row == 0, 1.0, 0.0)
    for g in range(NSA_GROUPS):
        ks_ref[g] = (ks[:, g * LANES:(g + 1) * LANES] + feat).astype(BF16)
        kw_ref[g] = (kw[:, g * LANES:(g + 1) * LANES] + feat).astype(BF16)
        gate_ref[g] = gate_t[g * GATE_ROWS:(g + 1) * GATE_ROWS]
        for src, dst in ((vst, vst_ref), (vwt, vwt_ref)):
            vt = jnp.concatenate([src[g * HEAD_DIM:(g + 1) * HEAD_DIM], tail], axis=0).astype(BF16)
            for j in range(tm // KEY_CHUNK):
                dst[g, j] = vt[:, j * KEY_CHUNK:(j + 1) * KEY_CHUNK]
    qkv = jnp.concatenate([(seg("qd") * (scale * LOG2E)).astype(BF16), seg("kd").astype(BF16),
                           seg("vd").astype(BF16)], axis=1)
    bounds = (0, DIL_Q_W, 2 * DIL_Q_W, 3 * DIL_Q_W)
    for i, (_, d) in enumerate(DIL_PATTERNS):
        by_phase = qkv if d == 1 else jnp.dot(perm_ref[i], qkv, preferred_element_type=F32).astype(BF16)
        for p in range(d):
            rows = by_phase[p * (tm // d):(p + 1) * (tm // d)]
            for j, ref in enumerate(dil_refs[3 * i:3 * i + 3]):
                width = bounds[j + 1] - bounds[j]
                ref[:, p * width:(p + 1) * width] = rows[:, bounds[j]:bounds[j + 1]]


def _proj_call(x, g, w, tm):
    B, S, D = x.shape
    G = NSA_GROUPS
    cmp_w = _PROJ_OFFS["cmp"][1] - _PROJ_OFFS["cmp"][0]
    row = lambda c: pl.BlockSpec((None, tm, c), lambda b, i: (b, i, 0))
    k_spec =pl.BlockSpec((None, G, tm, LANES), lambda b, i: (b, 0, i, 0))
    vt_spec = pl.BlockSpec((None, G, tm // KEY_CHUNK, VT_ROWS, KEY_CHUNK), lambda b, i: (b, 0, i, 0, 0))
    out_shape = (
        jax.ShapeDtypeStruct((B, S, 512), BF16),
        jax.ShapeDtypeStruct((B, S // CMP_STRIDE, CMP_STRIDE * cmp_w), F32),
        jax.ShapeDtypeStruct((B, G, S, LANES), BF16),
        jax.ShapeDtypeStruct((B, G, S, LANES), BF16),
        jax.ShapeDtypeStruct((B, G, S // KEY_CHUNK, VT_ROWS, KEY_CHUNK), BF16),
        jax.ShapeDtypeStruct((B, G, S // KEY_CHUNK, VT_ROWS, KEY_CHUNK), BF16),
        jax.ShapeDtypeStruct((B, G, GATE_ROWS, S), F32),
    )
    dil_specs = ()
    for _, d in DIL_PATTERNS:
        for width in (DIL_Q_W,) * 3:
            out_shape += (jax.ShapeDtypeStruct((B, S // d, d * width), BF16),)
            dil_specs += (pl.BlockSpec((None, tm // d, d * width), lambda b, i: (b, i, 0)),)
    r = np.arange(tm)
    perm = np.zeros((len(DIL_PATTERNS), tm, tm), np.float32)
    for i, (_, d) in enumerate(DIL_PATTERNS):
        perm[i, (r % d) * (tm // d) + r // d, r] = 1.0
    perm = jnp.asarray(perm, BF16)
    return pl.pallas_call(
        _proj_kernel,
        grid=(B, S // tm),
        in_specs=[row(D), pl.BlockSpec((1, D), lambda b, i: (0, 0)),
                  pl.BlockSpec(w.shape, lambda b, i: (0, 0)),
                  pl.BlockSpec(perm.shape, lambda b, i: (0, 0, 0))],
        out_specs=(row(512),
                   pl.BlockSpec((None, tm // CMP_STRIDE, CMP_STRIDE * cmp_w), lambda b, i: (b, i, 0)),
                   k_spec, k_spec, vt_spec, vt_spec,
                   pl.BlockSpec((None, G, GATE_ROWS, tm), lambda b, i: (b, 0, 0, i))) + dil_specs,
        out_shape=out_shape,
        scratch_shapes=[pltpu.VMEM((cmp_w // LANES, tm, LANES), F32)],
        compiler_params=pltpu.CompilerParams(
            dimension_semantics=("parallel", "parallel"), vmem_limit_bytes=VMEM_LIMIT),
        name="proj",
    )(x, g, w, perm)


def _compress_kernel(r_ref, posa_ref, posb_ref, wa_ref, wb_ref, w2_ref, ovt_ref, kc_ref, cvo_ref):
    r = r_ref[...]
    n = r.shape[0]
    ha = jnp.dot((r + posa_ref[...]).astype(BF16), wa_ref[...], preferred_element_type=F32)
    hb = jnp.dot((r + posb_ref[...]).astype(BF16), wb_ref[...], preferred_element_type=F32)
    hid = jax.nn.gelu(ha + pltpu.roll(hb, n - 1, 0))
    out = jnp.dot(hid.astype(BF16), w2_ref[...], preferred_element_type=F32)
    cmp_end = lax.broadcasted_iota(jnp.int32, (n, LANES), 0) * CMP_STRIDE + (CMP_BLOCK - 1)
    feat = _pos_features(cmp_end)
    v_t = out[:, NSA_GROUPS * LANES:].T
    for g in range(NSA_GROUPS):
        kc_ref[g] = (out[:, g * LANES:(g + 1) * LANES] + feat).astype(BF16)
        cvo_ref[g] = jnp.concatenate([v_t[g * HEAD_DIM:(g + 1) * HEAD_DIM], ovt_ref[...]],
                                     axis=0).astype(BF16)


def _compress_call(r, posa, posb, wa, wb, w2, ovt):
    B, R, C = r.shape
    G = NSA_GROUPS
    rows = HEAD_DIM + ovt.shape[0]
    full = lambda a: pl.BlockSpec(a.shape, lambda b: (0,) * a.ndim)
    return pl.pallas_call(
        _compress_kernel,
        grid=(B,),
        in_specs=[pl.BlockSpec((None, R, C), lambda b: (b, 0, 0)),
                  full(posa), full(posb), full(wa), full(wb), full(w2), full(ovt)],
        out_specs=(pl.BlockSpec((None, G, R, LANES), lambda b: (b, 0, 0, 0)),
                   pl.BlockSpec((None, G, rows, R), lambda b: (b, 0, 0, 0))),
        out_shape=(jax.ShapeDtypeStruct((B, G, R, LANES), BF16),
                   jax.ShapeDtypeStruct((B, G, rows, R), BF16)),
        compiler_params=pltpu.CompilerParams(
            dimension_semantics=("parallel",), vmem_limit_bytes=VMEM_LIMIT),
        name="compress",
    )(r, posa, posb, wa, wb, w2, ovt)


def _flash_step(s, ok, v_t, m_ref, acc_ref):
    ps, alphas = [], []
    qb = ok.shape[1]
    for h in range(NSA_HPG):
        cols = slice(h * qb, (h + 1) * qb)
        sh = jnp.where(ok, s[:, cols], -jnp.inf)
        m_old = m_ref[:, cols]
        m_new = jnp.maximum(m_old, jnp.max(sh, axis=0, keepdims=True))
        alphas.append(jnp.exp2(m_old - m_new))
        ps.append(jnp.exp2(sh - m_new).astype(BF16))
        m_ref[:, cols] = m_new
    pv = jnp.dot(v_t, jnp.concatenate(ps, axis=1), preferred_element_type=F32)
    acc_ref[...] = jnp.concatenate(alphas, axis=1) * acc_ref[...] + pv


def _flash_result(acc_ref):
    l = acc_ref[HEAD_DIM:HEAD_DIM + 1, :]
    return acc_ref[0:HEAD_DIM, :] * (1.0 / jnp.maximum(l, 1e-30))


def _softmax_pv(s, ok, v_t):
    ps = []
    qb = ok.shape[1]
    for h in range(NSA_HPG):
        sh = jnp.where(ok, s[:, h * qb:(h + 1) * qb], -jnp.inf)
        ps.append(jnp.exp2(sh - jnp.max(sh, axis=0, keepdims=True)).astype(BF16))
    r = jnp.dot(v_t, jnp.concatenate(ps, axis=1), preferred_element_type=F32)
    return r[0:HEAD_DIM] * (1.0 / jnp.maximum(r[HEAD_DIM:HEAD_DIM + 1], 1e-30))


def _nsa_kernel(qfeat_ref, q_ref, kc_ref, cvo_ref, ks_ref, vst_ref, kw_ref, vwt_ref, gate_ref,
                o_ref, sel_ref, ms_ref, accs_ref, flag_ref, list_ref):
    n = pl.program_id(2)
    qb, kc = NSA_QB, KEY_CHUNK
    t0 = n * qb
    lane = lax.broadcasted_iota(jnp.int32, (qb, LANES), 1)
    qf = q_ref[...].astype(F32)
    q = jnp.concatenate(
        [jnp.where(lane < HEAD_DIM, _head_tile(qf, h), qfeat_ref[h][0:1, :]) for h in range(NSA_HPG)],
        axis=0).astype(BF16)

    span = WIN + qb
    k_lo = pl.multiple_of(jnp.maximum(t0 - WIN, 0), kc)
    dist_w = ((t0 - k_lo) + lax.broadcasted_iota(jnp.int32, (span, qb), 1)
              - lax.broadcasted_iota(jnp.int32, (span, qb), 0))
    vt_w = jnp.concatenate([vwt_ref[k_lo // kc + u] for u in range(span // kc)], axis=1)
    o_win = _softmax_pv(_nt(kw_ref[pl.ds(k_lo, span), :], q), (dist_w >= 0) & (dist_w < WIN), vt_w)

    n_cmp = kc_ref.shape[0]
    n_sel = sel_ref.shape[0] - SEL_PAD_ROWS
    s = _nt(kc_ref[...], q)
    cmp_end = lax.broadcasted_iota(jnp.int32, (n_cmp, qb), 0) * CMP_STRIDE + (CMP_BLOCK - 1)
    ok_c = cmp_end <= t0 + lax.broadcasted_iota(jnp.int32, (n_cmp, qb), 1)
    ps = []
    for h in range(NSA_HPG):
        sh = jnp.where(ok_c, s[:, h * qb:(h + 1) * qb], -jnp.inf)
        m = jnp.max(sh, axis=0, keepdims=True)
        m = jnp.where(m == -jnp.inf, 0.0, m)
        e = jnp.exp2(sh - m)
        l = jnp.sum(e, axis=0, keepdims=True)
        ps.append((e * (1.0 / jnp.maximum(l, 1e-30))).astype(BF16))
    r = jnp.dot(cvo_ref[...], jnp.concatenate(ps, axis=1), preferred_element_type=F32)
    o_cmp = r[0:HEAD_DIM]
    imp_t = r[HEAD_DIM:, 0:qb]
    for h in range(1, NSA_HPG):
        imp_t = imp_t + r[HEAD_DIM:, h * qb:(h + 1) * qb]

    blk = lax.broadcasted_iota(jnp.int32, (n_sel, qb), 0).astype(F32)
    cur = ((t0 + lax.broadcasted_iota(jnp.int32, (n_sel, qb), 1)) // SLC_BLOCK).astype(F32)
    ago = cur - blk
    forced = jnp.where(blk == 0, 1.0, jnp.where(ago == 0, 1.0, jnp.where(ago == 1, 1.0, 0.0)))
    valid = ago >= 0
    score = jnp.where(valid, jnp.where(forced > 0.5, -1.0, imp_t), -1.0)
    for _ in range(max(min(SLC_TOPK, n_sel) - N_FORCED, 0)):
        mx = jnp.max(score, axis=0, keepdims=True)
        idx = jnp.min(jnp.where(score == mx, blk, float(n_sel)), axis=0, keepdims=True)
        score = jnp.where(blk == idx, -jnp.inf, score)
    sel_t = jnp.where(valid, jnp.where(score == -jnp.inf, 1.0, forced), 0.0)
    sel_ref[0:n_sel, :] = sel_t
    sel_ref[n_sel:, :] = jnp.zeros((SEL_PAD_ROWS, qb), F32)
    any_q = jnp.max(sel_t, axis=1, keepdims=True)
    bit = jnp.left_shift(1, lax.broadcasted_iota(jnp.int32, (n_sel, 1), 0) % FLAG_BITS).astype(F32)
    packed = any_q * bit
    for i in range(n_sel // FLAG_BITS):
        word = jnp.sum(packed[i * FLAG_BITS:(i + 1) * FLAG_BITS], axis=0, keepdims=True)
        flag_ref[i] = word.astype(jnp.int32)[0, 0]

    blocks_per_chunk = kc // SLC_BLOCK
    chunks_per_word = FLAG_BITS // blocks_per_chunk

    def scan_body(w, cnt):
        word = flag_ref[w]
        for j in range(chunks_per_word):
            bits = (word >> (j * blocks_per_chunk)) & ((1 << blocks_per_chunk) - 1)
            list_ref[cnt] = w * chunks_per_word + j
            cnt = cnt + (bits != 0).astype(jnp.int32)
        return cnt

    cnt = lax.fori_loop(0, (t0 + qb - 1) // (FLAG_BITS * SLC_BLOCK) + 1, scan_body, 0)
    for u in range(SLC_GROUP):
        list_ref[cnt + u] = -1

    row = lax.broadcasted_iota(jnp.int32, (kc, qb), 0)
    back = lax.broadcasted_iota(jnp.int32, (kc, qb), 1) - row
    ms_ref[...] = jnp.full(ms_ref.shape, NEG_INIT, F32)
    accs_ref[...] = jnp.zeros(accs_ref.shape, F32)

    def slc_body(gi, carry):
        keys, oks, vts = [], [], []
        for u in range(SLC_GROUP):
            c = list_ref[gi * SLC_GROUP + u]
            cc = jnp.maximum(c, 0)
            k0 = pl.multiple_of(cc * kc, kc)
            first = jnp.where(c >= 0, cc * blocks_per_chunk, n_sel)
            chosen = sel_ref[pl.ds(first, 1), :]
            for j in range(1, blocks_per_chunk):
                chosen = jnp.where(row < j * SLC_BLOCK, chosen, sel_ref[pl.ds(first + j, 1), :])
            oks.append(jnp.where(back + (t0 - k0) >= 0, chosen, 0.0))
            keys.append(ks_ref[pl.ds(k0, kc), :])
            vts.append(vst_ref[cc])
        _flash_step(_nt(jnp.concatenate(keys, axis=0), q), jnp.concatenate(oks, axis=0) > 0.5,
                    jnp.concatenate(vts, axis=1), ms_ref, accs_ref)
        return carry

    lax.fori_loop(0, (cnt + SLC_GROUP - 1) // SLC_GROUP, slc_body, 0)

    o_slc = _flash_result(accs_ref)
    gate = gate_ref[...]
    outs = []
    for h in range(NSA_HPG):
        cols = slice(h * qb, (h + 1) * qb)
        gc = [gate[j * NSA_HPG + h:j * NSA_HPG + h + 1, :] for j in range(3)]
        outs.append(gc[0] * o_cmp[:, cols] + gc[1] * o_slc[:, cols] + gc[2] * o_win[:, cols])
    o_ref[...] = jnp.concatenate(outs, axis=0).T


def _nsa_call(qfeat, qn, kc, cvo, ks, vst, kw, vwt, gates):
    B, S, _ = qn.shape
    G = NSA_GROUPS
    qb = NSA_QB
    nb = S // qb
    n_sel = S // SLC_BLOCK
    width = NSA_HPG * qb
    per_group = lambda a: pl.BlockSpec((None, None) + a.shape[2:],
                                       lambda b, g, n: (b, g) + (0,) * (a.ndim - 2))
    return pl.pallas_call(
        _nsa_kernel,
        grid=(B, G, nb),
        in_specs=[
            pl.BlockSpec((None,) + qfeat.shape[1:], lambda b, g, n: (g, 0, 0, 0)),
            pl.BlockSpec((None, qb, NSA_HPG * HEAD_DIM), lambda b, g, n: (b, n, g)),
            per_group(kc), per_group(cvo), per_group(ks), per_group(vst), per_group(kw), per_group(vwt),
            pl.BlockSpec((None, None, GATE_ROWS, qb), lambda b, g, n: (b, g, 0, n)),
        ],
        out_specs=pl.BlockSpec((None, qb, NSA_HPG * HEAD_DIM), lambda b, g, n: (b, n, g)),
        out_shape=jax.ShapeDtypeStruct((B, S, NSA_HEADS * HEAD_DIM), F32),
        scratch_shapes=[pltpu.VMEM((n_sel + SEL_PAD_ROWS, qb), F32),
                        pltpu.VMEM((1, width), F32), pltpu.VMEM((VT_ROWS, width), F32),
                        pltpu.SMEM((n_sel // FLAG_BITS,), jnp.int32),
                        pltpu.SMEM((S // KEY_CHUNK + SLC_GROUP,), jnp.int32)],
        compiler_params=pltpu.CompilerParams(
            dimension_semantics=("parallel", "parallel", "arbitrary"), vmem_limit_bytes=VMEM_LIMIT),
        name="nsa",
    )(qfeat, qn, kc, cvo, ks, vst, kw, vwt, gates)


def _dilated_kernel(qfeat_ref, q_ref, kp_ref, kc_ref, vp_ref, vc_ref, o_ref, lse_ref, *, wd, dilation):
    c = Q_BLOCK
    hd = HEAD_DIM
    blocks = q_ref.shape[0] // c
    lane = lax.broadcasted_iota(jnp.int32, (blocks * c, LANES), 1)
    key = lax.broadcasted_iota(jnp.int32, (2 * c, c), 0)
    qry = lax.broadcasted_iota(jnp.int32, (2 * c, c), 1)
    first = pl.program_id(2) == 0
    off_later = jnp.where(key < c, qry - (key + (wd - c)), (key - c) - qry)
    off_first = jnp.where(key < c, jnp.where(first, 2 * c, qry) - (key + (wd - c)), (key - c) - qry)
    bias_later = jnp.where(off_later <= 0, 0.0, -jnp.inf)
    bias_first = jnp.where(off_first <= 0, 0.0, -jnp.inf)
    sub = (pl.program_id(2) * blocks - 1) * c + lax.broadcasted_iota(jnp.int32, ((blocks + 1) * c, LANES), 0)
    pos_feat = _pos_features(sub * dilation + pl.program_id(1)).astype(BF16)
    tail_row = lax.broadcasted_iota(jnp.int32, (PAIR_VT_ROWS - 2 * hd, 2 * c), 0)
    tail = jnp.where(tail_row == 0, 1.0, 0.0).astype(BF16)
    for p in range(DIL_HEADS // 2):
        cols = slice(p * LANES, (p + 1) * LANES)
        q_pair = q_ref[:, cols]
        zero = jnp.zeros_like(q_pair)
        feats = [jnp.broadcast_to(qfeat_ref[2 * p + i][0:1, :], (blocks * c, LANES)).astype(BF16)
                 for i in range(2)]
        q_a = jnp.concatenate([jnp.where(lane < hd, q_pair, zero), feats[0]], axis=1)
        q_b = jnp.concatenate([jnp.where(lane < hd, zero, q_pair), feats[1]], axis=1)
        k_all = jnp.concatenate([kp_ref[:, cols], kc_ref[:, cols]], axis=0)
        k_all = jnp.concatenate([k_all, pos_feat], axis=1)
        vts = [vp_ref[:, cols].T] + [vc_ref[j * c:(j + 1) * c, cols].T for j in range(blocks)]
        for j in range(blocks):
            rows = slice(j * c, (j + 1) * c)
            q_aug = jnp.concatenate([q_a[rows], q_b[rows]], axis=0)
            s = _nt(k_all[j * c:(j + 2) * c], q_aug)
            bias = bias_later if j else bias_first
            ms, ps = [], []
            for i in range(2):
                si = s[:, i * c:(i + 1) * c] + bias
                ms.append(jnp.max(si, axis=0, keepdims=True))
                ps.append(jnp.exp2(si - ms[i]).astype(BF16))
            v_t = jnp.concatenate([jnp.concatenate(vts[j:j + 2], axis=1), tail], axis=0)
            r = jnp.dot(v_t, jnp.concatenate(ps, axis=1), preferred_element_type=F32)
            outs, lses = [], []
            for i in range(2):
                l = r[2 * hd:2 * hd + 1, i * c:(i + 1) * c]
                outs.append(r[i * hd:(i + 1) * hd, i * c:(i + 1) * c] * (1.0 / l))
                lses.append(jnp.broadcast_to(ms[i] + jnp.log2(l), (hd, c)))
            o_ref[rows, cols] = jnp.concatenate(outs, axis=0).T
            lse_ref[rows, cols] = jnp.concatenate(lses, axis=0).T


def _dilated_call(qfeat, q_v, k_v, v_v, window, dilation):
    B, L, _ = q_v.shape
    per_step = min(DIL_BLOCKS_PER_STEP, L // Q_BLOCK)
    nblk = L // (Q_BLOCK * per_step)
    blk = lambda cw: pl.BlockSpec((None, Q_BLOCK * per_step, cw), lambda b, p, i: (b, i, p))
    prev = pl.BlockSpec((None, Q_BLOCK, DIL_Q_W),
                        lambda b, p, i: (b, jnp.maximum(i * per_step - 1, 0), p))
    return pl.pallas_call(
        functools.partial(_dilated_kernel, wd=window // dilation, dilation=dilation),
        grid=(B, dilation, nblk),
        in_specs=[pl.BlockSpec(qfeat.shape, lambda b, p, i: (0, 0, 0)),
                  blk(DIL_Q_W), prev, blk(DIL_Q_W), prev, blk(DIL_Q_W)],
        out_specs=(blk(DIL_Q_W), blk(DIL_Q_W)),
        out_shape=(jax.ShapeDtypeStruct((B, L, dilation * DIL_Q_W), F32),
                   jax.ShapeDtypeStruct((B, L, dilation * DIL_Q_W), F32)),
        compiler_params=pltpu.CompilerParams(
            dimension_semantics=("parallel", "parallel", "arbitrary"), vmem_limit_bytes=VMEM_LIMIT),
        name=f"dilated_d{dilation}",
    )(qfeat, q_v, k_v, k_v, v_v, v_v)


def _merge_kernel(x_ref, g_ref, wg_ref, onsa_ref, o1_ref, o2_ref, o3_ref, l1_ref, l2_ref, l3_ref,
                  wbn_ref, wbd_ref, wo_ref, fg_ref, out_ref, tok_scr, *, final):
    x = x_ref[...]
    tm = x.shape[0]
    xb = _rms(x, g_ref[...]).astype(BF16)
    cz = NSA_HEADS * HEAD_DIM
    d = x.shape[-1]

    def gate_seg(c0, c1):
        return jnp.dot(xb, wg_ref[:, c0:c1], preferred_element_type=F32)

    def token_major(ref, slot, dil):
        if dil == 1:
            return ref[...]
        tiles = DIL_Q_W // LANES
        for p in range(dil):
            for j in range(tiles):
                tok_scr[slot * tiles + j, pl.ds(p, tm // dil, stride=dil), :] = \
                    ref[:, p * DIL_Q_W + j * LANES:p * DIL_Q_W + (j + 1) * LANES]
        return jnp.concatenate([tok_scr[slot * tiles + j] for j in range(tiles)], axis=1)

    o_nsa = onsa_ref[...] * jax.nn.silu(gate_seg(0, cz))
    dils = [dil for _, dil in DIL_PATTERNS]
    o1, o2, o3 = (token_major(r, i, dil) for i, (r, dil) in enumerate(zip((o1_ref, o2_ref, o3_ref), dils)))
    l1, l2, l3 = (token_major(r, 3 + i, dil) for i, (r, dil) in enumerate(zip((l1_ref, l2_ref, l3_ref), dils)))
    mx = jnp.maximum(jnp.maximum(l1, l2), l3)
    e1, e2, e3 = jnp.exp2(l1 - mx), jnp.exp2(l2 - mx), jnp.exp2(l3 - mx)
    den = e1 + e2 + e3
    o_dil = (e1 / den) * o1 + (e2 / den) * o2 + (e3 / den) * o3
    o_dil = o_dil * jax.nn.silu(gate_seg(cz, 2 * cz))
    a = jnp.dot(o_nsa.astype(BF16), wbn_ref[...], preferred_element_type=F32)
    bd = jnp.dot(o_dil.astype(BF16), wbd_ref[...], preferred_element_type=F32)
    merged = (jax.nn.sigmoid(gate_seg(2 * cz, 2 * cz + d)) * a
              + jax.nn.sigmoid(gate_seg(2 * cz + d, 2 * cz + 2 * d)) * bd)
    y = jnp.dot(merged.astype(BF16), wo_ref[...], preferred_element_type=F32)
    out_ref[...] = _rms(x + y, fg_ref[...]) if final else x + y


def _merge_call(x, g, wg, o_nsa, o_d, lse_d, wbn, wbd, wo, fg, tm, final):
    B, S, D = x.shape
    cz = o_nsa.shape[-1]
    row = lambda c: pl.BlockSpec((None, tm, c), lambda b, i: (b, i, 0))
    full = lambda a: pl.BlockSpec(a.shape, lambda b, i: (0,) * a.ndim)
    phase = [pl.BlockSpec((None, tm // dil, dil * cz), lambda b, i: (b, i, 0)) for _, dil in DIL_PATTERNS]
    return pl.pallas_call(
        functools.partial(_merge_kernel, final=final),
        grid=(B, S // tm),
        in_specs=[row(D), full(g), full(wg), row(cz)] + phase + phase
                 + [full(wbn), full(wbd), full(wo), full(fg)],
        out_specs=row(D),
        out_shape=jax.ShapeDtypeStruct((B, S, D), F32),
        scratch_shapes=[pltpu.VMEM((2 * len(DIL_PATTERNS) * cz // LANES, tm, LANES), F32)],
        compiler_params=pltpu.CompilerParams(
            dimension_semantics=("parallel", "parallel"), vmem_limit_bytes=VMEM_LIMIT),
        name="merge",
    )(x, g, wg, o_nsa, *o_d, *lse_d, wbn, wbd, wo, fg)


def _proj_columns(d_model):
    hd, G, H = HEAD_DIM, NSA_GROUPS, NSA_HEADS
    nsa_w = H * hd
    o_q, o_kv = 0, nsa_w
    o_g = o_kv + 6 * G * hd
    o_zn = o_g + 3 * H
    o_qkvd = o_zn + nsa_w
    dil_w = DIL_HEADS * hd
    o_zd = o_qkvd + 3 * dil_w
    o_mg = o_zd + dil_w
    kv = lambda j, g: o_kv + (j * G + g) * hd + np.arange(hd)
    pad = np.full(hd, -1)
    cols = [o_q + np.arange(nsa_w)]
    cols += [kv(j, g) for j in (0, 1) for g in range(G)]
    for j in (2, 4):
        for g in range(G):
            cols += [kv(j, g), pad]
    for j in (3, 5):
        cols += [kv(j, g) for g in range(G)]
    lanes = np.full(LANES, -1)
    for g in range(G):
        for j in range(3):
            lanes[g * GATE_ROWS + j * NSA_HPG + np.arange(NSA_HPG)] = o_g + j * H + g * NSA_HPG + np.arange(NSA_HPG)
    cols.append(lanes)
    cols.append(o_qkvd + np.arange(3 * dil_w))
    proj_cols = np.concatenate(cols)
    assert proj_cols.size == _PROJ_OFFS["vd"][1]
    gate_cols = np.concatenate([o_zn + np.arange(nsa_w), o_zd + np.arange(dil_w),
                                o_mg + np.arange(2 * d_model)])
    return proj_cols, gate_cols


def _take_columns(w, cols, dtype):
    pieces, start = [], 0
    for i in range(1, len(cols) + 1):
        if i < len(cols):
            same_run = (cols[i] < 0 and cols[i - 1] < 0) or (cols[i - 1] >= 0 and cols[i] == cols[i - 1] + 1)
        if i == len(cols) or not same_run:
            n = i - start
            pieces.append(jnp.zeros((w.shape[0], n), dtype) if cols[start] < 0
                          else w[:, int(cols[start]):int(cols[start]) + n].astype(dtype))
            start = i
    return jnp.concatenate(pieces, axis=1)


def _compress_weights(pos_k, w1_k, w2_k, pos_v, w1_v, w2_v):
    hd, G, half = HEAD_DIM, NSA_GROUPS, CMP_BLOCK // 2
    eye = jnp.eye(2 * G, dtype=F32)
    w1 = jnp.stack([w1_k] * G + [w1_v] * G)
    pos = jnp.stack([pos_k] * G + [pos_v] * G)

    def first_layer(lo):
        w = jnp.einsum('slde,st->lsdte', w1[:, lo:lo + half], eye)
        p = pos[:, lo:lo + half].transpose(1, 0, 2)
        return w.reshape(half * 2 * G * hd, 2 * G * hd).astype(BF16), p.reshape(1, half * 2 * G * hd)

    wa, posa = first_layer(0)
    wb, posb = first_layer(half)
    w2 = jnp.zeros((2 * G, hd, 3 * G, hd), F32)
    for g in range(G):
        w2 = w2.at[g, :, 2 * g].set(w2_k).at[G + g, :, 2 * G + g].set(w2_v)
    return posa, posb, wa, wb, w2.reshape(2 * G * hd, 3 * G * hd).astype(BF16)


def _overlap_t(n_cmp_pad, n_sel):
    i = np.arange(n_cmp_pad)[None, :]
    j = np.arange(n_sel)[:, None]
    lo = np.maximum(i * CMP_STRIDE, j * SLC_BLOCK)
    hi = np.minimum(i * CMP_STRIDE + CMP_BLOCK, (j + 1) * SLC_BLOCK)
    return (np.clip(hi - lo, 0, None) / CMP_BLOCK).astype(np.float32)


def kernel(x, norm_g, w_in, cmp_pos_k, cmp_w1_k, cmp_w2_k, cmp_pos_v, cmp_w1_v, cmp_w2_v,
           w_br_nsa, w_br_dil, w_out, final_g):
    B, S, D = x.shape
    n_sel = S // SLC_BLOCK
    assert S % (Q_BLOCK * max(d for _, d in DIL_PATTERNS)) == 0 and n_sel % FLAG_BITS == 0
    n_cmp_pad = S // CMP_STRIDE
    slopes_nsa, slopes_dil = _alibi_slopes()
    proj_cols, gate_cols = _proj_columns(D)
    ovt = jnp.asarray(_overlap_t(n_cmp_pad, n_sel))
    qfeat = jnp.asarray(np.broadcast_to(
        _slope_features(slopes_nsa).reshape(NSA_GROUPS, NSA_HPG, 1, LANES),
        (NSA_GROUPS, NSA_HPG, 8, LANES)))
    qfeat_dil = jnp.asarray(np.broadcast_to(
        _slope_features(slopes_dil).reshape(DIL_HEADS, 1, LANES), (DIL_HEADS, 8, LANES)))
    tm = 512

    h = x
    for layer in range(w_in.shape[0]):
        w_proj = _take_columns(w_in[layer], proj_cols, BF16)
        w_gate = _take_columns(w_in[layer], gate_cols, BF16)
        g_l = norm_g[layer].reshape(1, D)

        qn, cmp_src, ks, kw, vst, vwt, gates, *dil = _proj_call(h, g_l, w_proj, tm)
        cw = _compress_weights(cmp_pos_k[layer], cmp_w1_k[layer], cmp_w2_k[layer],
                               cmp_pos_v[layer], cmp_w1_v[layer], cmp_w2_v[layer])
        kc, cvo = _compress_call(cmp_src, *cw, ovt)
        o_nsa = _nsa_call(qfeat, qn, kc, cvo, ks, vst, kw, vwt, gates)
        o_d, lse_d = [], []
        for i, (window, dilation) in enumerate(DIL_PATTERNS):
            assert Q_BLOCK - 1 <= window // dilation <= Q_BLOCK
            o, lse = _dilated_call(qfeat_dil, *dil[3 * i:3 * i + 3], window, dilation)
            o_d.append(o)
            lse_d.append(lse)
        h = _merge_call(h, g_l, w_gate, o_nsa, o_d, lse_d, w_br_nsa[layer].astype(BF16),
                        w_br_dil[layer].astype(BF16), w_out[layer].astype(BF16),
                        final_g.reshape(1, D), tm, final=layer == w_in.shape[0] - 1)
    return h
```

```python
import functools
import math

import numpy as np
import jax
import jax.numpy as jnp
from jax import lax
from jax.experimental import pallas as pl
from jax.experimental.pallas import tpu as pltpu

F32 = jnp.float32
BF16 = jnp.bfloat16

HEAD_DIM = 64
LANES = 128
NSA_HEADS = 8
NSA_GROUPS = 2
NSA_HPG = NSA_HEADS // NSA_GROUPS
CMP_BLOCK = 32
CMP_STRIDE = 16
SLC_BLOCK = 64
SLC_TOPK = 16
WIN = 512
FORCE_SCORE = 1.0e4
DIL_HEADS = 8
DIL_PATTERNS = ((128, 1), (512, 4), (2048, 16))
Q_BLOCK = 128
RMS_EPS = 1e-6
KEY_CHUNK = 128
VT_ROWS = 80
PAIR_VT_ROWS = 144
GATE_ROWS = 16
FEAT_SPLIT = 3
FLAG_BITS = 16
SLC_GROUP = 4
NSA_QB = 512
DIL_BLOCKS_PER_STEP = 4
N_FORCED = 3
assert FORCE_SCORE > NSA_HPG + 1
SEL_PAD_ROWS = 8
NEG_INIT = -1.0e30
LOG2E = math.log2(math.e)
VMEM_LIMIT = 56 * 1024 * 1024


def _alibi_slopes():
    n = NSA_HEADS + DIL_HEADS
    s = 2.0 ** (-8.0 * np.arange(1, n + 1) / n)
    return s[0::2].astype(np.float32), s[1::2].astype(np.float32)


def _nt(a, b):
    return lax.dot_general(a, b, (((1,), (1,)), ((), ())), preferred_element_type=F32)


def _rms(x, g):
    return (x * lax.rsqrt(jnp.mean(x * x, axis=-1, keepdims=True) + RMS_EPS)) * g


def _head_tile(q_pairs, h):
    tile = q_pairs[:, (h // 2) * LANES:(h // 2 + 1) * LANES]
    return pltpu.roll(tile, HEAD_DIM, 1) if h % 2 else tile


def _pad_heads(q_pairs, n_heads):
    lane = lax.broadcasted_iota(jnp.int32, (q_pairs.shape[0], LANES), 1)
    return [jnp.where(lane < HEAD_DIM, _head_tile(q_pairs, h), 0.0) for h in range(n_heads)]


def _compact_pair(a_pad, b_pad):
    lane = lax.broadcasted_iota(jnp.int32, a_pad.shape, 1)
    return jnp.where(lane < HEAD_DIM, pltpu.roll(a_pad, HEAD_DIM, 1), b_pad)


def _pos_features(pos):
    lane = lax.broadcasted_iota(jnp.int32, pos.shape, 1) - HEAD_DIM
    hi = (pos // LANES).astype(F32)
    lo = (pos % LANES).astype(F32)
    return jnp.where(lane < 0, 0.0, jnp.where(lane < FEAT_SPLIT, hi, jnp.where(lane < 2 * FEAT_SPLIT, lo, 0.0)))


def _slope_features(slopes):
    out = np.zeros((len(slopes), LANES), np.float32)
    for h, s in enumerate(slopes):
        rest = np.float32(s) * np.float32(LOG2E)
        for i in range(FEAT_SPLIT):
            piece = np.float32(np.asarray(rest, np.float32).astype(BF16))
            out[h, HEAD_DIM + FEAT_SPLIT + i] = piece
            out[h, HEAD_DIM + i] = piece * LANES
            rest = np.float32(rest - piece)
    return out


_PROJ_SEGS = (("qn", 512), ("cmp", 256), ("ks", 256), ("kw", 256), ("vs", 128), ("vw", 128),
              ("gate", 128), ("qd", 512), ("kd", 512), ("vd", 512))
DIL_Q_W = DIL_HEADS * HEAD_DIM
_PROJ_OFFS = dict(zip((n for n, _ in _PROJ_SEGS),
                      zip(np.cumsum([0] + [w for _, w in _PROJ_SEGS])[:-1].tolist(),
                          np.cumsum([w for _, w in _PROJ_SEGS]).tolist())))


def _to_lane_tiles(scr, first, val):
    for j in range(val.shape[1] // LANES):
        scr[first + j] = val[:, j * LANES:(j + 1) * LANES]


def _regroup_rows(scr, lo, hi, out_ref, d, dtype):
    rows, width = scr.shape[1], (hi - lo) * LANES
    for p in range(d):
        for j in range(hi - lo):
            tile = scr[lo + j, pl.ds(p, rows // d, stride=d), :] if d > 1 else scr[lo + j]
            out_ref[:, p * width + j * LANES:p * width + (j + 1) * LANES] = tile.astype(dtype)


def _proj_kernel(x_ref, g_ref, w_ref, perm_ref, qn_ref, cmp_ref, ks_ref, kw_ref, vst_ref, vwt_ref,
                 gate_ref, *dil_refs):
    cmp_scr = dil_refs[-1]
    tm = x_ref.shape[0]
    xb = _rms(x_ref[...], g_ref[...]).astype(BF16)
    scale = HEAD_DIM ** -0.5

    def seg(name):
        c0, c1 = _PROJ_OFFS[name]
        return jnp.dot(xb, w_ref[:, c0:c1], preferred_element_type=F32)

    qn_ref[...] = (seg("qn") * (scale * LOG2E)).astype(BF16)
    _to_lane_tiles(cmp_scr, 0, seg("cmp"))
    _regroup_rows(cmp_scr, 0, cmp_scr.shape[0], cmp_ref, CMP_STRIDE, F32)
    feat = _pos_features(pl.program_id(1) * tm + lax.broadcasted_iota(jnp.int32, (tm, LANES), 0))
    ks, kw = seg("ks"), seg("kw")
    vst, vwt = seg("vs").T, seg("vw").T
    gate_t = jax.nn.sigmoid(seg("gate")).T
    tail_row = lax.broadcasted_iota(jnp.int32, (VT_ROWS - HEAD_DIM, tm), 0)
    tail = jnp.where(tail_row == 0, 1.0, 0.0)
    for g in range(NSA_GROUPS):
        ks_ref[g] = (ks[:, g * LANES:(g + 1) * LANES] + feat).astype(BF16)
        kw_ref[g] = (kw[:, g * LANES:(g + 1) * LANES] + feat).astype(BF16)
        gate_ref[g] = gate_t[g * GATE_ROWS:(g + 1) * GATE_ROWS]
        for src, dst in ((vst, vst_ref), (vwt, vwt_ref)):
            vt = jnp.concatenate([src[g * HEAD_DIM:(g + 1) * HEAD_DIM], tail], axis=0).astype(BF16)
            for j in range(tm // KEY_CHUNK):
                dst[g, j] = vt[:, j * KEY_CHUNK:(j + 1) * KEY_CHUNK]
    qkv = jnp.concatenate([(seg("qd") * (scale * LOG2E)).astype(BF16), seg("kd").astype(BF16),
                           seg("vd").astype(BF16)], axis=1)
    bounds = (0, DIL_Q_W, 2 * DIL_Q_W, 3 * DIL_Q_W)
    for i, (_, d) in enumerate(DIL_PATTERNS):
        by_phase = qkv if d == 1 else jnp.dot(perm_ref[i], qkv, preferred_element_type=F32).astype(BF16)
        for p in range(d):
            rows = by_phase[p * (tm // d):(p + 1) * (tm // d)]
            for j, ref in enumerate(dil_refs[3 * i:3 * i + 3]):
                width = bounds[j + 1] - bounds[j]
                ref[:, p * width:(p + 1) * width] = rows[:, bounds[j]:bounds[j + 1]]


def _proj_call(x, g, w, tm):
    B, S, D = x.shape
    G = NSA_GROUPS
    cmp_w = _PROJ_OFFS["cmp"][1] - _PROJ_OFFS["cmp"][0]
    row = lambda c: pl.BlockSpec((None, tm, c), lambda b, i: (b, i, 0))
    k_spec =pl.BlockSpec((None, G, tm, LANES), lambda b, i: (b, 0, i, 0))
    vt_spec = pl.BlockSpec((None, G, tm // KEY_CHUNK, VT_ROWS, KEY_CHUNK), lambda b, i: (b, 0, i, 0, 0))
    out_shape = (
        jax.ShapeDtypeStruct((B, S, 512), BF16),
        jax.ShapeDtypeStruct((B, S // CMP_STRIDE, CMP_STRIDE * cmp_w), F32),
        jax.ShapeDtypeStruct((B, G, S, LANES), BF16),
        jax.ShapeDtypeStruct((B, G, S, LANES), BF16),
        jax.ShapeDtypeStruct((B, G, S // KEY_CHUNK, VT_ROWS, KEY_CHUNK), BF16),
        jax.ShapeDtypeStruct((B, G, S // KEY_CHUNK, VT_ROWS, KEY_CHUNK), BF16),
        jax.ShapeDtypeStruct((B, G, GATE_ROWS, S), F32),
    )
    dil_specs = ()
    for _, d in DIL_PATTERNS:
        for width in (DIL_Q_W,) * 3:
            out_shape += (jax.ShapeDtypeStruct((B, S // d, d * width), BF16),)
            dil_specs += (pl.BlockSpec((None, tm // d, d * width), lambda b, i: (b, i, 0)),)
    r = np.arange(tm)
    perm = np.zeros((len(DIL_PATTERNS), tm, tm), np.float32)
    for i, (_, d) in enumerate(DIL_PATTERNS):
        perm[i, (r % d) * (tm // d) + r // d, r] = 1.0
    perm = jnp.asarray(perm, BF16)
    return pl.pallas_call(
        _proj_kernel,
        grid=(B, S // tm),
        in_specs=[row(D), pl.BlockSpec((1, D), lambda b, i: (0, 0)),
                  pl.BlockSpec(w.shape, lambda b, i: (0, 0)),
                  pl.BlockSpec(perm.shape, lambda b, i: (0, 0, 0))],
        out_specs=(row(512),
                   pl.BlockSpec((None, tm // CMP_STRIDE, CMP_STRIDE * cmp_w), lambda b, i: (b, i, 0)),
                   k_spec, k_spec, vt_spec, vt_spec,
                   pl.BlockSpec((None, G, GATE_ROWS, tm), lambda b, i: (b, 0, 0, i))) + dil_specs,
        out_shape=out_shape,
        scratch_shapes=[pltpu.VMEM((cmp_w // LANES, tm, LANES), F32)],
        compiler_params=pltpu.CompilerParams(
            dimension_semantics=("parallel", "parallel"), vmem_limit_bytes=VMEM_LIMIT),
        name="proj",
    )(x, g, w, perm)


def _compress_kernel(r_ref, posa_ref, posb_ref, wa_ref, wb_ref, w2_ref, ovt_ref, kc_ref, cvo_ref):
    r = r_ref[...]
    n = r.shape[0]
    ha = jnp.dot((r + posa_ref[...]).astype(BF16), wa_ref[...], preferred_element_type=F32)
    hb = jnp.dot((r + posb_ref[...]).astype(BF16), wb_ref[...], preferred_element_type=F32)
    hid = jax.nn.gelu(ha + pltpu.roll(hb, n - 1, 0))
    out = jnp.dot(hid.astype(BF16), w2_ref[...], preferred_element_type=F32)
    cmp_end = lax.broadcasted_iota(jnp.int32, (n, LANES), 0) * CMP_STRIDE + (CMP_BLOCK - 1)
    feat = _pos_features(cmp_end)
    v_t = out[:, NSA_GROUPS * LANES:].T
    for g in range(NSA_GROUPS):
        kc_ref[g] = (out[:, g * LANES:(g + 1) * LANES] + feat).astype(BF16)
        cvo_ref[g] = jnp.concatenate([v_t[g * HEAD_DIM:(g + 1) * HEAD_DIM], ovt_ref[...]],
                                     axis=0).astype(BF16)


def _compress_call(r, posa, posb, wa, wb, w2, ovt):
    B, R, C = r.shape
    G = NSA_GROUPS
    rows = HEAD_DIM + ovt.shape[0]
    full = lambda a: pl.BlockSpec(a.shape, lambda b: (0,) * a.ndim)
    return pl.pallas_call(
        _compress_kernel,
        grid=(B,),
        in_specs=[pl.BlockSpec((None, R, C), lambda b: (b, 0, 0)),
                  full(posa), full(posb), full(wa), full(wb), full(w2), full(ovt)],
        out_specs=(pl.BlockSpec((None, G, R, LANES), lambda b: (b, 0, 0, 0)),
                   pl.BlockSpec((None, G, rows, R), lambda b: (b, 0, 0, 0))),
        out_shape=(jax.ShapeDtypeStruct((B, G, R, LANES), BF16),
                   jax.ShapeDtypeStruct((B, G, rows, R), BF16)),
        compiler_params=pltpu.CompilerParams(
            dimension_semantics=("parallel",), vmem_limit_bytes=VMEM_LIMIT),
        name="compress",
    )(r, posa, posb, wa, wb, w2, ovt)


def _flash_step(s, ok, v_t, m_ref, acc_ref):
    ps, alphas = [], []
    qb = ok.shape[1]
    for h in range(NSA_HPG):
        cols = slice(h * qb, (h + 1) * qb)
        sh = jnp.where(ok, s[:, cols], -jnp.inf)
        m_old = m_ref[:, cols]
        m_new = jnp.maximum(m_old, jnp.max(sh, axis=0, keepdims=True))
        alphas.append(jnp.exp2(m_old - m_new))
        ps.append(jnp.exp2(sh - m_new).astype(BF16))
        m_ref[:, cols] = m_new
    pv = jnp.dot(v_t, jnp.concatenate(ps, axis=1), preferred_element_type=F32)
    acc_ref[...] = jnp.concatenate(alphas, axis=1) * acc_ref[...] + pv


def _flash_result(acc_ref):
    l = acc_ref[HEAD_DIM:HEAD_DIM + 1, :]
    return acc_ref[0:HEAD_DIM, :] * (1.0 / jnp.maximum(l, 1e-30))


def _softmax_pv(s, ok, v_t):
    ps = []
    qb = ok.shape[1]
    for h in range(NSA_HPG):
        sh = jnp.where(ok, s[:, h * qb:(h + 1) * qb], -jnp.inf)
        ps.append(jnp.exp2(sh - jnp.max(sh, axis=0, keepdims=True)).astype(BF16))
    r = jnp.dot(v_t, jnp.concatenate(ps, axis=1), preferred_element_type=F32)
    return r[0:HEAD_DIM] * (1.0 / jnp.maximum(r[HEAD_DIM:HEAD_DIM + 1], 1e-30))


def _nsa_kernel(qfeat_ref, q_ref, kc_ref, cvo_ref, ks_ref, vst_ref, kw_ref, vwt_ref, gate_ref,
                o_ref, sel_ref, ms_ref, accs_ref, flag_ref, list_ref):
    n = pl.program_id(2)
    qb, kc = NSA_QB, KEY_CHUNK
    t0 = n * qb
    lane = lax.broadcasted_iota(jnp.int32, (qb, LANES), 1)
    qf = q_ref[...].astype(F32)
    q = jnp.concatenate(
        [jnp.where(lane < HEAD_DIM, _head_tile(qf, h), qfeat_ref[h][0:1, :]) for h in range(NSA_HPG)],
        axis=0).astype(BF16)

    span = WIN + qb
    k_lo = pl.multiple_of(jnp.maximum(t0 - WIN, 0), kc)
    dist_w = ((t0 - k_lo) + lax.broadcasted_iota(jnp.int32, (span, qb), 1)
              - lax.broadcasted_iota(jnp.int32, (span, qb), 0))
    vt_w = jnp.concatenate([vwt_ref[k_lo // kc + u] for u in range(span // kc)], axis=1)
    o_win = _softmax_pv(_nt(kw_ref[pl.ds(k_lo, span), :], q), (dist_w >= 0) & (dist_w < WIN), vt_w)

    n_cmp = kc_ref.shape[0]
    n_sel = sel_ref.shape[0] - SEL_PAD_ROWS
    s = _nt(kc_ref[...], q)
    cmp_end = lax.broadcasted_iota(jnp.int32, (n_cmp, qb), 0) * CMP_STRIDE + (CMP_BLOCK - 1)
    ok_c = cmp_end <= t0 + lax.broadcasted_iota(jnp.int32, (n_cmp, qb), 1)
    ps = []
    for h in range(NSA_HPG):
        sh = jnp.where(ok_c, s[:, h * qb:(h + 1) * qb], -jnp.inf)
        m = jnp.max(sh, axis=0, keepdims=True)
        m = jnp.where(m == -jnp.inf, 0.0, m)
        e = jnp.exp2(sh - m)
        l = jnp.sum(e, axis=0, keepdims=True)
        ps.append((e * (1.0 / jnp.maximum(l, 1e-30))).astype(BF16))
    r = jnp.dot(cvo_ref[...], jnp.concatenate(ps, axis=1), preferred_element_type=F32)
    o_cmp = r[0:HEAD_DIM]
    imp_t = r[HEAD_DIM:, 0:qb]
    for h in range(1, NSA_HPG):
        imp_t = imp_t + r[HEAD_DIM:, h * qb:(h + 1) * qb]

    blk = lax.broadcasted_iota(jnp.int32, (n_sel, qb), 0).astype(F32)
    cur = ((t0 + lax.broadcasted_iota(jnp.int32, (n_sel, qb), 1)) // SLC_BLOCK).astype(F32)
    ago = cur - blk
    forced = jnp.where(blk == 0, 1.0, jnp.where(ago == 0, 1.0, jnp.where(ago == 1, 1.0, 0.0)))
    valid = ago >= 0
    score = jnp.where(valid, jnp.where(forced > 0.5, -1.0, imp_t), -1.0)
    for _ in range(max(min(SLC_TOPK, n_sel) - N_FORCED, 0)):
        mx = jnp.max(score, axis=0, keepdims=True)
        idx = jnp.min(jnp.where(score == mx, blk, float(n_sel)), axis=0, keepdims=True)
        score = jnp.where(blk == idx, -jnp.inf, score)
    sel_t = jnp.where(valid, jnp.where(score == -jnp.inf, 1.0, forced), 0.0)
    sel_ref[0:n_sel, :] = sel_t
    sel_ref[n_sel:, :] = jnp.zeros((SEL_PAD_ROWS, qb), F32)
    any_q = jnp.max(sel_t, axis=1, keepdims=True)
    bit = jnp.left_shift(1, lax.broadcasted_iota(jnp.int32, (n_sel, 1), 0) % FLAG_BITS).astype(F32)
    packed = any_q * bit
    for i in range(n_sel // FLAG_BITS):
        word = jnp.sum(packed[i * FLAG_BITS:(i + 1) * FLAG_BITS], axis=0, keepdims=True)
        flag_ref[i] = word.astype(jnp.int32)[0, 0]

    blocks_per_chunk = kc // SLC_BLOCK
    chunks_per_word = FLAG_BITS // blocks_per_chunk

    def scan_body(w, cnt):
        word = flag_ref[w]
        for j in range(chunks_per_word):
            bits = (word >> (j * blocks_per_chunk)) & ((1 << blocks_per_chunk) - 1)
            list_ref[cnt] = w * chunks_per_word + j
            cnt = cnt + (bits != 0).astype(jnp.int32)
        return cnt

    cnt = lax.fori_loop(0, (t0 + qb - 1) // (FLAG_BITS * SLC_BLOCK) + 1, scan_body, 0)
    for u in range(SLC_GROUP):
        list_ref[cnt + u] = -1

    row = lax.broadcasted_iota(jnp.int32, (kc, qb), 0)
    back = lax.broadcasted_iota(jnp.int32, (kc, qb), 1) - row
    ms_ref[...] = jnp.full(ms_ref.shape, NEG_INIT, F32)
    accs_ref[...] = jnp.zeros(accs_ref.shape, F32)

    def slc_body(gi, carry):
        keys, oks, vts = [], [], []
        for u in range(SLC_GROUP):
            c = list_ref[gi * SLC_GROUP + u]
            cc = jnp.maximum(c, 0)
            k0 = pl.multiple_of(cc * kc, kc)
            first = jnp.where(c >= 0, cc * blocks_per_chunk, n_sel)
            chosen = sel_ref[pl.ds(first, 1), :]
            for j in range(1, blocks_per_chunk):
                chosen = jnp.where(row < j * SLC_BLOCK, chosen, sel_ref[pl.ds(first + j, 1), :])
            oks.append(jnp.where(back + (t0 - k0) >= 0, chosen, 0.0))
            keys.append(ks_ref[pl.ds(k0, kc), :])
            vts.append(vst_ref[cc])
        _flash_step(_nt(jnp.concatenate(keys, axis=0), q), jnp.concatenate(oks, axis=0) > 0.5,
                    jnp.concatenate(vts, axis=1), ms_ref, accs_ref)
        return carry

    lax.fori_loop(0, (cnt + SLC_GROUP - 1) // SLC_GROUP, slc_body, 0)

    o_slc = _flash_result(accs_ref)
    gate = gate_ref[...]
    outs = []
    for h in range(NSA_HPG):
        cols = slice(h * qb, (h + 1) * qb)
        gc = [gate[j * NSA_HPG + h:j * NSA_HPG + h + 1, :] for j in range(3)]
        outs.append(gc[0] * o_cmp[:, cols] + gc[1] * o_slc[:, cols] + gc[2] * o_win[:, cols])
    o_ref[...] = jnp.concatenate(outs, axis=0).T


def _nsa_call(qfeat, qn, kc, cvo, ks, vst, kw, vwt, gates):
    B, S, _ = qn.shape
    G = NSA_GROUPS
    qb = NSA_QB
    nb = S // qb
    n_sel = S // SLC_BLOCK
    width = NSA_HPG * qb
    per_group = lambda a: pl.BlockSpec((None, None) + a.shape[2:],
                                       lambda b, g, n: (b, g) + (0,) * (a.ndim - 2))
    return pl.pallas_call(
        _nsa_kernel,
        grid=(B, G, nb),
        in_specs=[
            pl.BlockSpec((None,) + qfeat.shape[1:], lambda b, g, n: (g, 0, 0, 0)),
            pl.BlockSpec((None, qb, NSA_HPG * HEAD_DIM), lambda b, g, n: (b, n, g)),
            per_group(kc), per_group(cvo), per_group(ks), per_group(vst), per_group(kw), per_group(vwt),
            pl.BlockSpec((None, None, GATE_ROWS, qb), lambda b, g, n: (b, g, 0, n)),
        ],
        out_specs=pl.BlockSpec((None, qb, NSA_HPG * HEAD_DIM), lambda b, g, n: (b, n, g)),
        out_shape=jax.ShapeDtypeStruct((B, S, NSA_HEADS * HEAD_DIM), F32),
        scratch_shapes=[pltpu.VMEM((n_sel + SEL_PAD_ROWS, qb), F32),
                        pltpu.VMEM((1, width), F32), pltpu.VMEM((VT_ROWS, width), F32),
                        pltpu.SMEM((n_sel // FLAG_BITS,), jnp.int32),
                        pltpu.SMEM((S // KEY_CHUNK + SLC_GROUP,), jnp.int32)],
        compiler_params=pltpu.CompilerParams(
            dimension_semantics=("parallel", "parallel", "arbitrary"), vmem_limit_bytes=VMEM_LIMIT),
        name="nsa",
    )(qfeat, qn, kc, cvo, ks, vst, kw, vwt, gates)


def _dilated_kernel(qfeat_ref, q_ref, kp_ref, kc_ref, vp_ref, vc_ref, o_ref, lse_ref, *, wd, dilation):
    c = Q_BLOCK
    hd = HEAD_DIM
    blocks = q_ref.shape[0] // c
    lane = lax.broadcasted_iota(jnp.int32, (blocks * c, LANES), 1)
    key = lax.broadcasted_iota(jnp.int32, (2 * c, c), 0)
    qry = lax.broadcasted_iota(jnp.int32, (2 * c, c), 1)
    first = pl.program_id(2) == 0
    off_later = jnp.where(key < c, qry - (key + (wd - c)), (key - c) - qry)
    off_first = jnp.where(key < c, jnp.where(first, 2 * c, qry) - (key + (wd - c)), (key - c) - qry)
    bias_later = jnp.where(off_later <= 0, 0.0, -jnp.inf)
    bias_first = jnp.where(off_first <= 0, 0.0, -jnp.inf)
    sub = (pl.program_id(2) * blocks - 1) * c + lax.broadcasted_iota(jnp.int32, ((blocks + 1) * c, LANES), 0)
    pos_feat = _pos_features(sub * dilation + pl.program_id(1)).astype(BF16)
    tail_row = lax.broadcasted_iota(jnp.int32, (PAIR_VT_ROWS - 2 * hd, 2 * c), 0)
    tail = jnp.where(tail_row == 0, 1.0, 0.0).astype(BF16)
    for p in range(DIL_HEADS // 2):
        cols = slice(p * LANES, (p + 1) * LANES)
        q_pair = q_ref[:, cols]
        zero = jnp.zeros_like(q_pair)
        feats = [jnp.broadcast_to(qfeat_ref[2 * p + i][0:1, :], (blocks * c, LANES)).astype(BF16)
                 for i in range(2)]
        q_a = jnp.concatenate([jnp.where(lane < hd, q_pair, zero), feats[0]], axis=1)
        q_b = jnp.concatenate([jnp.where(lane < hd, zero, q_pair), feats[1]], axis=1)
        k_all = jnp.concatenate([kp_ref[:, cols], kc_ref[:, cols]], axis=0)
        k_all = jnp.concatenate([k_all, pos_feat], axis=1)
        vts = [vp_ref[:, cols].T] + [vc_ref[j * c:(j + 1) * c, cols].T for j in range(blocks)]
        for j in range(blocks):
            rows = slice(j * c, (j + 1) * c)
            q_aug = jnp.concatenate([q_a[rows], q_b[rows]], axis=0)
            s = _nt(k_all[j * c:(j + 2) * c], q_aug)
            bias = bias_later if j else bias_first
            ms, ps = [], []
            for i in range(2):
                si = s[:, i * c:(i + 1) * c] + bias
                ms.append(jnp.max(si, axis=0, keepdims=True))
                ps.append(jnp.exp2(si - ms[i]).astype(BF16))
            v_t = jnp.concatenate([jnp.concatenate(vts[j:j + 2], axis=1), tail], axis=0)
            r = jnp.dot(v_t, jnp.concatenate(ps, axis=1), preferred_element_type=F32)
            outs, lses = [], []
            for i in range(2):
                l = r[2 * hd:2 * hd + 1, i * c:(i + 1) * c]
                outs.append(r[i * hd:(i + 1) * hd, i * c:(i + 1) * c] * (1.0 / l))
                lses.append(jnp.broadcast_to(ms[i] + jnp.log2(l), (hd, c)))
            o_ref[rows, cols] = jnp.concatenate(outs, axis=0).T
            lse_ref[rows, cols] = jnp.concatenate(lses, axis=0).T


def _dilated_call(qfeat, q_v, k_v, v_v, window, dilation):
    B, L, _ = q_v.shape
    per_step = min(DIL_BLOCKS_PER_STEP, L // Q_BLOCK)
    nblk = L // (Q_BLOCK * per_step)
    blk = lambda cw: pl.BlockSpec((None, Q_BLOCK * per_step, cw), lambda b, p, i: (b, i, p))
    prev = pl.BlockSpec((None, Q_BLOCK, DIL_Q_W),
                        lambda b, p, i: (b, jnp.maximum(i * per_step - 1, 0), p))
    return pl.pallas_call(
        functools.partial(_dilated_kernel, wd=window // dilation, dilation=dilation),
        grid=(B, dilation, nblk),
        in_specs=[pl.BlockSpec(qfeat.shape, lambda b, p, i: (0, 0, 0)),
                  blk(DIL_Q_W), prev, blk(DIL_Q_W), prev, blk(DIL_Q_W)],
        out_specs=(blk(DIL_Q_W), blk(DIL_Q_W)),
        out_shape=(jax.ShapeDtypeStruct((B, L, dilation * DIL_Q_W), F32),
                   jax.ShapeDtypeStruct((B, L, dilation * DIL_Q_W), F32)),
        compiler_params=pltpu.CompilerParams(
            dimension_semantics=("parallel", "parallel", "arbitrary"), vmem_limit_bytes=VMEM_LIMIT),
        name=f"dilated_d{dilation}",
    )(qfeat, q_v, k_v, k_v, v_v, v_v)


def _merge_kernel(x_ref, g_ref, wg_ref, onsa_ref, o1_ref, o2_ref, o3_ref, l1_ref, l2_ref, l3_ref,
                  wbn_ref, wbd_ref, wo_ref, fg_ref, out_ref, tok_scr, *, final):
    x = x_ref[...]
    tm = x.shape[0]
    xb = _rms(x, g_ref[...]).astype(BF16)
    cz = NSA_HEADS * HEAD_DIM
    d = x.shape[-1]

    def gate_seg(c0, c1):
        return jnp.dot(xb, wg_ref[:, c0:c1], preferred_element_type=F32)

    def token_major(ref, slot, dil):
        if dil == 1:
            return ref[...]
        tiles = DIL_Q_W // LANES
        for p in range(dil):
            for j in range(tiles):
                tok_scr[slot * tiles + j, pl.ds(p, tm // dil, stride=dil), :] = \
                    ref[:, p * DIL_Q_W + j * LANES:p * DIL_Q_W + (j + 1) * LANES]
        return jnp.concatenate([tok_scr[slot * tiles + j] for j in range(tiles)], axis=1)

    o_nsa = onsa_ref[...] * jax.nn.silu(gate_seg(0, cz))
    dils = [dil for _, dil in DIL_PATTERNS]
    o1, o2, o3 = (token_major(r, i, dil) for i, (r, dil) in enumerate(zip((o1_ref, o2_ref, o3_ref), dils)))
    l1, l2, l3 = (token_major(r, 3 + i, dil) for i, (r, dil) in enumerate(zip((l1_ref, l2_ref, l3_ref), dils)))
    mx = jnp.maximum(jnp.maximum(l1, l2), l3)
    e1, e2, e3 = jnp.exp2(l1 - mx), jnp.exp2(l2 - mx), jnp.exp2(l3 - mx)
    den = e1 + e2 + e3
    o_dil = (e1 / den) * o1 + (e2 / den) * o2 + (e3 / den) * o3
    o_dil = o_dil * jax.nn.silu(gate_seg(cz, 2 * cz))
    a = jnp.dot(o_nsa.astype(BF16), wbn_ref[...], preferred_element_type=F32)
    bd = jnp.dot(o_dil.astype(BF16), wbd_ref[...], preferred_element_type=F32)
    merged = (jax.nn.sigmoid(gate_seg(2 * cz, 2 * cz + d)) * a
              + jax.nn.sigmoid(gate_seg(2 * cz + d, 2 * cz + 2 * d)) * bd)
    y = jnp.dot(merged.astype(BF16), wo_ref[...], preferred_element_type=F32)
    out_ref[...] = _rms(x + y, fg_ref[...]) if final else x + y


def _merge_call(x, g, wg, o_nsa, o_d, lse_d, wbn, wbd, wo, fg, tm, final):
    B, S, D = x.shape
    cz = o_nsa.shape[-1]
    row = lambda c: pl.BlockSpec((None, tm, c), lambda b, i: (b, i, 0))
    full = lambda a: pl.BlockSpec(a.shape, lambda b, i: (0,) * a.ndim)
    phase = [pl.BlockSpec((None, tm // dil, dil * cz), lambda b, i: (b, i, 0)) for _, dil in DIL_PATTERNS]
    return pl.pallas_call(
        functools.partial(_merge_kernel, final=final),
        grid=(B, S // tm),
        in_specs=[row(D), full(g), full(wg), row(cz)] + phase + phase
                 + [full(wbn), full(wbd), full(wo), full(fg)],
        out_specs=row(D),
        out_shape=jax.ShapeDtypeStruct((B, S, D), F32),
        scratch_shapes=[pltpu.VMEM((2 * len(DIL_PATTERNS) * cz // LANES, tm, LANES), F32)],
        compiler_params=pltpu.CompilerParams(
            dimension_semantics=("parallel", "parallel"), vmem_limit_bytes=VMEM_LIMIT),
        name="merge",
    )(x, g, wg, o_nsa, *o_d, *lse_d, wbn, wbd, wo, fg)


def _proj_columns(d_model):
    hd, G, H = HEAD_DIM, NSA_GROUPS, NSA_HEADS
    nsa_w = H * hd
    o_q, o_kv = 0, nsa_w
    o_g = o_kv + 6 * G * hd
    o_zn = o_g + 3 * H
    o_qkvd = o_zn + nsa_w
    dil_w = DIL_HEADS * hd
    o_zd = o_qkvd + 3 * dil_w
    o_mg = o_zd + dil_w
    kv = lambda j, g: o_kv + (j * G + g) * hd + np.arange(hd)
    pad = np.full(hd, -1)
    cols = [o_q + np.arange(nsa_w)]
    cols += [kv(j, g) for j in (0, 1) for g in range(G)]
    for j in (2, 4):
        for g in range(G):
            cols += [kv(j, g), pad]
    for j in (3, 5):
        cols += [kv(j, g) for g in range(G)]
    lanes = np.full(LANES, -1)
    for g in range(G):
        for j in range(3):
            lanes[g * GATE_ROWS + j * NSA_HPG + np.arange(NSA_HPG)] = o_g + j * H + g * NSA_HPG + np.arange(NSA_HPG)
    cols.append(lanes)
    cols.append(o_qkvd + np.arange(3 * dil_w))
    proj_cols = np.concatenate(cols)
    assert proj_cols.size == _PROJ_OFFS["vd"][1]
    gate_cols = np.concatenate([o_zn + np.arange(nsa_w), o_zd + np.arange(dil_w),
                                o_mg + np.arange(2 * d_model)])
    return proj_cols, gate_cols


def _take_columns(w, cols, dtype):
    pieces, start = [], 0
    for i in range(1, len(cols) + 1):
        if i < len(cols):
            same_run = (cols[i] < 0 and cols[i - 1] < 0) or (cols[i - 1] >= 0 and cols[i] == cols[i - 1] + 1)
        if i == len(cols) or not same_run:
            n = i - start
            pieces.append(jnp.zeros((w.shape[0], n), dtype) if cols[start] < 0
                          else w[:, int(cols[start]):int(cols[start]) + n].astype(dtype))
            start = i
    return jnp.concatenate(pieces, axis=1)


def _compress_weights(pos_k, w1_k, w2_k, pos_v, w1_v, w2_v):
    hd, G, half = HEAD_DIM, NSA_GROUPS, CMP_BLOCK // 2
    eye = jnp.eye(2 * G, dtype=F32)
    w1 = jnp.stack([w1_k] * G + [w1_v] * G)
    pos = jnp.stack([pos_k] * G + [pos_v] * G)

    def first_layer(lo):
        w = jnp.einsum('slde,st->lsdte', w1[:, lo:lo + half], eye)
        p = pos[:, lo:lo + half].transpose(1, 0, 2)
        return w.reshape(half * 2 * G * hd, 2 * G * hd).astype(BF16), p.reshape(1, half * 2 * G * hd)

    wa, posa = first_layer(0)
    wb, posb = first_layer(half)
    w2 = jnp.zeros((2 * G, hd, 3 * G, hd), F32)
    for g in range(G):
        w2 = w2.at[g, :, 2 * g].set(w2_k).at[G + g, :, 2 * G + g].set(w2_v)
    return posa, posb, wa, wb, w2.reshape(2 * G * hd, 3 * G * hd).astype(BF16)


def _overlap_t(n_cmp_pad, n_sel):
    i = np.arange(n_cmp_pad)[None, :]
    j = np.arange(n_sel)[:, None]
    lo = np.maximum(i * CMP_STRIDE, j * SLC_BLOCK)
    hi = np.minimum(i * CMP_STRIDE + CMP_BLOCK, (j + 1) * SLC_BLOCK)
    return (np.clip(hi - lo, 0, None) / CMP_BLOCK).astype(np.float32)


def kernel(x, norm_g, w_in, cmp_pos_k, cmp_w1_k, cmp_w2_k, cmp_pos_v, cmp_w1_v, cmp_w2_v,
           w_br_nsa, w_br_dil, w_out, final_g):
    B, S, D = x.shape
    n_sel = S // SLC_BLOCK
    assert S % (Q_BLOCK * max(d for _, d in DIL_PATTERNS)) == 0 and n_sel % FLAG_BITS == 0
    n_cmp_pad = S // CMP_STRIDE
    slopes_nsa, slopes_dil = _alibi_slopes()
    proj_cols, gate_cols = _proj_columns(D)
    ovt = jnp.asarray(_overlap_t(n_cmp_pad, n_sel))
    qfeat = jnp.asarray(np.broadcast_to(
        _slope_features(slopes_nsa).reshape(NSA_GROUPS, NSA_HPG, 1, LANES),
        (NSA_GROUPS, NSA_HPG, 8, LANES)))
    qfeat_dil = jnp.asarray(np.broadcast_to(
        _slope_features(slopes_dil).reshape(DIL_HEADS, 1, LANES), (DIL_HEADS, 8, LANES)))
    tm = 512

    h = x
    for layer in range(w_in.shape[0]):
        w_proj = _take_columns(w_in[layer], proj_cols, BF16)
        w_gate = _take_columns(w_in[layer], gate_cols, BF16)
        g_l = norm_g[layer].reshape(1, D)

        qn, cmp_src, ks, kw, vst, vwt, gates, *dil = _proj_call(h, g_l, w_proj, tm)
        cw = _compress_weights(cmp_pos_k[layer], cmp_w1_k[layer], cmp_w2_k[layer],
                               cmp_pos_v[layer], cmp_w1_v[layer], cmp_w2_v[layer])
        kc, cvo = _compress_call(cmp_src, *cw, ovt)
        o_nsa = _nsa_call(qfeat, qn, kc, cvo, ks, vst, kw, vwt, gates)
        o_d, lse_d = [], []
        for i, (window, dilation) in enumerate(DIL_PATTERNS):
            assert Q_BLOCK - 1 <= window // dilation <= Q_BLOCK
            o, lse = _dilated_call(qfeat_dil, *dil[3 * i:3 * i + 3], window, dilation)
            o_d.append(o)
            lse_d.append(lse)
        h = _merge_call(h, g_l, w_gate, o_nsa, o_d, lse_d, w_br_nsa[layer].astype(BF16),
                        w_br_dil[layer].astype(BF16), w_out[layer].astype(BF16),
                        final_g.reshape(1, D), tm, final=layer == w_in.shape[0] - 1)
    return h
```

```python
import functools
import math

import numpy as np
import jax
import jax.numpy as jnp
from jax import lax
from jax.experimental import pallas as pl
from jax.experimental.pallas import tpu as pltpu

F32 = jnp.float32
BF16 = jnp.bfloat16

HEAD_DIM = 64
LANES = 128
NSA_HEADS = 8
NSA_GROUPS = 2
NSA_HPG = NSA_HEADS // NSA_GROUPS
CMP_BLOCK = 32
CMP_STRIDE = 16
SLC_BLOCK = 64
SLC_TOPK = 16
WIN = 512
FORCE_SCORE = 1.0e4
DIL_HEADS = 8
DIL_PATTERNS = ((128, 1), (512, 4), (2048, 16))
Q_BLOCK = 128
RMS_EPS = 1e-6
KEY_CHUNK = 128
VT_ROWS = 80
PAIR_VT_ROWS = 144
GATE_ROWS = 16
FEAT_SPLIT = 3
FLAG_BITS = 16
SLC_GROUP = 4
WIN_QB = 256
NSA_QB = 512
DIL_BLOCKS_PER_STEP = 4
N_FORCED = 3
assert FORCE_SCORE > NSA_HPG + 1
SEL_PAD_ROWS = 8
LOG2E = math.log2(math.e)
VMEM_LIMIT = 56 * 1024 * 1024


def _alibi_slopes():
    n = NSA_HEADS + DIL_HEADS
    s = 2.0 ** (-8.0 * np.arange(1, n + 1) / n)
    return s[0::2].astype(np.float32), s[1::2].astype(np.float32)


def _nt(a, b):
    return lax.dot_general(a, b, (((1,), (1,)), ((), ())), preferred_element_type=F32)


def _rms(x, g):
    return (x * lax.rsqrt(jnp.mean(x * x, axis=-1, keepdims=True) + RMS_EPS)) * g


def _head_tile(q_pairs, h):
    tile = q_pairs[:, (h // 2) * LANES:(h // 2 + 1) * LANES]
    return pltpu.roll(tile, HEAD_DIM, 1) if h % 2 else tile


def _pos_features(pos):
    lane = lax.broadcasted_iota(jnp.int32, pos.shape, 1) - HEAD_DIM
    hi = (pos // LANES).astype(F32)
    lo = (pos % LANES).astype(F32)
    return jnp.where(lane < 0, 0.0, jnp.where(lane < FEAT_SPLIT, hi, jnp.where(lane < 2 * FEAT_SPLIT, lo, 0.0)))


def _slope_features(slopes):
    out = np.zeros((len(slopes), LANES), np.float32)
    for h, s in enumerate(slopes):
        rest = np.float32(s) * np.float32(LOG2E)
        for i in range(FEAT_SPLIT):
            piece = np.float32(np.asarray(rest, np.float32).astype(BF16))
            out[h, HEAD_DIM + FEAT_SPLIT + i] = piece
            out[h, HEAD_DIM + i] = piece * LANES
            rest = np.float32(rest - piece)
    return out


_PROJ_SEGS = (("qn", 512), ("cmp", 256), ("ks", 256), ("kw", 256), ("vs", 128), ("vw", 128),
              ("gate", 128), ("qd", 512), ("kd", 512), ("vd", 512))
DIL_Q_W = DIL_HEADS * HEAD_DIM
_PROJ_OFFS = dict(zip((n for n, _ in _PROJ_SEGS),
                      zip(np.cumsum([0] + [w for _, w in _PROJ_SEGS])[:-1].tolist(),
                          np.cumsum([w for _, w in _PROJ_SEGS]).tolist())))


def _to_lane_tiles(scr, first, val):
    for j in range(val.shape[1] // LANES):
        scr[first + j] = val[:, j * LANES:(j + 1) * LANES]


def _regroup_rows(scr, lo, hi, out_ref, d, dtype):
    rows, width = scr.shape[1], (hi - lo) * LANES
    for p in range(d):
        for j in range(hi - lo):
            tile = scr[lo + j, pl.ds(p, rows // d, stride=d), :] if d > 1 else scr[lo + j]
            out_ref[:, p * width + j * LANES:p * width + (j + 1) * LANES] = tile.astype(dtype)


def _proj_kernel(x_ref, g_ref, w_ref, perm_ref, qn_ref, cmp_ref, ks_ref, kw_ref, vst_ref, vwt_ref,
                 gate_ref, *dil_refs):
    cmp_scr = dil_refs[-1]
    tm = x_ref.shape[0]
    xb = _rms(x_ref[...], g_ref[...]).astype(BF16)
    scale = HEAD_DIM ** -0.5

    def seg(name):
        c0, c1 = _PROJ_OFFS[name]
        return jnp.dot(xb, w_ref[:, c0:c1], preferred_element_type=F32)

    qn_ref[...] = (seg("qn") * (scale * LOG2E)).astype(BF16)
    _to_lane_tiles(cmp_scr, 0, seg("cmp"))
    _regroup_rows(cmp_scr, 0, cmp_scr.shape[0], cmp_ref, CMP_STRIDE, F32)
    feat = _pos_features(pl.program_id(1) * tm + lax.broadcasted_iota(jnp.int32, (tm, LANES), 0))
    ks, kw = seg("ks"), seg("kw")
    vst, vwt = seg("vs").T, seg("vw").T
    gate_t = jax.nn.sigmoid(seg("gate")).T
    tail_row = lax.broadcasted_iota(jnp.int32, (VT_ROWS - HEAD_DIM, tm), 0)
    tail = jnp.where(tail_row == 0, 1.0, 0.0)
    for g in range(NSA_GROUPS):
        ks_ref[g] = (ks[:, g * LANES:(g + 1) * LANES] + feat).astype(BF16)
        kw_ref[g] = (kw[:, g * LANES:(g + 1) * LANES] + feat).astype(BF16)
        gate_ref[g] = gate_t[g * GATE_ROWS:(g + 1) * GATE_ROWS]
        for src, dst in ((vst, vst_ref), (vwt, vwt_ref)):
            vt = jnp.concatenate([src[g * HEAD_DIM:(g + 1) * HEAD_DIM], tail], axis=0).astype(BF16)
            for j in range(tm // KEY_CHUNK):
                dst[g, j] = vt[:, j * KEY_CHUNK:(j + 1) * KEY_CHUNK]
    qkv = jnp.concatenate([(seg("qd") * (scale * LOG2E)).astype(BF16), seg("kd").astype(BF16),
                           seg("vd").astype(BF16)], axis=1)
    bounds = (0, DIL_Q_W, 2 * DIL_Q_W, 3 * DIL_Q_W)
    for i, (_, d) in enumerate(DIL_PATTERNS):
        by_phase = qkv if d == 1 else jnp.dot(perm_ref[i], qkv, preferred_element_type=F32).astype(BF16)
        for p in range(d):
            rows = by_phase[p * (tm // d):(p + 1) * (tm // d)]
            for j, ref in enumerate(dil_refs[3 * i:3 * i + 3]):
                width = bounds[j + 1] - bounds[j]
                ref[:, p * width:(p + 1) * width] = rows[:, bounds[j]:bounds[j + 1]]


def _proj_call(x, g, w, tm):
    B, S, D = x.shape
    G = NSA_GROUPS
    cmp_w = _PROJ_OFFS["cmp"][1] - _PROJ_OFFS["cmp"][0]
    row = lambda c: pl.BlockSpec((None, tm, c), lambda b, i: (b, i, 0))
    k_spec =pl.BlockSpec((None, G, tm, LANES), lambda b, i: (b, 0, i, 0))
    vt_spec = pl.BlockSpec((None, G, tm // KEY_CHUNK, VT_ROWS, KEY_CHUNK), lambda b, i: (b, 0, i, 0, 0))
    out_shape = (
        jax.ShapeDtypeStruct((B, S, 512), BF16),
        jax.ShapeDtypeStruct((B, S // CMP_STRIDE, CMP_STRIDE * cmp_w), F32),
        jax.ShapeDtypeStruct((B, G, S, LANES), BF16),
        jax.ShapeDtypeStruct((B, G, S, LANES), BF16),
        jax.ShapeDtypeStruct((B, G, S // KEY_CHUNK, VT_ROWS, KEY_CHUNK), BF16),
        jax.ShapeDtypeStruct((B, G, S // KEY_CHUNK, VT_ROWS, KEY_CHUNK), BF16),
        jax.ShapeDtypeStruct((B, G, GATE_ROWS, S), F32),
    )
    dil_specs = ()
    for _, d in DIL_PATTERNS:
        for width in (DIL_Q_W,) * 3:
            out_shape += (jax.ShapeDtypeStruct((B, S // d, d * width), BF16),)
            dil_specs += (pl.BlockSpec((None, tm // d, d * width), lambda b, i: (b, i, 0)),)
    r = np.arange(tm)
    perm = np.zeros((len(DIL_PATTERNS), tm, tm), np.float32)
    for i, (_, d) in enumerate(DIL_PATTERNS):
        perm[i, (r % d) * (tm // d) + r // d, r] = 1.0
    perm = jnp.asarray(perm, BF16)
    return pl.pallas_call(
        _proj_kernel,
        grid=(B, S // tm),
        in_specs=[row(D), pl.BlockSpec((1, D), lambda b, i: (0, 0)),
                  pl.BlockSpec(w.shape, lambda b, i: (0, 0)),
                  pl.BlockSpec(perm.shape, lambda b, i: (0, 0, 0))],
        out_specs=(row(512),
                   pl.BlockSpec((None, tm // CMP_STRIDE, CMP_STRIDE * cmp_w), lambda b, i: (b, i, 0)),
                   k_spec, k_spec, vt_spec, vt_spec,
                   pl.BlockSpec((None, G, GATE_ROWS, tm), lambda b, i: (b, 0, 0, i))) + dil_specs,
        out_shape=out_shape,
        scratch_shapes=[pltpu.VMEM((cmp_w // LANES, tm, LANES), F32)],
        compiler_params=pltpu.CompilerParams(
            dimension_semantics=("parallel", "parallel"), vmem_limit_bytes=VMEM_LIMIT),
        name="proj",
    )(x, g, w, perm)


def _compress_kernel(r_ref, posa_ref, posb_ref, wa_ref, wb_ref, w2_ref, ovt_ref, kc_ref, cvo_ref):
    r = r_ref[...]
    n = r.shape[0]
    ha = jnp.dot((r + posa_ref[...]).astype(BF16), wa_ref[...], preferred_element_type=F32)
    hb = jnp.dot((r + posb_ref[...]).astype(BF16), wb_ref[...], preferred_element_type=F32)
    hid = jax.nn.gelu(ha + pltpu.roll(hb, n - 1, 0))
    out = jnp.dot(hid.astype(BF16), w2_ref[...], preferred_element_type=F32)
    cmp_end = lax.broadcasted_iota(jnp.int32, (n, LANES), 0) * CMP_STRIDE + (CMP_BLOCK - 1)
    feat = _pos_features(cmp_end)
    v_t = out[:, NSA_GROUPS * LANES:].T
    for g in range(NSA_GROUPS):
        kc_ref[g] = (out[:, g * LANES:(g + 1) * LANES] + feat).astype(BF16)
        cvo_ref[g] = jnp.concatenate([v_t[g * HEAD_DIM:(g + 1) * HEAD_DIM], ovt_ref[...]],
                                     axis=0).astype(BF16)


def _compress_call(r, posa, posb, wa, wb, w2, ovt):
    B, R, C = r.shape
    G = NSA_GROUPS
    rows = HEAD_DIM + ovt.shape[0]
    full = lambda a: pl.BlockSpec(a.shape, lambda b: (0,) * a.ndim)
    return pl.pallas_call(
        _compress_kernel,
        grid=(B,),
        in_specs=[pl.BlockSpec((None, R, C), lambda b: (b, 0, 0)),
                  full(posa), full(posb), full(wa), full(wb), full(w2), full(ovt)],
        out_specs=(pl.BlockSpec((None, G, R, LANES), lambda b: (b, 0, 0, 0)),
                   pl.BlockSpec((None, G, rows, R), lambda b: (b, 0, 0, 0))),
        out_shape=(jax.ShapeDtypeStruct((B, G, R, LANES), BF16),
                   jax.ShapeDtypeStruct((B, G, rows, R), BF16)),
        compiler_params=pltpu.CompilerParams(
            dimension_semantics=("parallel",), vmem_limit_bytes=VMEM_LIMIT),
        name="compress",
    )(r, posa, posb, wa, wb, w2, ovt)


def _flash_step(s, bias, v_t, m_ref, acc_ref):
    ps, alphas = [], []
    qb = bias.shape[1]
    for h in range(NSA_HPG):
        cols = slice(h * qb, (h + 1) * qb)
        sh = s[:, cols] + bias
        m_old = m_ref[:, cols]
        m_new = jnp.maximum(m_old, jnp.max(sh, axis=0, keepdims=True))
        alphas.append(jnp.exp2(m_old - m_new))
        ps.append(jnp.exp2(sh - m_new).astype(BF16))
        m_ref[:, cols] = m_new
    pv = jnp.dot(v_t, jnp.concatenate(ps, axis=1), preferred_element_type=F32)
    acc_ref[...] = jnp.concatenate(alphas, axis=1) * acc_ref[...] + pv


def _flash_result(acc_ref):
    l = acc_ref[HEAD_DIM:HEAD_DIM + 1, :]
    return acc_ref[0:HEAD_DIM, :] * (1.0 / jnp.maximum(l, 1e-30))


def _softmax_pv(s, bias, v_t):
    ms, ps = [], []
    qb = bias.shape[1]
    for h in range(NSA_HPG):
        sh = s[:, h * qb:(h + 1) * qb] + bias
        ms.append(jnp.max(sh, axis=0, keepdims=True))
        ps.append(jnp.exp2(sh - ms[h]).astype(BF16))
    r = jnp.dot(v_t, jnp.concatenate(ps, axis=1), preferred_element_type=F32)
    return jnp.concatenate(ms, axis=1), r


def _normalise(acc):
    return acc[0:HEAD_DIM] * (1.0 / jnp.maximum(acc[HEAD_DIM:HEAD_DIM + 1], 1e-30))


def _nsa_kernel(qfeat_ref, q_ref, kc_ref, cvo_ref, ks_ref, vst_ref, kw_ref, vwt_ref, gate_ref,
                o_ref, sel_ref, ms_ref, accs_ref, flag_ref, list_ref):
    n = pl.program_id(2)
    qb, kc = NSA_QB, KEY_CHUNK
    t0 = n * qb
    lane = lax.broadcasted_iota(jnp.int32, (qb, LANES), 1)
    qf = q_ref[...].astype(F32)
    q = jnp.concatenate(
        [jnp.where(lane < HEAD_DIM, _head_tile(qf, h), qfeat_ref[h][0:1, :]) for h in range(NSA_HPG)],
        axis=0).astype(BF16)

    wq = min(WIN_QB, qb)
    span = WIN + wq
    back_w = (lax.broadcasted_iota(jnp.int32, (span, wq), 1)
              - lax.broadcasted_iota(jnp.int32, (span, wq), 0))
    win_parts = []
    for j in range(qb // wq):
        k_lo = pl.multiple_of(jnp.maximum(t0 + j * wq - WIN, 0), kc)
        dist = back_w + (t0 + j * wq - k_lo)
        bias = jnp.where(dist >= 0, jnp.where(dist < WIN, 0.0, -jnp.inf), -jnp.inf)
        q_part = jnp.concatenate([q[h * qb + j * wq:h * qb + (j + 1) * wq] for h in range(NSA_HPG)], axis=0)
        vt_w = jnp.concatenate([vwt_ref[k_lo // kc + u] for u in range(span // kc)], axis=1)
        win_parts.append(_normalise(_softmax_pv(_nt(kw_ref[pl.ds(k_lo, span), :], q_part), bias, vt_w)[1]))
    o_win = jnp.concatenate([part[:, h * wq:(h + 1) * wq] for h in range(NSA_HPG) for part in win_parts],
                            axis=1)

    n_cmp = kc_ref.shape[0]
    n_sel = sel_ref.shape[0] - SEL_PAD_ROWS
    s = _nt(kc_ref[...], q)
    cmp_end = lax.broadcasted_iota(jnp.int32, (n_cmp, qb), 0) * CMP_STRIDE + (CMP_BLOCK - 1)
    ok_c = cmp_end <= t0 + lax.broadcasted_iota(jnp.int32, (n_cmp, qb), 1)
    ps = []
    for h in range(NSA_HPG):
        sh = jnp.where(ok_c, s[:, h * qb:(h + 1) * qb], -jnp.inf)
        m = jnp.max(sh, axis=0, keepdims=True)
        m = jnp.where(m == -jnp.inf, 0.0, m)
        e = jnp.exp2(sh - m)
        l = jnp.sum(e, axis=0, keepdims=True)
        ps.append((e * (1.0 / jnp.maximum(l, 1e-30))).astype(BF16))
    r = jnp.dot(cvo_ref[...], jnp.concatenate(ps, axis=1), preferred_element_type=F32)
    o_cmp = r[0:HEAD_DIM]
    imp_t = r[HEAD_DIM:, 0:qb]
    for h in range(1, NSA_HPG):
        imp_t = imp_t + r[HEAD_DIM:, h * qb:(h + 1) * qb]

    blk = lax.broadcasted_iota(jnp.int32, (n_sel, qb), 0).astype(F32)
    cur = ((t0 + lax.broadcasted_iota(jnp.int32, (n_sel, qb), 1)) // SLC_BLOCK).astype(F32)
    ago = cur - blk
    forced = jnp.where(blk == 0, 1.0, jnp.where(ago == 0, 1.0, jnp.where(ago == 1, 1.0, 0.0)))
    valid = ago >= 0
    score = jnp.where(valid, jnp.where(forced > 0.5, -1.0, imp_t), -1.0)
    for _ in range(max(min(SLC_TOPK, n_sel) - N_FORCED, 0)):
        mx = jnp.max(score, axis=0, keepdims=True)
        idx = jnp.min(jnp.where(score == mx, blk, float(n_sel)), axis=0, keepdims=True)
        score = jnp.where(blk == idx, -jnp.inf, score)
    sel_t = jnp.where(valid, jnp.where(score == -jnp.inf, 1.0, forced), 0.0)
    sel_ref[0:n_sel, :] = jnp.where(sel_t > 0.5, 0.0, -jnp.inf)
    sel_ref[n_sel:, :] = jnp.full((SEL_PAD_ROWS, qb), -jnp.inf, F32)
    any_q = jnp.max(sel_t, axis=1, keepdims=True)
    bit = jnp.left_shift(1, lax.broadcasted_iota(jnp.int32, (n_sel, 1), 0) % FLAG_BITS).astype(F32)
    packed = any_q * bit
    for i in range(n_sel // FLAG_BITS):
        word = jnp.sum(packed[i * FLAG_BITS:(i + 1) * FLAG_BITS], axis=0, keepdims=True)
        flag_ref[i] = word.astype(jnp.int32)[0, 0]

    blocks_per_chunk = kc // SLC_BLOCK
    chunks_per_word = FLAG_BITS // blocks_per_chunk
    row = lax.broadcasted_iota(jnp.int32, (kc, qb), 0)

    def chunk_bias(first_block):
        bias = sel_ref[pl.ds(first_block, 1), :]
        for j in range(1, blocks_per_chunk):
            bias = jnp.where(row < j * SLC_BLOCK, bias, sel_ref[pl.ds(first_block + j, 1), :])
        return bias

    own = qb // kc
    first_own = t0 // kc
    off_first = jnp.where(n == 0, -jnp.inf, 0.0)
    query = lax.broadcasted_iota(jnp.int32, (kc, qb), 1)
    biases = [chunk_bias(0) + off_first]
    for u in range(own):
        causal = jnp.where(row + u * kc <= query, 0.0, -jnp.inf)
        biases.append(chunk_bias((first_own + u) * blocks_per_chunk) + causal)
    keys = jnp.concatenate([ks_ref[0:kc, :], ks_ref[pl.ds(pl.multiple_of(t0, kc), qb), :]], axis=0)
    vt_s = jnp.concatenate([vst_ref[0]] + [vst_ref[first_own + u] for u in range(own)], axis=1)
    m_s, acc_s = _softmax_pv(_nt(keys, q), jnp.concatenate(biases, axis=0), vt_s)
    ms_ref[...] = m_s
    accs_ref[...] = acc_s

    def scan_body(w, cnt):
        word = flag_ref[w]
        for j in range(chunks_per_word):
            c = w * chunks_per_word + j
            bits = (word >> (j * blocks_per_chunk)) & ((1 << blocks_per_chunk) - 1)
            list_ref[cnt] = c
            cnt = cnt + ((bits != 0) & (c >= 1) & (c < first_own)).astype(jnp.int32)
        return cnt

    cnt = lax.fori_loop(0, (first_own + chunks_per_word - 1) // chunks_per_word, scan_body, 0)
    for u in range(SLC_GROUP):
        list_ref[cnt + u] = -1

    def slc_body(gi, carry):
        keys, biases, vts = [], [], []
        for u in range(SLC_GROUP):
            c = list_ref[gi * SLC_GROUP + u]
            cc = jnp.maximum(c, 0)
            biases.append(chunk_bias(jnp.where(c >= 0, cc * blocks_per_chunk, n_sel)))
            keys.append(ks_ref[pl.ds(pl.multiple_of(cc * kc, kc), kc), :])
            vts.append(vst_ref[cc])
        _flash_step(_nt(jnp.concatenate(keys, axis=0), q), jnp.concatenate(biases, axis=0),
                    jnp.concatenate(vts, axis=1), ms_ref, accs_ref)
        return carry

    lax.fori_loop(0, (cnt + SLC_GROUP - 1) // SLC_GROUP, slc_body, 0)

    o_slc = _flash_result(accs_ref)
    gate = gate_ref[...]
    outs = []
    for h in range(NSA_HPG):
        cols = slice(h * qb, (h + 1) * qb)
        gc = [gate[j * NSA_HPG + h:j * NSA_HPG + h + 1, :] for j in range(3)]
        outs.append(gc[0] * o_cmp[:, cols] + gc[1] * o_slc[:, cols] + gc[2] * o_win[:, cols])
    o_ref[...] = jnp.concatenate(outs, axis=0).T


def _nsa_call(qfeat, qn, kc, cvo, ks, vst, kw, vwt, gates):
    B, S, _ = qn.shape
    G = NSA_GROUPS
    qb = NSA_QB
    nb = S // qb
    n_sel = S // SLC_BLOCK
    width = NSA_HPG * qb
    per_group = lambda a: pl.BlockSpec((None, None) + a.shape[2:],
                                       lambda b, g, n: (b, g) + (0,) * (a.ndim - 2))
    return pl.pallas_call(
        _nsa_kernel,
        grid=(B, G, nb),
        in_specs=[
            pl.BlockSpec((None,) + qfeat.shape[1:], lambda b, g, n: (g, 0, 0, 0)),
            pl.BlockSpec((None, qb, NSA_HPG * HEAD_DIM), lambda b, g, n: (b, n, g)),
            per_group(kc), per_group(cvo), per_group(ks), per_group(vst), per_group(kw), per_group(vwt),
            pl.BlockSpec((None, None, GATE_ROWS, qb), lambda b, g, n: (b, g, 0, n)),
        ],
        out_specs=pl.BlockSpec((None, qb, NSA_HPG * HEAD_DIM), lambda b, g, n: (b, n, g)),
        out_shape=jax.ShapeDtypeStruct((B, S, NSA_HEADS * HEAD_DIM), F32),
        scratch_shapes=[pltpu.VMEM((n_sel + SEL_PAD_ROWS, qb), F32),
                        pltpu.VMEM((1, width), F32), pltpu.VMEM((VT_ROWS, width), F32),
                        pltpu.SMEM((n_sel // FLAG_BITS,), jnp.int32),
                        pltpu.SMEM((S // KEY_CHUNK + SLC_GROUP,), jnp.int32)],
        compiler_params=pltpu.CompilerParams(
            dimension_semantics=("parallel", "parallel", "arbitrary"), vmem_limit_bytes=VMEM_LIMIT),
        name="nsa",
    )(qfeat, qn, kc, cvo, ks, vst, kw, vwt, gates)


def _dilated_kernel(qfeat_ref, q_ref, kp_ref, kc_ref, vp_ref, vc_ref, o_ref, lse_ref, *, wd, dilation):
    c = Q_BLOCK
    hd = HEAD_DIM
    blocks = q_ref.shape[0] // c
    lane = lax.broadcasted_iota(jnp.int32, (blocks * c, LANES), 1)
    key = lax.broadcasted_iota(jnp.int32, (2 * c, c), 0)
    qry = lax.broadcasted_iota(jnp.int32, (2 * c, c), 1)
    first = pl.program_id(2) == 0
    off_later = jnp.where(key < c, qry - (key + (wd - c)), (key - c) - qry)
    off_first = jnp.where(key < c, jnp.where(first, 2 * c, qry) - (key + (wd - c)), (key - c) - qry)
    bias_later = jnp.where(off_later <= 0, 0.0, -jnp.inf)
    bias_first = jnp.where(off_first <= 0, 0.0, -jnp.inf)
    sub = (pl.program_id(2) * blocks - 1) * c + lax.broadcasted_iota(jnp.int32, ((blocks + 1) * c, LANES), 0)
    pos_feat = _pos_features(sub * dilation + pl.program_id(1)).astype(BF16)
    tail_row = lax.broadcasted_iota(jnp.int32, (PAIR_VT_ROWS - 2 * hd, 2 * c), 0)
    tail = jnp.where(tail_row == 0, 1.0, 0.0).astype(BF16)
    for p in range(DIL_HEADS // 2):
        cols = slice(p * LANES, (p + 1) * LANES)
        q_pair = q_ref[:, cols]
        zero = jnp.zeros_like(q_pair)
        feats = [jnp.broadcast_to(qfeat_ref[2 * p + i][0:1, :], (blocks * c, LANES)).astype(BF16)
                 for i in range(2)]
        q_a = jnp.concatenate([jnp.where(lane < hd, q_pair, zero), feats[0]], axis=1)
        q_b = jnp.concatenate([jnp.where(lane < hd, zero, q_pair), feats[1]], axis=1)
        k_all = jnp.concatenate([kp_ref[:, cols], kc_ref[:, cols]], axis=0)
        k_all = jnp.concatenate([k_all, pos_feat], axis=1)
        vts = [vp_ref[:, cols].T] + [vc_ref[j * c:(j + 1) * c, cols].T for j in range(blocks)]
        for j in range(blocks):
            rows = slice(j * c, (j + 1) * c)
            q_aug = jnp.concatenate([q_a[rows], q_b[rows]], axis=0)
            s = _nt(k_all[j * c:(j + 2) * c], q_aug)
            bias = bias_later if j else bias_first
            ms, ps = [], []
            for i in range(2):
                si = s[:, i * c:(i + 1) * c] + bias
                ms.append(jnp.max(si, axis=0, keepdims=True))
                ps.append(jnp.exp2(si - ms[i]).astype(BF16))
            v_t = jnp.concatenate([jnp.concatenate(vts[j:j + 2], axis=1), tail], axis=0)
            r = jnp.dot(v_t, jnp.concatenate(ps, axis=1), preferred_element_type=F32)
            outs, lses = [], []
            for i in range(2):
                l = r[2 * hd:2 * hd + 1, i * c:(i + 1) * c]
                outs.append(r[i * hd:(i + 1) * hd, i * c:(i + 1) * c] * (1.0 / l))
                lses.append(jnp.broadcast_to(ms[i] + jnp.log2(l), (hd, c)))
            o_ref[rows, cols] = jnp.concatenate(outs, axis=0).T
            lse_ref[rows, cols] = jnp.concatenate(lses, axis=0).T


def _dilated_call(qfeat, q_v, k_v, v_v, window, dilation):
    B, L, _ = q_v.shape
    per_step = min(DIL_BLOCKS_PER_STEP, L // Q_BLOCK)
    nblk = L // (Q_BLOCK * per_step)
    blk = lambda cw: pl.BlockSpec((None, Q_BLOCK * per_step, cw), lambda b, p, i: (b, i, p))
    prev = pl.BlockSpec((None, Q_BLOCK, DIL_Q_W),
                        lambda b, p, i: (b, jnp.maximum(i * per_step - 1, 0), p))
    return pl.pallas_call(
        functools.partial(_dilated_kernel, wd=window // dilation, dilation=dilation),
        grid=(B, dilation, nblk),
        in_specs=[pl.BlockSpec(qfeat.shape, lambda b, p, i: (0, 0, 0)),
                  blk(DIL_Q_W), prev, blk(DIL_Q_W), prev, blk(DIL_Q_W)],
        out_specs=(blk(DIL_Q_W), blk(DIL_Q_W)),
        out_shape=(jax.ShapeDtypeStruct((B, L, dilation * DIL_Q_W), F32),
                   jax.ShapeDtypeStruct((B, L, dilation * DIL_Q_W), F32)),
        compiler_params=pltpu.CompilerParams(
            dimension_semantics=("parallel", "parallel", "arbitrary"), vmem_limit_bytes=VMEM_LIMIT),
        name=f"dilated_d{dilation}",
    )(qfeat, q_v, k_v, k_v, v_v, v_v)


def _merge_kernel(x_ref, g_ref, wg_ref, onsa_ref, o1_ref, o2_ref, o3_ref, l1_ref, l2_ref, l3_ref,
                  wbn_ref, wbd_ref, wo_ref, fg_ref, out_ref, tok_scr, *, final):
    x = x_ref[...]
    tm = x.shape[0]
    xb = _rms(x, g_ref[...]).astype(BF16)
    cz = NSA_HEADS * HEAD_DIM
    d = x.shape[-1]

    def gate_seg(c0, c1):
        return jnp.dot(xb, wg_ref[:, c0:c1], preferred_element_type=F32)

    def token_major(ref, slot, dil):
        if dil == 1:
            return ref[...]
        tiles = DIL_Q_W // LANES
        for p in range(dil):
            for j in range(tiles):
                tok_scr[slot * tiles + j, pl.ds(p, tm // dil, stride=dil), :] = \
                    ref[:, p * DIL_Q_W + j * LANES:p * DIL_Q_W + (j + 1) * LANES]
        return jnp.concatenate([tok_scr[slot * tiles + j] for j in range(tiles)], axis=1)

    o_nsa = onsa_ref[...] * jax.nn.silu(gate_seg(0, cz))
    dils = [dil for _, dil in DIL_PATTERNS]
    o1, o2, o3 = (token_major(r, i, dil) for i, (r, dil) in enumerate(zip((o1_ref, o2_ref, o3_ref), dils)))
    l1, l2, l3 = (token_major(r, 3 + i, dil) for i, (r, dil) in enumerate(zip((l1_ref, l2_ref, l3_ref), dils)))
    mx = jnp.maximum(jnp.maximum(l1, l2), l3)
    e1, e2, e3 = jnp.exp2(l1 - mx), jnp.exp2(l2 - mx), jnp.exp2(l3 - mx)
    den = e1 + e2 + e3
    o_dil = (e1 / den) * o1 + (e2 / den) * o2 + (e3 / den) * o3
    o_dil = o_dil * jax.nn.silu(gate_seg(cz, 2 * cz))
    a = jnp.dot(o_nsa.astype(BF16), wbn_ref[...], preferred_element_type=F32)
    bd = jnp.dot(o_dil.astype(BF16), wbd_ref[...], preferred_element_type=F32)
    merged = (jax.nn.sigmoid(gate_seg(2 * cz, 2 * cz + d)) * a
              + jax.nn.sigmoid(gate_seg(2 * cz + d, 2 * cz + 2 * d)) * bd)
    y = jnp.dot(merged.astype(BF16), wo_ref[...], preferred_element_type=F32)
    out_ref[...] = _rms(x + y, fg_ref[...]) if final else x + y


def _merge_call(x, g, wg, o_nsa, o_d, lse_d, wbn, wbd, wo, fg, tm, final):
    B, S, D = x.shape
    cz = o_nsa.shape[-1]
    row = lambda c: pl.BlockSpec((None, tm, c), lambda b, i: (b, i, 0))
    full = lambda a: pl.BlockSpec(a.shape, lambda b, i: (0,) * a.ndim)
    phase = [pl.BlockSpec((None, tm // dil, dil * cz), lambda b, i: (b, i, 0)) for _, dil in DIL_PATTERNS]
    return pl.pallas_call(
        functools.partial(_merge_kernel, final=final),
        grid=(B, S // tm),
        in_specs=[row(D), full(g), full(wg), row(cz)] + phase + phase
                 + [full(wbn), full(wbd), full(wo), full(fg)],
        out_specs=row(D),
        out_shape=jax.ShapeDtypeStruct((B, S, D), F32),
        scratch_shapes=[pltpu.VMEM((2 * len(DIL_PATTERNS) * cz // LANES, tm, LANES), F32)],
        compiler_params=pltpu.CompilerParams(
            dimension_semantics=("parallel", "parallel"), vmem_limit_bytes=VMEM_LIMIT),
        name="merge",
    )(x, g, wg, o_nsa, *o_d, *lse_d, wbn, wbd, wo, fg)


def _proj_columns(d_model):
    hd, G, H = HEAD_DIM, NSA_GROUPS, NSA_HEADS
    nsa_w = H * hd
    o_q, o_kv = 0, nsa_w
    o_g = o_kv + 6 * G * hd
    o_zn = o_g + 3 * H
    o_qkvd = o_zn + nsa_w
    dil_w = DIL_HEADS * hd
    o_zd = o_qkvd + 3 * dil_w
    o_mg = o_zd + dil_w
    kv = lambda j, g: o_kv + (j * G + g) * hd + np.arange(hd)
    pad = np.full(hd, -1)
    cols = [o_q + np.arange(nsa_w)]
    cols += [kv(j, g) for j in (0, 1) for g in range(G)]
    for j in (2, 4):
        for g in range(G):
            cols += [kv(j, g), pad]
    for j in (3, 5):
        cols += [kv(j, g) for g in range(G)]
    lanes = np.full(LANES, -1)
    for g in range(G):
        for j in range(3):
            lanes[g * GATE_ROWS + j * NSA_HPG + np.arange(NSA_HPG)] = o_g + j * H + g * NSA_HPG + np.arange(NSA_HPG)
    cols.append(lanes)
    cols.append(o_qkvd + np.arange(3 * dil_w))
    proj_cols = np.concatenate(cols)
    assert proj_cols.size == _PROJ_OFFS["vd"][1]
    gate_cols = np.concatenate([o_zn + np.arange(nsa_w), o_zd + np.arange(dil_w),
                                o_mg + np.arange(2 * d_model)])
    return proj_cols, gate_cols


def _take_columns(w, cols, dtype):
    pieces, start = [], 0
    for i in range(1, len(cols) + 1):
        if i < len(cols):
            same_run = (cols[i] < 0 and cols[i - 1] < 0) or (cols[i - 1] >= 0 and cols[i] == cols[i - 1] + 1)
        if i == len(cols) or not same_run:
            n = i - start
            pieces.append(jnp.zeros((w.shape[0], n), dtype) if cols[start] < 0
                          else w[:, int(cols[start]):int(cols[start]) + n].astype(dtype))
            start = i
    return jnp.concatenate(pieces, axis=1)


def _compress_weights(pos_k, w1_k, w2_k, pos_v, w1_v, w2_v):
    hd, G, half = HEAD_DIM, NSA_GROUPS, CMP_BLOCK // 2
    eye = jnp.eye(2 * G, dtype=F32)
    w1 = jnp.stack([w1_k] * G + [w1_v] * G)
    pos = jnp.stack([pos_k] * G + [pos_v] * G)

    def first_layer(lo):
        w = jnp.einsum('slde,st->lsdte', w1[:, lo:lo + half], eye)
        p = pos[:, lo:lo + half].transpose(1, 0, 2)
        return w.reshape(half * 2 * G * hd, 2 * G * hd).astype(BF16), p.reshape(1, half * 2 * G * hd)

    wa, posa = first_layer(0)
    wb, posb = first_layer(half)
    w2 = jnp.zeros((2 * G, hd, 3 * G, hd), F32)
    for g in range(G):
        w2 = w2.at[g, :, 2 * g].set(w2_k).at[G + g, :, 2 * G + g].set(w2_v)
    return posa, posb, wa, wb, w2.reshape(2 * G * hd, 3 * G * hd).astype(BF16)


def _overlap_t(n_cmp_pad, n_sel):
    i = np.arange(n_cmp_pad)[None, :]
    j = np.arange(n_sel)[:, None]
    lo = np.maximum(i * CMP_STRIDE, j * SLC_BLOCK)
    hi = np.minimum(i * CMP_STRIDE + CMP_BLOCK, (j + 1) * SLC_BLOCK)
    return (np.clip(hi - lo, 0, None) / CMP_BLOCK).astype(np.float32)


def kernel(x, norm_g, w_in, cmp_pos_k, cmp_w1_k, cmp_w2_k, cmp_pos_v, cmp_w1_v, cmp_w2_v,
           w_br_nsa, w_br_dil, w_out, final_g):
    B, S, D = x.shape
    n_sel = S // SLC_BLOCK
    assert S % (Q_BLOCK * max(d for _, d in DIL_PATTERNS)) == 0 and n_sel % FLAG_BITS == 0
    n_cmp_pad = S // CMP_STRIDE
    slopes_nsa, slopes_dil = _alibi_slopes()
    proj_cols, gate_cols = _proj_columns(D)
    ovt = jnp.asarray(_overlap_t(n_cmp_pad, n_sel))
    qfeat = jnp.asarray(np.broadcast_to(
        _slope_features(slopes_nsa).reshape(NSA_GROUPS, NSA_HPG, 1, LANES),
        (NSA_GROUPS, NSA_HPG, 8, LANES)))
    qfeat_dil = jnp.asarray(np.broadcast_to(
        _slope_features(slopes_dil).reshape(DIL_HEADS, 1, LANES), (DIL_HEADS, 8, LANES)))
    tm = 512

    h = x
    for layer in range(w_in.shape[0]):
        w_proj = _take_columns(w_in[layer], proj_cols, BF16)
        w_gate = _take_columns(w_in[layer], gate_cols, BF16)
        g_l = norm_g[layer].reshape(1, D)

        qn, cmp_src, ks, kw, vst, vwt, gates, *dil = _proj_call(h, g_l, w_proj, tm)
        cw = _compress_weights(cmp_pos_k[layer], cmp_w1_k[layer], cmp_w2_k[layer],
                               cmp_pos_v[layer], cmp_w1_v[layer], cmp_w2_v[layer])
        kc, cvo = _compress_call(cmp_src, *cw, ovt)
        o_nsa = _nsa_call(qfeat, qn, kc, cvo, ks, vst, kw, vwt, gates)
        o_d, lse_d = [], []
        for i, (window, dilation) in enumerate(DIL_PATTERNS):
            assert Q_BLOCK - 1 <= window // dilation <= Q_BLOCK
            o, lse = _dilated_call(qfeat_dil, *dil[3 * i:3 * i + 3], window, dilation)
            o_d.append(o)
            lse_d.append(lse)
        h = _merge_call(h, g_l, w_gate, o_nsa, o_d, lse_d, w_br_nsa[layer].astype(BF16),
                        w_br_dil[layer].astype(BF16), w_out[layer].astype(BF16),
                        final_g.reshape(1, D), tm, final=layer == w_in.shape[0] - 1)
    return h
```

```python
import functools
import math

import numpy as np
import jax
import jax.numpy as jnp
from jax import lax
from jax.experimental import pallas as pl
from jax.experimental.pallas import tpu as pltpu

F32 = jnp.float32
BF16 = jnp.bfloat16

HEAD_DIM = 64
LANES = 128
NSA_HEADS = 8
NSA_GROUPS = 2
NSA_HPG = NSA_HEADS // NSA_GROUPS
CMP_BLOCK = 32
CMP_STRIDE = 16
SLC_BLOCK = 64
SLC_TOPK = 16
WIN = 512
FORCE_SCORE = 1.0e4
DIL_HEADS = 8
DIL_PATTERNS = ((128, 1), (512, 4), (2048, 16))
Q_BLOCK = 128
RMS_EPS = 1e-6
KEY_CHUNK = 128
VT_ROWS = 80
PAIR_VT_ROWS = 144
GATE_ROWS = 16
FEAT_SPLIT = 3
FLAG_BITS = 16
SLC_GROUP = 4
PERM_ROWS = 128
WIN_QB = 256
NSA_QB = 512
DIL_BLOCKS_PER_STEP = 4
N_FORCED = 3
assert FORCE_SCORE > NSA_HPG + 1
SEL_PAD_ROWS = 8
LOG2E = math.log2(math.e)
VMEM_LIMIT = 56 * 1024 * 1024


def _alibi_slopes():
    n = NSA_HEADS + DIL_HEADS
    s = 2.0 ** (-8.0 * np.arange(1, n + 1) / n)
    return s[0::2].astype(np.float32), s[1::2].astype(np.float32)


def _nt(a, b):
    return lax.dot_general(a, b, (((1,), (1,)), ((), ())), preferred_element_type=F32)


def _rms(x, g):
    return (x * lax.rsqrt(jnp.mean(x * x, axis=-1, keepdims=True) + RMS_EPS)) * g


def _head_tile(q_pairs, h):
    tile = q_pairs[:, (h // 2) * LANES:(h // 2 + 1) * LANES]
    return pltpu.roll(tile, HEAD_DIM, 1) if h % 2 else tile


def _pos_features(pos):
    lane = lax.broadcasted_iota(jnp.int32, pos.shape, 1) - HEAD_DIM
    hi = (pos // LANES).astype(F32)
    lo = (pos % LANES).astype(F32)
    return jnp.where(lane < 0, 0.0, jnp.where(lane < FEAT_SPLIT, hi, jnp.where(lane < 2 * FEAT_SPLIT, lo, 0.0)))


def _slope_features(slopes):
    out = np.zeros((len(slopes), LANES), np.float32)
    for h, s in enumerate(slopes):
        rest = np.float32(s) * np.float32(LOG2E)
        for i in range(FEAT_SPLIT):
            piece = np.float32(np.asarray(rest, np.float32).astype(BF16))
            out[h, HEAD_DIM + FEAT_SPLIT + i] = piece
            out[h, HEAD_DIM + i] = piece * LANES
            rest = np.float32(rest - piece)
    return out


_PROJ_SEGS = (("qn", 512), ("cmp", 256), ("ks", 128), ("kw", 128), ("vs", 128), ("vw", 128),
              ("gate", 128), ("qd", 512), ("kd", 512), ("vd", 512))
DIL_Q_W = DIL_HEADS * HEAD_DIM
_PROJ_OFFS = dict(zip((n for n, _ in _PROJ_SEGS),
                      zip(np.cumsum([0] + [w for _, w in _PROJ_SEGS])[:-1].tolist(),
                          np.cumsum([w for _, w in _PROJ_SEGS]).tolist())))


def _to_lane_tiles(scr, first, val):
    for j in range(val.shape[1] // LANES):
        scr[first + j] = val[:, j * LANES:(j + 1) * LANES]


def _regroup_rows(scr, lo, hi, out_ref, d, dtype):
    rows, width = scr.shape[1], (hi - lo) * LANES
    for p in range(d):
        for j in range(hi - lo):
            tile = scr[lo + j, pl.ds(p, rows // d, stride=d), :] if d > 1 else scr[lo + j]
            out_ref[:, p * width + j * LANES:p * width + (j + 1) * LANES] = tile.astype(dtype)


def _proj_kernel(x_ref, g_ref, w_ref, perm_ref, qn_ref, cmp_ref, ks_ref, kw_ref, vst_ref, vwt_ref,
                 gate_ref, *dil_refs):
    cmp_scr = dil_refs[-1]
    tm = x_ref.shape[0]
    xb = _rms(x_ref[...], g_ref[...]).astype(BF16)
    scale = HEAD_DIM ** -0.5

    def seg(name, last=None):
        c0, c1 = _PROJ_OFFS[name][0], _PROJ_OFFS[last or name][1]
        return jnp.dot(xb, w_ref[:, c0:c1], preferred_element_type=F32)

    qn_ref[...] = (seg("qn") * (scale * LOG2E)).astype(BF16)
    _to_lane_tiles(cmp_scr, 0, seg("cmp"))
    _regroup_rows(cmp_scr, 0, cmp_scr.shape[0], cmp_ref, CMP_STRIDE, F32)
    feat = _pos_features(pl.program_id(1) * tm + lax.broadcasted_iota(jnp.int32, (tm, LANES), 0))
    k_sw, v_sw = seg("ks", "kw"), seg("vs", "vw")
    ks, kw = k_sw[:, 0:LANES], k_sw[:, LANES:]
    vst, vwt = v_sw[:, 0:LANES].T, v_sw[:, LANES:].T
    gate_t = jax.nn.sigmoid(seg("gate")).T
    tail_row = lax.broadcasted_iota(jnp.int32, (VT_ROWS - HEAD_DIM, tm), 0)
    tail = jnp.where(tail_row == 0, 1.0, 0.0)
    lane = lax.broadcasted_iota(jnp.int32, (tm, LANES), 1)
    for g in range(NSA_GROUPS):
        ks_ref[g] = jnp.where(lane < HEAD_DIM, _head_tile(ks, g), feat).astype(BF16)
        kw_ref[g] = jnp.where(lane < HEAD_DIM, _head_tile(kw, g), feat).astype(BF16)
        gate_ref[g] = gate_t[g * GATE_ROWS:(g + 1) * GATE_ROWS]
        for src, dst in ((vst, vst_ref), (vwt, vwt_ref)):
            vt = jnp.concatenate([src[g * HEAD_DIM:(g + 1) * HEAD_DIM], tail], axis=0).astype(BF16)
            for j in range(tm // KEY_CHUNK):
                dst[g, j] = vt[:, j * KEY_CHUNK:(j + 1) * KEY_CHUNK]
    qkv = jnp.concatenate([(seg("qd") * (scale * LOG2E)).astype(BF16), seg("kd").astype(BF16),
                           seg("vd").astype(BF16)], axis=1)
    bounds = (0, DIL_Q_W, 2 * DIL_Q_W, 3 * DIL_Q_W)
    subs = tm // PERM_ROWS
    for i, (_, d) in enumerate(DIL_PATTERNS):
        if d > 1:
            by_phase = [jnp.dot(perm_ref[i], qkv[s * PERM_ROWS:(s + 1) * PERM_ROWS],
                                preferred_element_type=F32) for s in range(subs)]
        r = PERM_ROWS // d
        for p in range(d):
            rows = qkv if d == 1 else jnp.concatenate(
                [by_phase[s][p * r:(p + 1) * r] for s in range(subs)], axis=0).astype(BF16)
            for j, ref in enumerate(dil_refs[3 * i:3 * i + 3]):
                width = bounds[j + 1] - bounds[j]
                ref[:, p * width:(p + 1) * width] = rows[:, bounds[j]:bounds[j + 1]]


def _proj_call(x, g, w, tm):
    B, S, D = x.shape
    G = NSA_GROUPS
    cmp_w = _PROJ_OFFS["cmp"][1] - _PROJ_OFFS["cmp"][0]
    row = lambda c: pl.BlockSpec((None, tm, c), lambda b, i: (b, i, 0))
    k_spec =pl.BlockSpec((None, G, tm, LANES), lambda b, i: (b, 0, i, 0))
    vt_spec = pl.BlockSpec((None, G, tm // KEY_CHUNK, VT_ROWS, KEY_CHUNK), lambda b, i: (b, 0, i, 0, 0))
    out_shape = (
        jax.ShapeDtypeStruct((B, S, 512), BF16),
        jax.ShapeDtypeStruct((B, S // CMP_STRIDE, CMP_STRIDE * cmp_w), F32),
        jax.ShapeDtypeStruct((B, G, S, LANES), BF16),
        jax.ShapeDtypeStruct((B, G, S, LANES), BF16),
        jax.ShapeDtypeStruct((B, G, S // KEY_CHUNK, VT_ROWS, KEY_CHUNK), BF16),
        jax.ShapeDtypeStruct((B, G, S // KEY_CHUNK, VT_ROWS, KEY_CHUNK), BF16),
        jax.ShapeDtypeStruct((B, G, GATE_ROWS, S), F32),
    )
    dil_specs = ()
    for _, d in DIL_PATTERNS:
        for width in (DIL_Q_W,) * 3:
            out_shape += (jax.ShapeDtypeStruct((B, S // d, d * width), BF16),)
            dil_specs += (pl.BlockSpec((None, tm // d, d * width), lambda b, i: (b, i, 0)),)
    r = np.arange(PERM_ROWS)
    perm = np.zeros((len(DIL_PATTERNS), PERM_ROWS, PERM_ROWS), np.float32)
    for i, (_, d) in enumerate(DIL_PATTERNS):
        assert (PERM_ROWS // d) % 8 == 0
        perm[i, (r % d) * (PERM_ROWS // d) + r // d, r] = 1.0
    perm = jnp.asarray(perm, BF16)
    return pl.pallas_call(
        _proj_kernel,
        grid=(B, S // tm),
        in_specs=[row(D), pl.BlockSpec((1, D), lambda b, i: (0, 0)),
                  pl.BlockSpec(w.shape, lambda b, i: (0, 0)),
                  pl.BlockSpec(perm.shape, lambda b, i: (0, 0, 0))],
        out_specs=(row(512),
                   pl.BlockSpec((None, tm // CMP_STRIDE, CMP_STRIDE * cmp_w), lambda b, i: (b, i, 0)),
                   k_spec, k_spec, vt_spec, vt_spec,
                   pl.BlockSpec((None, G, GATE_ROWS, tm), lambda b, i: (b, 0, 0, i))) + dil_specs,
        out_shape=out_shape,
        scratch_shapes=[pltpu.VMEM((cmp_w // LANES, tm, LANES), F32)],
        compiler_params=pltpu.CompilerParams(
            dimension_semantics=("parallel", "parallel"), vmem_limit_bytes=VMEM_LIMIT),
        name="proj",
    )(x, g, w, perm)


def _compress_kernel(r_ref, posa_ref, posb_ref, wa_ref, wb_ref, w2_ref, ovt_ref, kc_ref, cvo_ref):
    r = r_ref[...]
    n = r.shape[0]
    ha = jnp.dot((r + posa_ref[...]).astype(BF16), wa_ref[...], preferred_element_type=F32)
    hb = jnp.dot((r + posb_ref[...]).astype(BF16), wb_ref[...], preferred_element_type=F32)
    hid = jax.nn.gelu(ha + pltpu.roll(hb, n - 1, 0))
    out = jnp.dot(hid.astype(BF16), w2_ref[...], preferred_element_type=F32)
    cmp_end = lax.broadcasted_iota(jnp.int32, (n, LANES), 0) * CMP_STRIDE + (CMP_BLOCK - 1)
    feat = _pos_features(cmp_end)
    v_t = out[:, NSA_GROUPS * LANES:].T
    for g in range(NSA_GROUPS):
        kc_ref[g] = (out[:, g * LANES:(g + 1) * LANES] + feat).astype(BF16)
        cvo_ref[g] = jnp.concatenate([v_t[g * HEAD_DIM:(g + 1) * HEAD_DIM], ovt_ref[...]],
                                     axis=0).astype(BF16)


def _compress_call(r, posa, posb, wa, wb, w2, ovt):
    B, R, C = r.shape
    G = NSA_GROUPS
    rows = HEAD_DIM + ovt.shape[0]
    full = lambda a: pl.BlockSpec(a.shape, lambda b: (0,) * a.ndim)
    return pl.pallas_call(
        _compress_kernel,
        grid=(B,),
        in_specs=[pl.BlockSpec((None, R, C), lambda b: (b, 0, 0)),
                  full(posa), full(posb), full(wa), full(wb), full(w2), full(ovt)],
        out_specs=(pl.BlockSpec((None, G, R, LANES), lambda b: (b, 0, 0, 0)),
                   pl.BlockSpec((None, G, rows, R), lambda b: (b, 0, 0, 0))),
        out_shape=(jax.ShapeDtypeStruct((B, G, R, LANES), BF16),
                   jax.ShapeDtypeStruct((B, G, rows, R), BF16)),
        compiler_params=pltpu.CompilerParams(
            dimension_semantics=("parallel",), vmem_limit_bytes=VMEM_LIMIT),
        name="compress",
    )(r, posa, posb, wa, wb, w2, ovt)


def _flash_step(s, bias, v_t, m_ref, acc_ref):
    ps, alphas = [], []
    qb = bias.shape[1]
    for h in range(NSA_HPG):
        cols = slice(h * qb, (h + 1) * qb)
        sh = s[:, cols] + bias
        m_old = m_ref[:, cols]
        m_new = jnp.maximum(m_old, jnp.max(sh, axis=0, keepdims=True))
        alphas.append(jnp.exp2(m_old - m_new))
        ps.append(jnp.exp2(sh - m_new).astype(BF16))
        m_ref[:, cols] = m_new
    pv = jnp.dot(v_t, jnp.concatenate(ps, axis=1), preferred_element_type=F32)
    acc_ref[...] = jnp.concatenate(alphas, axis=1) * acc_ref[...] + pv


def _flash_result(acc_ref):
    l = acc_ref[HEAD_DIM:HEAD_DIM + 1, :]
    return acc_ref[0:HEAD_DIM, :] * (1.0 / jnp.maximum(l, 1e-30))


def _softmax_pv(s, bias, v_t):
    ms, ps = [], []
    qb = bias.shape[1]
    for h in range(NSA_HPG):
        sh = s[:, h * qb:(h + 1) * qb] + bias
        ms.append(jnp.max(sh, axis=0, keepdims=True))
        ps.append(jnp.exp2(sh - ms[h]).astype(BF16))
    r = jnp.dot(v_t, jnp.concatenate(ps, axis=1), preferred_element_type=F32)
    return jnp.concatenate(ms, axis=1), r


def _normalise(acc):
    return acc[0:HEAD_DIM] * (1.0 / jnp.maximum(acc[HEAD_DIM:HEAD_DIM + 1], 1e-30))


def _nsa_kernel(qfeat_ref, q_ref, kc_ref, cvo_ref, ks_ref, vst_ref, kw_ref, vwt_ref, gate_ref,
                o_ref, sel_ref, ms_ref, accs_ref, flag_ref, list_ref):
    n = pl.program_id(2)
    qb, kc = NSA_QB, KEY_CHUNK
    t0 = n * qb
    lane = lax.broadcasted_iota(jnp.int32, (qb, LANES), 1)
    qf = q_ref[...].astype(F32)
    q = jnp.concatenate(
        [jnp.where(lane < HEAD_DIM, _head_tile(qf, h), qfeat_ref[h][0:1, :]) for h in range(NSA_HPG)],
        axis=0).astype(BF16)

    wq = min(WIN_QB, qb)
    span = WIN + wq
    back_w = (lax.broadcasted_iota(jnp.int32, (span, wq), 1)
              - lax.broadcasted_iota(jnp.int32, (span, wq), 0))
    win_parts = []
    for j in range(qb // wq):
        k_lo = pl.multiple_of(jnp.maximum(t0 + j * wq - WIN, 0), kc)
        dist = back_w + (t0 + j * wq - k_lo)
        bias = jnp.where(dist >= 0, jnp.where(dist < WIN, 0.0, -jnp.inf), -jnp.inf)
        q_part = jnp.concatenate([q[h * qb + j * wq:h * qb + (j + 1) * wq] for h in range(NSA_HPG)], axis=0)
        vt_w = jnp.concatenate([vwt_ref[k_lo // kc + u] for u in range(span // kc)], axis=1)
        win_parts.append(_normalise(_softmax_pv(_nt(kw_ref[pl.ds(k_lo, span), :], q_part), bias, vt_w)[1]))
    o_win = jnp.concatenate([part[:, h * wq:(h + 1) * wq] for h in range(NSA_HPG) for part in win_parts],
                            axis=1)

    n_cmp = kc_ref.shape[0]
    n_sel = sel_ref.shape[0] - SEL_PAD_ROWS
    s = _nt(kc_ref[...], q)
    cmp_end = lax.broadcasted_iota(jnp.int32, (n_cmp, qb), 0) * CMP_STRIDE + (CMP_BLOCK - 1)
    ok_c = cmp_end <= t0 + lax.broadcasted_iota(jnp.int32, (n_cmp, qb), 1)
    ps = []
    for h in range(NSA_HPG):
        sh = jnp.where(ok_c, s[:, h * qb:(h + 1) * qb], -jnp.inf)
        m = jnp.max(sh, axis=0, keepdims=True)
        m = jnp.where(m == -jnp.inf, 0.0, m)
        e = jnp.exp2(sh - m)
        l = jnp.sum(e, axis=0, keepdims=True)
        ps.append((e * (1.0 / jnp.maximum(l, 1e-30))).astype(BF16))
    r = jnp.dot(cvo_ref[...], jnp.concatenate(ps, axis=1), preferred_element_type=F32)
    o_cmp = r[0:HEAD_DIM]
    imp_t = r[HEAD_DIM:, 0:qb]
    for h in range(1, NSA_HPG):
        imp_t = imp_t + r[HEAD_DIM:, h * qb:(h + 1) * qb]

    blk = lax.broadcasted_iota(jnp.int32, (n_sel, qb), 0).astype(F32)
    cur = ((t0 + lax.broadcasted_iota(jnp.int32, (n_sel, qb), 1)) // SLC_BLOCK).astype(F32)
    ago = cur - blk
    forced = jnp.where(blk == 0, 1.0, jnp.where(ago == 0, 1.0, jnp.where(ago == 1, 1.0, 0.0)))
    valid = ago >= 0
    score = jnp.where(valid, jnp.where(forced > 0.5, -1.0, imp_t), -1.0)
    for _ in range(max(min(SLC_TOPK, n_sel) - N_FORCED, 0)):
        mx = jnp.max(score, axis=0, keepdims=True)
        idx = jnp.min(jnp.where(score == mx, blk, float(n_sel)), axis=0, keepdims=True)
        score = jnp.where(blk == idx, -jnp.inf, score)
    sel_t = jnp.where(valid, jnp.where(score == -jnp.inf, 1.0, forced), 0.0)
    sel_ref[0:n_sel, :] = jnp.where(sel_t > 0.5, 0.0, -jnp.inf)
    sel_ref[n_sel:, :] = jnp.full((SEL_PAD_ROWS, qb), -jnp.inf, F32)
    any_q = jnp.max(sel_t, axis=1, keepdims=True)
    bit = jnp.left_shift(1, lax.broadcasted_iota(jnp.int32, (n_sel, 1), 0) % FLAG_BITS).astype(F32)
    packed = any_q * bit
    for i in range(n_sel // FLAG_BITS):
        word = jnp.sum(packed[i * FLAG_BITS:(i + 1) * FLAG_BITS], axis=0, keepdims=True)
        flag_ref[i] = word.astype(jnp.int32)[0, 0]

    blocks_per_chunk = kc // SLC_BLOCK
    chunks_per_word = FLAG_BITS // blocks_per_chunk
    row = lax.broadcasted_iota(jnp.int32, (kc, qb), 0)

    def chunk_bias(first_block):
        bias = sel_ref[pl.ds(first_block, 1), :]
        for j in range(1, blocks_per_chunk):
            bias = jnp.where(row < j * SLC_BLOCK, bias, sel_ref[pl.ds(first_block + j, 1), :])
        return bias

    own = qb // kc
    first_own = t0 // kc
    off_first = jnp.where(n == 0, -jnp.inf, 0.0)
    query = lax.broadcasted_iota(jnp.int32, (kc, qb), 1)
    biases = [chunk_bias(0) + off_first]
    for u in range(own):
        causal = jnp.where(row + u * kc <= query, 0.0, -jnp.inf)
        biases.append(chunk_bias((first_own + u) * blocks_per_chunk) + causal)
    keys = jnp.concatenate([ks_ref[0:kc, :], ks_ref[pl.ds(pl.multiple_of(t0, kc), qb), :]], axis=0)
    vt_s = jnp.concatenate([vst_ref[0]] + [vst_ref[first_own + u] for u in range(own)], axis=1)
    m_s, acc_s = _softmax_pv(_nt(keys, q), jnp.concatenate(biases, axis=0), vt_s)
    ms_ref[...] = m_s
    accs_ref[...] = acc_s

    def scan_body(w, cnt):
        word = flag_ref[w]
        for j in range(chunks_per_word):
            c = w * chunks_per_word + j
            bits = (word >> (j * blocks_per_chunk)) & ((1 << blocks_per_chunk) - 1)
            list_ref[cnt] = c
            cnt = cnt + ((bits != 0) & (c >= 1) & (c < first_own)).astype(jnp.int32)
        return cnt

    cnt = lax.fori_loop(0, (first_own + chunks_per_word - 1) // chunks_per_word, scan_body, 0)
    for u in range(SLC_GROUP):
        list_ref[cnt + u] = -1

    def slc_body(gi, carry):
        keys, biases, vts = [], [], []
        for u in range(SLC_GROUP):
            c = list_ref[gi * SLC_GROUP + u]
            cc = jnp.maximum(c, 0)
            biases.append(chunk_bias(jnp.where(c >= 0, cc * blocks_per_chunk, n_sel)))
            keys.append(ks_ref[pl.ds(pl.multiple_of(cc * kc, kc), kc), :])
            vts.append(vst_ref[cc])
        _flash_step(_nt(jnp.concatenate(keys, axis=0), q), jnp.concatenate(biases, axis=0),
                    jnp.concatenate(vts, axis=1), ms_ref, accs_ref)
        return carry

    lax.fori_loop(0, (cnt + SLC_GROUP - 1) // SLC_GROUP, slc_body, 0)

    o_slc = _flash_result(accs_ref)
    gate = gate_ref[...]
    outs = []
    for h in range(NSA_HPG):
        cols = slice(h * qb, (h + 1) * qb)
        gc = [gate[j * NSA_HPG + h:j * NSA_HPG + h + 1, :] for j in range(3)]
        outs.append(gc[0] * o_cmp[:, cols] + gc[1] * o_slc[:, cols] + gc[2] * o_win[:, cols])
    o_ref[...] = jnp.concatenate(outs, axis=0).T


def _nsa_call(qfeat, qn, kc, cvo, ks, vst, kw, vwt, gates):
    B, S, _ = qn.shape
    G = NSA_GROUPS
    qb = NSA_QB
    nb = S // qb
    n_sel = S // SLC_BLOCK
    width = NSA_HPG * qb
    per_group = lambda a: pl.BlockSpec((None, None) + a.shape[2:],
                                       lambda b, g, n: (b, g) + (0,) * (a.ndim - 2))
    return pl.pallas_call(
        _nsa_kernel,
        grid=(B, G, nb),
        in_specs=[
            pl.BlockSpec((None,) + qfeat.shape[1:], lambda b, g, n: (g, 0, 0, 0)),
            pl.BlockSpec((None, qb, NSA_HPG * HEAD_DIM), lambda b, g, n: (b, n, g)),
            per_group(kc), per_group(cvo), per_group(ks), per_group(vst), per_group(kw), per_group(vwt),
            pl.BlockSpec((None, None, GATE_ROWS, qb), lambda b, g, n: (b, g, 0, n)),
        ],
        out_specs=pl.BlockSpec((None, qb, NSA_HPG * HEAD_DIM), lambda b, g, n: (b, n, g)),
        out_shape=jax.ShapeDtypeStruct((B, S, NSA_HEADS * HEAD_DIM), F32),
        scratch_shapes=[pltpu.VMEM((n_sel + SEL_PAD_ROWS, qb), F32),
                        pltpu.VMEM((1, width), F32), pltpu.VMEM((VT_ROWS, width), F32),
                        pltpu.SMEM((n_sel // FLAG_BITS,), jnp.int32),
                        pltpu.SMEM((S // KEY_CHUNK + SLC_GROUP,), jnp.int32)],
        compiler_params=pltpu.CompilerParams(
            dimension_semantics=("parallel", "parallel", "arbitrary"), vmem_limit_bytes=VMEM_LIMIT),
        name="nsa",
    )(qfeat, qn, kc, cvo, ks, vst, kw, vwt, gates)


def _dilated_kernel(qfeat_ref, q_ref, kp_ref, kc_ref, vp_ref, vc_ref, o_ref, lse_ref, *, wd, dilation):
    c = Q_BLOCK
    hd = HEAD_DIM
    blocks = q_ref.shape[0] // c
    lane = lax.broadcasted_iota(jnp.int32, (blocks * c, LANES), 1)
    key = lax.broadcasted_iota(jnp.int32, (2 * c, c), 0)
    qry = lax.broadcasted_iota(jnp.int32, (2 * c, c), 1)
    first = pl.program_id(2) == 0
    off_later = jnp.where(key < c, qry - (key + (wd - c)), (key - c) - qry)
    off_first = jnp.where(key < c, jnp.where(first, 2 * c, qry) - (key + (wd - c)), (key - c) - qry)
    bias_later = jnp.where(off_later <= 0, 0.0, -jnp.inf)
    bias_first = jnp.where(off_first <= 0, 0.0, -jnp.inf)
    sub = (pl.program_id(2) * blocks - 1) * c + lax.broadcasted_iota(jnp.int32, ((blocks + 1) * c, LANES), 0)
    pos_feat = _pos_features(sub * dilation + pl.program_id(1)).astype(BF16)
    tail_row = lax.broadcasted_iota(jnp.int32, (PAIR_VT_ROWS - 2 * hd, 2 * c), 0)
    tail = jnp.where(tail_row == 0, 1.0, 0.0).astype(BF16)
    for p in range(DIL_HEADS // 2):
        cols = slice(p * LANES, (p + 1) * LANES)
        q_pair = q_ref[:, cols]
        zero = jnp.zeros_like(q_pair)
        feats = [jnp.broadcast_to(qfeat_ref[2 * p + i][0:1, :], (blocks * c, LANES)).astype(BF16)
                 for i in range(2)]
        q_a = jnp.concatenate([jnp.where(lane < hd, q_pair, zero), feats[0]], axis=1)
        q_b = jnp.concatenate([jnp.where(lane < hd, zero, q_pair), feats[1]], axis=1)
        k_all = jnp.concatenate([kp_ref[:, cols], kc_ref[:, cols]], axis=0)
        k_all = jnp.concatenate([k_all, pos_feat], axis=1)
        vts = [vp_ref[:, cols].T] + [vc_ref[j * c:(j + 1) * c, cols].T for j in range(blocks)]
        for j in range(blocks):
            rows = slice(j * c, (j + 1) * c)
            q_aug = jnp.concatenate([q_a[rows], q_b[rows]], axis=0)
            s = _nt(k_all[j * c:(j + 2) * c], q_aug)
            bias = bias_later if j else bias_first
            ms, ps = [], []
            for i in range(2):
                si = s[:, i * c:(i + 1) * c] + bias
                ms.append(jnp.max(si, axis=0, keepdims=True))
                ps.append(jnp.exp2(si - ms[i]).astype(BF16))
            v_t = jnp.concatenate([jnp.concatenate(vts[j:j + 2], axis=1), tail], axis=0)
            r = jnp.dot(v_t, jnp.concatenate(ps, axis=1), preferred_element_type=F32)
            outs, lses = [], []
            for i in range(2):
                l = r[2 * hd:2 * hd + 1, i * c:(i + 1) * c]
                outs.append(r[i * hd:(i + 1) * hd, i * c:(i + 1) * c] * (1.0 / l))
                lses.append(jnp.broadcast_to(ms[i] + jnp.log2(l), (hd, c)))
            o_ref[rows, cols] = jnp.concatenate(outs, axis=0).T
            lse_ref[rows, cols] = jnp.concatenate(lses, axis=0).T


def _dilated_call(qfeat, q_v, k_v, v_v, window, dilation):
    B, L, _ = q_v.shape
    per_step = min(DIL_BLOCKS_PER_STEP, L // Q_BLOCK)
    nblk = L // (Q_BLOCK * per_step)
    blk = lambda cw: pl.BlockSpec((None, Q_BLOCK * per_step, cw), lambda b, p, i: (b, i, p))
    prev = pl.BlockSpec((None, Q_BLOCK, DIL_Q_W),
                        lambda b, p, i: (b, jnp.maximum(i * per_step - 1, 0), p))
    return pl.pallas_call(
        functools.partial(_dilated_kernel, wd=window // dilation, dilation=dilation),
        grid=(B, dilation, nblk),
        in_specs=[pl.BlockSpec(qfeat.shape, lambda b, p, i: (0, 0, 0)),
                  blk(DIL_Q_W), prev, blk(DIL_Q_W), prev, blk(DIL_Q_W)],
        out_specs=(blk(DIL_Q_W), blk(DIL_Q_W)),
        out_shape=(jax.ShapeDtypeStruct((B, L, dilation * DIL_Q_W), F32),
                   jax.ShapeDtypeStruct((B, L, dilation * DIL_Q_W), F32)),
        compiler_params=pltpu.CompilerParams(
            dimension_semantics=("parallel", "parallel", "arbitrary"), vmem_limit_bytes=VMEM_LIMIT),
        name=f"dilated_d{dilation}",
    )(qfeat, q_v, k_v, k_v, v_v, v_v)


def _merge_kernel(x_ref, g_ref, wg_ref, onsa_ref, o1_ref, o2_ref, o3_ref, l1_ref, l2_ref, l3_ref,
                  wbn_ref, wbd_ref, wo_ref, fg_ref, out_ref, tok_scr, *, final):
    x = x_ref[...]
    tm = x.shape[0]
    xb = _rms(x, g_ref[...]).astype(BF16)
    cz = NSA_HEADS * HEAD_DIM
    d = x.shape[-1]

    def gate_seg(c0, c1):
        return jnp.dot(xb, wg_ref[:, c0:c1], preferred_element_type=F32)

    def token_major(ref, slot, dil):
        if dil == 1:
            return ref[...]
        tiles = DIL_Q_W // LANES
        for p in range(dil):
            for j in range(tiles):
                tok_scr[slot * tiles + j, pl.ds(p, tm // dil, stride=dil), :] = \
                    ref[:, p * DIL_Q_W + j * LANES:p * DIL_Q_W + (j + 1) * LANES]
        return jnp.concatenate([tok_scr[slot * tiles + j] for j in range(tiles)], axis=1)

    o_nsa = onsa_ref[...] * jax.nn.silu(gate_seg(0, cz))
    dils = [dil for _, dil in DIL_PATTERNS]
    o1, o2, o3 = (token_major(r, i, dil) for i, (r, dil) in enumerate(zip((o1_ref, o2_ref, o3_ref), dils)))
    l1, l2, l3 = (token_major(r, 3 + i, dil) for i, (r, dil) in enumerate(zip((l1_ref, l2_ref, l3_ref), dils)))
    mx = jnp.maximum(jnp.maximum(l1, l2), l3)
    e1, e2, e3 = jnp.exp2(l1 - mx), jnp.exp2(l2 - mx), jnp.exp2(l3 - mx)
    inv = 1.0 / (e1 + e2 + e3)
    o_dil = (e1 * inv) * o1 + (e2 * inv) * o2 + (e3 * inv) * o3
    o_dil = o_dil * jax.nn.silu(gate_seg(cz, 2 * cz))
    a = jnp.dot(o_nsa.astype(BF16), wbn_ref[...], preferred_element_type=F32)
    bd = jnp.dot(o_dil.astype(BF16), wbd_ref[...], preferred_element_type=F32)
    merged = (jax.nn.sigmoid(gate_seg(2 * cz, 2 * cz + d)) * a
              + jax.nn.sigmoid(gate_seg(2 * cz + d, 2 * cz + 2 * d)) * bd)
    y = jnp.dot(merged.astype(BF16), wo_ref[...], preferred_element_type=F32)
    out_ref[...] = _rms(x + y, fg_ref[...]) if final else x + y


def _merge_call(x, g, wg, o_nsa, o_d, lse_d, wbn, wbd, wo, fg, tm, final):
    B, S, D = x.shape
    cz = o_nsa.shape[-1]
    row = lambda c: pl.BlockSpec((None, tm, c), lambda b, i: (b, i, 0))
    full = lambda a: pl.BlockSpec(a.shape, lambda b, i: (0,) * a.ndim)
    phase = [pl.BlockSpec((None, tm // dil, dil * cz), lambda b, i: (b, i, 0)) for _, dil in DIL_PATTERNS]
    return pl.pallas_call(
        functools.partial(_merge_kernel, final=final),
        grid=(B, S // tm),
        in_specs=[row(D), full(g), full(wg), row(cz)] + phase + phase
                 + [full(wbn), full(wbd), full(wo), full(fg)],
        out_specs=row(D),
        out_shape=jax.ShapeDtypeStruct((B, S, D), F32),
        scratch_shapes=[pltpu.VMEM((2 * len(DIL_PATTERNS) * cz // LANES, tm, LANES), F32)],
        compiler_params=pltpu.CompilerParams(
            dimension_semantics=("parallel", "parallel"), vmem_limit_bytes=VMEM_LIMIT),
        name="merge",
    )(x, g, wg, o_nsa, *o_d, *lse_d, wbn, wbd, wo, fg)


def _proj_columns(d_model):
    hd, G, H = HEAD_DIM, NSA_GROUPS, NSA_HEADS
    nsa_w = H * hd
    o_q, o_kv = 0, nsa_w
    o_g = o_kv + 6 * G * hd
    o_zn = o_g + 3 * H
    o_qkvd = o_zn + nsa_w
    dil_w = DIL_HEADS * hd
    o_zd = o_qkvd + 3 * dil_w
    o_mg = o_zd + dil_w
    kv = lambda j, g: o_kv + (j * G + g) * hd + np.arange(hd)
    cols = [o_q + np.arange(nsa_w)]
    cols += [kv(j, g) for j in (0, 1) for g in range(G)]
    for j in (2, 4, 3, 5):
        cols += [kv(j, g) for g in range(G)]
    lanes = np.full(LANES, -1)
    for g in range(G):
        for j in range(3):
            lanes[g * GATE_ROWS + j * NSA_HPG + np.arange(NSA_HPG)] = o_g + j * H + g * NSA_HPG + np.arange(NSA_HPG)
    cols.append(lanes)
    cols.append(o_qkvd + np.arange(3 * dil_w))
    proj_cols = np.concatenate(cols)
    assert proj_cols.size == _PROJ_OFFS["vd"][1]
    gate_cols = np.concatenate([o_zn + np.arange(nsa_w), o_zd + np.arange(dil_w),
                                o_mg + np.arange(2 * d_model)])
    return proj_cols, gate_cols


def _take_columns(w, cols, dtype):
    pieces, start = [], 0
    for i in range(1, len(cols) + 1):
        if i < len(cols):
            same_run = (cols[i] < 0 and cols[i - 1] < 0) or (cols[i - 1] >= 0 and cols[i] == cols[i - 1] + 1)
        if i == len(cols) or not same_run:
            n = i - start
            pieces.append(jnp.zeros((w.shape[0], n), dtype) if cols[start] < 0
                          else w[:, int(cols[start]):int(cols[start]) + n].astype(dtype))
            start = i
    return jnp.concatenate(pieces, axis=1)


def _compress_weights(pos_k, w1_k, w2_k, pos_v, w1_v, w2_v):
    hd, G, half = HEAD_DIM, NSA_GROUPS, CMP_BLOCK // 2
    eye = jnp.eye(2 * G, dtype=F32)
    w1 = jnp.stack([w1_k] * G + [w1_v] * G)
    pos = jnp.stack([pos_k] * G + [pos_v] * G)

    def first_layer(lo):
        w = jnp.einsum('slde,st->lsdte', w1[:, lo:lo + half], eye)
        p = pos[:, lo:lo + half].transpose(1, 0, 2)
        return w.reshape(half * 2 * G * hd, 2 * G * hd).astype(BF16), p.reshape(1, half * 2 * G * hd)

    wa, posa = first_layer(0)
    wb, posb = first_layer(half)
    w2 = jnp.zeros((2 * G, hd, 3 * G, hd), F32)
    for g in range(G):
        w2 = w2.at[g, :, 2 * g].set(w2_k).at[G + g, :, 2 * G + g].set(w2_v)
    return posa, posb, wa, wb, w2.reshape(2 * G * hd, 3 * G * hd).astype(BF16)


def _overlap_t(n_cmp_pad, n_sel):
    i = np.arange(n_cmp_pad)[None, :]
    j = np.arange(n_sel)[:, None]
    lo = np.maximum(i * CMP_STRIDE, j * SLC_BLOCK)
    hi = np.minimum(i * CMP_STRIDE + CMP_BLOCK, (j + 1) * SLC_BLOCK)
    return (np.clip(hi - lo, 0, None) / CMP_BLOCK).astype(np.float32)


def kernel(x, norm_g, w_in, cmp_pos_k, cmp_w1_k, cmp_w2_k, cmp_pos_v, cmp_w1_v, cmp_w2_v,
           w_br_nsa, w_br_dil, w_out, final_g):
    B, S, D = x.shape
    n_sel = S // SLC_BLOCK
    assert S % (Q_BLOCK * max(d for _, d in DIL_PATTERNS)) == 0 and n_sel % FLAG_BITS == 0
    n_cmp_pad = S // CMP_STRIDE
    slopes_nsa, slopes_dil = _alibi_slopes()
    proj_cols, gate_cols = _proj_columns(D)
    ovt = jnp.asarray(_overlap_t(n_cmp_pad, n_sel))
    qfeat = jnp.asarray(np.broadcast_to(
        _slope_features(slopes_nsa).reshape(NSA_GROUPS, NSA_HPG, 1, LANES),
        (NSA_GROUPS, NSA_HPG, 8, LANES)))
    qfeat_dil = jnp.asarray(np.broadcast_to(
        _slope_features(slopes_dil).reshape(DIL_HEADS, 1, LANES), (DIL_HEADS, 8, LANES)))
    tm = 512

    h = x
    for layer in range(w_in.shape[0]):
        w_proj = _take_columns(w_in[layer], proj_cols, BF16)
        w_gate = _take_columns(w_in[layer], gate_cols, BF16)
        g_l = norm_g[layer].reshape(1, D)

        qn, cmp_src, ks, kw, vst, vwt, gates, *dil = _proj_call(h, g_l, w_proj, tm)
        cw = _compress_weights(cmp_pos_k[layer], cmp_w1_k[layer], cmp_w2_k[layer],
                               cmp_pos_v[layer], cmp_w1_v[layer], cmp_w2_v[layer])
        kc, cvo = _compress_call(cmp_src, *cw, ovt)
        o_nsa = _nsa_call(qfeat, qn, kc, cvo, ks, vst, kw, vwt, gates)
        o_d, lse_d = [], []
        for i, (window, dilation) in enumerate(DIL_PATTERNS):
            assert Q_BLOCK - 1 <= window // dilation <= Q_BLOCK
            o, lse = _dilated_call(qfeat_dil, *dil[3 * i:3 * i + 3], window, dilation)
            o_d.append(o)
            lse_d.append(lse)
        h = _merge_call(h, g_l, w_gate, o_nsa, o_d, lse_d, w_br_nsa[layer].astype(BF16),
                        w_br_dil[layer].astype(BF16), w_out[layer].astype(BF16),
                        final_g.reshape(1, D), tm, final=layer == w_in.shape[0] - 1)
    return h
```

```python
import functools
import math

import numpy as np
import jax
import jax.numpy as jnp
from jax import lax
from jax.experimental import pallas as pl
from jax.experimental.pallas import tpu as pltpu

F32 = jnp.float32
BF16 = jnp.bfloat16

HEAD_DIM = 64
LANES = 128
NSA_HEADS = 8
NSA_GROUPS = 2
NSA_HPG = NSA_HEADS // NSA_GROUPS
CMP_BLOCK = 32
CMP_STRIDE = 16
SLC_BLOCK = 64
SLC_TOPK = 16
WIN = 512
FORCE_SCORE = 1.0e4
DIL_HEADS = 8
DIL_PATTERNS = ((128, 1), (512, 4), (2048, 16))
Q_BLOCK = 128
RMS_EPS = 1e-6
KEY_CHUNK = 128
VT_ROWS = 80
PAIR_VT_ROWS = 144
GATE_ROWS = 16
FEAT_SPLIT = 3
FLAG_BITS = 16
SLC_GROUP = 4
PERM_ROWS = 128
WIN_QB = 256
NSA_QB = 512
DIL_BAND = 4
N_FORCED = 3
assert FORCE_SCORE > NSA_HPG + 1
SEL_PAD_ROWS = 8
LOG2E = math.log2(math.e)
VMEM_LIMIT = 56 * 1024 * 1024


def _alibi_slopes():
    n = NSA_HEADS + DIL_HEADS
    s = 2.0 ** (-8.0 * np.arange(1, n + 1) / n)
    return s[0::2].astype(np.float32), s[1::2].astype(np.float32)


def _nt(a, b):
    return lax.dot_general(a, b, (((1,), (1,)), ((), ())), preferred_element_type=F32)


def _rms(x, g):
    return (x * lax.rsqrt(jnp.mean(x * x, axis=-1, keepdims=True) + RMS_EPS)) * g


def _head_tile(q_pairs, h):
    tile = q_pairs[:, (h // 2) * LANES:(h // 2 + 1) * LANES]
    return pltpu.roll(tile, HEAD_DIM, 1) if h % 2 else tile


def _pos_features(pos):
    lane = lax.broadcasted_iota(jnp.int32, pos.shape, 1) - HEAD_DIM
    hi = (pos // LANES).astype(F32)
    lo = (pos % LANES).astype(F32)
    return jnp.where(lane < 0, 0.0, jnp.where(lane < FEAT_SPLIT, hi, jnp.where(lane < 2 * FEAT_SPLIT, lo, 0.0)))


def _slope_features(slopes):
    out = np.zeros((len(slopes), LANES), np.float32)
    for h, s in enumerate(slopes):
        rest = np.float32(s) * np.float32(LOG2E)
        for i in range(FEAT_SPLIT):
            piece = np.float32(np.asarray(rest, np.float32).astype(BF16))
            out[h, HEAD_DIM + FEAT_SPLIT + i] = piece
            out[h, HEAD_DIM + i] = piece * LANES
            rest = np.float32(rest - piece)
    return out


_PROJ_SEGS = (("qn", 512), ("cmp", 256), ("ks", 128), ("kw", 128), ("vs", 128), ("vw", 128),
              ("gate", 128), ("qd", 512), ("kd", 512), ("vd", 512))
DIL_Q_W = DIL_HEADS * HEAD_DIM
_PROJ_OFFS = dict(zip((n for n, _ in _PROJ_SEGS),
                      zip(np.cumsum([0] + [w for _, w in _PROJ_SEGS])[:-1].tolist(),
                          np.cumsum([w for _, w in _PROJ_SEGS]).tolist())))


def _to_lane_tiles(scr, first, val):
    for j in range(val.shape[1] // LANES):
        scr[first + j] = val[:, j * LANES:(j + 1) * LANES]


def _regroup_rows(scr, lo, hi, out_ref, d, dtype):
    rows, width = scr.shape[1], (hi - lo) * LANES
    for p in range(d):
        for j in range(hi - lo):
            tile = scr[lo + j, pl.ds(p, rows // d, stride=d), :] if d > 1 else scr[lo + j]
            out_ref[:, p * width + j * LANES:p * width + (j + 1) * LANES] = tile.astype(dtype)


def _proj_kernel(x_ref, g_ref, w_ref, perm_ref, qn_ref, cmp_ref, ks_ref, kw_ref, vst_ref, vwt_ref,
                 gate_ref, *dil_refs):
    cmp_scr = dil_refs[-1]
    tm = x_ref.shape[0]
    xb = _rms(x_ref[...], g_ref[...]).astype(BF16)
    scale = HEAD_DIM ** -0.5

    def seg(name, last=None):
        c0, c1 = _PROJ_OFFS[name][0], _PROJ_OFFS[last or name][1]
        return jnp.dot(xb, w_ref[:, c0:c1], preferred_element_type=F32)

    qn_ref[...] = (seg("qn") * (scale * LOG2E)).astype(BF16)
    _to_lane_tiles(cmp_scr, 0, seg("cmp"))
    _regroup_rows(cmp_scr, 0, cmp_scr.shape[0], cmp_ref, CMP_STRIDE, F32)
    feat = _pos_features(pl.program_id(1) * tm + lax.broadcasted_iota(jnp.int32, (tm, LANES), 0))
    k_sw, v_sw = seg("ks", "kw"), seg("vs", "vw")
    ks, kw = k_sw[:, 0:LANES], k_sw[:, LANES:]
    vst, vwt = v_sw[:, 0:LANES].T, v_sw[:, LANES:].T
    gate_t = jax.nn.sigmoid(seg("gate")).T
    tail_row = lax.broadcasted_iota(jnp.int32, (VT_ROWS - HEAD_DIM, tm), 0)
    tail = jnp.where(tail_row == 0, 1.0, 0.0)
    lane = lax.broadcasted_iota(jnp.int32, (tm, LANES), 1)
    for g in range(NSA_GROUPS):
        ks_ref[g] = jnp.where(lane < HEAD_DIM, _head_tile(ks, g), feat).astype(BF16)
        kw_ref[g] = jnp.where(lane < HEAD_DIM, _head_tile(kw, g), feat).astype(BF16)
        gate_ref[g] = gate_t[g * GATE_ROWS:(g + 1) * GATE_ROWS]
        for src, dst in ((vst, vst_ref), (vwt, vwt_ref)):
            vt = jnp.concatenate([src[g * HEAD_DIM:(g + 1) * HEAD_DIM], tail], axis=0).astype(BF16)
            for j in range(tm // KEY_CHUNK):
                dst[g, j] = vt[:, j * KEY_CHUNK:(j + 1) * KEY_CHUNK]
    qkv = jnp.concatenate([(seg("qd") * (scale * LOG2E)).astype(BF16), seg("kd").astype(BF16),
                           seg("vd").astype(BF16)], axis=1)
    bounds = (0, DIL_Q_W, 2 * DIL_Q_W, 3 * DIL_Q_W)
    subs = tm // PERM_ROWS
    for i, (_, d) in enumerate(DIL_PATTERNS):
        if d > 1:
            by_phase = [jnp.dot(perm_ref[i], qkv[s * PERM_ROWS:(s + 1) * PERM_ROWS],
                                preferred_element_type=F32) for s in range(subs)]
        r = PERM_ROWS // d
        for p in range(d):
            rows = qkv if d == 1 else jnp.concatenate(
                [by_phase[s][p * r:(p + 1) * r] for s in range(subs)], axis=0).astype(BF16)
            for j, ref in enumerate(dil_refs[3 * i:3 * i + 3]):
                width = bounds[j + 1] - bounds[j]
                ref[:, p * width:(p + 1) * width] = rows[:, bounds[j]:bounds[j + 1]]


def _proj_call(x, g, w, tm):
    B, S, D = x.shape
    G = NSA_GROUPS
    cmp_w = _PROJ_OFFS["cmp"][1] - _PROJ_OFFS["cmp"][0]
    row = lambda c: pl.BlockSpec((None, tm, c), lambda b, i: (b, i, 0))
    k_spec =pl.BlockSpec((None, G, tm, LANES), lambda b, i: (b, 0, i, 0))
    vt_spec = pl.BlockSpec((None, G, tm // KEY_CHUNK, VT_ROWS, KEY_CHUNK), lambda b, i: (b, 0, i, 0, 0))
    out_shape = (
        jax.ShapeDtypeStruct((B, S, 512), BF16),
        jax.ShapeDtypeStruct((B, S // CMP_STRIDE, CMP_STRIDE * cmp_w), F32),
        jax.ShapeDtypeStruct((B, G, S, LANES), BF16),
        jax.ShapeDtypeStruct((B, G, S, LANES), BF16),
        jax.ShapeDtypeStruct((B, G, S // KEY_CHUNK, VT_ROWS, KEY_CHUNK), BF16),
        jax.ShapeDtypeStruct((B, G, S // KEY_CHUNK, VT_ROWS, KEY_CHUNK), BF16),
        jax.ShapeDtypeStruct((B, G, GATE_ROWS, S), F32),
    )
    dil_specs = ()
    for _, d in DIL_PATTERNS:
        for width in (DIL_Q_W,) * 3:
            out_shape += (jax.ShapeDtypeStruct((B, S // d, d * width), BF16),)
            dil_specs += (pl.BlockSpec((None, tm // d, d * width), lambda b, i: (b, i, 0)),)
    r = np.arange(PERM_ROWS)
    perm = np.zeros((len(DIL_PATTERNS), PERM_ROWS, PERM_ROWS), np.float32)
    for i, (_, d) in enumerate(DIL_PATTERNS):
        assert (PERM_ROWS // d) % 8 == 0
        perm[i, (r % d) * (PERM_ROWS // d) + r // d, r] = 1.0
    perm = jnp.asarray(perm, BF16)
    return pl.pallas_call(
        _proj_kernel,
        grid=(B, S // tm),
        in_specs=[row(D), pl.BlockSpec((1, D), lambda b, i: (0, 0)),
                  pl.BlockSpec(w.shape, lambda b, i: (0, 0)),
                  pl.BlockSpec(perm.shape, lambda b, i: (0, 0, 0))],
        out_specs=(row(512),
                   pl.BlockSpec((None, tm // CMP_STRIDE, CMP_STRIDE * cmp_w), lambda b, i: (b, i, 0)),
                   k_spec, k_spec, vt_spec, vt_spec,
                   pl.BlockSpec((None, G, GATE_ROWS, tm), lambda b, i: (b, 0, 0, i))) + dil_specs,
        out_shape=out_shape,
        scratch_shapes=[pltpu.VMEM((cmp_w // LANES, tm, LANES), F32)],
        compiler_params=pltpu.CompilerParams(
            dimension_semantics=("parallel", "parallel"), vmem_limit_bytes=VMEM_LIMIT),
        name="proj",
    )(x, g, w, perm)


def _compress_kernel(r_ref, posa_ref, posb_ref, wa_ref, wb_ref, w2_ref, ovt_ref, kc_ref, cvo_ref):
    r = r_ref[...]
    n = r.shape[0]
    ha = jnp.dot((r + posa_ref[...]).astype(BF16), wa_ref[...], preferred_element_type=F32)
    hb = jnp.dot((r + posb_ref[...]).astype(BF16), wb_ref[...], preferred_element_type=F32)
    hid = jax.nn.gelu(ha + pltpu.roll(hb, n - 1, 0))
    out = jnp.dot(hid.astype(BF16), w2_ref[...], preferred_element_type=F32)
    cmp_end = lax.broadcasted_iota(jnp.int32, (n, LANES), 0) * CMP_STRIDE + (CMP_BLOCK - 1)
    feat = _pos_features(cmp_end)
    v_t = out[:, NSA_GROUPS * LANES:].T
    for g in range(NSA_GROUPS):
        kc_ref[g] = (out[:, g * LANES:(g + 1) * LANES] + feat).astype(BF16)
        cvo_ref[g] = jnp.concatenate([v_t[g * HEAD_DIM:(g + 1) * HEAD_DIM], ovt_ref[...]],
                                     axis=0).astype(BF16)


def _compress_call(r, posa, posb, wa, wb, w2, ovt):
    B, R, C = r.shape
    G = NSA_GROUPS
    rows = HEAD_DIM + ovt.shape[0]
    full = lambda a: pl.BlockSpec(a.shape, lambda b: (0,) * a.ndim)
    return pl.pallas_call(
        _compress_kernel,
        grid=(B,),
        in_specs=[pl.BlockSpec((None, R, C), lambda b: (b, 0, 0)),
                  full(posa), full(posb), full(wa), full(wb), full(w2), full(ovt)],
        out_specs=(pl.BlockSpec((None, G, R, LANES), lambda b: (b, 0, 0, 0)),
                   pl.BlockSpec((None, G, rows, R), lambda b: (b, 0, 0, 0))),
        out_shape=(jax.ShapeDtypeStruct((B, G, R, LANES), BF16),
                   jax.ShapeDtypeStruct((B, G, rows, R), BF16)),
        compiler_params=pltpu.CompilerParams(
            dimension_semantics=("parallel",), vmem_limit_bytes=VMEM_LIMIT),
        name="compress",
    )(r, posa, posb, wa, wb, w2, ovt)


def _flash_step(s, bias, v_t, m_ref, acc_ref):
    ps, alphas = [], []
    qb = bias.shape[1]
    for h in range(NSA_HPG):
        cols = slice(h * qb, (h + 1) * qb)
        sh = s[:, cols] + bias
        m_old = m_ref[:, cols]
        m_new = jnp.maximum(m_old, jnp.max(sh, axis=0, keepdims=True))
        alphas.append(jnp.exp2(m_old - m_new))
        ps.append(jnp.exp2(sh - m_new).astype(BF16))
        m_ref[:, cols] = m_new
    pv = jnp.dot(v_t, jnp.concatenate(ps, axis=1), preferred_element_type=F32)
    acc_ref[...] = jnp.concatenate(alphas, axis=1) * acc_ref[...] + pv


def _flash_result(acc_ref):
    l = acc_ref[HEAD_DIM:HEAD_DIM + 1, :]
    return acc_ref[0:HEAD_DIM, :] * (1.0 / jnp.maximum(l, 1e-30))


def _softmax_pv(s, bias, v_t):
    ms, ps = [], []
    qb = bias.shape[1]
    for h in range(NSA_HPG):
        sh = s[:, h * qb:(h + 1) * qb] + bias
        ms.append(jnp.max(sh, axis=0, keepdims=True))
        ps.append(jnp.exp2(sh - ms[h]).astype(BF16))
    r = jnp.dot(v_t, jnp.concatenate(ps, axis=1), preferred_element_type=F32)
    return jnp.concatenate(ms, axis=1), r


def _normalise(acc):
    return acc[0:HEAD_DIM] * (1.0 / jnp.maximum(acc[HEAD_DIM:HEAD_DIM + 1], 1e-30))


def _nsa_kernel(qfeat_ref, q_ref, kc_ref, cvo_ref, ks_ref, vst_ref, kw_ref, vwt_ref, gate_ref,
                o_ref, sel_ref, ms_ref, accs_ref, flag_ref, list_ref):
    n = pl.program_id(2)
    qb, kc = NSA_QB, KEY_CHUNK
    t0 = n * qb
    lane = lax.broadcasted_iota(jnp.int32, (qb, LANES), 1)
    qf = q_ref[...].astype(F32)
    q = jnp.concatenate(
        [jnp.where(lane < HEAD_DIM, _head_tile(qf, h), qfeat_ref[h][0:1, :]) for h in range(NSA_HPG)],
        axis=0).astype(BF16)

    wq = min(WIN_QB, qb)
    span = WIN + wq
    back_w = (lax.broadcasted_iota(jnp.int32, (span, wq), 1)
              - lax.broadcasted_iota(jnp.int32, (span, wq), 0))
    win_parts = []
    for j in range(qb // wq):
        k_lo = pl.multiple_of(jnp.maximum(t0 + j * wq - WIN, 0), kc)
        dist = back_w + (t0 + j * wq - k_lo)
        bias = jnp.where(dist >= 0, jnp.where(dist < WIN, 0.0, -jnp.inf), -jnp.inf)
        q_part = jnp.concatenate([q[h * qb + j * wq:h * qb + (j + 1) * wq] for h in range(NSA_HPG)], axis=0)
        vt_w = jnp.concatenate([vwt_ref[k_lo // kc + u] for u in range(span // kc)], axis=1)
        win_parts.append(_normalise(_softmax_pv(_nt(kw_ref[pl.ds(k_lo, span), :], q_part), bias, vt_w)[1]))
    o_win = jnp.concatenate([part[:, h * wq:(h + 1) * wq] for h in range(NSA_HPG) for part in win_parts],
                            axis=1)

    n_cmp = kc_ref.shape[0]
    n_sel = sel_ref.shape[0] - SEL_PAD_ROWS
    s = _nt(kc_ref[...], q)
    cmp_end = lax.broadcasted_iota(jnp.int32, (n_cmp, qb), 0) * CMP_STRIDE + (CMP_BLOCK - 1)
    ok_c = cmp_end <= t0 + lax.broadcasted_iota(jnp.int32, (n_cmp, qb), 1)
    ps = []
    for h in range(NSA_HPG):
        sh = jnp.where(ok_c, s[:, h * qb:(h + 1) * qb], -jnp.inf)
        m = jnp.max(sh, axis=0, keepdims=True)
        m = jnp.where(m == -jnp.inf, 0.0, m)
        e = jnp.exp2(sh - m)
        l = jnp.sum(e, axis=0, keepdims=True)
        ps.append((e * (1.0 / jnp.maximum(l, 1e-30))).astype(BF16))
    r = jnp.dot(cvo_ref[...], jnp.concatenate(ps, axis=1), preferred_element_type=F32)
    o_cmp = r[0:HEAD_DIM]
    imp_t = r[HEAD_DIM:, 0:qb]
    for h in range(1, NSA_HPG):
        imp_t = imp_t + r[HEAD_DIM:, h * qb:(h + 1) * qb]

    blk = lax.broadcasted_iota(jnp.int32, (n_sel, qb), 0).astype(F32)
    cur = ((t0 + lax.broadcasted_iota(jnp.int32, (n_sel, qb), 1)) // SLC_BLOCK).astype(F32)
    ago = cur - blk
    forced = jnp.where(blk == 0, 1.0, jnp.where(ago == 0, 1.0, jnp.where(ago == 1, 1.0, 0.0)))
    valid = ago >= 0
    score = jnp.where(valid, jnp.where(forced > 0.5, -1.0, imp_t), -1.0)
    for _ in range(max(min(SLC_TOPK, n_sel) - N_FORCED, 0)):
        mx = jnp.max(score, axis=0, keepdims=True)
        idx = jnp.min(jnp.where(score == mx, blk, float(n_sel)), axis=0, keepdims=True)
        score = jnp.where(blk == idx, -jnp.inf, score)
    sel_t = jnp.where(valid, jnp.where(score == -jnp.inf, 1.0, forced), 0.0)
    sel_ref[0:n_sel, :] = jnp.where(sel_t > 0.5, 0.0, -jnp.inf)
    sel_ref[n_sel:, :] = jnp.full((SEL_PAD_ROWS, qb), -jnp.inf, F32)
    any_q = jnp.max(sel_t, axis=1, keepdims=True)
    bit = jnp.left_shift(1, lax.broadcasted_iota(jnp.int32, (n_sel, 1), 0) % FLAG_BITS).astype(F32)
    packed = any_q * bit
    for i in range(n_sel // FLAG_BITS):
        word = jnp.sum(packed[i * FLAG_BITS:(i + 1) * FLAG_BITS], axis=0, keepdims=True)
        flag_ref[i] = word.astype(jnp.int32)[0, 0]

    blocks_per_chunk = kc // SLC_BLOCK
    chunks_per_word = FLAG_BITS // blocks_per_chunk
    row = lax.broadcasted_iota(jnp.int32, (kc, qb), 0)

    def chunk_bias(first_block):
        bias = sel_ref[pl.ds(first_block, 1), :]
        for j in range(1, blocks_per_chunk):
            bias = jnp.where(row < j * SLC_BLOCK, bias, sel_ref[pl.ds(first_block + j, 1), :])
        return bias

    own = qb // kc
    first_own = t0 // kc
    off_first = jnp.where(n == 0, -jnp.inf, 0.0)
    query = lax.broadcasted_iota(jnp.int32, (kc, qb), 1)
    biases = [chunk_bias(0) + off_first]
    for u in range(own):
        causal = jnp.where(row + u * kc <= query, 0.0, -jnp.inf)
        biases.append(chunk_bias((first_own + u) * blocks_per_chunk) + causal)
    keys = jnp.concatenate([ks_ref[0:kc, :], ks_ref[pl.ds(pl.multiple_of(t0, kc), qb), :]], axis=0)
    vt_s = jnp.concatenate([vst_ref[0]] + [vst_ref[first_own + u] for u in range(own)], axis=1)
    m_s, acc_s = _softmax_pv(_nt(keys, q), jnp.concatenate(biases, axis=0), vt_s)
    ms_ref[...] = m_s
    accs_ref[...] = acc_s

    def scan_body(w, cnt):
        word = flag_ref[w]
        for j in range(chunks_per_word):
            c = w * chunks_per_word + j
            bits = (word >> (j * blocks_per_chunk)) & ((1 << blocks_per_chunk) - 1)
            list_ref[cnt] = c
            cnt = cnt + ((bits != 0) & (c >= 1) & (c < first_own)).astype(jnp.int32)
        return cnt

    cnt = lax.fori_loop(0, (first_own + chunks_per_word - 1) // chunks_per_word, scan_body, 0)
    for u in range(SLC_GROUP):
        list_ref[cnt + u] = -1

    def slc_body(gi, carry):
        keys, biases, vts = [], [], []
        for u in range(SLC_GROUP):
            c = list_ref[gi * SLC_GROUP + u]
            cc = jnp.maximum(c, 0)
            biases.append(chunk_bias(jnp.where(c >= 0, cc * blocks_per_chunk, n_sel)))
            keys.append(ks_ref[pl.ds(pl.multiple_of(cc * kc, kc), kc), :])
            vts.append(vst_ref[cc])
        _flash_step(_nt(jnp.concatenate(keys, axis=0), q), jnp.concatenate(biases, axis=0),
                    jnp.concatenate(vts, axis=1), ms_ref, accs_ref)
        return carry

    lax.fori_loop(0, (cnt + SLC_GROUP - 1) // SLC_GROUP, slc_body, 0)

    o_slc = _flash_result(accs_ref)
    gate = gate_ref[...]
    outs = []
    for h in range(NSA_HPG):
        cols = slice(h * qb, (h + 1) * qb)
        gc = [gate[j * NSA_HPG + h:j * NSA_HPG + h + 1, :] for j in range(3)]
        outs.append(gc[0] * o_cmp[:, cols] + gc[1] * o_slc[:, cols] + gc[2] * o_win[:, cols])
    o_ref[...] = jnp.concatenate(outs, axis=0).T


def _nsa_call(qfeat, qn, kc, cvo, ks, vst, kw, vwt, gates):
    B, S, _ = qn.shape
    G = NSA_GROUPS
    qb = NSA_QB
    nb = S // qb
    n_sel = S // SLC_BLOCK
    width = NSA_HPG * qb
    per_group = lambda a: pl.BlockSpec((None, None) + a.shape[2:],
                                       lambda b, g, n: (b, g) + (0,) * (a.ndim - 2))
    return pl.pallas_call(
        _nsa_kernel,
        grid=(B, G, nb),
        in_specs=[
            pl.BlockSpec((None,) + qfeat.shape[1:], lambda b, g, n: (g, 0, 0, 0)),
            pl.BlockSpec((None, qb, NSA_HPG * HEAD_DIM), lambda b, g, n: (b, n, g)),
            per_group(kc), per_group(cvo), per_group(ks), per_group(vst), per_group(kw), per_group(vwt),
            pl.BlockSpec((None, None, GATE_ROWS, qb), lambda b, g, n: (b, g, 0, n)),
        ],
        out_specs=pl.BlockSpec((None, qb, NSA_HPG * HEAD_DIM), lambda b, g, n: (b, n, g)),
        out_shape=jax.ShapeDtypeStruct((B, S, NSA_HEADS * HEAD_DIM), F32),
        scratch_shapes=[pltpu.VMEM((n_sel + SEL_PAD_ROWS, qb), F32),
                        pltpu.VMEM((1, width), F32), pltpu.VMEM((VT_ROWS, width), F32),
                        pltpu.SMEM((n_sel // FLAG_BITS,), jnp.int32),
                        pltpu.SMEM((S // KEY_CHUNK + SLC_GROUP,), jnp.int32)],
        compiler_params=pltpu.CompilerParams(
            dimension_semantics=("parallel", "parallel", "arbitrary"), vmem_limit_bytes=VMEM_LIMIT),
        name="nsa",
    )(qfeat, qn, kc, cvo, ks, vst, kw, vwt, gates)


def _dilated_kernel(qfeat_ref, *refs):
    n_pat = len(DIL_PATTERNS)
    j = pl.program_id(1)
    for i, (window, dilation) in enumerate(DIL_PATTERNS):
        steps_per_phase = max(d for _, d in DIL_PATTERNS) // dilation
        _dilated_blocks(qfeat_ref, *refs[5 * i:5 * i + 5], *refs[5 * n_pat + 2 * i:5 * n_pat + 2 * i + 2],
                        wd=window // dilation, dilation=dilation,
                        phase=j // steps_per_phase, step=j % steps_per_phase)


def _dilated_blocks(qfeat_ref, q_ref, kp_ref, kc_ref, vp_ref, vc_ref, o_ref, lse_ref, *,
                    wd, dilation, phase, step):
    c = Q_BLOCK
    hd = HEAD_DIM
    blocks = q_ref.shape[0] // c
    lane = lax.broadcasted_iota(jnp.int32, (blocks * c, LANES), 1)
    key = lax.broadcasted_iota(jnp.int32, (2 * c, c), 0)
    qry = lax.broadcasted_iota(jnp.int32, (2 * c, c), 1)
    first = step == 0
    off_later = jnp.where(key < c, qry - (key + (wd - c)), (key - c) - qry)
    off_first = jnp.where(key < c, jnp.where(first, 2 * c, qry) - (key + (wd - c)), (key - c) - qry)
    bias_later = jnp.where(off_later <= 0, 0.0, -jnp.inf)
    bias_first = jnp.where(off_first <= 0, 0.0, -jnp.inf)
    sub = (step * blocks - 1) * c + lax.broadcasted_iota(jnp.int32, ((blocks + 1) * c, LANES), 0)
    pos_feat = _pos_features(sub * dilation + phase).astype(BF16)
    n_keys = (blocks + 1) * c
    tail_row = lax.broadcasted_iota(jnp.int32, (PAIR_VT_ROWS - 2 * hd, n_keys), 0)
    tail = jnp.where(tail_row == 0, 1.0, 0.0).astype(BF16)
    for p in range(DIL_HEADS // 2):
        cols = slice(p * LANES, (p + 1) * LANES)
        q_pair = q_ref[:, cols]
        zero = jnp.zeros_like(q_pair)
        feats = [jnp.broadcast_to(qfeat_ref[2 * p + i][0:1, :], (blocks * c, LANES)).astype(BF16)
                 for i in range(2)]
        q_a = jnp.concatenate([jnp.where(lane < hd, q_pair, zero), feats[0]], axis=1)
        q_b = jnp.concatenate([jnp.where(lane < hd, zero, q_pair), feats[1]], axis=1)
        k_all = jnp.concatenate([kp_ref[:, cols], kc_ref[:, cols]], axis=0)
        k_all = jnp.concatenate([k_all, pos_feat], axis=1)
        v_t = jnp.concatenate([jnp.concatenate([vp_ref[:, cols].T, vc_ref[:, cols].T], axis=1), tail],
                              axis=0)
        nb = min(DIL_BAND, blocks)
        for g0 in range(0, blocks, nb):
            q_aug = jnp.concatenate([part[j * c:(j + 1) * c] for j in range(g0, g0 + nb)
                                     for part in (q_a, q_b)], axis=0)
            s_all = _nt(k_all[g0 * c:(g0 + nb + 1) * c], q_aug)
            ms, band = [], []
            for j in range(nb):
                bias = bias_later if g0 + j else bias_first
                probs = []
                for i in range(2):
                    si = s_all[j * c:(j + 2) * c, (2 * j + i) * c:(2 * j + i + 1) * c] + bias
                    ms.append(jnp.max(si, axis=0, keepdims=True))
                    probs.append(jnp.exp2(si - ms[-1]).astype(BF16))
                pieces = [jnp.zeros((j * c, 2 * c), BF16)] if j else []
                pieces.append(jnp.concatenate(probs, axis=1))
                if j < nb - 1:
                    pieces.append(jnp.zeros(((nb - 1 - j) * c, 2 * c), BF16))
                band.append(jnp.concatenate(pieces, axis=0))
            r_all = jnp.dot(v_t[:, g0 * c:(g0 + nb + 1) * c], jnp.concatenate(band, axis=1),
                            preferred_element_type=F32)
            for j in range(nb):
                outs, lses = [], []
                for i in range(2):
                    at = slice((2 * j + i) * c, (2 * j + i + 1) * c)
                    l = r_all[2 * hd:2 * hd + 1, at]
                    outs.append(r_all[i * hd:(i + 1) * hd, at] * (1.0 / l))
                    lses.append(jnp.broadcast_to(ms[2 * j + i] + jnp.log2(l), (hd, c)))
                rows = slice((g0 + j) * c, (g0 + j + 1) * c)
                o_ref[rows, cols] = jnp.concatenate(outs, axis=0).T
                lse_ref[rows, cols] = jnp.concatenate(lses, axis=0).T


def _dilated_call(qfeat, qkv_by_pattern):
    B = qkv_by_pattern[0][0].shape[0]
    steps = max(d for _, d in DIL_PATTERNS)
    in_specs = [pl.BlockSpec(qfeat.shape, lambda b, j: (0, 0, 0))]
    out_specs, out_shape, operands = [], [], [qfeat]
    for (window, d), (q_v, k_v, v_v) in zip(DIL_PATTERNS, qkv_by_pattern):
        assert Q_BLOCK - 1 <= window // d <= Q_BLOCK and steps % d == 0
        L = q_v.shape[1]
        spp = steps // d
        per_step = L // (Q_BLOCK * spp)
        blk = pl.BlockSpec((None, Q_BLOCK * per_step, DIL_Q_W),
                           lambda b, j, spp=spp: (b, j % spp, j // spp))
        prev = pl.BlockSpec((None, Q_BLOCK, DIL_Q_W),
                            lambda b, j, spp=spp, per_step=per_step:
                            (b, jnp.maximum((j % spp) * per_step - 1, 0), j // spp))
        in_specs += [blk, prev, blk, prev, blk]
        operands += [q_v, k_v, k_v, v_v, v_v]
        out_specs += [blk, blk]
        out_shape += [jax.ShapeDtypeStruct((B, L, d * DIL_Q_W), F32)] * 2
    outs = pl.pallas_call(
        _dilated_kernel,
        grid=(B, steps),
        in_specs=in_specs,
        out_specs=tuple(out_specs),
        out_shape=tuple(out_shape),
        compiler_params=pltpu.CompilerParams(
            dimension_semantics=("parallel", "arbitrary"), vmem_limit_bytes=VMEM_LIMIT),
        name="dilated",
    )(*operands)
    return [(outs[2 * i], outs[2 * i + 1]) for i in range(len(DIL_PATTERNS))]


def _merge_kernel(x_ref, g_ref, wg_ref, onsa_ref, o1_ref, o2_ref, o3_ref, l1_ref, l2_ref, l3_ref,
                  wbn_ref, wbd_ref, wo_ref, fg_ref, out_ref, tok_scr, *, final):
    x = x_ref[...]
    tm = x.shape[0]
    xb = _rms(x, g_ref[...]).astype(BF16)
    cz = NSA_HEADS * HEAD_DIM
    d = x.shape[-1]

    def gate_seg(c0, c1):
        return jnp.dot(xb, wg_ref[:, c0:c1], preferred_element_type=F32)

    def token_major(ref, slot, dil):
        if dil == 1:
            return ref[...]
        tiles = DIL_Q_W // LANES
        for p in range(dil):
            for j in range(tiles):
                tok_scr[slot * tiles + j, pl.ds(p, tm // dil, stride=dil), :] = \
                    ref[:, p * DIL_Q_W + j * LANES:p * DIL_Q_W + (j + 1) * LANES]
        return jnp.concatenate([tok_scr[slot * tiles + j] for j in range(tiles)], axis=1)

    o_nsa = onsa_ref[...] * jax.nn.silu(gate_seg(0, cz))
    dils = [dil for _, dil in DIL_PATTERNS]
    o1, o2, o3 = (token_major(r, i, dil) for i, (r, dil) in enumerate(zip((o1_ref, o2_ref, o3_ref), dils)))
    l1, l2, l3 = (token_major(r, 3 + i, dil) for i, (r, dil) in enumerate(zip((l1_ref, l2_ref, l3_ref), dils)))
    mx = jnp.maximum(jnp.maximum(l1, l2), l3)
    e1, e2, e3 = jnp.exp2(l1 - mx), jnp.exp2(l2 - mx), jnp.exp2(l3 - mx)
    inv = 1.0 / (e1 + e2 + e3)
    o_dil = (e1 * inv) * o1 + (e2 * inv) * o2 + (e3 * inv) * o3
    o_dil = o_dil * jax.nn.silu(gate_seg(cz, 2 * cz))
    a = jnp.dot(o_nsa.astype(BF16), wbn_ref[...], preferred_element_type=F32)
    bd = jnp.dot(o_dil.astype(BF16), wbd_ref[...], preferred_element_type=F32)
    merged = (jax.nn.sigmoid(gate_seg(2 * cz, 2 * cz + d)) * a
              + jax.nn.sigmoid(gate_seg(2 * cz + d, 2 * cz + 2 * d)) * bd)
    y = jnp.dot(merged.astype(BF16), wo_ref[...], preferred_element_type=F32)
    out_ref[...] = _rms(x + y, fg_ref[...]) if final else x + y


def _merge_call(x, g, wg, o_nsa, o_d, lse_d, wbn, wbd, wo, fg, tm, final):
    B, S, D = x.shape
    cz = o_nsa.shape[-1]
    row = lambda c: pl.BlockSpec((None, tm, c), lambda b, i: (b, i, 0))
    full = lambda a: pl.BlockSpec(a.shape, lambda b, i: (0,) * a.ndim)
    phase = [pl.BlockSpec((None, tm // dil, dil * cz), lambda b, i: (b, i, 0)) for _, dil in DIL_PATTERNS]
    return pl.pallas_call(
        functools.partial(_merge_kernel, final=final),
        grid=(B, S // tm),
        in_specs=[row(D), full(g), full(wg), row(cz)] + phase + phase
                 + [full(wbn), full(wbd), full(wo), full(fg)],
        out_specs=row(D),
        out_shape=jax.ShapeDtypeStruct((B, S, D), F32),
        scratch_shapes=[pltpu.VMEM((2 * len(DIL_PATTERNS) * cz // LANES, tm, LANES), F32)],
        compiler_params=pltpu.CompilerParams(
            dimension_semantics=("parallel", "parallel"), vmem_limit_bytes=VMEM_LIMIT),
        name="merge",
    )(x, g, wg, o_nsa, *o_d, *lse_d, wbn, wbd, wo, fg)


def _proj_columns(d_model):
    hd, G, H = HEAD_DIM, NSA_GROUPS, NSA_HEADS
    nsa_w = H * hd
    o_q, o_kv = 0, nsa_w
    o_g = o_kv + 6 * G * hd
    o_zn = o_g + 3 * H
    o_qkvd = o_zn + nsa_w
    dil_w = DIL_HEADS * hd
    o_zd = o_qkvd + 3 * dil_w
    o_mg = o_zd + dil_w
    kv = lambda j, g: o_kv + (j * G + g) * hd + np.arange(hd)
    cols = [o_q + np.arange(nsa_w)]
    cols += [kv(j, g) for j in (0, 1) for g in range(G)]
    for j in (2, 4, 3, 5):
        cols += [kv(j, g) for g in range(G)]
    lanes = np.full(LANES, -1)
    for g in range(G):
        for j in range(3):
            lanes[g * GATE_ROWS + j * NSA_HPG + np.arange(NSA_HPG)] = o_g + j * H + g * NSA_HPG + np.arange(NSA_HPG)
    cols.append(lanes)
    cols.append(o_qkvd + np.arange(3 * dil_w))
    proj_cols = np.concatenate(cols)
    assert proj_cols.size == _PROJ_OFFS["vd"][1]
    gate_cols = np.concatenate([o_zn + np.arange(nsa_w), o_zd + np.arange(dil_w),
                                o_mg + np.arange(2 * d_model)])
    return proj_cols, gate_cols


def _take_columns(w, cols, dtype):
    pieces, start = [], 0
    for i in range(1, len(cols) + 1):
        if i < len(cols):
            same_run = (cols[i] < 0 and cols[i - 1] < 0) or (cols[i - 1] >= 0 and cols[i] == cols[i - 1] + 1)
        if i == len(cols) or not same_run:
            n = i - start
            pieces.append(jnp.zeros((w.shape[0], n), dtype) if cols[start] < 0
                          else w[:, int(cols[start]):int(cols[start]) + n].astype(dtype))
            start = i
    return jnp.concatenate(pieces, axis=1)


def _compress_weights(pos_k, w1_k, w2_k, pos_v, w1_v, w2_v):
    hd, G, half = HEAD_DIM, NSA_GROUPS, CMP_BLOCK // 2
    eye = jnp.eye(2 * G, dtype=F32)
    w1 = jnp.stack([w1_k] * G + [w1_v] * G)
    pos = jnp.stack([pos_k] * G + [pos_v] * G)

    def first_layer(lo):
        w = jnp.einsum('slde,st->lsdte', w1[:, lo:lo + half], eye)
        p = pos[:, lo:lo + half].transpose(1, 0, 2)
        return w.reshape(half * 2 * G * hd, 2 * G * hd).astype(BF16), p.reshape(1, half * 2 * G * hd)

    wa, posa = first_layer(0)
    wb, posb = first_layer(half)
    w2 = jnp.zeros((2 * G, hd, 3 * G, hd), F32)
    for g in range(G):
        w2 = w2.at[g, :, 2 * g].set(w2_k).at[G + g, :, 2 * G + g].set(w2_v)
    return posa, posb, wa, wb, w2.reshape(2 * G * hd, 3 * G * hd).astype(BF16)


def _overlap_t(n_cmp_pad, n_sel):
    i = np.arange(n_cmp_pad)[None, :]
    j = np.arange(n_sel)[:, None]
    lo = np.maximum(i * CMP_STRIDE, j * SLC_BLOCK)
    hi = np.minimum(i * CMP_STRIDE + CMP_BLOCK, (j + 1) * SLC_BLOCK)
    return (np.clip(hi - lo, 0, None) / CMP_BLOCK).astype(np.float32)


def kernel(x, norm_g, w_in, cmp_pos_k, cmp_w1_k, cmp_w2_k, cmp_pos_v, cmp_w1_v, cmp_w2_v,
           w_br_nsa, w_br_dil, w_out, final_g):
    B, S, D = x.shape
    n_sel = S // SLC_BLOCK
    assert S % (Q_BLOCK * max(d for _, d in DIL_PATTERNS)) == 0 and n_sel % FLAG_BITS == 0
    n_cmp_pad = S // CMP_STRIDE
    slopes_nsa, slopes_dil = _alibi_slopes()
    proj_cols, gate_cols = _proj_columns(D)
    ovt = jnp.asarray(_overlap_t(n_cmp_pad, n_sel))
    qfeat = jnp.asarray(np.broadcast_to(
        _slope_features(slopes_nsa).reshape(NSA_GROUPS, NSA_HPG, 1, LANES),
        (NSA_GROUPS, NSA_HPG, 8, LANES)))
    qfeat_dil = jnp.asarray(np.broadcast_to(
        _slope_features(slopes_dil).reshape(DIL_HEADS, 1, LANES), (DIL_HEADS, 8, LANES)))
    tm = 512

    h = x
    for layer in range(w_in.shape[0]):
        w_proj = _take_columns(w_in[layer], proj_cols, BF16)
        w_gate = _take_columns(w_in[layer], gate_cols, BF16)
        g_l = norm_g[layer].reshape(1, D)

        qn, cmp_src, ks, kw, vst, vwt, gates, *dil = _proj_call(h, g_l, w_proj, tm)
        cw = _compress_weights(cmp_pos_k[layer], cmp_w1_k[layer], cmp_w2_k[layer],
                               cmp_pos_v[layer], cmp_w1_v[layer], cmp_w2_v[layer])
        kc, cvo = _compress_call(cmp_src, *cw, ovt)
        o_nsa = _nsa_call(qfeat, qn, kc, cvo, ks, vst, kw, vwt, gates)
        dil_out = _dilated_call(qfeat_dil, [dil[3 * i:3 * i + 3] for i in range(len(DIL_PATTERNS))])
        o_d, lse_d = [o for o, _ in dil_out], [lse for _, lse in dil_out]
        h = _merge_call(h, g_l, w_gate, o_nsa, o_d, lse_d, w_br_nsa[layer].astype(BF16),
                        w_br_dil[layer].astype(BF16), w_out[layer].astype(BF16),
                        final_g.reshape(1, D), tm, final=layer == w_in.shape[0] - 1)
    return h
```

```python
import functools
import math

import numpy as np
import jax
import jax.numpy as jnp
from jax import lax
from jax.experimental import pallas as pl
from jax.experimental.pallas import tpu as pltpu

F32 = jnp.float32
BF16 = jnp.bfloat16

HEAD_DIM = 64
LANES = 128
NSA_HEADS = 8
NSA_GROUPS = 2
NSA_HPG = NSA_HEADS // NSA_GROUPS
CMP_BLOCK = 32
CMP_STRIDE = 16
SLC_BLOCK = 64
SLC_TOPK = 16
WIN = 512
FORCE_SCORE = 1.0e4
DIL_HEADS = 8
DIL_PATTERNS = ((128, 1), (512, 4), (2048, 16))
Q_BLOCK = 128
RMS_EPS = 1e-6
KEY_CHUNK = 128
VT_ROWS = 80
PAIR_VT_ROWS = 144
GATE_ROWS = 16
FEAT_SPLIT = 3
FLAG_BITS = 16
NSA_NEAR_CHUNKS = 8
SLC_GROUP = 4
MERGE_ROWS = 512
PERM_ROWS = 128
WIN_QB = 256
NSA_QB = 512
DIL_PV_BAND = 1
DIL_BAND = 4
N_FORCED = 3
assert FORCE_SCORE > NSA_HPG + 1
SEL_PAD_ROWS = 8
LOG2E = math.log2(math.e)
VMEM_LIMIT = 56 * 1024 * 1024


def _alibi_slopes():
    n = NSA_HEADS + DIL_HEADS
    s = 2.0 ** (-8.0 * np.arange(1, n + 1) / n)
    return s[0::2].astype(np.float32), s[1::2].astype(np.float32)


def _nt(a, b):
    return lax.dot_general(a, b, (((1,), (1,)), ((), ())), preferred_element_type=F32)


def _rms(x, g):
    return (x * lax.rsqrt(jnp.mean(x * x, axis=-1, keepdims=True) + RMS_EPS)) * g


def _head_tile(q_pairs, h):
    tile = q_pairs[:, (h // 2) * LANES:(h // 2 + 1) * LANES]
    return pltpu.roll(tile, HEAD_DIM, 1) if h % 2 else tile


def _pos_features(pos):
    lane = lax.broadcasted_iota(jnp.int32, pos.shape, 1) - HEAD_DIM
    hi = (pos // LANES).astype(F32)
    lo = (pos % LANES).astype(F32)
    return jnp.where(lane < 0, 0.0, jnp.where(lane < FEAT_SPLIT, hi, jnp.where(lane < 2 * FEAT_SPLIT, lo, 0.0)))


def _slope_features(slopes):
    out = np.zeros((len(slopes), LANES), np.float32)
    for h, s in enumerate(slopes):
        rest = np.float32(s) * np.float32(LOG2E)
        for i in range(FEAT_SPLIT):
            piece = np.float32(np.asarray(rest, np.float32).astype(BF16))
            out[h, HEAD_DIM + FEAT_SPLIT + i] = piece
            out[h, HEAD_DIM + i] = piece * LANES
            rest = np.float32(rest - piece)
    return out


_PROJ_SEGS = (("qn", 512), ("cmp", 256), ("ks", 128), ("kw", 128), ("vs", 128), ("vw", 128),
              ("gate", 128), ("qd", 512), ("kd", 512), ("vd", 512))
DIL_Q_W = DIL_HEADS * HEAD_DIM
_PROJ_OFFS = dict(zip((n for n, _ in _PROJ_SEGS),
                      zip(np.cumsum([0] + [w for _, w in _PROJ_SEGS])[:-1].tolist(),
                          np.cumsum([w for _, w in _PROJ_SEGS]).tolist())))


def _to_lane_tiles(scr, first, val):
    for j in range(val.shape[1] // LANES):
        scr[first + j] = val[:, j * LANES:(j + 1) * LANES]


def _regroup_rows(scr, lo, hi, out_ref, d, dtype):
    rows, width = scr.shape[1], (hi - lo) * LANES
    for p in range(d):
        for j in range(hi - lo):
            tile = scr[lo + j, pl.ds(p, rows // d, stride=d), :] if d > 1 else scr[lo + j]
            out_ref[:, p * width + j * LANES:p * width + (j + 1) * LANES] = tile.astype(dtype)


def _proj_kernel(x_ref, g_ref, w_ref, perm_ref, qn_ref, cmp_ref, ks_ref, kw_ref, vst_ref, vwt_ref,
                 gate_ref, *dil_refs):
    cmp_scr = dil_refs[-1]
    tm = x_ref.shape[0]
    xb = _rms(x_ref[...], g_ref[...]).astype(BF16)
    scale = HEAD_DIM ** -0.5

    def seg(name, last=None):
        c0, c1 = _PROJ_OFFS[name][0], _PROJ_OFFS[last or name][1]
        return jnp.dot(xb, w_ref[:, c0:c1], preferred_element_type=F32)

    qn_ref[...] = (seg("qn") * (scale * LOG2E)).astype(BF16)
    _to_lane_tiles(cmp_scr, 0, seg("cmp"))
    _regroup_rows(cmp_scr, 0, cmp_scr.shape[0], cmp_ref, CMP_STRIDE, F32)
    feat = _pos_features(pl.program_id(1) * tm + lax.broadcasted_iota(jnp.int32, (tm, LANES), 0))
    k_sw, v_sw = seg("ks", "kw"), seg("vs", "vw")
    ks, kw = k_sw[:, 0:LANES], k_sw[:, LANES:]
    vst, vwt = v_sw[:, 0:LANES].T, v_sw[:, LANES:].T
    gate_t = jax.nn.sigmoid(seg("gate")).T
    tail_row = lax.broadcasted_iota(jnp.int32, (VT_ROWS - HEAD_DIM, tm), 0)
    tail = jnp.where(tail_row == 0, 1.0, 0.0)
    lane = lax.broadcasted_iota(jnp.int32, (tm, LANES), 1)
    for g in range(NSA_GROUPS):
        ks_ref[g] = jnp.where(lane < HEAD_DIM, _head_tile(ks, g), feat).astype(BF16)
        kw_ref[g] = jnp.where(lane < HEAD_DIM, _head_tile(kw, g), feat).astype(BF16)
        gate_ref[g] = gate_t[g * GATE_ROWS:(g + 1) * GATE_ROWS]
        for src, dst in ((vst, vst_ref), (vwt, vwt_ref)):
            vt = jnp.concatenate([src[g * HEAD_DIM:(g + 1) * HEAD_DIM], tail], axis=0).astype(BF16)
            for j in range(tm // KEY_CHUNK):
                dst[g, j] = vt[:, j * KEY_CHUNK:(j + 1) * KEY_CHUNK]
    qkv = jnp.concatenate([(seg("qd") * (scale * LOG2E)).astype(BF16), seg("kd").astype(BF16),
                           seg("vd").astype(BF16)], axis=1)
    bounds = (0, DIL_Q_W, 2 * DIL_Q_W, 3 * DIL_Q_W)
    subs = tm // PERM_ROWS
    for i, (_, d) in enumerate(DIL_PATTERNS):
        if d > 1:
            by_phase = [jnp.dot(perm_ref[i], qkv[s * PERM_ROWS:(s + 1) * PERM_ROWS],
                                preferred_element_type=F32) for s in range(subs)]
        r = PERM_ROWS // d
        for p in range(d):
            rows = qkv if d == 1 else jnp.concatenate(
                [by_phase[s][p * r:(p + 1) * r] for s in range(subs)], axis=0).astype(BF16)
            for j, ref in enumerate(dil_refs[3 * i:3 * i + 3]):
                width = bounds[j + 1] - bounds[j]
                ref[:, p * width:(p + 1) * width] = rows[:, bounds[j]:bounds[j + 1]]


def _proj_call(x, g, w, tm):
    B, S, D = x.shape
    G = NSA_GROUPS
    cmp_w = _PROJ_OFFS["cmp"][1] - _PROJ_OFFS["cmp"][0]
    row = lambda c: pl.BlockSpec((None, tm, c), lambda b, i: (b, i, 0))
    k_spec =pl.BlockSpec((None, G, tm, LANES), lambda b, i: (b, 0, i, 0))
    vt_spec = pl.BlockSpec((None, G, tm // KEY_CHUNK, VT_ROWS, KEY_CHUNK), lambda b, i: (b, 0, i, 0, 0))
    out_shape = (
        jax.ShapeDtypeStruct((B, S, 512), BF16),
        jax.ShapeDtypeStruct((B, S // CMP_STRIDE, CMP_STRIDE * cmp_w), F32),
        jax.ShapeDtypeStruct((B, G, S, LANES), BF16),
        jax.ShapeDtypeStruct((B, G, S, LANES), BF16),
        jax.ShapeDtypeStruct((B, G, S // KEY_CHUNK, VT_ROWS, KEY_CHUNK), BF16),
        jax.ShapeDtypeStruct((B, G, S // KEY_CHUNK, VT_ROWS, KEY_CHUNK), BF16),
        jax.ShapeDtypeStruct((B, G, GATE_ROWS, S), F32),
    )
    dil_specs = ()
    for _, d in DIL_PATTERNS:
        for width in (DIL_Q_W,) * 3:
            out_shape += (jax.ShapeDtypeStruct((B, S // d, d * width), BF16),)
            dil_specs += (pl.BlockSpec((None, tm // d, d * width), lambda b, i: (b, i, 0)),)
    r = np.arange(PERM_ROWS)
    perm = np.zeros((len(DIL_PATTERNS), PERM_ROWS, PERM_ROWS), np.float32)
    for i, (_, d) in enumerate(DIL_PATTERNS):
        assert (PERM_ROWS // d) % 8 == 0
        perm[i, (r % d) * (PERM_ROWS // d) + r // d, r] = 1.0
    perm = jnp.asarray(perm, BF16)
    return pl.pallas_call(
        _proj_kernel,
        grid=(B, S // tm),
        in_specs=[row(D), pl.BlockSpec((1, D), lambda b, i: (0, 0)),
                  pl.BlockSpec(w.shape, lambda b, i: (0, 0)),
                  pl.BlockSpec(perm.shape, lambda b, i: (0, 0, 0))],
        out_specs=(row(512),
                   pl.BlockSpec((None, tm // CMP_STRIDE, CMP_STRIDE * cmp_w), lambda b, i: (b, i, 0)),
                   k_spec, k_spec, vt_spec, vt_spec,
                   pl.BlockSpec((None, G, GATE_ROWS, tm), lambda b, i: (b, 0, 0, i))) + dil_specs,
        out_shape=out_shape,
        scratch_shapes=[pltpu.VMEM((cmp_w // LANES, tm, LANES), F32)],
        compiler_params=pltpu.CompilerParams(
            dimension_semantics=("parallel", "parallel"), vmem_limit_bytes=VMEM_LIMIT),
        name="proj",
    )(x, g, w, perm)


def _compress_kernel(r_ref, posa_ref, posb_ref, wa_ref, wb_ref, w2_ref, ovt_ref, kc_ref, cvo_ref):
    r = r_ref[...]
    n = r.shape[0]
    ha = jnp.dot((r + posa_ref[...]).astype(BF16), wa_ref[...], preferred_element_type=F32)
    hb = jnp.dot((r + posb_ref[...]).astype(BF16), wb_ref[...], preferred_element_type=F32)
    hid = jax.nn.gelu(ha + pltpu.roll(hb, n - 1, 0))
    out = jnp.dot(hid.astype(BF16), w2_ref[...], preferred_element_type=F32)
    cmp_end = lax.broadcasted_iota(jnp.int32, (n, LANES), 0) * CMP_STRIDE + (CMP_BLOCK - 1)
    feat = _pos_features(cmp_end)
    v_t = out[:, NSA_GROUPS * LANES:].T
    for g in range(NSA_GROUPS):
        kc_ref[g] = (out[:, g * LANES:(g + 1) * LANES] + feat).astype(BF16)
        cvo_ref[g] = jnp.concatenate([v_t[g * HEAD_DIM:(g + 1) * HEAD_DIM], ovt_ref[...]],
                                     axis=0).astype(BF16)


def _compress_call(r, posa, posb, wa, wb, w2, ovt):
    B, R, C = r.shape
    G = NSA_GROUPS
    rows = HEAD_DIM + ovt.shape[0]
    full = lambda a: pl.BlockSpec(a.shape, lambda b: (0,) * a.ndim)
    return pl.pallas_call(
        _compress_kernel,
        grid=(B,),
        in_specs=[pl.BlockSpec((None, R, C), lambda b: (b, 0, 0)),
                  full(posa), full(posb), full(wa), full(wb), full(w2), full(ovt)],
        out_specs=(pl.BlockSpec((None, G, R, LANES), lambda b: (b, 0, 0, 0)),
                   pl.BlockSpec((None, G, rows, R), lambda b: (b, 0, 0, 0))),
        out_shape=(jax.ShapeDtypeStruct((B, G, R, LANES), BF16),
                   jax.ShapeDtypeStruct((B, G, rows, R), BF16)),
        compiler_params=pltpu.CompilerParams(
            dimension_semantics=("parallel",), vmem_limit_bytes=VMEM_LIMIT),
        name="compress",
    )(r, posa, posb, wa, wb, w2, ovt)


def _flash_step(s, bias, v_t, m_ref, acc_ref):
    ps, alphas = [], []
    qb = bias.shape[1]
    for h in range(NSA_HPG):
        cols = slice(h * qb, (h + 1) * qb)
        sh = s[:, cols] + bias
        m_old = m_ref[:, cols]
        m_new = jnp.maximum(m_old, jnp.max(sh, axis=0, keepdims=True))
        alphas.append(jnp.exp2(m_old - m_new))
        ps.append(jnp.exp2(sh - m_new).astype(BF16))
        m_ref[:, cols] = m_new
    pv = jnp.dot(v_t, jnp.concatenate(ps, axis=1), preferred_element_type=F32)
    acc_ref[...] = jnp.concatenate(alphas, axis=1) * acc_ref[...] + pv


def _flash_result(acc_ref):
    l = acc_ref[HEAD_DIM:HEAD_DIM + 1, :]
    return acc_ref[0:HEAD_DIM, :] * (1.0 / jnp.maximum(l, 1e-30))


def _softmax_pv(s, bias, v_t):
    ms, ps = [], []
    qb = bias.shape[1]
    for h in range(NSA_HPG):
        sh = s[:, h * qb:(h + 1) * qb] + bias
        ms.append(jnp.max(sh, axis=0, keepdims=True))
        ps.append(jnp.exp2(sh - ms[h]).astype(BF16))
    r = jnp.dot(v_t, jnp.concatenate(ps, axis=1), preferred_element_type=F32)
    return jnp.concatenate(ms, axis=1), r


def _normalise(acc):
    return acc[0:HEAD_DIM] * (1.0 / jnp.maximum(acc[HEAD_DIM:HEAD_DIM + 1], 1e-30))


def _nsa_kernel(qfeat_ref, q_ref, kc_ref, cvo_ref, ks_ref, vst_ref, kw_ref, vwt_ref, gate_ref,
                o_ref, sel_ref, ms_ref, accs_ref, flag_ref, list_ref):
    n = pl.program_id(2)
    qb, kc = NSA_QB, KEY_CHUNK
    t0 = n * qb
    lane = lax.broadcasted_iota(jnp.int32, (qb, LANES), 1)
    qf = q_ref[...].astype(F32)
    q = jnp.concatenate(
        [jnp.where(lane < HEAD_DIM, _head_tile(qf, h), qfeat_ref[h][0:1, :]) for h in range(NSA_HPG)],
        axis=0).astype(BF16)

    wq = min(WIN_QB, qb)
    span = WIN + wq
    back_w = (lax.broadcasted_iota(jnp.int32, (span, wq), 1)
              - lax.broadcasted_iota(jnp.int32, (span, wq), 0))
    win_parts = []
    for j in range(qb // wq):
        k_lo = pl.multiple_of(jnp.maximum(t0 + j * wq - WIN, 0), kc)
        dist = back_w + (t0 + j * wq - k_lo)
        bias = jnp.where(dist >= 0, jnp.where(dist < WIN, 0.0, -jnp.inf), -jnp.inf)
        q_part = jnp.concatenate([q[h * qb + j * wq:h * qb + (j + 1) * wq] for h in range(NSA_HPG)], axis=0)
        vt_w = jnp.concatenate([vwt_ref[k_lo // kc + u] for u in range(span // kc)], axis=1)
        win_parts.append(_normalise(_softmax_pv(_nt(kw_ref[pl.ds(k_lo, span), :], q_part), bias, vt_w)[1]))
    o_win = jnp.concatenate([part[:, h * wq:(h + 1) * wq] for h in range(NSA_HPG) for part in win_parts],
                            axis=1)

    n_cmp = kc_ref.shape[0]
    n_sel = sel_ref.shape[0] - SEL_PAD_ROWS
    s = _nt(kc_ref[...], q)
    cmp_end = lax.broadcasted_iota(jnp.int32, (n_cmp, qb), 0) * CMP_STRIDE + (CMP_BLOCK - 1)
    ok_c = cmp_end <= t0 + lax.broadcasted_iota(jnp.int32, (n_cmp, qb), 1)
    ps = []
    for h in range(NSA_HPG):
        sh = jnp.where(ok_c, s[:, h * qb:(h + 1) * qb], -jnp.inf)
        m = jnp.max(sh, axis=0, keepdims=True)
        m = jnp.where(m == -jnp.inf, 0.0, m)
        e = jnp.exp2(sh - m)
        l = jnp.sum(e, axis=0, keepdims=True)
        ps.append((e * (1.0 / jnp.maximum(l, 1e-30))).astype(BF16))
    r = jnp.dot(cvo_ref[...], jnp.concatenate(ps, axis=1), preferred_element_type=F32)
    o_cmp = r[0:HEAD_DIM]
    imp_t = r[HEAD_DIM:, 0:qb]
    for h in range(1, NSA_HPG):
        imp_t = imp_t + r[HEAD_DIM:, h * qb:(h + 1) * qb]

    blk = lax.broadcasted_iota(jnp.int32, (n_sel, qb), 0).astype(F32)
    cur = ((t0 + lax.broadcasted_iota(jnp.int32, (n_sel, qb), 1)) // SLC_BLOCK).astype(F32)
    ago = cur - blk
    forced = jnp.where(blk == 0, 1.0, jnp.where(ago == 0, 1.0, jnp.where(ago == 1, 1.0, 0.0)))
    valid = ago >= 0
    score = jnp.where(valid, jnp.where(forced > 0.5, -1.0, imp_t), -1.0)
    for _ in range(max(min(SLC_TOPK, n_sel) - N_FORCED, 0)):
        mx = jnp.max(score, axis=0, keepdims=True)
        idx = jnp.min(jnp.where(score == mx, blk, float(n_sel)), axis=0, keepdims=True)
        score = jnp.where(blk == idx, -jnp.inf, score)
    sel_t = jnp.where(valid, jnp.where(score == -jnp.inf, 1.0, forced), 0.0)
    sel_ref[0:n_sel, :] = jnp.where(sel_t > 0.5, 0.0, -jnp.inf)
    sel_ref[n_sel:, :] = jnp.full((SEL_PAD_ROWS, qb), -jnp.inf, F32)
    any_q = jnp.max(sel_t, axis=1, keepdims=True)
    bit = jnp.left_shift(1, lax.broadcasted_iota(jnp.int32, (n_sel, 1), 0) % FLAG_BITS).astype(F32)
    packed = any_q * bit
    for i in range(n_sel // FLAG_BITS):
        word = jnp.sum(packed[i * FLAG_BITS:(i + 1) * FLAG_BITS], axis=0, keepdims=True)
        flag_ref[i] = word.astype(jnp.int32)[0, 0]

    blocks_per_chunk = kc // SLC_BLOCK
    chunks_per_word = FLAG_BITS // blocks_per_chunk
    row = lax.broadcasted_iota(jnp.int32, (kc, qb), 0)

    def chunk_bias(first_block):
        bias = sel_ref[pl.ds(first_block, 1), :]
        for j in range(1, blocks_per_chunk):
            bias = jnp.where(row < j * SLC_BLOCK, bias, sel_ref[pl.ds(first_block + j, 1), :])
        return bias

    own = qb // kc
    first_own = t0 // kc
    near = min(NSA_NEAR_CHUNKS, ks_ref.shape[0] // kc)
    k_near = pl.multiple_of(jnp.maximum(t0 - near * kc, 0), kc)
    first_near = k_near // kc
    off_first = jnp.where(first_near >= 1, 0.0, -jnp.inf)
    query = lax.broadcasted_iota(jnp.int32, (kc, qb), 1)
    biases = [chunk_bias(0) + off_first]
    for u in range(own):
        causal = jnp.where(row + u * kc <= query, 0.0, -jnp.inf)
        biases.append(chunk_bias((first_own + u) * blocks_per_chunk) + causal)
    keys = jnp.concatenate([ks_ref[0:kc, :], ks_ref[pl.ds(pl.multiple_of(t0, kc), qb), :]], axis=0)
    vt_s = jnp.concatenate([vst_ref[0]] + [vst_ref[first_own + u] for u in range(own)], axis=1)
    m_s, acc_s = _softmax_pv(_nt(keys, q), jnp.concatenate(biases, axis=0), vt_s)
    ms_ref[...] = m_s
    accs_ref[...] = acc_s

    biases = []
    for u in range(near):
        off = jnp.where(first_near + u < first_own, 0.0, -jnp.inf)
        biases.append(chunk_bias((first_near + u) * blocks_per_chunk) + off)
    _flash_step(_nt(ks_ref[pl.ds(k_near, near * kc), :], q), jnp.concatenate(biases, axis=0),
                jnp.concatenate([vst_ref[first_near + u] for u in range(near)], axis=1), ms_ref, accs_ref)

    def scan_body(w, cnt):
        word = flag_ref[w]
        for j in range(chunks_per_word):
            c = w * chunks_per_word + j
            bits = (word >> (j * blocks_per_chunk)) & ((1 << blocks_per_chunk) - 1)
            list_ref[cnt] = c
            cnt = cnt + ((bits != 0) & (c >= 1) & (c < first_near)).astype(jnp.int32)
        return cnt

    cnt = lax.fori_loop(0, (first_near + chunks_per_word - 1) // chunks_per_word, scan_body, 0)
    for u in range(SLC_GROUP):
        list_ref[cnt + u] = -1

    def slc_body(gi, carry):
        keys, biases, vts = [], [], []
        for u in range(SLC_GROUP):
            c = list_ref[gi * SLC_GROUP + u]
            cc = jnp.maximum(c, 0)
            biases.append(chunk_bias(jnp.where(c >= 0, cc * blocks_per_chunk, n_sel)))
            keys.append(ks_ref[pl.ds(pl.multiple_of(cc * kc, kc), kc), :])
            vts.append(vst_ref[cc])
        _flash_step(_nt(jnp.concatenate(keys, axis=0), q), jnp.concatenate(biases, axis=0),
                    jnp.concatenate(vts, axis=1), ms_ref, accs_ref)
        return carry

    lax.fori_loop(0, (cnt + SLC_GROUP - 1) // SLC_GROUP, slc_body, 0)

    o_slc = _flash_result(accs_ref)
    gate = gate_ref[...]
    outs = []
    for h in range(NSA_HPG):
        cols = slice(h * qb, (h + 1) * qb)
        gc = [gate[j * NSA_HPG + h:j * NSA_HPG + h + 1, :] for j in range(3)]
        outs.append(gc[0] * o_cmp[:, cols] + gc[1] * o_slc[:, cols] + gc[2] * o_win[:, cols])
    o_ref[...] = jnp.concatenate(outs, axis=0).T


def _nsa_call(qfeat, qn, kc, cvo, ks, vst, kw, vwt, gates):
    B, S, _ = qn.shape
    G = NSA_GROUPS
    qb = NSA_QB
    nb = S // qb
    n_sel = S // SLC_BLOCK
    width = NSA_HPG * qb
    per_group = lambda a: pl.BlockSpec((None, None) + a.shape[2:],
                                       lambda b, g, n: (b, g) + (0,) * (a.ndim - 2))
    return pl.pallas_call(
        _nsa_kernel,
        grid=(B, G, nb),
        in_specs=[
            pl.BlockSpec((None,) + qfeat.shape[1:], lambda b, g, n: (g, 0, 0, 0)),
            pl.BlockSpec((None, qb, NSA_HPG * HEAD_DIM), lambda b, g, n: (b, n, g)),
            per_group(kc), per_group(cvo), per_group(ks), per_group(vst), per_group(kw), per_group(vwt),
            pl.BlockSpec((None, None, GATE_ROWS, qb), lambda b, g, n: (b, g, 0, n)),
        ],
        out_specs=pl.BlockSpec((None, qb, NSA_HPG * HEAD_DIM), lambda b, g, n: (b, n, g)),
        out_shape=jax.ShapeDtypeStruct((B, S, NSA_HEADS * HEAD_DIM), F32),
        scratch_shapes=[pltpu.VMEM((n_sel + SEL_PAD_ROWS, qb), F32),
                        pltpu.VMEM((1, width), F32), pltpu.VMEM((VT_ROWS, width), F32),
                        pltpu.SMEM((n_sel // FLAG_BITS,), jnp.int32),
                        pltpu.SMEM((S // KEY_CHUNK + SLC_GROUP,), jnp.int32)],
        compiler_params=pltpu.CompilerParams(
            dimension_semantics=("parallel", "parallel", "arbitrary"), vmem_limit_bytes=VMEM_LIMIT),
        name="nsa",
    )(qfeat, qn, kc, cvo, ks, vst, kw, vwt, gates)


def _dilated_kernel(qfeat_ref, *refs):
    n_pat = len(DIL_PATTERNS)
    j = pl.program_id(1)
    for i, (window, dilation) in enumerate(DIL_PATTERNS):
        steps_per_phase = max(d for _, d in DIL_PATTERNS) // dilation
        _dilated_blocks(qfeat_ref, *refs[5 * i:5 * i + 5], *refs[5 * n_pat + 2 * i:5 * n_pat + 2 * i + 2],
                        wd=window // dilation, dilation=dilation,
                        phase=j // steps_per_phase, step=j % steps_per_phase)


def _dilated_blocks(qfeat_ref, q_ref, kp_ref, kc_ref, vp_ref, vc_ref, o_ref, lse_ref, *,
                    wd, dilation, phase, step):
    c = Q_BLOCK
    hd = HEAD_DIM
    blocks = q_ref.shape[0] // c
    lane = lax.broadcasted_iota(jnp.int32, (blocks * c, LANES), 1)
    key = lax.broadcasted_iota(jnp.int32, (2 * c, c), 0)
    qry = lax.broadcasted_iota(jnp.int32, (2 * c, c), 1)
    first = step == 0
    off_later = jnp.where(key < c, qry - (key + (wd - c)), (key - c) - qry)
    off_first = jnp.where(key < c, jnp.where(first, 2 * c, qry) - (key + (wd - c)), (key - c) - qry)
    bias_later = jnp.where(off_later <= 0, 0.0, -jnp.inf)
    bias_first = jnp.where(off_first <= 0, 0.0, -jnp.inf)
    sub = (step * blocks - 1) * c + lax.broadcasted_iota(jnp.int32, ((blocks + 1) * c, LANES), 0)
    pos_feat = _pos_features(sub * dilation + phase).astype(BF16)
    n_keys = (blocks + 1) * c
    tail_row = lax.broadcasted_iota(jnp.int32, (PAIR_VT_ROWS - 2 * hd, n_keys), 0)
    tail = jnp.where(tail_row == 0, 1.0, 0.0).astype(BF16)
    for p in range(DIL_HEADS // 2):
        cols = slice(p * LANES, (p + 1) * LANES)
        q_pair = q_ref[:, cols]
        zero = jnp.zeros_like(q_pair)
        feats = [jnp.broadcast_to(qfeat_ref[2 * p + i][0:1, :], (blocks * c, LANES)).astype(BF16)
                 for i in range(2)]
        q_a = jnp.concatenate([jnp.where(lane < hd, q_pair, zero), feats[0]], axis=1)
        q_b = jnp.concatenate([jnp.where(lane < hd, zero, q_pair), feats[1]], axis=1)
        k_all = jnp.concatenate([kp_ref[:, cols], kc_ref[:, cols]], axis=0)
        k_all = jnp.concatenate([k_all, pos_feat], axis=1)
        v_t = jnp.concatenate([jnp.concatenate([vp_ref[:, cols].T, vc_ref[:, cols].T], axis=1), tail],
                              axis=0)
        nb = min(DIL_BAND, blocks)
        for g0 in range(0, blocks, nb):
            q_aug = jnp.concatenate([part[j * c:(j + 1) * c] for j in range(g0, g0 + nb)
                                     for part in (q_a, q_b)], axis=0)
            s_all = _nt(k_all[g0 * c:(g0 + nb + 1) * c], q_aug)
            ms, probs = [], []
            for j in range(nb):
                bias = bias_later if g0 + j else bias_first
                pair = []
                for i in range(2):
                    si = s_all[j * c:(j + 2) * c, (2 * j + i) * c:(2 * j + i + 1) * c] + bias
                    ms.append(jnp.max(si, axis=0, keepdims=True))
                    pair.append(jnp.exp2(si - ms[-1]).astype(BF16))
                probs.append(jnp.concatenate(pair, axis=1))
            pb = min(DIL_PV_BAND, nb)
            r_parts = []
            for h0 in range(0, nb, pb):
                band = []
                for j in range(h0, h0 + pb):
                    pieces = [jnp.zeros(((j - h0) * c, 2 * c), BF16)] if j > h0 else []
                    pieces.append(probs[j])
                    if j < h0 + pb - 1:
                        pieces.append(jnp.zeros(((h0 + pb - 1 - j) * c, 2 * c), BF16))
                    band.append(jnp.concatenate(pieces, axis=0))
                r_parts.append(jnp.dot(v_t[:, (g0 + h0) * c:(g0 + h0 + pb + 1) * c],
                                       jnp.concatenate(band, axis=1), preferred_element_type=F32))
            r_all = jnp.concatenate(r_parts, axis=1)
            for j in range(nb):
                outs, lses = [], []
                for i in range(2):
                    at = slice((2 * j + i) * c, (2 * j + i + 1) * c)
                    l = r_all[2 * hd:2 * hd + 1, at]
                    outs.append(r_all[i * hd:(i + 1) * hd, at] * (1.0 / l))
                    lses.append(jnp.broadcast_to(ms[2 * j + i] + jnp.log2(l), (hd, c)))
                rows = slice((g0 + j) * c, (g0 + j + 1) * c)
                o_ref[rows, cols] = jnp.concatenate(outs, axis=0).T
                lse_ref[rows, cols] = jnp.concatenate(lses, axis=0).T


def _dilated_call(qfeat, qkv_by_pattern):
    B = qkv_by_pattern[0][0].shape[0]
    steps = max(d for _, d in DIL_PATTERNS)
    in_specs = [pl.BlockSpec(qfeat.shape, lambda b, j: (0, 0, 0))]
    out_specs, out_shape, operands = [], [], [qfeat]
    for (window, d), (q_v, k_v, v_v) in zip(DIL_PATTERNS, qkv_by_pattern):
        assert Q_BLOCK - 1 <= window // d <= Q_BLOCK and steps % d == 0
        L = q_v.shape[1]
        spp = steps // d
        per_step = L // (Q_BLOCK * spp)
        blk = pl.BlockSpec((None, Q_BLOCK * per_step, DIL_Q_W),
                           lambda b, j, spp=spp: (b, j % spp, j // spp))
        prev = pl.BlockSpec((None, Q_BLOCK, DIL_Q_W),
                            lambda b, j, spp=spp, per_step=per_step:
                            (b, jnp.maximum((j % spp) * per_step - 1, 0), j // spp))
        in_specs += [blk, prev, blk, prev, blk]
        operands += [q_v, k_v, k_v, v_v, v_v]
        out_specs += [blk, blk]
        out_shape += [jax.ShapeDtypeStruct((B, L, d * DIL_Q_W), F32)] * 2
    outs = pl.pallas_call(
        _dilated_kernel,
        grid=(B, steps),
        in_specs=in_specs,
        out_specs=tuple(out_specs),
        out_shape=tuple(out_shape),
        compiler_params=pltpu.CompilerParams(
            dimension_semantics=("parallel", "arbitrary"), vmem_limit_bytes=VMEM_LIMIT),
        name="dilated",
    )(*operands)
    return [(outs[2 * i], outs[2 * i + 1]) for i in range(len(DIL_PATTERNS))]


def _merge_kernel(x_ref, g_ref, wg_ref, onsa_ref, o1_ref, o2_ref, o3_ref, l1_ref, l2_ref, l3_ref,
                  wbn_ref, wbd_ref, wo_ref, fg_ref, out_ref, tok_scr, *, final):
    x = x_ref[...]
    tm = x.shape[0]
    xb = _rms(x, g_ref[...]).astype(BF16)
    cz = NSA_HEADS * HEAD_DIM
    d = x.shape[-1]

    def gate_seg(c0, c1):
        return jnp.dot(xb, wg_ref[:, c0:c1], preferred_element_type=F32)

    def token_major(ref, slot, dil):
        if dil == 1:
            return ref[...]
        tiles = DIL_Q_W // LANES
        for p in range(dil):
            for j in range(tiles):
                tok_scr[slot * tiles + j, pl.ds(p, tm // dil, stride=dil), :] = \
                    ref[:, p * DIL_Q_W + j * LANES:p * DIL_Q_W + (j + 1) * LANES]
        return jnp.concatenate([tok_scr[slot * tiles + j] for j in range(tiles)], axis=1)

    o_nsa = onsa_ref[...] * jax.nn.silu(gate_seg(0, cz))
    dils = [dil for _, dil in DIL_PATTERNS]
    o1, o2, o3 = (token_major(r, i, dil) for i, (r, dil) in enumerate(zip((o1_ref, o2_ref, o3_ref), dils)))
    l1, l2, l3 = (token_major(r, 3 + i, dil) for i, (r, dil) in enumerate(zip((l1_ref, l2_ref, l3_ref), dils)))
    mx = jnp.maximum(jnp.maximum(l1, l2), l3)
    e1, e2, e3 = jnp.exp2(l1 - mx), jnp.exp2(l2 - mx), jnp.exp2(l3 - mx)
    inv = 1.0 / (e1 + e2 + e3)
    o_dil = (e1 * inv) * o1 + (e2 * inv) * o2 + (e3 * inv) * o3
    o_dil = o_dil * jax.nn.silu(gate_seg(cz, 2 * cz))
    a = jnp.dot(o_nsa.astype(BF16), wbn_ref[...], preferred_element_type=F32)
    bd = jnp.dot(o_dil.astype(BF16), wbd_ref[...], preferred_element_type=F32)
    merged = (jax.nn.sigmoid(gate_seg(2 * cz, 2 * cz + d)) * a
              + jax.nn.sigmoid(gate_seg(2 * cz + d, 2 * cz + 2 * d)) * bd)
    y = jnp.dot(merged.astype(BF16), wo_ref[...], preferred_element_type=F32)
    out_ref[...] = _rms(x + y, fg_ref[...]) if final else x + y


def _merge_call(x, g, wg, o_nsa, o_d, lse_d, wbn, wbd, wo, fg, tm, final):
    B, S, D = x.shape
    cz = o_nsa.shape[-1]
    row = lambda c: pl.BlockSpec((None, tm, c), lambda b, i: (b, i, 0))
    full = lambda a: pl.BlockSpec(a.shape, lambda b, i: (0,) * a.ndim)
    phase = [pl.BlockSpec((None, tm // dil, dil * cz), lambda b, i: (b, i, 0)) for _, dil in DIL_PATTERNS]
    return pl.pallas_call(
        functools.partial(_merge_kernel, final=final),
        grid=(B, S // tm),
        in_specs=[row(D), full(g), full(wg), row(cz)] + phase + phase
                 + [full(wbn), full(wbd), full(wo), full(fg)],
        out_specs=row(D),
        out_shape=jax.ShapeDtypeStruct((B, S, D), F32),
        scratch_shapes=[pltpu.VMEM((2 * len(DIL_PATTERNS) * cz // LANES, tm, LANES), F32)],
        compiler_params=pltpu.CompilerParams(
            dimension_semantics=("parallel", "parallel"), vmem_limit_bytes=VMEM_LIMIT),
        name="merge",
    )(x, g, wg, o_nsa, *o_d, *lse_d, wbn, wbd, wo, fg)


def _proj_columns(d_model):
    hd, G, H = HEAD_DIM, NSA_GROUPS, NSA_HEADS
    nsa_w = H * hd
    o_q, o_kv = 0, nsa_w
    o_g = o_kv + 6 * G * hd
    o_zn = o_g + 3 * H
    o_qkvd = o_zn + nsa_w
    dil_w = DIL_HEADS * hd
    o_zd = o_qkvd + 3 * dil_w
    o_mg = o_zd + dil_w
    kv = lambda j, g: o_kv + (j * G + g) * hd + np.arange(hd)
    cols = [o_q + np.arange(nsa_w)]
    cols += [kv(j, g) for j in (0, 1) for g in range(G)]
    for j in (2, 4, 3, 5):
        cols += [kv(j, g) for g in range(G)]
    lanes = np.full(LANES, -1)
    for g in range(G):
        for j in range(3):
            lanes[g * GATE_ROWS + j * NSA_HPG + np.arange(NSA_HPG)] = o_g + j * H + g * NSA_HPG + np.arange(NSA_HPG)
    cols.append(lanes)
    cols.append(o_qkvd + np.arange(3 * dil_w))
    proj_cols = np.concatenate(cols)
    assert proj_cols.size == _PROJ_OFFS["vd"][1]
    gate_cols = np.concatenate([o_zn + np.arange(nsa_w), o_zd + np.arange(dil_w),
                                o_mg + np.arange(2 * d_model)])
    return proj_cols, gate_cols


def _take_columns(w, cols, dtype):
    pieces, start = [], 0
    for i in range(1, len(cols) + 1):
        if i < len(cols):
            same_run = (cols[i] < 0 and cols[i - 1] < 0) or (cols[i - 1] >= 0 and cols[i] == cols[i - 1] + 1)
        if i == len(cols) or not same_run:
            n = i - start
            pieces.append(jnp.zeros((w.shape[0], n), dtype) if cols[start] < 0
                          else w[:, int(cols[start]):int(cols[start]) + n].astype(dtype))
            start = i
    return jnp.concatenate(pieces, axis=1)


def _compress_weights(pos_k, w1_k, w2_k, pos_v, w1_v, w2_v):
    hd, G, half = HEAD_DIM, NSA_GROUPS, CMP_BLOCK // 2
    eye = jnp.eye(2 * G, dtype=F32)
    w1 = jnp.stack([w1_k] * G + [w1_v] * G)
    pos = jnp.stack([pos_k] * G + [pos_v] * G)

    def first_layer(lo):
        w = jnp.einsum('slde,st->lsdte', w1[:, lo:lo + half], eye)
        p = pos[:, lo:lo + half].transpose(1, 0, 2)
        return w.reshape(half * 2 * G * hd, 2 * G * hd).astype(BF16), p.reshape(1, half * 2 * G * hd)

    wa, posa = first_layer(0)
    wb, posb = first_layer(half)
    w2 = jnp.zeros((2 * G, hd, 3 * G, hd), F32)
    for g in range(G):
        w2 = w2.at[g, :, 2 * g].set(w2_k).at[G + g, :, 2 * G + g].set(w2_v)
    return posa, posb, wa, wb, w2.reshape(2 * G * hd, 3 * G * hd).astype(BF16)


def _overlap_t(n_cmp_pad, n_sel):
    i = np.arange(n_cmp_pad)[None, :]
    j = np.arange(n_sel)[:, None]
    lo = np.maximum(i * CMP_STRIDE, j * SLC_BLOCK)
    hi = np.minimum(i * CMP_STRIDE + CMP_BLOCK, (j + 1) * SLC_BLOCK)
    return (np.clip(hi - lo, 0, None) / CMP_BLOCK).astype(np.float32)


def kernel(x, norm_g, w_in, cmp_pos_k, cmp_w1_k, cmp_w2_k, cmp_pos_v, cmp_w1_v, cmp_w2_v,
           w_br_nsa, w_br_dil, w_out, final_g):
    B, S, D = x.shape
    n_sel = S // SLC_BLOCK
    assert S % (Q_BLOCK * max(d for _, d in DIL_PATTERNS)) == 0 and n_sel % FLAG_BITS == 0
    n_cmp_pad = S // CMP_STRIDE
    slopes_nsa, slopes_dil = _alibi_slopes()
    proj_cols, gate_cols = _proj_columns(D)
    ovt = jnp.asarray(_overlap_t(n_cmp_pad, n_sel))
    qfeat = jnp.asarray(np.broadcast_to(
        _slope_features(slopes_nsa).reshape(NSA_GROUPS, NSA_HPG, 1, LANES),
        (NSA_GROUPS, NSA_HPG, 8, LANES)))
    qfeat_dil = jnp.asarray(np.broadcast_to(
        _slope_features(slopes_dil).reshape(DIL_HEADS, 1, LANES), (DIL_HEADS, 8, LANES)))
    tm = 512

    h = x
    for layer in range(w_in.shape[0]):
        w_proj = _take_columns(w_in[layer], proj_cols, BF16)
        w_gate = _take_columns(w_in[layer], gate_cols, BF16)
        g_l = norm_g[layer].reshape(1, D)

        qn, cmp_src, ks, kw, vst, vwt, gates, *dil = _proj_call(h, g_l, w_proj, tm)
        cw = _compress_weights(cmp_pos_k[layer], cmp_w1_k[layer], cmp_w2_k[layer],
                               cmp_pos_v[layer], cmp_w1_v[layer], cmp_w2_v[layer])
        kc, cvo = _compress_call(cmp_src, *cw, ovt)
        o_nsa = _nsa_call(qfeat, qn, kc, cvo, ks, vst, kw, vwt, gates)
        dil_out = _dilated_call(qfeat_dil, [dil[3 * i:3 * i + 3] for i in range(len(DIL_PATTERNS))])
        o_d, lse_d = [o for o, _ in dil_out], [lse for _, lse in dil_out]
        h = _merge_call(h, g_l, w_gate, o_nsa, o_d, lse_d, w_br_nsa[layer].astype(BF16),
                        w_br_dil[layer].astype(BF16), w_out[layer].astype(BF16),
                        final_g.reshape(1, D), MERGE_ROWS, final=layer == w_in.shape[0] - 1)
    return h
```

```python
import functools
import math

import numpy as np
import jax
import jax.numpy as jnp
from jax import lax
from jax.experimental import pallas as pl
from jax.experimental.pallas import tpu as pltpu

F32 = jnp.float32
BF16 = jnp.bfloat16

HEAD_DIM = 64
LANES = 128
NSA_HEADS = 8
NSA_GROUPS = 2
NSA_HPG = NSA_HEADS // NSA_GROUPS
CMP_BLOCK = 32
CMP_STRIDE = 16
SLC_BLOCK = 64
SLC_TOPK = 16
WIN = 512
FORCE_SCORE = 1.0e4
DIL_HEADS = 8
DIL_PATTERNS = ((128, 1), (512, 4), (2048, 16))
Q_BLOCK = 128
RMS_EPS = 1e-6
KEY_CHUNK = 128
VT_ROWS = 80
PAIR_VT_ROWS = 144
GATE_ROWS = 16
FEAT_SPLIT = 3
FLAG_BITS = 16
SLC_GROUP = 4
PERM_ROWS = 128
WIN_QB = 256
NSA_QB = 512
DIL_BAND = 4
N_FORCED = 3
assert FORCE_SCORE > NSA_HPG + 1
SEL_PAD_ROWS = 8
ONES_ROWS = 16
LOG2E = math.log2(math.e)
VMEM_LIMIT = 56 * 1024 * 1024


def _alibi_slopes():
    n = NSA_HEADS + DIL_HEADS
    s = 2.0 ** (-8.0 * np.arange(1, n + 1) / n)
    return s[0::2].astype(np.float32), s[1::2].astype(np.float32)


def _nt(a, b):
    return lax.dot_general(a, b, (((1,), (1,)), ((), ())), preferred_element_type=F32)


def _rms(x, g):
    return (x * lax.rsqrt(jnp.mean(x * x, axis=-1, keepdims=True) + RMS_EPS)) * g


def _head_tile(q_pairs, h):
    tile = q_pairs[:, (h // 2) * LANES:(h // 2 + 1) * LANES]
    return pltpu.roll(tile, HEAD_DIM, 1) if h % 2 else tile


def _pos_features(pos):
    lane = lax.broadcasted_iota(jnp.int32, pos.shape, 1) - HEAD_DIM
    hi = (pos // LANES).astype(F32)
    lo = (pos % LANES).astype(F32)
    return jnp.where(lane < 0, 0.0, jnp.where(lane < FEAT_SPLIT, hi, jnp.where(lane < 2 * FEAT_SPLIT, lo, 0.0)))


def _slope_features(slopes):
    out = np.zeros((len(slopes), LANES), np.float32)
    for h, s in enumerate(slopes):
        rest = np.float32(s) * np.float32(LOG2E)
        for i in range(FEAT_SPLIT):
            piece = np.float32(np.asarray(rest, np.float32).astype(BF16))
            out[h, HEAD_DIM + FEAT_SPLIT + i] = piece
            out[h, HEAD_DIM + i] = piece * LANES
            rest = np.float32(rest - piece)
    return out


def _ones_rows(width, dtype):
    row = lax.broadcasted_iota(jnp.int32, (ONES_ROWS, width), 0)
    return jnp.where(row == 0, 1.0, 0.0).astype(dtype)


_PROJ_SEGS = (("qn", 512), ("cmp", 256), ("ks", 128), ("kw", 128), ("vs", 128), ("vw", 128),
              ("gate", 128), ("qd", 512), ("kd", 512), ("vd", 512))
DIL_Q_W = DIL_HEADS * HEAD_DIM
_PROJ_OFFS = dict(zip((n for n, _ in _PROJ_SEGS),
                      zip(np.cumsum([0] + [w for _, w in _PROJ_SEGS])[:-1].tolist(),
                          np.cumsum([w for _, w in _PROJ_SEGS]).tolist())))


def _to_lane_tiles(scr, first, val):
    for j in range(val.shape[1] // LANES):
        scr[first + j] = val[:, j * LANES:(j + 1) * LANES]


def _regroup_rows(scr, lo, hi, out_ref, d, dtype):
    rows, width = scr.shape[1], (hi - lo) * LANES
    for p in range(d):
        for j in range(hi - lo):
            tile = scr[lo + j, pl.ds(p, rows // d, stride=d), :] if d > 1 else scr[lo + j]
            out_ref[:, p * width + j * LANES:p * width + (j + 1) * LANES] = tile.astype(dtype)


def _proj_kernel(x_ref, g_ref, w_ref, perm_ref, qn_ref, cmp_ref, ks_ref, kw_ref, vst_ref, vwt_ref,
                 gate_ref, *dil_refs):
    cmp_scr = dil_refs[-1]
    tm = x_ref.shape[0]
    xb = _rms(x_ref[...], g_ref[...]).astype(BF16)
    scale = HEAD_DIM ** -0.5

    def seg(name, last=None):
        c0, c1 = _PROJ_OFFS[name][0], _PROJ_OFFS[last or name][1]
        return jnp.dot(xb, w_ref[:, c0:c1], preferred_element_type=F32)

    qn_ref[...] = (seg("qn") * (scale * LOG2E)).astype(BF16)
    _to_lane_tiles(cmp_scr, 0, seg("cmp"))
    _regroup_rows(cmp_scr, 0, cmp_scr.shape[0], cmp_ref, CMP_STRIDE, F32)
    feat = _pos_features(pl.program_id(1) * tm + lax.broadcasted_iota(jnp.int32, (tm, LANES), 0))
    k_sw, v_sw = seg("ks", "kw"), seg("vs", "vw")
    ks, kw = k_sw[:, 0:LANES], k_sw[:, LANES:]
    vst, vwt = v_sw[:, 0:LANES].T, v_sw[:, LANES:].T
    gate_t = jax.nn.sigmoid(seg("gate")).T
    tail = _ones_rows(tm, F32)
    lane = lax.broadcasted_iota(jnp.int32, (tm, LANES), 1)
    for g in range(NSA_GROUPS):
        ks_ref[g] = jnp.where(lane < HEAD_DIM, _head_tile(ks, g), feat).astype(BF16)
        kw_ref[g] = jnp.where(lane < HEAD_DIM, _head_tile(kw, g), feat).astype(BF16)
        gate_ref[g] = gate_t[g * GATE_ROWS:(g + 1) * GATE_ROWS]
        for src, dst in ((vst, vst_ref), (vwt, vwt_ref)):
            vt = jnp.concatenate([src[g * HEAD_DIM:(g + 1) * HEAD_DIM], tail], axis=0).astype(BF16)
            for j in range(tm // KEY_CHUNK):
                dst[g, j] = vt[:, j * KEY_CHUNK:(j + 1) * KEY_CHUNK]
    qkv = jnp.concatenate([(seg("qd") * (scale * LOG2E)).astype(BF16), seg("kd").astype(BF16),
                           seg("vd").astype(BF16)], axis=1)
    bounds = (0, DIL_Q_W, 2 * DIL_Q_W, 3 * DIL_Q_W)
    subs = tm // PERM_ROWS
    for i, (_, d) in enumerate(DIL_PATTERNS):
        if d > 1:
            by_phase = [jnp.dot(perm_ref[i], qkv[s * PERM_ROWS:(s + 1) * PERM_ROWS],
                                preferred_element_type=F32) for s in range(subs)]
        r = PERM_ROWS // d
        for p in range(d):
            rows = qkv if d == 1 else jnp.concatenate(
                [by_phase[s][p * r:(p + 1) * r] for s in range(subs)], axis=0).astype(BF16)
            for j, ref in enumerate(dil_refs[3 * i:3 * i + 3]):
                width = bounds[j + 1] - bounds[j]
                ref[:, p * width:(p + 1) * width] = rows[:, bounds[j]:bounds[j + 1]]


def _proj_call(x, g, w, tm):
    B, S, D = x.shape
    G = NSA_GROUPS
    cmp_w = _PROJ_OFFS["cmp"][1] - _PROJ_OFFS["cmp"][0]
    row = lambda c: pl.BlockSpec((None, tm, c), lambda b, i: (b, i, 0))
    k_spec = pl.BlockSpec((None, G, tm, LANES), lambda b, i: (b, 0, i, 0))
    vt_spec = pl.BlockSpec((None, G, tm // KEY_CHUNK, VT_ROWS, KEY_CHUNK), lambda b, i: (b, 0, i, 0, 0))
    out_shape = (
        jax.ShapeDtypeStruct((B, S, 512), BF16),
        jax.ShapeDtypeStruct((B, S // CMP_STRIDE, CMP_STRIDE * cmp_w), F32),
        jax.ShapeDtypeStruct((B, G, S, LANES), BF16),
        jax.ShapeDtypeStruct((B, G, S, LANES), BF16),
        jax.ShapeDtypeStruct((B, G, S // KEY_CHUNK, VT_ROWS, KEY_CHUNK), BF16),
        jax.ShapeDtypeStruct((B, G, S // KEY_CHUNK, VT_ROWS, KEY_CHUNK), BF16),
        jax.ShapeDtypeStruct((B, G, GATE_ROWS, S), F32),
    )
    dil_specs = ()
    for _, d in DIL_PATTERNS:
        for width in (DIL_Q_W,) * 3:
            out_shape += (jax.ShapeDtypeStruct((B, S // d, d * width), BF16),)
            dil_specs += (pl.BlockSpec((None, tm // d, d * width), lambda b, i: (b, i, 0)),)
    r = np.arange(PERM_ROWS)
    perm = np.zeros((len(DIL_PATTERNS), PERM_ROWS, PERM_ROWS), np.float32)
    for i, (_, d) in enumerate(DIL_PATTERNS):
        assert (PERM_ROWS // d) % 8 == 0
        perm[i, (r % d) * (PERM_ROWS // d) + r // d, r] = 1.0
    perm = jnp.asarray(perm, BF16)
    return pl.pallas_call(
        _proj_kernel,
        grid=(B, S // tm),
        in_specs=[row(D), pl.BlockSpec((1, D), lambda b, i: (0, 0)),
                  pl.BlockSpec(w.shape, lambda b, i: (0, 0)),
                  pl.BlockSpec(perm.shape, lambda b, i: (0, 0, 0))],
        out_specs=(row(512),
                   pl.BlockSpec((None, tm // CMP_STRIDE, CMP_STRIDE * cmp_w), lambda b, i: (b, i, 0)),
                   k_spec, k_spec, vt_spec, vt_spec,
                   pl.BlockSpec((None, G, GATE_ROWS, tm), lambda b, i: (b, 0, 0, i))) + dil_specs,
        out_shape=out_shape,
        scratch_shapes=[pltpu.VMEM((cmp_w // LANES, tm, LANES), F32)],
        compiler_params=pltpu.CompilerParams(
            dimension_semantics=("parallel", "parallel"), vmem_limit_bytes=VMEM_LIMIT),
        name="proj",
    )(x, g, w, perm)


def _compress_kernel(r_ref, posa_ref, posb_ref, wa_ref, wb_ref, w2_ref, ovt_ref, kc_ref, cvo_ref):
    r = r_ref[...]
    n = r.shape[0]
    ha = jnp.dot((r + posa_ref[...]).astype(BF16), wa_ref[...], preferred_element_type=F32)
    hb = jnp.dot((r + posb_ref[...]).astype(BF16), wb_ref[...], preferred_element_type=F32)
    hid = jax.nn.gelu(ha + pltpu.roll(hb, n - 1, 0))
    out = jnp.dot(hid.astype(BF16), w2_ref[...], preferred_element_type=F32)
    cmp_end = lax.broadcasted_iota(jnp.int32, (n, LANES), 0) * CMP_STRIDE + (CMP_BLOCK - 1)
    feat = _pos_features(cmp_end)
    v_t = out[:, NSA_GROUPS * LANES:].T
    for g in range(NSA_GROUPS):
        kc_ref[g] = (out[:, g * LANES:(g + 1) * LANES] + feat).astype(BF16)
        cvo_ref[g] = jnp.concatenate([v_t[g * HEAD_DIM:(g + 1) * HEAD_DIM], ovt_ref[...],
                                      _ones_rows(n, F32)], axis=0).astype(BF16)


def _compress_call(r, posa, posb, wa, wb, w2, ovt):
    B, R, C = r.shape
    G = NSA_GROUPS
    rows = HEAD_DIM + ovt.shape[0] + ONES_ROWS
    full = lambda a: pl.BlockSpec(a.shape, lambda b: (0,) * a.ndim)
    return pl.pallas_call(
        _compress_kernel,
        grid=(B,),
        in_specs=[pl.BlockSpec((None, R, C), lambda b: (b, 0, 0)),
                  full(posa), full(posb), full(wa), full(wb), full(w2), full(ovt)],
        out_specs=(pl.BlockSpec((None, G, R, LANES), lambda b: (b, 0, 0, 0)),
                   pl.BlockSpec((None, G, rows, R), lambda b: (b, 0, 0, 0))),
        out_shape=(jax.ShapeDtypeStruct((B, G, R, LANES), BF16),
                   jax.ShapeDtypeStruct((B, G, rows, R), BF16)),
        compiler_params=pltpu.CompilerParams(
            dimension_semantics=("parallel",), vmem_limit_bytes=VMEM_LIMIT),
        name="compress",
    )(r, posa, posb, wa, wb, w2, ovt)


def _flash_step(s, bias, v_t, m_ref, acc_ref):
    ps, alphas = [], []
    qb = bias.shape[1]
    for h in range(NSA_HPG):
        cols = slice(h * qb, (h + 1) * qb)
        sh = s[:, cols] + bias
        m_old = m_ref[:, cols]
        m_new = jnp.maximum(m_old, jnp.max(sh, axis=0, keepdims=True))
        alphas.append(jnp.exp2(m_old - m_new))
        ps.append(jnp.exp2(sh - m_new).astype(BF16))
        m_ref[:, cols] = m_new
    pv = jnp.dot(v_t, jnp.concatenate(ps, axis=1), preferred_element_type=F32)
    acc_ref[...] = jnp.concatenate(alphas, axis=1) * acc_ref[...] + pv


def _softmax_pv(s, bias, v_t):
    ms, ps = [], []
    qb = bias.shape[1]
    for h in range(NSA_HPG):
        sh = s[:, h * qb:(h + 1) * qb] + bias
        ms.append(jnp.max(sh, axis=0, keepdims=True))
        ps.append(jnp.exp2(sh - ms[h]).astype(BF16))
    r = jnp.dot(v_t, jnp.concatenate(ps, axis=1), preferred_element_type=F32)
    return jnp.concatenate(ms, axis=1), r


def _normalise(acc):
    return acc[0:HEAD_DIM] * (1.0 / jnp.maximum(acc[HEAD_DIM:HEAD_DIM + 1], 1e-30))


def _nsa_kernel(qfeat_ref, q_ref, kc_ref, cvo_ref, ks_ref, vst_ref, kw_ref, vwt_ref, gate_ref,
                o_ref, sel_ref, ms_ref, accs_ref, flag_ref, list_ref):
    n = pl.program_id(2)
    qb, kc = NSA_QB, KEY_CHUNK
    t0 = n * qb
    lane = lax.broadcasted_iota(jnp.int32, (qb, LANES), 1)
    qf = q_ref[...].astype(F32)
    q = jnp.concatenate(
        [jnp.where(lane < HEAD_DIM, _head_tile(qf, h), qfeat_ref[h][0:1, :]) for h in range(NSA_HPG)],
        axis=0).astype(BF16)

    wq = min(WIN_QB, qb)
    span = WIN + wq
    back_w = (lax.broadcasted_iota(jnp.int32, (span, wq), 1)
              - lax.broadcasted_iota(jnp.int32, (span, wq), 0))
    win_parts = []
    for j in range(qb // wq):
        k_lo = pl.multiple_of(jnp.maximum(t0 + j * wq - WIN, 0), kc)
        dist = back_w + (t0 + j * wq - k_lo)
        bias = jnp.where(dist >= 0, jnp.where(dist < WIN, 0.0, -jnp.inf), -jnp.inf)
        q_part = jnp.concatenate([q[h * qb + j * wq:h * qb + (j + 1) * wq] for h in range(NSA_HPG)], axis=0)
        vt_w = jnp.concatenate([vwt_ref[k_lo // kc + u] for u in range(span // kc)], axis=1)
        win_parts.append(_normalise(_softmax_pv(_nt(kw_ref[pl.ds(k_lo, span), :], q_part), bias, vt_w)[1]))
    o_win = jnp.concatenate([part[:, h * wq:(h + 1) * wq] for h in range(NSA_HPG) for part in win_parts],
                            axis=1)

    n_cmp = kc_ref.shape[0]
    n_sel = sel_ref.shape[0] - SEL_PAD_ROWS
    s = _nt(kc_ref[...], q)
    cmp_end = lax.broadcasted_iota(jnp.int32, (n_cmp, qb), 0) * CMP_STRIDE + (CMP_BLOCK - 1)
    bias_c = jnp.where(cmp_end <= t0 + lax.broadcasted_iota(jnp.int32, (n_cmp, qb), 1), 0.0, -jnp.inf)
    es = []
    for h in range(NSA_HPG):
        sh = s[:, h * qb:(h + 1) * qb] + bias_c
        m = jnp.max(sh, axis=0, keepdims=True)
        es.append(jnp.exp2(sh - jnp.where(m == -jnp.inf, 0.0, m)).astype(BF16))
    r = jnp.dot(cvo_ref[...], jnp.concatenate(es, axis=1), preferred_element_type=F32)
    inv_l = 1.0 / jnp.maximum(r[HEAD_DIM + n_sel:HEAD_DIM + n_sel + 1], 1e-30)
    o_cmp = r[0:HEAD_DIM] * inv_l
    imp = r[HEAD_DIM:HEAD_DIM + n_sel] * inv_l
    imp_t = imp[:, 0:qb]
    for h in range(1, NSA_HPG):
        imp_t = imp_t + imp[:, h * qb:(h + 1) * qb]

    blk = lax.broadcasted_iota(jnp.int32, (n_sel, qb), 0).astype(F32)
    cur = ((t0 + lax.broadcasted_iota(jnp.int32, (n_sel, qb), 1)) // SLC_BLOCK).astype(F32)
    ago = cur - blk
    forced = jnp.where(blk == 0, 1.0, jnp.where(ago == 0, 1.0, jnp.where(ago == 1, 1.0, 0.0)))
    valid = ago >= 0
    score = jnp.where(valid, jnp.where(forced > 0.5, -1.0, imp_t), -1.0)
    for _ in range(max(min(SLC_TOPK, n_sel) - N_FORCED, 0)):
        mx = jnp.max(score, axis=0, keepdims=True)
        idx = jnp.min(jnp.where(score == mx, blk, float(n_sel)), axis=0, keepdims=True)
        score = jnp.where(blk == idx, -jnp.inf, score)
    sel_t = jnp.where(valid, jnp.where(score == -jnp.inf, 1.0, forced), 0.0)
    sel_ref[0:n_sel, :] = jnp.where(sel_t > 0.5, 0.0, -jnp.inf)
    sel_ref[n_sel:, :] = jnp.full((SEL_PAD_ROWS, qb), -jnp.inf, F32)
    any_q = jnp.max(sel_t, axis=1, keepdims=True)
    bit = jnp.left_shift(1, lax.broadcasted_iota(jnp.int32, (n_sel, 1), 0) % FLAG_BITS).astype(F32)
    packed = any_q * bit
    for i in range(n_sel // FLAG_BITS):
        word = jnp.sum(packed[i * FLAG_BITS:(i + 1) * FLAG_BITS], axis=0, keepdims=True)
        flag_ref[i] = word.astype(jnp.int32)[0, 0]

    blocks_per_chunk = kc // SLC_BLOCK
    chunks_per_word = FLAG_BITS // blocks_per_chunk
    row = lax.broadcasted_iota(jnp.int32, (kc, qb), 0)

    def chunk_bias(first_block):
        bias = sel_ref[pl.ds(first_block, 1), :]
        for j in range(1, blocks_per_chunk):
            bias = jnp.where(row < j * SLC_BLOCK, bias, sel_ref[pl.ds(first_block + j, 1), :])
        return bias

    own = qb // kc
    first_own = t0 // kc
    off_first = jnp.where(n == 0, -jnp.inf, 0.0)
    query = lax.broadcasted_iota(jnp.int32, (kc, qb), 1)
    biases = [chunk_bias(0) + off_first]
    for u in range(own):
        causal = jnp.where(row + u * kc <= query, 0.0, -jnp.inf)
        biases.append(chunk_bias((first_own + u) * blocks_per_chunk) + causal)
    keys = jnp.concatenate([ks_ref[0:kc, :], ks_ref[pl.ds(pl.multiple_of(t0, kc), qb), :]], axis=0)
    vt_s = jnp.concatenate([vst_ref[0]] + [vst_ref[first_own + u] for u in range(own)], axis=1)
    m_s, acc_s = _softmax_pv(_nt(keys, q), jnp.concatenate(biases, axis=0), vt_s)
    ms_ref[...] = m_s
    accs_ref[...] = acc_s

    def scan_body(w, cnt):
        word = flag_ref[w]
        for j in range(chunks_per_word):
            c = w * chunks_per_word + j
            bits = (word >> (j * blocks_per_chunk)) & ((1 << blocks_per_chunk) - 1)
            list_ref[cnt] = c
            cnt = cnt + ((bits != 0) & (c >= 1) & (c < first_own)).astype(jnp.int32)
        return cnt

    cnt = lax.fori_loop(0, (first_own + chunks_per_word - 1) // chunks_per_word, scan_body, 0)
    for u in range(SLC_GROUP):
        list_ref[cnt + u] = -1

    def slc_body(gi, carry):
        keys, biases, vts = [], [], []
        for u in range(SLC_GROUP):
            c = list_ref[gi * SLC_GROUP + u]
            cc = jnp.maximum(c, 0)
            biases.append(chunk_bias(jnp.where(c >= 0, cc * blocks_per_chunk, n_sel)))
            keys.append(ks_ref[pl.ds(pl.multiple_of(cc * kc, kc), kc), :])
            vts.append(vst_ref[cc])
        _flash_step(_nt(jnp.concatenate(keys, axis=0), q), jnp.concatenate(biases, axis=0),
                    jnp.concatenate(vts, axis=1), ms_ref, accs_ref)
        return carry

    lax.fori_loop(0, (cnt + SLC_GROUP - 1) // SLC_GROUP, slc_body, 0)

    o_slc = _normalise(accs_ref[...])
    gate = gate_ref[...]
    outs = []
    for h in range(NSA_HPG):
        cols = slice(h * qb, (h + 1) * qb)
        gc = [gate[j * NSA_HPG + h:j * NSA_HPG + h + 1, :] for j in range(3)]
        outs.append(gc[0] * o_cmp[:, cols] + gc[1] * o_slc[:, cols] + gc[2] * o_win[:, cols])
    o_ref[...] = jnp.concatenate(outs, axis=0).T


def _nsa_call(qfeat, qn, kc, cvo, ks, vst, kw, vwt, gates):
    B, S, _ = qn.shape
    G = NSA_GROUPS
    qb = NSA_QB
    nb = S // qb
    n_sel = S // SLC_BLOCK
    width = NSA_HPG * qb
    per_group = lambda a: pl.BlockSpec((None, None) + a.shape[2:],
                                       lambda b, g, n: (b, g) + (0,) * (a.ndim - 2))
    return pl.pallas_call(
        _nsa_kernel,
        grid=(B, G, nb),
        in_specs=[
            pl.BlockSpec((None,) + qfeat.shape[1:], lambda b, g, n: (g, 0, 0, 0)),
            pl.BlockSpec((None, qb, NSA_HPG * HEAD_DIM), lambda b, g, n: (b, n, g)),
            per_group(kc), per_group(cvo), per_group(ks), per_group(vst), per_group(kw), per_group(vwt),
            pl.BlockSpec((None, None, GATE_ROWS, qb), lambda b, g, n: (b, g, 0, n)),
        ],
        out_specs=pl.BlockSpec((None, qb, NSA_HPG * HEAD_DIM), lambda b, g, n: (b, n, g)),
        out_shape=jax.ShapeDtypeStruct((B, S, NSA_HEADS * HEAD_DIM), F32),
        scratch_shapes=[pltpu.VMEM((n_sel + SEL_PAD_ROWS, qb), F32),
                        pltpu.VMEM((1, width), F32), pltpu.VMEM((VT_ROWS, width), F32),
                        pltpu.SMEM((n_sel // FLAG_BITS,), jnp.int32),
                        pltpu.SMEM((S // KEY_CHUNK + SLC_GROUP,), jnp.int32)],
        compiler_params=pltpu.CompilerParams(
            dimension_semantics=("parallel", "parallel", "arbitrary"), vmem_limit_bytes=VMEM_LIMIT),
        name="nsa",
    )(qfeat, qn, kc, cvo, ks, vst, kw, vwt, gates)


def _dilated_kernel(qfeat_ref, *refs):
    n_pat = len(DIL_PATTERNS)
    j = pl.program_id(1)
    for i, (window, dilation) in enumerate(DIL_PATTERNS):
        steps_per_phase = max(d for _, d in DIL_PATTERNS) // dilation
        _dilated_blocks(qfeat_ref, *refs[5 * i:5 * i + 5], *refs[5 * n_pat + 2 * i:5 * n_pat + 2 * i + 2],
                        wd=window // dilation, dilation=dilation,
                        phase=j // steps_per_phase, step=j % steps_per_phase)


def _dilated_blocks(qfeat_ref, q_ref, kp_ref, kc_ref, vp_ref, vc_ref, o_ref, lse_ref, *,
                    wd, dilation, phase, step):
    c = Q_BLOCK
    hd = HEAD_DIM
    blocks = q_ref.shape[0] // c
    lane = lax.broadcasted_iota(jnp.int32, (blocks * c, LANES), 1)
    key = lax.broadcasted_iota(jnp.int32, (2 * c, c), 0)
    qry = lax.broadcasted_iota(jnp.int32, (2 * c, c), 1)
    first = step == 0
    off_later = jnp.where(key < c, qry - (key + (wd - c)), (key - c) - qry)
    off_first = jnp.where(key < c, jnp.where(first, 2 * c, qry) - (key + (wd - c)), (key - c) - qry)
    bias_later = jnp.where(off_later <= 0, 0.0, -jnp.inf)
    bias_first = jnp.where(off_first <= 0, 0.0, -jnp.inf)
    sub = (step * blocks - 1) * c + lax.broadcasted_iota(jnp.int32, ((blocks + 1) * c, LANES), 0)
    pos_feat = _pos_features(sub * dilation + phase).astype(BF16)
    n_keys = (blocks + 1) * c
    tail = _ones_rows(n_keys, BF16)
    assert PAIR_VT_ROWS == 2 * hd + ONES_ROWS
    for p in range(DIL_HEADS // 2):
        cols = slice(p * LANES, (p + 1) * LANES)
        q_pair = q_ref[:, cols]
        zero = jnp.zeros_like(q_pair)
        feats = [jnp.broadcast_to(qfeat_ref[2 * p + i][0:1, :], (blocks * c, LANES)).astype(BF16)
                 for i in range(2)]
        q_a = jnp.concatenate([jnp.where(lane < hd, q_pair, zero), feats[0]], axis=1)
        q_b = jnp.concatenate([jnp.where(lane < hd, zero, q_pair), feats[1]], axis=1)
        k_all = jnp.concatenate([kp_ref[:, cols], kc_ref[:, cols]], axis=0)
        k_all = jnp.concatenate([k_all, pos_feat], axis=1)
        v_t = jnp.concatenate([jnp.concatenate([vp_ref[:, cols].T, vc_ref[:, cols].T], axis=1), tail],
                              axis=0)
        nb = min(DIL_BAND, blocks)
        for g0 in range(0, blocks, nb):
            q_aug = jnp.concatenate([part[j * c:(j + 1) * c] for j in range(g0, g0 + nb)
                                     for part in (q_a, q_b)], axis=0)
            s_all = _nt(k_all[g0 * c:(g0 + nb + 1) * c], q_aug)
            ms, band = [], []
            for j in range(nb):
                bias = bias_later if g0 + j else bias_first
                probs = []
                for i in range(2):
                    si = s_all[j * c:(j + 2) * c, (2 * j + i) * c:(2 * j + i + 1) * c] + bias
                    ms.append(jnp.max(si, axis=0, keepdims=True))
                    probs.append(jnp.exp2(si - ms[-1]).astype(BF16))
                pieces = [jnp.zeros((j * c, 2 * c), BF16)] if j else []
                pieces.append(jnp.concatenate(probs, axis=1))
                if j < nb - 1:
                    pieces.append(jnp.zeros(((nb - 1 - j) * c, 2 * c), BF16))
                band.append(jnp.concatenate(pieces, axis=0))
            r_all = jnp.dot(v_t[:, g0 * c:(g0 + nb + 1) * c], jnp.concatenate(band, axis=1),
                            preferred_element_type=F32)
            for j in range(nb):
                outs, lses = [], []
                for i in range(2):
                    at = slice((2 * j + i) * c, (2 * j + i + 1) * c)
                    l = r_all[2 * hd:2 * hd + 1, at]
                    outs.append(r_all[i * hd:(i + 1) * hd, at] * (1.0 / l))
                    lses.append(jnp.broadcast_to(ms[2 * j + i] + jnp.log2(l), (hd, c)))
                rows = slice((g0 + j) * c, (g0 + j + 1) * c)
                o_ref[rows, cols] = jnp.concatenate(outs, axis=0).T
                lse_ref[rows, cols] = jnp.concatenate(lses, axis=0).T


def _dilated_call(qfeat, qkv_by_pattern):
    B = qkv_by_pattern[0][0].shape[0]
    steps = max(d for _, d in DIL_PATTERNS)
    in_specs = [pl.BlockSpec(qfeat.shape, lambda b, j: (0, 0, 0))]
    out_specs, out_shape, operands = [], [], [qfeat]
    for (window, d), (q_v, k_v, v_v) in zip(DIL_PATTERNS, qkv_by_pattern):
        assert Q_BLOCK - 1 <= window // d <= Q_BLOCK and steps % d == 0
        L = q_v.shape[1]
        spp = steps // d
        per_step = L // (Q_BLOCK * spp)
        blk = pl.BlockSpec((None, Q_BLOCK * per_step, DIL_Q_W),
                           lambda b, j, spp=spp: (b, j % spp, j // spp))
        prev = pl.BlockSpec((None, Q_BLOCK, DIL_Q_W),
                            lambda b, j, spp=spp, per_step=per_step:
                            (b, jnp.maximum((j % spp) * per_step - 1, 0), j // spp))
        in_specs += [blk, prev, blk, prev, blk]
        operands += [q_v, k_v, k_v, v_v, v_v]
        out_specs += [blk, blk]
        out_shape += [jax.ShapeDtypeStruct((B, L, d * DIL_Q_W), F32)] * 2
    outs = pl.pallas_call(
        _dilated_kernel,
        grid=(B, steps),
        in_specs=in_specs,
        out_specs=tuple(out_specs),
        out_shape=tuple(out_shape),
        compiler_params=pltpu.CompilerParams(
            dimension_semantics=("parallel", "arbitrary"), vmem_limit_bytes=VMEM_LIMIT),
        name="dilated",
    )(*operands)
    return [(outs[2 * i], outs[2 * i + 1]) for i in range(len(DIL_PATTERNS))]


def _merge_kernel(x_ref, g_ref, wg_ref, onsa_ref, o1_ref, o2_ref, o3_ref, l1_ref, l2_ref, l3_ref,
                  wbn_ref, wbd_ref, wo_ref, fg_ref, out_ref, tok_scr, *, final):
    x = x_ref[...]
    tm = x.shape[0]
    xb = _rms(x, g_ref[...]).astype(BF16)
    cz = NSA_HEADS * HEAD_DIM
    d = x.shape[-1]

    def gate_seg(c0, c1):
        return jnp.dot(xb, wg_ref[:, c0:c1], preferred_element_type=F32)

    def token_major(ref, slot, dil):
        if dil == 1:
            return ref[...]
        tiles = DIL_Q_W // LANES
        for p in range(dil):
            for j in range(tiles):
                tok_scr[slot * tiles + j, pl.ds(p, tm // dil, stride=dil), :] = \
                    ref[:, p * DIL_Q_W + j * LANES:p * DIL_Q_W + (j + 1) * LANES]
        return jnp.concatenate([tok_scr[slot * tiles + j] for j in range(tiles)], axis=1)

    o_nsa = onsa_ref[...] * jax.nn.silu(gate_seg(0, cz))
    dils = [dil for _, dil in DIL_PATTERNS]
    o1, o2, o3 = (token_major(r, i, dil) for i, (r, dil) in enumerate(zip((o1_ref, o2_ref, o3_ref), dils)))
    l1, l2, l3 = (token_major(r, 3 + i, dil) for i, (r, dil) in enumerate(zip((l1_ref, l2_ref, l3_ref), dils)))
    mx = jnp.maximum(jnp.maximum(l1, l2), l3)
    e1, e2, e3 = jnp.exp2(l1 - mx), jnp.exp2(l2 - mx), jnp.exp2(l3 - mx)
    inv = 1.0 / (e1 + e2 + e3)
    o_dil = (e1 * inv) * o1 + (e2 * inv) * o2 + (e3 * inv) * o3
    o_dil = o_dil * jax.nn.silu(gate_seg(cz, 2 * cz))
    a = jnp.dot(o_nsa.astype(BF16), wbn_ref[...], preferred_element_type=F32)
    bd = jnp.dot(o_dil.astype(BF16), wbd_ref[...], preferred_element_type=F32)
    merged = (jax.nn.sigmoid(gate_seg(2 * cz, 2 * cz + d)) * a
              + jax.nn.sigmoid(gate_seg(2 * cz + d, 2 * cz + 2 * d)) * bd)
    y = jnp.dot(merged.astype(BF16), wo_ref[...], preferred_element_type=F32)
    out_ref[...] = _rms(x + y, fg_ref[...]) if final else x + y


def _merge_call(x, g, wg, o_nsa, o_d, lse_d, wbn, wbd, wo, fg, tm, final):
    B, S, D = x.shape
    cz = o_nsa.shape[-1]
    row = lambda c: pl.BlockSpec((None, tm, c), lambda b, i: (b, i, 0))
    full = lambda a: pl.BlockSpec(a.shape, lambda b, i: (0,) * a.ndim)
    phase = [pl.BlockSpec((None, tm // dil, dil * cz), lambda b, i: (b, i, 0)) for _, dil in DIL_PATTERNS]
    return pl.pallas_call(
        functools.partial(_merge_kernel, final=final),
        grid=(B, S // tm),
        in_specs=[row(D), full(g), full(wg), row(cz)] + phase + phase
                 + [full(wbn), full(wbd), full(wo), full(fg)],
        out_specs=row(D),
        out_shape=jax.ShapeDtypeStruct((B, S, D), F32),
        scratch_shapes=[pltpu.VMEM((2 * len(DIL_PATTERNS) * cz // LANES, tm, LANES), F32)],
        compiler_params=pltpu.CompilerParams(
            dimension_semantics=("parallel", "parallel"), vmem_limit_bytes=VMEM_LIMIT),
        name="merge",
    )(x, g, wg, o_nsa, *o_d, *lse_d, wbn, wbd, wo, fg)


def _proj_columns(d_model):
    hd, G, H = HEAD_DIM, NSA_GROUPS, NSA_HEADS
    nsa_w = H * hd
    o_q, o_kv = 0, nsa_w
    o_g = o_kv + 6 * G * hd
    o_zn = o_g + 3 * H
    o_qkvd = o_zn + nsa_w
    dil_w = DIL_HEADS * hd
    o_zd = o_qkvd + 3 * dil_w
    o_mg = o_zd + dil_w
    kv = lambda j, g: o_kv + (j * G + g) * hd + np.arange(hd)
    cols = [o_q + np.arange(nsa_w)]
    cols += [kv(j, g) for j in (0, 1) for g in range(G)]
    for j in (2, 4, 3, 5):
        cols += [kv(j, g) for g in range(G)]
    lanes = np.full(LANES, -1)
    for g in range(G):
        for j in range(3):
            lanes[g * GATE_ROWS + j * NSA_HPG + np.arange(NSA_HPG)] = o_g + j * H + g * NSA_HPG + np.arange(NSA_HPG)
    cols.append(lanes)
    cols.append(o_qkvd + np.arange(3 * dil_w))
    proj_cols = np.concatenate(cols)
    assert proj_cols.size == _PROJ_OFFS["vd"][1]
    gate_cols = np.concatenate([o_zn + np.arange(nsa_w), o_zd + np.arange(dil_w),
                                o_mg + np.arange(2 * d_model)])
    return proj_cols, gate_cols


def _take_columns(w, cols, dtype):
    pieces, start = [], 0
    for i in range(1, len(cols) + 1):
        if i < len(cols):
            same_run = (cols[i] < 0 and cols[i - 1] < 0) or (cols[i - 1] >= 0 and cols[i] == cols[i - 1] + 1)
        if i == len(cols) or not same_run:
            n = i - start
            pieces.append(jnp.zeros((w.shape[0], n), dtype) if cols[start] < 0
                          else w[:, int(cols[start]):int(cols[start]) + n].astype(dtype))
            start = i
    return jnp.concatenate(pieces, axis=1)


def _compress_weights(pos_k, w1_k, w2_k, pos_v, w1_v, w2_v):
    hd, G, half = HEAD_DIM, NSA_GROUPS, CMP_BLOCK // 2
    eye = jnp.eye(2 * G, dtype=F32)
    w1 = jnp.stack([w1_k] * G + [w1_v] * G)
    pos = jnp.stack([pos_k] * G + [pos_v] * G)

    def first_layer(lo):
        w = jnp.einsum('slde,st->lsdte', w1[:, lo:lo + half], eye)
        p = pos[:, lo:lo + half].transpose(1, 0, 2)
        return w.reshape(half * 2 * G * hd, 2 * G * hd).astype(BF16), p.reshape(1, half * 2 * G * hd)

    wa, posa = first_layer(0)
    wb, posb = first_layer(half)
    w2 = jnp.zeros((2 * G, hd, 3 * G, hd), F32)
    for g in range(G):
        w2 = w2.at[g, :, 2 * g].set(w2_k).at[G + g, :, 2 * G + g].set(w2_v)
    return posa, posb, wa, wb, w2.reshape(2 * G * hd, 3 * G * hd).astype(BF16)


def _overlap_t(n_cmp_pad, n_sel):
    i = np.arange(n_cmp_pad)[None, :]
    j = np.arange(n_sel)[:, None]
    lo = np.maximum(i * CMP_STRIDE, j * SLC_BLOCK)
    hi = np.minimum(i * CMP_STRIDE + CMP_BLOCK, (j + 1) * SLC_BLOCK)
    return (np.clip(hi - lo, 0, None) / CMP_BLOCK).astype(np.float32)


def kernel(x, norm_g, w_in, cmp_pos_k, cmp_w1_k, cmp_w2_k, cmp_pos_v, cmp_w1_v, cmp_w2_v,
           w_br_nsa, w_br_dil, w_out, final_g):
    B, S, D = x.shape
    n_sel = S // SLC_BLOCK
    assert S % (Q_BLOCK * max(d for _, d in DIL_PATTERNS)) == 0 and n_sel % FLAG_BITS == 0
    n_cmp_pad = S // CMP_STRIDE
    slopes_nsa, slopes_dil = _alibi_slopes()
    proj_cols, gate_cols = _proj_columns(D)
    ovt = jnp.asarray(_overlap_t(n_cmp_pad, n_sel))
    qfeat = jnp.asarray(np.broadcast_to(
        _slope_features(slopes_nsa).reshape(NSA_GROUPS, NSA_HPG, 1, LANES),
        (NSA_GROUPS, NSA_HPG, 8, LANES)))
    qfeat_dil = jnp.asarray(np.broadcast_to(
        _slope_features(slopes_dil).reshape(DIL_HEADS, 1, LANES), (DIL_HEADS, 8, LANES)))
    tm = 512

    h = x
    for layer in range(w_in.shape[0]):
        w_proj = _take_columns(w_in[layer], proj_cols, BF16)
        w_gate = _take_columns(w_in[layer], gate_cols, BF16)
        g_l = norm_g[layer].reshape(1, D)

        qn, cmp_src, ks, kw, vst, vwt, gates, *dil = _proj_call(h, g_l, w_proj, tm)
        cw = _compress_weights(cmp_pos_k[layer], cmp_w1_k[layer], cmp_w2_k[layer],
                               cmp_pos_v[layer], cmp_w1_v[layer], cmp_w2_v[layer])
        kc, cvo = _compress_call(cmp_src, *cw, ovt)
        o_nsa = _nsa_call(qfeat, qn, kc, cvo, ks, vst, kw, vwt, gates)
        dil_out = _dilated_call(qfeat_dil, [dil[3 * i:3 * i + 3] for i in range(len(DIL_PATTERNS))])
        o_d, lse_d = [o for o, _ in dil_out], [lse for _, lse in dil_out]
        h = _merge_call(h, g_l, w_gate, o_nsa, o_d, lse_d, w_br_nsa[layer].astype(BF16),
                        w_br_dil[layer].astype(BF16), w_out[layer].astype(BF16),
                        final_g.reshape(1, D), tm, final=layer == w_in.shape[0] - 1)
    return h
```

```python
import functools
import math

import numpy as np
import jax
import jax.numpy as jnp
from jax import lax
from jax.experimental import pallas as pl
from jax.experimental.pallas import tpu as pltpu

F32 = jnp.float32
BF16 = jnp.bfloat16

HEAD_DIM = 64
LANES = 128
NSA_HEADS = 8
NSA_GROUPS = 2
NSA_HPG = NSA_HEADS // NSA_GROUPS
CMP_BLOCK = 32
CMP_STRIDE = 16
SLC_BLOCK = 64
SLC_TOPK = 16
WIN = 512
FORCE_SCORE = 1.0e4
DIL_HEADS = 8
DIL_PATTERNS = ((128, 1), (512, 4), (2048, 16))
Q_BLOCK = 128
RMS_EPS = 1e-6
KEY_CHUNK = 128
VT_ROWS = 80
PAIR_VT_ROWS = 144
GATE_ROWS = 16
FEAT_SPLIT = 3
FLAG_BITS = 16
SLC_GROUP = 4
PERM_ROWS = 128
WIN_QB = 256
NSA_QB = 512
DIL_BAND = 4
N_FORCED = 3
assert FORCE_SCORE > NSA_HPG + 1
SEL_PAD_ROWS = 8
ONES_ROWS = 16
LOG2E = math.log2(math.e)
VMEM_LIMIT = 56 * 1024 * 1024


def _alibi_slopes():
    n = NSA_HEADS + DIL_HEADS
    s = 2.0 ** (-8.0 * np.arange(1, n + 1) / n)
    return s[0::2].astype(np.float32), s[1::2].astype(np.float32)


def _nt(a, b):
    return lax.dot_general(a, b, (((1,), (1,)), ((), ())), preferred_element_type=F32)


def _rms(x, g):
    return (x * lax.rsqrt(jnp.mean(x * x, axis=-1, keepdims=True) + RMS_EPS)) * g


def _head_tile(q_pairs, h):
    tile = q_pairs[:, (h // 2) * LANES:(h // 2 + 1) * LANES]
    return pltpu.roll(tile, HEAD_DIM, 1) if h % 2 else tile


def _pos_features(pos):
    lane = lax.broadcasted_iota(jnp.int32, pos.shape, 1) - HEAD_DIM
    hi = (pos // LANES).astype(F32)
    lo = (pos % LANES).astype(F32)
    return jnp.where(lane < 0, 0.0, jnp.where(lane < FEAT_SPLIT, hi, jnp.where(lane < 2 * FEAT_SPLIT, lo, 0.0)))


def _slope_features(slopes):
    out = np.zeros((len(slopes), LANES), np.float32)
    for h, s in enumerate(slopes):
        rest = np.float32(s) * np.float32(LOG2E)
        for i in range(FEAT_SPLIT):
            piece = np.float32(np.asarray(rest, np.float32).astype(BF16))
            out[h, HEAD_DIM + FEAT_SPLIT + i] = piece
            out[h, HEAD_DIM + i] = piece * LANES
            rest = np.float32(rest - piece)
    return out


def _ones_rows(width, dtype):
    row = lax.broadcasted_iota(jnp.int32, (ONES_ROWS, width), 0)
    return jnp.where(row == 0, 1.0, 0.0).astype(dtype)


_PROJ_SEGS = (("qn", 512), ("cmp", 256), ("ks", 128), ("kw", 128), ("vs", 128), ("vw", 128),
              ("gate", 128), ("qd", 512), ("kd", 512), ("vd", 512))
DIL_Q_W = DIL_HEADS * HEAD_DIM
_PROJ_OFFS = dict(zip((n for n, _ in _PROJ_SEGS),
                      zip(np.cumsum([0] + [w for _, w in _PROJ_SEGS])[:-1].tolist(),
                          np.cumsum([w for _, w in _PROJ_SEGS]).tolist())))


def _to_lane_tiles(scr, first, val):
    for j in range(val.shape[1] // LANES):
        scr[first + j] = val[:, j * LANES:(j + 1) * LANES]


def _proj_kernel(x_ref, g_ref, w_ref, perm_ref, qn_ref, cmp_ref, ks_ref, kw_ref, vst_ref, vwt_ref,
                 gate_ref, *dil_refs):
    cmp_scr = dil_refs[-1]
    tm = x_ref.shape[0]
    xb = _rms(x_ref[...], g_ref[...]).astype(BF16)
    scale = HEAD_DIM ** -0.5

    def seg(name, last=None):
        c0, c1 = _PROJ_OFFS[name][0], _PROJ_OFFS[last or name][1]
        return jnp.dot(xb, w_ref[:, c0:c1], preferred_element_type=F32)

    qn_ref[...] = (seg("qn") * (scale * LOG2E)).astype(BF16)
    _to_lane_tiles(cmp_scr, 0, seg("cmp"))
    lane_c = lax.broadcasted_iota(jnp.int32, (tm // CMP_STRIDE, LANES), 1)
    slot_w = CMP_STRIDE * HEAD_DIM
    for t in range(cmp_scr.shape[0]):
        for p in range(0, CMP_STRIDE, 2):
            even, odd = (cmp_scr[t, pl.ds(p + i, tm // CMP_STRIDE, stride=CMP_STRIDE), :] for i in range(2))
            for half in range(2):
                pair = (jnp.where(lane_c < HEAD_DIM, even, pltpu.roll(odd, HEAD_DIM, 1)) if half == 0
                        else jnp.where(lane_c < HEAD_DIM, pltpu.roll(even, HEAD_DIM, 1), odd))
                c0 = (2 * t + half) * slot_w + p * HEAD_DIM
                cmp_ref[:, c0:c0 + LANES] = pair
    feat =_pos_features(pl.program_id(1) * tm + lax.broadcasted_iota(jnp.int32, (tm, LANES), 0))
    k_sw, v_sw = seg("ks", "kw"), seg("vs", "vw")
    ks, kw = k_sw[:, 0:LANES], k_sw[:, LANES:]
    vst, vwt = v_sw[:, 0:LANES].T, v_sw[:, LANES:].T
    gate_t = jax.nn.sigmoid(seg("gate")).T
    tail = _ones_rows(tm, F32)
    lane = lax.broadcasted_iota(jnp.int32, (tm, LANES), 1)
    for g in range(NSA_GROUPS):
        ks_ref[g] = jnp.where(lane < HEAD_DIM, _head_tile(ks, g), feat).astype(BF16)
        kw_ref[g] = jnp.where(lane < HEAD_DIM, _head_tile(kw, g), feat).astype(BF16)
        gate_ref[g] = gate_t[g * GATE_ROWS:(g + 1) * GATE_ROWS]
        for src, dst in ((vst, vst_ref), (vwt, vwt_ref)):
            vt = jnp.concatenate([src[g * HEAD_DIM:(g + 1) * HEAD_DIM], tail], axis=0).astype(BF16)
            for j in range(tm // KEY_CHUNK):
                dst[g, j] = vt[:, j * KEY_CHUNK:(j + 1) * KEY_CHUNK]
    qkv = jnp.concatenate([(seg("qd") * (scale * LOG2E)).astype(BF16), seg("kd").astype(BF16),
                           seg("vd").astype(BF16)], axis=1)
    bounds = (0, DIL_Q_W, 2 * DIL_Q_W, 3 * DIL_Q_W)
    subs = tm // PERM_ROWS
    for i, (_, d) in enumerate(DIL_PATTERNS):
        if d > 1:
            by_phase = [jnp.dot(perm_ref[i], qkv[s * PERM_ROWS:(s + 1) * PERM_ROWS],
                                preferred_element_type=F32) for s in range(subs)]
        r = PERM_ROWS // d
        for p in range(d):
            rows = qkv if d == 1 else jnp.concatenate(
                [by_phase[s][p * r:(p + 1) * r] for s in range(subs)], axis=0).astype(BF16)
            for j, ref in enumerate(dil_refs[3 * i:3 * i + 3]):
                width = bounds[j + 1] - bounds[j]
                ref[:, p * width:(p + 1) * width] = rows[:, bounds[j]:bounds[j + 1]]


def _proj_call(x, g, w, tm):
    B, S, D = x.shape
    G = NSA_GROUPS
    cmp_w = _PROJ_OFFS["cmp"][1] - _PROJ_OFFS["cmp"][0]
    row = lambda c: pl.BlockSpec((None, tm, c), lambda b, i: (b, i, 0))
    k_spec = pl.BlockSpec((None, G, tm, LANES), lambda b, i: (b, 0, i, 0))
    vt_spec = pl.BlockSpec((None, G, tm // KEY_CHUNK, VT_ROWS, KEY_CHUNK), lambda b, i: (b, 0, i, 0, 0))
    out_shape = (
        jax.ShapeDtypeStruct((B, S, 512), BF16),
        jax.ShapeDtypeStruct((B, S // CMP_STRIDE, CMP_STRIDE * cmp_w), F32),
        jax.ShapeDtypeStruct((B, G, S, LANES), BF16),
        jax.ShapeDtypeStruct((B, G, S, LANES), BF16),
        jax.ShapeDtypeStruct((B, G, S // KEY_CHUNK, VT_ROWS, KEY_CHUNK), BF16),
        jax.ShapeDtypeStruct((B, G, S // KEY_CHUNK, VT_ROWS, KEY_CHUNK), BF16),
        jax.ShapeDtypeStruct((B, G, GATE_ROWS, S), F32),
    )
    dil_specs = ()
    for _, d in DIL_PATTERNS:
        for width in (DIL_Q_W,) * 3:
            out_shape += (jax.ShapeDtypeStruct((B, S // d, d * width), BF16),)
            dil_specs += (pl.BlockSpec((None, tm // d, d * width), lambda b, i: (b, i, 0)),)
    r = np.arange(PERM_ROWS)
    perm = np.zeros((len(DIL_PATTERNS), PERM_ROWS, PERM_ROWS), np.float32)
    for i, (_, d) in enumerate(DIL_PATTERNS):
        assert (PERM_ROWS // d) % 8 == 0
        perm[i, (r % d) * (PERM_ROWS // d) + r // d, r] = 1.0
    perm = jnp.asarray(perm, BF16)
    return pl.pallas_call(
        _proj_kernel,
        grid=(B, S // tm),
        in_specs=[row(D), pl.BlockSpec((1, D), lambda b, i: (0, 0)),
                  pl.BlockSpec(w.shape, lambda b, i: (0, 0)),
                  pl.BlockSpec(perm.shape, lambda b, i: (0, 0, 0))],
        out_specs=(row(512),
                   pl.BlockSpec((None, tm // CMP_STRIDE, CMP_STRIDE * cmp_w), lambda b, i: (b, i, 0)),
                   k_spec, k_spec, vt_spec, vt_spec,
                   pl.BlockSpec((None, G, GATE_ROWS, tm), lambda b, i: (b, 0, 0, i))) + dil_specs,
        out_shape=out_shape,
        scratch_shapes=[pltpu.VMEM((cmp_w // LANES, tm, LANES), F32)],
        compiler_params=pltpu.CompilerParams(
            dimension_semantics=("parallel", "parallel"), vmem_limit_bytes=VMEM_LIMIT),
        name="proj",
    )(x, g, w, perm)


def _compress_kernel(r_ref, posa_ref, posb_ref, wa_ref, wb_ref, w2_ref, ovt_ref, kc_ref, cvo_ref):
    n = r_ref.shape[0]
    slot_w = CMP_STRIDE * HEAD_DIM
    outs = []
    for s in range(2 * NSA_GROUPS):
        r = r_ref[:, s * slot_w:(s + 1) * slot_w]
        ha = jnp.dot((r + posa_ref[s]).astype(BF16), wa_ref[s], preferred_element_type=F32)
        hb = jnp.dot((r + posb_ref[s]).astype(BF16), wb_ref[s], preferred_element_type=F32)
        hid = jax.nn.gelu(ha + pltpu.roll(hb, n - 1, 0))
        outs.append(jnp.dot(hid.astype(BF16), w2_ref[s], preferred_element_type=F32))
    cmp_end = lax.broadcasted_iota(jnp.int32, (n, LANES), 0) * CMP_STRIDE + (CMP_BLOCK - 1)
    feat = _pos_features(cmp_end)
    zeros = jnp.zeros((n, LANES - HEAD_DIM), F32)
    for g in range(NSA_GROUPS):
        kc_ref[g] = jnp.concatenate([outs[g], feat[:, HEAD_DIM:]], axis=1).astype(BF16)
        v_t = jnp.concatenate([outs[NSA_GROUPS + g], zeros], axis=1).T[0:HEAD_DIM]
        cvo_ref[g] = jnp.concatenate([v_t, ovt_ref[...], _ones_rows(n, F32)], axis=0).astype(BF16)


def _compress_call(r, posa, posb, wa, wb, w2, ovt):
    B, R, C = r.shape
    G = NSA_GROUPS
    rows = HEAD_DIM + ovt.shape[0] + ONES_ROWS
    full = lambda a: pl.BlockSpec(a.shape, lambda b: (0,) * a.ndim)
    return pl.pallas_call(
        _compress_kernel,
        grid=(B,),
        in_specs=[pl.BlockSpec((None, R, C), lambda b: (b, 0, 0)),
                  full(posa), full(posb), full(wa), full(wb), full(w2), full(ovt)],
        out_specs=(pl.BlockSpec((None, G, R, LANES), lambda b: (b, 0, 0, 0)),
                   pl.BlockSpec((None, G, rows, R), lambda b: (b, 0, 0, 0))),
        out_shape=(jax.ShapeDtypeStruct((B, G, R, LANES), BF16),
                   jax.ShapeDtypeStruct((B, G, rows, R), BF16)),
        compiler_params=pltpu.CompilerParams(
            dimension_semantics=("parallel",), vmem_limit_bytes=VMEM_LIMIT),
        name="compress",
    )(r, posa, posb, wa, wb, w2, ovt)


def _flash_step(s, bias, v_t, m_ref, acc_ref):
    ps, alphas = [], []
    qb = bias.shape[1]
    for h in range(NSA_HPG):
        cols = slice(h * qb, (h + 1) * qb)
        sh = s[:, cols] + bias
        m_old = m_ref[:, cols]
        m_new = jnp.maximum(m_old, jnp.max(sh, axis=0, keepdims=True))
        alphas.append(jnp.exp2(m_old - m_new))
        ps.append(jnp.exp2(sh - m_new).astype(BF16))
        m_ref[:, cols] = m_new
    pv = jnp.dot(v_t, jnp.concatenate(ps, axis=1), preferred_element_type=F32)
    acc_ref[...] = jnp.concatenate(alphas, axis=1) * acc_ref[...] + pv


def _softmax_pv(s, bias, v_t):
    ms, ps = [], []
    qb = bias.shape[1]
    for h in range(NSA_HPG):
        sh = s[:, h * qb:(h + 1) * qb] + bias
        ms.append(jnp.max(sh, axis=0, keepdims=True))
        ps.append(jnp.exp2(sh - ms[h]).astype(BF16))
    r = jnp.dot(v_t, jnp.concatenate(ps, axis=1), preferred_element_type=F32)
    return jnp.concatenate(ms, axis=1), r


def _normalise(acc):
    return acc[0:HEAD_DIM] * (1.0 / jnp.maximum(acc[HEAD_DIM:HEAD_DIM + 1], 1e-30))


def _nsa_kernel(qfeat_ref, q_ref, kc_ref, cvo_ref, ks_ref, vst_ref, kw_ref, vwt_ref, gate_ref,
                o_ref, sel_ref, ms_ref, accs_ref, flag_ref, list_ref):
    n = pl.program_id(2)
    qb, kc = NSA_QB, KEY_CHUNK
    t0 = n * qb
    lane = lax.broadcasted_iota(jnp.int32, (qb, LANES), 1)
    qf = q_ref[...].astype(F32)
    q = jnp.concatenate(
        [jnp.where(lane < HEAD_DIM, _head_tile(qf, h), qfeat_ref[h][0:1, :]) for h in range(NSA_HPG)],
        axis=0).astype(BF16)

    wq = min(WIN_QB, qb)
    span = WIN + wq
    back_w = (lax.broadcasted_iota(jnp.int32, (span, wq), 1)
              - lax.broadcasted_iota(jnp.int32, (span, wq), 0))
    win_parts = []
    for j in range(qb // wq):
        k_lo = pl.multiple_of(jnp.maximum(t0 + j * wq - WIN, 0), kc)
        dist = back_w + (t0 + j * wq - k_lo)
        bias = jnp.where(dist >= 0, jnp.where(dist < WIN, 0.0, -jnp.inf), -jnp.inf)
        q_part = jnp.concatenate([q[h * qb + j * wq:h * qb + (j + 1) * wq] for h in range(NSA_HPG)], axis=0)
        vt_w = jnp.concatenate([vwt_ref[k_lo // kc + u] for u in range(span // kc)], axis=1)
        win_parts.append(_normalise(_softmax_pv(_nt(kw_ref[pl.ds(k_lo, span), :], q_part), bias, vt_w)[1]))
    o_win = jnp.concatenate([part[:, h * wq:(h + 1) * wq] for h in range(NSA_HPG) for part in win_parts],
                            axis=1)

    n_cmp = kc_ref.shape[0]
    n_sel = sel_ref.shape[0] - SEL_PAD_ROWS
    s = _nt(kc_ref[...], q)
    cmp_end = lax.broadcasted_iota(jnp.int32, (n_cmp, qb), 0) * CMP_STRIDE + (CMP_BLOCK - 1)
    bias_c = jnp.where(cmp_end <= t0 + lax.broadcasted_iota(jnp.int32, (n_cmp, qb), 1), 0.0, -jnp.inf)
    es = []
    for h in range(NSA_HPG):
        sh = s[:, h * qb:(h + 1) * qb] + bias_c
        m = jnp.max(sh, axis=0, keepdims=True)
        es.append(jnp.exp2(sh - jnp.where(m == -jnp.inf, 0.0, m)).astype(BF16))
    r = jnp.dot(cvo_ref[...], jnp.concatenate(es, axis=1), preferred_element_type=F32)
    inv_l = 1.0 / jnp.maximum(r[HEAD_DIM + n_sel:HEAD_DIM + n_sel + 1], 1e-30)
    o_cmp = r[0:HEAD_DIM] * inv_l
    imp = r[HEAD_DIM:HEAD_DIM + n_sel] * inv_l
    imp_t = imp[:, 0:qb]
    for h in range(1, NSA_HPG):
        imp_t = imp_t + imp[:, h * qb:(h + 1) * qb]

    blk = lax.broadcasted_iota(jnp.int32, (n_sel, qb), 0).astype(F32)
    cur = ((t0 + lax.broadcasted_iota(jnp.int32, (n_sel, qb), 1)) // SLC_BLOCK).astype(F32)
    ago = cur - blk
    forced = jnp.where(blk == 0, 1.0, jnp.where(ago == 0, 1.0, jnp.where(ago == 1, 1.0, 0.0)))
    valid = ago >= 0
    score = jnp.where(valid, jnp.where(forced > 0.5, -1.0, imp_t), -1.0)
    for _ in range(max(min(SLC_TOPK, n_sel) - N_FORCED, 0)):
        mx = jnp.max(score, axis=0, keepdims=True)
        idx = jnp.min(jnp.where(score == mx, blk, float(n_sel)), axis=0, keepdims=True)
        score = jnp.where(blk == idx, -jnp.inf, score)
    sel_t = jnp.where(valid, jnp.where(score == -jnp.inf, 1.0, forced), 0.0)
    sel_ref[0:n_sel, :] = jnp.where(sel_t > 0.5, 0.0, -jnp.inf)
    sel_ref[n_sel:, :] = jnp.full((SEL_PAD_ROWS, qb), -jnp.inf, F32)
    any_q = jnp.max(sel_t, axis=1, keepdims=True)
    bit = jnp.left_shift(1, lax.broadcasted_iota(jnp.int32, (n_sel, 1), 0) % FLAG_BITS).astype(F32)
    packed = any_q * bit
    for i in range(n_sel // FLAG_BITS):
        word = jnp.sum(packed[i * FLAG_BITS:(i + 1) * FLAG_BITS], axis=0, keepdims=True)
        flag_ref[i] = word.astype(jnp.int32)[0, 0]

    blocks_per_chunk = kc // SLC_BLOCK
    chunks_per_word = FLAG_BITS // blocks_per_chunk
    row = lax.broadcasted_iota(jnp.int32, (kc, qb), 0)

    def chunk_bias(first_block):
        bias = sel_ref[pl.ds(first_block, 1), :]
        for j in range(1, blocks_per_chunk):
            bias = jnp.where(row < j * SLC_BLOCK, bias, sel_ref[pl.ds(first_block + j, 1), :])
        return bias

    own = qb // kc
    first_own = t0 // kc
    off_first = jnp.where(n == 0, -jnp.inf, 0.0)
    query = lax.broadcasted_iota(jnp.int32, (kc, qb), 1)
    biases = [chunk_bias(0) + off_first]
    for u in range(own):
        causal = jnp.where(row + u * kc <= query, 0.0, -jnp.inf)
        biases.append(chunk_bias((first_own + u) * blocks_per_chunk) + causal)
    keys = jnp.concatenate([ks_ref[0:kc, :], ks_ref[pl.ds(pl.multiple_of(t0, kc), qb), :]], axis=0)
    vt_s = jnp.concatenate([vst_ref[0]] + [vst_ref[first_own + u] for u in range(own)], axis=1)
    m_s, acc_s = _softmax_pv(_nt(keys, q), jnp.concatenate(biases, axis=0), vt_s)
    ms_ref[...] = m_s
    accs_ref[...] = acc_s

    def scan_body(w, cnt):
        word = flag_ref[w]
        for j in range(chunks_per_word):
            c = w * chunks_per_word + j
            bits = (word >> (j * blocks_per_chunk)) & ((1 << blocks_per_chunk) - 1)
            list_ref[cnt] = c
            cnt = cnt + ((bits != 0) & (c >= 1) & (c < first_own)).astype(jnp.int32)
        return cnt

    cnt = lax.fori_loop(0, (first_own + chunks_per_word - 1) // chunks_per_word, scan_body, 0)
    for u in range(SLC_GROUP):
        list_ref[cnt + u] = -1

    def slc_body(gi, carry):
        keys, biases, vts = [], [], []
        for u in range(SLC_GROUP):
            c = list_ref[gi * SLC_GROUP + u]
            cc = jnp.maximum(c, 0)
            biases.append(chunk_bias(jnp.where(c >= 0, cc * blocks_per_chunk, n_sel)))
            keys.append(ks_ref[pl.ds(pl.multiple_of(cc * kc, kc), kc), :])
            vts.append(vst_ref[cc])
        _flash_step(_nt(jnp.concatenate(keys, axis=0), q), jnp.concatenate(biases, axis=0),
                    jnp.concatenate(vts, axis=1), ms_ref, accs_ref)
        return carry

    lax.fori_loop(0, (cnt + SLC_GROUP - 1) // SLC_GROUP, slc_body, 0)

    o_slc = _normalise(accs_ref[...])
    gate = gate_ref[...]
    outs = []
    for h in range(NSA_HPG):
        cols = slice(h * qb, (h + 1) * qb)
        gc = [gate[j * NSA_HPG + h:j * NSA_HPG + h + 1, :] for j in range(3)]
        outs.append(gc[0] * o_cmp[:, cols] + gc[1] * o_slc[:, cols] + gc[2] * o_win[:, cols])
    o_ref[...] = jnp.concatenate(outs, axis=0).T


def _nsa_call(qfeat, qn, kc, cvo, ks, vst, kw, vwt, gates):
    B, S, _ = qn.shape
    G = NSA_GROUPS
    qb = NSA_QB
    nb = S // qb
    n_sel = S // SLC_BLOCK
    width = NSA_HPG * qb
    per_group = lambda a: pl.BlockSpec((None, None) + a.shape[2:],
                                       lambda b, g, n: (b, g) + (0,) * (a.ndim - 2))
    return pl.pallas_call(
        _nsa_kernel,
        grid=(B, G, nb),
        in_specs=[
            pl.BlockSpec((None,) + qfeat.shape[1:], lambda b, g, n: (g, 0, 0, 0)),
            pl.BlockSpec((None, qb, NSA_HPG * HEAD_DIM), lambda b, g, n: (b, n, g)),
            per_group(kc), per_group(cvo), per_group(ks), per_group(vst), per_group(kw), per_group(vwt),
            pl.BlockSpec((None, None, GATE_ROWS, qb), lambda b, g, n: (b, g, 0, n)),
        ],
        out_specs=pl.BlockSpec((None, qb, NSA_HPG * HEAD_DIM), lambda b, g, n: (b, n, g)),
        out_shape=jax.ShapeDtypeStruct((B, S, NSA_HEADS * HEAD_DIM), F32),
        scratch_shapes=[pltpu.VMEM((n_sel + SEL_PAD_ROWS, qb), F32),
                        pltpu.VMEM((1, width), F32), pltpu.VMEM((VT_ROWS, width), F32),
                        pltpu.SMEM((n_sel // FLAG_BITS,), jnp.int32),
                        pltpu.SMEM((S // KEY_CHUNK + SLC_GROUP,), jnp.int32)],
        compiler_params=pltpu.CompilerParams(
            dimension_semantics=("parallel", "parallel", "arbitrary"), vmem_limit_bytes=VMEM_LIMIT),
        name="nsa",
    )(qfeat, qn, kc, cvo, ks, vst, kw, vwt, gates)


def _dilated_kernel(qfeat_ref, *refs):
    n_pat = len(DIL_PATTERNS)
    j = pl.program_id(1)
    for i, (window, dilation) in enumerate(DIL_PATTERNS):
        steps_per_phase = max(d for _, d in DIL_PATTERNS) // dilation
        _dilated_blocks(qfeat_ref, *refs[5 * i:5 * i + 5], *refs[5 * n_pat + 2 * i:5 * n_pat + 2 * i + 2],
                        wd=window // dilation, dilation=dilation,
                        phase=j // steps_per_phase, step=j % steps_per_phase)


def _dilated_blocks(qfeat_ref, q_ref, kp_ref, kc_ref, vp_ref, vc_ref, o_ref, lse_ref, *,
                    wd, dilation, phase, step):
    c = Q_BLOCK
    hd = HEAD_DIM
    blocks = q_ref.shape[0] // c
    lane = lax.broadcasted_iota(jnp.int32, (blocks * c, LANES), 1)
    key = lax.broadcasted_iota(jnp.int32, (2 * c, c), 0)
    qry = lax.broadcasted_iota(jnp.int32, (2 * c, c), 1)
    first = step == 0
    off_later = jnp.where(key < c, qry - (key + (wd - c)), (key - c) - qry)
    off_first = jnp.where(key < c, jnp.where(first, 2 * c, qry) - (key + (wd - c)), (key - c) - qry)
    bias_later = jnp.where(off_later <= 0, 0.0, -jnp.inf)
    bias_first = jnp.where(off_first <= 0, 0.0, -jnp.inf)
    sub = (step * blocks - 1) * c + lax.broadcasted_iota(jnp.int32, ((blocks + 1) * c, LANES), 0)
    pos_feat = _pos_features(sub * dilation + phase).astype(BF16)
    n_keys = (blocks + 1) * c
    tail = _ones_rows(n_keys, BF16)
    assert PAIR_VT_ROWS == 2 * hd + ONES_ROWS
    for p in range(DIL_HEADS // 2):
        cols = slice(p * LANES, (p + 1) * LANES)
        q_pair = q_ref[:, cols]
        zero = jnp.zeros_like(q_pair)
        feats = [jnp.broadcast_to(qfeat_ref[2 * p + i][0:1, :], (blocks * c, LANES)).astype(BF16)
                 for i in range(2)]
        q_a = jnp.concatenate([jnp.where(lane < hd, q_pair, zero), feats[0]], axis=1)
        q_b = jnp.concatenate([jnp.where(lane < hd, zero, q_pair), feats[1]], axis=1)
        k_all = jnp.concatenate([kp_ref[:, cols], kc_ref[:, cols]], axis=0)
        k_all = jnp.concatenate([k_all, pos_feat], axis=1)
        v_t = jnp.concatenate([jnp.concatenate([vp_ref[:, cols].T, vc_ref[:, cols].T], axis=1), tail],
                              axis=0)
        nb = min(DIL_BAND, blocks)
        for g0 in range(0, blocks, nb):
            q_aug = jnp.concatenate([part[j * c:(j + 1) * c] for j in range(g0, g0 + nb)
                                     for part in (q_a, q_b)], axis=0)
            s_all = _nt(k_all[g0 * c:(g0 + nb + 1) * c], q_aug)
            ms, band = [], []
            for j in range(nb):
                bias = bias_later if g0 + j else bias_first
                probs = []
                for i in range(2):
                    si = s_all[j * c:(j + 2) * c, (2 * j + i) * c:(2 * j + i + 1) * c] + bias
                    ms.append(jnp.max(si, axis=0, keepdims=True))
                    probs.append(jnp.exp2(si - ms[-1]).astype(BF16))
                pieces = [jnp.zeros((j * c, 2 * c), BF16)] if j else []
                pieces.append(jnp.concatenate(probs, axis=1))
                if j < nb - 1:
                    pieces.append(jnp.zeros(((nb - 1 - j) * c, 2 * c), BF16))
                band.append(jnp.concatenate(pieces, axis=0))
            r_all = jnp.dot(v_t[:, g0 * c:(g0 + nb + 1) * c], jnp.concatenate(band, axis=1),
                            preferred_element_type=F32)
            for j in range(nb):
                outs, lses = [], []
                for i in range(2):
                    at = slice((2 * j + i) * c, (2 * j + i + 1) * c)
                    l = r_all[2 * hd:2 * hd + 1, at]
                    outs.append(r_all[i * hd:(i + 1) * hd, at] * (1.0 / l))
                    lses.append(jnp.broadcast_to(ms[2 * j + i] + jnp.log2(l), (hd, c)))
                rows = slice((g0 + j) * c, (g0 + j + 1) * c)
                o_ref[rows, cols] = jnp.concatenate(outs, axis=0).T
                lse_ref[rows, cols] = jnp.concatenate(lses, axis=0).T


def _dilated_call(qfeat, qkv_by_pattern):
    B = qkv_by_pattern[0][0].shape[0]
    steps = max(d for _, d in DIL_PATTERNS)
    in_specs = [pl.BlockSpec(qfeat.shape, lambda b, j: (0, 0, 0))]
    out_specs, out_shape, operands = [], [], [qfeat]
    for (window, d), (q_v, k_v, v_v) in zip(DIL_PATTERNS, qkv_by_pattern):
        assert Q_BLOCK - 1 <= window // d <= Q_BLOCK and steps % d == 0
        L = q_v.shape[1]
        spp = steps // d
        per_step = L // (Q_BLOCK * spp)
        blk = pl.BlockSpec((None, Q_BLOCK * per_step, DIL_Q_W),
                           lambda b, j, spp=spp: (b, j % spp, j // spp))
        prev = pl.BlockSpec((None, Q_BLOCK, DIL_Q_W),
                            lambda b, j, spp=spp, per_step=per_step:
                            (b, jnp.maximum((j % spp) * per_step - 1, 0), j // spp))
        in_specs += [blk, prev, blk, prev, blk]
        operands += [q_v, k_v, k_v, v_v, v_v]
        out_specs += [blk, blk]
        out_shape += [jax.ShapeDtypeStruct((B, L, d * DIL_Q_W), F32)] * 2
    outs = pl.pallas_call(
        _dilated_kernel,
        grid=(B, steps),
        in_specs=in_specs,
        out_specs=tuple(out_specs),
        out_shape=tuple(out_shape),
        compiler_params=pltpu.CompilerParams(
            dimension_semantics=("parallel", "arbitrary"), vmem_limit_bytes=VMEM_LIMIT),
        name="dilated",
    )(*operands)
    return [(outs[2 * i], outs[2 * i + 1]) for i in range(len(DIL_PATTERNS))]


def _merge_kernel(x_ref, g_ref, wg_ref, onsa_ref, o1_ref, o2_ref, o3_ref, l1_ref, l2_ref, l3_ref,
                  wbn_ref, wbd_ref, wo_ref, fg_ref, out_ref, tok_scr, *, final):
    x = x_ref[...]
    tm = x.shape[0]
    xb = _rms(x, g_ref[...]).astype(BF16)
    cz = NSA_HEADS * HEAD_DIM
    d = x.shape[-1]

    def gate_seg(c0, c1):
        return jnp.dot(xb, wg_ref[:, c0:c1], preferred_element_type=F32)

    def token_major(ref, slot, dil):
        if dil == 1:
            return ref[...]
        tiles = DIL_Q_W // LANES
        for p in range(dil):
            for j in range(tiles):
                tok_scr[slot * tiles + j, pl.ds(p, tm // dil, stride=dil), :] = \
                    ref[:, p * DIL_Q_W + j * LANES:p * DIL_Q_W + (j + 1) * LANES]
        return jnp.concatenate([tok_scr[slot * tiles + j] for j in range(tiles)], axis=1)

    o_nsa = onsa_ref[...] * jax.nn.silu(gate_seg(0, cz))
    dils = [dil for _, dil in DIL_PATTERNS]
    o1, o2, o3 = (token_major(r, i, dil) for i, (r, dil) in enumerate(zip((o1_ref, o2_ref, o3_ref), dils)))
    l1, l2, l3 = (token_major(r, 3 + i, dil) for i, (r, dil) in enumerate(zip((l1_ref, l2_ref, l3_ref), dils)))
    mx = jnp.maximum(jnp.maximum(l1, l2), l3)
    e1, e2, e3 = jnp.exp2(l1 - mx), jnp.exp2(l2 - mx), jnp.exp2(l3 - mx)
    inv = 1.0 / (e1 + e2 + e3)
    o_dil = (e1 * inv) * o1 + (e2 * inv) * o2 + (e3 * inv) * o3
    o_dil = o_dil * jax.nn.silu(gate_seg(cz, 2 * cz))
    a = jnp.dot(o_nsa.astype(BF16), wbn_ref[...], preferred_element_type=F32)
    bd = jnp.dot(o_dil.astype(BF16), wbd_ref[...], preferred_element_type=F32)
    merged = (jax.nn.sigmoid(gate_seg(2 * cz, 2 * cz + d)) * a
              + jax.nn.sigmoid(gate_seg(2 * cz + d, 2 * cz + 2 * d)) * bd)
    y = jnp.dot(merged.astype(BF16), wo_ref[...], preferred_element_type=F32)
    out_ref[...] = _rms(x + y, fg_ref[...]) if final else x + y


def _merge_call(x, g, wg, o_nsa, o_d, lse_d, wbn, wbd, wo, fg, tm, final):
    B, S, D = x.shape
    cz = o_nsa.shape[-1]
    row = lambda c: pl.BlockSpec((None, tm, c), lambda b, i: (b, i, 0))
    full = lambda a: pl.BlockSpec(a.shape, lambda b, i: (0,) * a.ndim)
    phase = [pl.BlockSpec((None, tm // dil, dil * cz), lambda b, i: (b, i, 0)) for _, dil in DIL_PATTERNS]
    return pl.pallas_call(
        functools.partial(_merge_kernel, final=final),
        grid=(B, S // tm),
        in_specs=[row(D), full(g), full(wg), row(cz)] + phase + phase
                 + [full(wbn), full(wbd), full(wo), full(fg)],
        out_specs=row(D),
        out_shape=jax.ShapeDtypeStruct((B, S, D), F32),
        scratch_shapes=[pltpu.VMEM((2 * len(DIL_PATTERNS) * cz // LANES, tm, LANES), F32)],
        compiler_params=pltpu.CompilerParams(
            dimension_semantics=("parallel", "parallel"), vmem_limit_bytes=VMEM_LIMIT),
        name="merge",
    )(x, g, wg, o_nsa, *o_d, *lse_d, wbn, wbd, wo, fg)


def _proj_columns(d_model):
    hd, G, H = HEAD_DIM, NSA_GROUPS, NSA_HEADS
    nsa_w = H * hd
    o_q, o_kv = 0, nsa_w
    o_g = o_kv + 6 * G * hd
    o_zn = o_g + 3 * H
    o_qkvd = o_zn + nsa_w
    dil_w = DIL_HEADS * hd
    o_zd = o_qkvd + 3 * dil_w
    o_mg = o_zd + dil_w
    kv = lambda j, g: o_kv + (j * G + g) * hd + np.arange(hd)
    cols = [o_q + np.arange(nsa_w)]
    cols += [kv(j, g) for j in (0, 1) for g in range(G)]
    for j in (2, 4, 3, 5):
        cols += [kv(j, g) for g in range(G)]
    lanes = np.full(LANES, -1)
    for g in range(G):
        for j in range(3):
            lanes[g * GATE_ROWS + j * NSA_HPG + np.arange(NSA_HPG)] = o_g + j * H + g * NSA_HPG + np.arange(NSA_HPG)
    cols.append(lanes)
    cols.append(o_qkvd + np.arange(3 * dil_w))
    proj_cols = np.concatenate(cols)
    assert proj_cols.size == _PROJ_OFFS["vd"][1]
    gate_cols = np.concatenate([o_zn + np.arange(nsa_w), o_zd + np.arange(dil_w),
                                o_mg + np.arange(2 * d_model)])
    return proj_cols, gate_cols


def _take_columns(w, cols, dtype):
    pieces, start = [], 0
    for i in range(1, len(cols) + 1):
        if i < len(cols):
            same_run = (cols[i] < 0 and cols[i - 1] < 0) or (cols[i - 1] >= 0 and cols[i] == cols[i - 1] + 1)
        if i == len(cols) or not same_run:
            n = i - start
            pieces.append(jnp.zeros((w.shape[0], n), dtype) if cols[start] < 0
                          else w[:, int(cols[start]):int(cols[start]) + n].astype(dtype))
            start = i
    return jnp.concatenate(pieces, axis=1)


def _compress_weights(pos_k, w1_k, w2_k, pos_v, w1_v, w2_v):
    hd, G, half = HEAD_DIM, NSA_GROUPS, CMP_BLOCK // 2
    w1 = jnp.stack([w1_k] * G + [w1_v] * G).astype(BF16)
    pos = jnp.stack([pos_k] * G + [pos_v] * G)
    w2 = jnp.stack([w2_k] * G + [w2_v] * G).astype(BF16)
    halves = lambda a, tail: (a[:, :half].reshape((2 * G,) + tail), a[:, half:].reshape((2 * G,) + tail))
    posa, posb = halves(pos, (1, half * hd))
    wa, wb = halves(w1, (half * hd, hd))
    return posa, posb, wa, wb, w2


def _overlap_t(n_cmp_pad, n_sel):
    i = np.arange(n_cmp_pad)[None, :]
    j = np.arange(n_sel)[:, None]
    lo = np.maximum(i * CMP_STRIDE, j * SLC_BLOCK)
    hi = np.minimum(i * CMP_STRIDE + CMP_BLOCK, (j + 1) * SLC_BLOCK)
    return (np.clip(hi - lo, 0, None) / CMP_BLOCK).astype(np.float32)


def kernel(x, norm_g, w_in, cmp_pos_k, cmp_w1_k, cmp_w2_k, cmp_pos_v, cmp_w1_v, cmp_w2_v,
           w_br_nsa, w_br_dil, w_out, final_g):
    B, S, D = x.shape
    n_sel = S // SLC_BLOCK
    assert S % (Q_BLOCK * max(d for _, d in DIL_PATTERNS)) == 0 and n_sel % FLAG_BITS == 0
    n_cmp_pad = S // CMP_STRIDE
    slopes_nsa, slopes_dil = _alibi_slopes()
    proj_cols, gate_cols = _proj_columns(D)
    ovt = jnp.asarray(_overlap_t(n_cmp_pad, n_sel))
    qfeat = jnp.asarray(np.broadcast_to(
        _slope_features(slopes_nsa).reshape(NSA_GROUPS, NSA_HPG, 1, LANES),
        (NSA_GROUPS, NSA_HPG, 8, LANES)))
    qfeat_dil = jnp.asarray(np.broadcast_to(
        _slope_features(slopes_dil).reshape(DIL_HEADS, 1, LANES), (DIL_HEADS, 8, LANES)))
    tm = 512

    h = x
    for layer in range(w_in.shape[0]):
        w_proj = _take_columns(w_in[layer], proj_cols, BF16)
        w_gate = _take_columns(w_in[layer], gate_cols, BF16)
        g_l = norm_g[layer].reshape(1, D)

        qn, cmp_src, ks, kw, vst, vwt, gates, *dil = _proj_call(h, g_l, w_proj, tm)
        cw = _compress_weights(cmp_pos_k[layer], cmp_w1_k[layer], cmp_w2_k[layer],
                               cmp_pos_v[layer], cmp_w1_v[layer], cmp_w2_v[layer])
        kc, cvo = _compress_call(cmp_src, *cw, ovt)
        o_nsa = _nsa_call(qfeat, qn, kc, cvo, ks, vst, kw, vwt, gates)
        dil_out = _dilated_call(qfeat_dil, [dil[3 * i:3 * i + 3] for i in range(len(DIL_PATTERNS))])
        o_d, lse_d = [o for o, _ in dil_out], [lse for _, lse in dil_out]
        h = _merge_call(h, g_l, w_gate, o_nsa, o_d, lse_d, w_br_nsa[layer].astype(BF16),
                        w_br_dil[layer].astype(BF16), w_out[layer].astype(BF16),
                        final_g.reshape(1, D), tm, final=layer == w_in.shape[0] - 1)
    return h
```

```python
import functools
import math

import numpy as np
import jax
import jax.numpy as jnp
from jax import lax
from jax.experimental import pallas as pl
from jax.experimental.pallas import tpu as pltpu

F32 = jnp.float32
BF16 = jnp.bfloat16

HEAD_DIM = 64
LANES = 128
NSA_HEADS = 8
NSA_GROUPS = 2
NSA_HPG = NSA_HEADS // NSA_GROUPS
CMP_BLOCK = 32
CMP_STRIDE = 16
SLC_BLOCK = 64
SLC_TOPK = 16
WIN = 512
FORCE_SCORE = 1.0e4
DIL_HEADS = 8
DIL_PATTERNS = ((128, 1), (512, 4), (2048, 16))
Q_BLOCK = 128
RMS_EPS = 1e-6
KEY_CHUNK = 128
VT_ROWS = 80
PAIR_VT_ROWS = 144
GATE_ROWS = 16
FEAT_SPLIT = 3
FLAG_BITS = 16
SLC_GROUP = 4
PERM_ROWS = 128
WIN_QB = 256
NSA_QB = 512
DIL_BAND = 4
N_FORCED = 3
assert FORCE_SCORE > NSA_HPG + 1
SEL_PAD_ROWS = 8
ONES_ROWS = 16
LOG2E = math.log2(math.e)
VMEM_LIMIT = 56 * 1024 * 1024


def _alibi_slopes():
    n = NSA_HEADS + DIL_HEADS
    s = 2.0 ** (-8.0 * np.arange(1, n + 1) / n)
    return s[0::2].astype(np.float32), s[1::2].astype(np.float32)


def _nt(a, b):
    return lax.dot_general(a, b, (((1,), (1,)), ((), ())), preferred_element_type=F32)


def _rms(x, g):
    return (x * lax.rsqrt(jnp.mean(x * x, axis=-1, keepdims=True) + RMS_EPS)) * g


def _head_tile(q_pairs, h):
    tile = q_pairs[:, (h // 2) * LANES:(h // 2 + 1) * LANES]
    return pltpu.roll(tile, HEAD_DIM, 1) if h % 2 else tile


def _pos_features(pos):
    lane = lax.broadcasted_iota(jnp.int32, pos.shape, 1) - HEAD_DIM
    hi = (pos // LANES).astype(F32)
    lo = (pos % LANES).astype(F32)
    return jnp.where(lane < 0, 0.0, jnp.where(lane < FEAT_SPLIT, hi, jnp.where(lane < 2 * FEAT_SPLIT, lo, 0.0)))


def _slope_features(slopes):
    out = np.zeros((len(slopes), LANES), np.float32)
    for h, s in enumerate(slopes):
        rest = np.float32(s) * np.float32(LOG2E)
        for i in range(FEAT_SPLIT):
            piece = np.float32(np.asarray(rest, np.float32).astype(BF16))
            out[h, HEAD_DIM + FEAT_SPLIT + i] = piece
            out[h, HEAD_DIM + i] = piece * LANES
            rest = np.float32(rest - piece)
    return out


def _ones_rows(width, dtype):
    row = lax.broadcasted_iota(jnp.int32, (ONES_ROWS, width), 0)
    return jnp.where(row == 0, 1.0, 0.0).astype(dtype)


_PROJ_SEGS = (("qn", 512), ("cmp", 256), ("ks", 128), ("kw", 128), ("vs", 128), ("vw", 128),
              ("gate", 128), ("qd", 512), ("kd", 512), ("vd", 512))
DIL_Q_W = DIL_HEADS * HEAD_DIM
_PROJ_OFFS = dict(zip((n for n, _ in _PROJ_SEGS),
                      zip(np.cumsum([0] + [w for _, w in _PROJ_SEGS])[:-1].tolist(),
                          np.cumsum([w for _, w in _PROJ_SEGS]).tolist())))


def _to_lane_tiles(scr, first, val):
    for j in range(val.shape[1] // LANES):
        scr[first + j] = val[:, j * LANES:(j + 1) * LANES]


def _proj_kernel(x_ref, g_ref, w_ref, perm_ref, qn_ref, cmp_ref, ks_ref, kw_ref, vst_ref, vwt_ref,
                 gate_ref, *dil_refs):
    cmp_scr = dil_refs[-1]
    tm = x_ref.shape[0]
    xb = _rms(x_ref[...], g_ref[...]).astype(BF16)
    scale = HEAD_DIM ** -0.5

    def seg(name, last=None):
        c0, c1 = _PROJ_OFFS[name][0], _PROJ_OFFS[last or name][1]
        return jnp.dot(xb, w_ref[:, c0:c1], preferred_element_type=F32)

    qn_ref[...] = (seg("qn") * (scale * LOG2E)).astype(BF16)
    _to_lane_tiles(cmp_scr, 0, seg("cmp"))
    lane_c = lax.broadcasted_iota(jnp.int32, (tm // CMP_STRIDE, LANES), 1)
    slot_w = CMP_STRIDE * HEAD_DIM
    for t in range(cmp_scr.shape[0]):
        for p in range(0, CMP_STRIDE, 2):
            even, odd = (cmp_scr[t, pl.ds(p + i, tm // CMP_STRIDE, stride=CMP_STRIDE), :] for i in range(2))
            for half in range(2):
                pair = (jnp.where(lane_c < HEAD_DIM, even, pltpu.roll(odd, HEAD_DIM, 1)) if half == 0
                        else jnp.where(lane_c < HEAD_DIM, pltpu.roll(even, HEAD_DIM, 1), odd))
                c0 = (2 * t + half) * slot_w + p * HEAD_DIM
                cmp_ref[:, c0:c0 + LANES] = pair
    feat =_pos_features(pl.program_id(1) * tm + lax.broadcasted_iota(jnp.int32, (tm, LANES), 0))
    k_sw, v_sw = seg("ks", "kw"), seg("vs", "vw")
    ks, kw = k_sw[:, 0:LANES], k_sw[:, LANES:]
    vst, vwt = v_sw[:, 0:LANES].T, v_sw[:, LANES:].T
    gate_t = jax.nn.sigmoid(seg("gate")).T
    tail = _ones_rows(tm, F32)
    lane = lax.broadcasted_iota(jnp.int32, (tm, LANES), 1)
    for g in range(NSA_GROUPS):
        ks_ref[g] = jnp.where(lane < HEAD_DIM, _head_tile(ks, g), feat).astype(BF16)
        kw_ref[g] = jnp.where(lane < HEAD_DIM, _head_tile(kw, g), feat).astype(BF16)
        gate_ref[g] = gate_t[g * GATE_ROWS:(g + 1) * GATE_ROWS]
        for src, dst in ((vst, vst_ref), (vwt, vwt_ref)):
            vt = jnp.concatenate([src[g * HEAD_DIM:(g + 1) * HEAD_DIM], tail], axis=0).astype(BF16)
            for j in range(tm // KEY_CHUNK):
                dst[g, j] = vt[:, j * KEY_CHUNK:(j + 1) * KEY_CHUNK]
    qkv = jnp.concatenate([(seg("qd") * (scale * LOG2E)).astype(BF16), seg("kd").astype(BF16),
                           seg("vd").astype(BF16)], axis=1)
    bounds = (0, DIL_Q_W, 2 * DIL_Q_W, 3 * DIL_Q_W)
    subs = tm // PERM_ROWS
    for i, (_, d) in enumerate(DIL_PATTERNS):
        if d > 1:
            by_phase = [jnp.dot(perm_ref[i], qkv[s * PERM_ROWS:(s + 1) * PERM_ROWS],
                                preferred_element_type=F32) for s in range(subs)]
        r = PERM_ROWS // d
        for p in range(d):
            rows = qkv if d == 1 else jnp.concatenate(
                [by_phase[s][p * r:(p + 1) * r] for s in range(subs)], axis=0).astype(BF16)
            for j, ref in enumerate(dil_refs[3 * i:3 * i + 3]):
                width = bounds[j + 1] - bounds[j]
                ref[:, p * width:(p + 1) * width] = rows[:, bounds[j]:bounds[j + 1]]


def _proj_call(x, g, w, tm):
    B, S, D = x.shape
    G = NSA_GROUPS
    cmp_w = _PROJ_OFFS["cmp"][1] - _PROJ_OFFS["cmp"][0]
    row = lambda c: pl.BlockSpec((None, tm, c), lambda b, i: (b, i, 0))
    k_spec = pl.BlockSpec((None, G, tm, LANES), lambda b, i: (b, 0, i, 0))
    vt_spec = pl.BlockSpec((None, G, tm // KEY_CHUNK, VT_ROWS, KEY_CHUNK), lambda b, i: (b, 0, i, 0, 0))
    out_shape = (
        jax.ShapeDtypeStruct((B, S, 512), BF16),
        jax.ShapeDtypeStruct((B, S // CMP_STRIDE, CMP_STRIDE * cmp_w), F32),
        jax.ShapeDtypeStruct((B, G, S, LANES), BF16),
        jax.ShapeDtypeStruct((B, G, S, LANES), BF16),
        jax.ShapeDtypeStruct((B, G, S // KEY_CHUNK, VT_ROWS, KEY_CHUNK), BF16),
        jax.ShapeDtypeStruct((B, G, S // KEY_CHUNK, VT_ROWS, KEY_CHUNK), BF16),
        jax.ShapeDtypeStruct((B, G, GATE_ROWS, S), F32),
    )
    dil_specs = ()
    for _, d in DIL_PATTERNS:
        for width in (DIL_Q_W,) * 3:
            out_shape += (jax.ShapeDtypeStruct((B, S // d, d * width), BF16),)
            dil_specs += (pl.BlockSpec((None, tm // d, d * width), lambda b, i: (b, i, 0)),)
    r = np.arange(PERM_ROWS)
    perm = np.zeros((len(DIL_PATTERNS), PERM_ROWS, PERM_ROWS), np.float32)
    for i, (_, d) in enumerate(DIL_PATTERNS):
        assert (PERM_ROWS // d) % 8 == 0
        perm[i, (r % d) * (PERM_ROWS // d) + r // d, r] = 1.0
    perm = jnp.asarray(perm, BF16)
    return pl.pallas_call(
        _proj_kernel,
        grid=(B, S // tm),
        in_specs=[row(D), pl.BlockSpec((1, D), lambda b, i: (0, 0)),
                  pl.BlockSpec(w.shape, lambda b, i: (0, 0)),
                  pl.BlockSpec(perm.shape, lambda b, i: (0, 0, 0))],
        out_specs=(row(512),
                   pl.BlockSpec((None, tm // CMP_STRIDE, CMP_STRIDE * cmp_w), lambda b, i: (b, i, 0)),
                   k_spec, k_spec, vt_spec, vt_spec,
                   pl.BlockSpec((None, G, GATE_ROWS, tm), lambda b, i: (b, 0, 0, i))) + dil_specs,
        out_shape=out_shape,
        scratch_shapes=[pltpu.VMEM((cmp_w // LANES, tm, LANES), F32)],
        compiler_params=pltpu.CompilerParams(
            dimension_semantics=("parallel", "parallel"), vmem_limit_bytes=VMEM_LIMIT),
        name="proj",
    )(x, g, w, perm)


def _compress_kernel(r_ref, posa_ref, posb_ref, wa_ref, wb_ref, w2_ref, ovt_ref, kc_ref, cvo_ref):
    n = r_ref.shape[0]
    slot_w = CMP_STRIDE * HEAD_DIM
    outs = []
    for s in range(2 * NSA_GROUPS):
        r = r_ref[:, s * slot_w:(s + 1) * slot_w]
        ha = jnp.dot((r + posa_ref[s]).astype(BF16), wa_ref[s], preferred_element_type=F32)
        hb = jnp.dot((r + posb_ref[s]).astype(BF16), wb_ref[s], preferred_element_type=F32)
        hid = jax.nn.gelu(ha + pltpu.roll(hb, n - 1, 0))
        outs.append(jnp.dot(hid.astype(BF16), w2_ref[s], preferred_element_type=F32))
    cmp_end = lax.broadcasted_iota(jnp.int32, (n, LANES), 0) * CMP_STRIDE + (CMP_BLOCK - 1)
    feat = _pos_features(cmp_end)
    zeros = jnp.zeros((n, LANES - HEAD_DIM), F32)
    for g in range(NSA_GROUPS):
        kc_ref[g] = jnp.concatenate([outs[g], feat[:, HEAD_DIM:]], axis=1).astype(BF16)
        v_t = jnp.concatenate([outs[NSA_GROUPS + g], zeros], axis=1).T[0:HEAD_DIM]
        cvo_ref[g] = jnp.concatenate([v_t, ovt_ref[...], _ones_rows(n, F32)], axis=0).astype(BF16)


def _compress_call(r, posa, posb, wa, wb, w2, ovt):
    B, R, C = r.shape
    G = NSA_GROUPS
    rows = HEAD_DIM + ovt.shape[0] + ONES_ROWS
    full = lambda a: pl.BlockSpec(a.shape, lambda b: (0,) * a.ndim)
    return pl.pallas_call(
        _compress_kernel,
        grid=(B,),
        in_specs=[pl.BlockSpec((None, R, C), lambda b: (b, 0, 0)),
                  full(posa), full(posb), full(wa), full(wb), full(w2), full(ovt)],
        out_specs=(pl.BlockSpec((None, G, R, LANES), lambda b: (b, 0, 0, 0)),
                   pl.BlockSpec((None, G, rows, R), lambda b: (b, 0, 0, 0))),
        out_shape=(jax.ShapeDtypeStruct((B, G, R, LANES), BF16),
                   jax.ShapeDtypeStruct((B, G, rows, R), BF16)),
        compiler_params=pltpu.CompilerParams(
            dimension_semantics=("parallel",), vmem_limit_bytes=VMEM_LIMIT),
        name="compress",
    )(r, posa, posb, wa, wb, w2, ovt)


def _flash_step(s, bias, v_t, m_ref, acc_ref):
    ps, alphas = [], []
    qb = bias.shape[1]
    for h in range(NSA_HPG):
        cols = slice(h * qb, (h + 1) * qb)
        sh = s[:, cols] + bias
        m_old = m_ref[:, cols]
        m_new = jnp.maximum(m_old, jnp.max(sh, axis=0, keepdims=True))
        alphas.append(jnp.exp2(m_old - m_new))
        ps.append(jnp.exp2(sh - m_new).astype(BF16))
        m_ref[:, cols] = m_new
    pv = jnp.dot(v_t, jnp.concatenate(ps, axis=1), preferred_element_type=F32)
    acc_ref[...] = jnp.concatenate(alphas, axis=1) * acc_ref[...] + pv


def _softmax_pv(s, bias, v_t):
    ms, ps = [], []
    qb = bias.shape[1]
    for h in range(NSA_HPG):
        sh = s[:, h * qb:(h + 1) * qb] + bias
        ms.append(jnp.max(sh, axis=0, keepdims=True))
        ps.append(jnp.exp2(sh - ms[h]).astype(BF16))
    r = jnp.dot(v_t, jnp.concatenate(ps, axis=1), preferred_element_type=F32)
    return jnp.concatenate(ms, axis=1), r


def _normalise(acc):
    return acc[0:HEAD_DIM] * (1.0 / jnp.maximum(acc[HEAD_DIM:HEAD_DIM + 1], 1e-30))


def _nsa_kernel(qfeat_ref, q_ref, kc_ref, cvo_ref, ks_ref, vst_ref, kw_ref, vwt_ref, gate_ref,
                o_ref, sel_ref, ms_ref, accs_ref, flag_ref, list_ref):
    n = pl.program_id(2)
    qb, kc = NSA_QB, KEY_CHUNK
    t0 = n * qb
    lane = lax.broadcasted_iota(jnp.int32, (qb, LANES), 1)
    qf = q_ref[...].astype(F32)
    q = jnp.concatenate(
        [jnp.where(lane < HEAD_DIM, _head_tile(qf, h), qfeat_ref[h][0:1, :]) for h in range(NSA_HPG)],
        axis=0).astype(BF16)

    wq = min(WIN_QB, qb)
    span = WIN + wq
    back_w = (lax.broadcasted_iota(jnp.int32, (span, wq), 1)
              - lax.broadcasted_iota(jnp.int32, (span, wq), 0))
    win_parts = []
    for j in range(qb // wq):
        k_lo = pl.multiple_of(jnp.maximum(t0 + j * wq - WIN, 0), kc)
        dist = back_w + (t0 + j * wq - k_lo)
        bias = jnp.where(dist >= 0, jnp.where(dist < WIN, 0.0, -jnp.inf), -jnp.inf)
        q_part = jnp.concatenate([q[h * qb + j * wq:h * qb + (j + 1) * wq] for h in range(NSA_HPG)], axis=0)
        vt_w = jnp.concatenate([vwt_ref[k_lo // kc + u] for u in range(span // kc)], axis=1)
        win_parts.append(_normalise(_softmax_pv(_nt(kw_ref[pl.ds(k_lo, span), :], q_part), bias, vt_w)[1]))
    o_win = jnp.concatenate([part[:, h * wq:(h + 1) * wq] for h in range(NSA_HPG) for part in win_parts],
                            axis=1)

    n_cmp = kc_ref.shape[0]
    n_sel = sel_ref.shape[0] - SEL_PAD_ROWS
    s = _nt(kc_ref[...], q)
    cmp_end = lax.broadcasted_iota(jnp.int32, (n_cmp, qb), 0) * CMP_STRIDE + (CMP_BLOCK - 1)
    bias_c = jnp.where(cmp_end <= t0 + lax.broadcasted_iota(jnp.int32, (n_cmp, qb), 1), 0.0, -jnp.inf)
    es = []
    for h in range(NSA_HPG):
        sh = s[:, h * qb:(h + 1) * qb] + bias_c
        m = jnp.max(sh, axis=0, keepdims=True)
        es.append(jnp.exp2(sh - jnp.where(m == -jnp.inf, 0.0, m)).astype(BF16))
    r = jnp.dot(cvo_ref[...], jnp.concatenate(es, axis=1), preferred_element_type=F32)
    inv_l = 1.0 / jnp.maximum(r[HEAD_DIM + n_sel:HEAD_DIM + n_sel + 1], 1e-30)
    o_cmp = r[0:HEAD_DIM] * inv_l
    imp = r[HEAD_DIM:HEAD_DIM + n_sel] * inv_l
    imp_t = imp[:, 0:qb]
    for h in range(1, NSA_HPG):
        imp_t = imp_t + imp[:, h * qb:(h + 1) * qb]

    blk = lax.broadcasted_iota(jnp.int32, (n_sel, qb), 0).astype(F32)
    cur = ((t0 + lax.broadcasted_iota(jnp.int32, (n_sel, qb), 1)) // SLC_BLOCK).astype(F32)
    ago = cur - blk
    forced = jnp.where(blk == 0, 1.0, jnp.where(ago == 0, 1.0, jnp.where(ago == 1, 1.0, 0.0)))
    valid = ago >= 0
    score = jnp.where(valid, jnp.where(forced > 0.5, -1.0, imp_t), -1.0)
    for _ in range(max(min(SLC_TOPK, n_sel) - N_FORCED, 0)):
        mx = jnp.max(score, axis=0, keepdims=True)
        idx = jnp.min(jnp.where(score == mx, blk, float(n_sel)), axis=0, keepdims=True)
        score = jnp.where(blk == idx, -jnp.inf, score)
    sel_t = jnp.where(valid, jnp.where(score == -jnp.inf, 1.0, forced), 0.0)
    sel_ref[0:n_sel, :] = jnp.where(sel_t > 0.5, 0.0, -jnp.inf)
    sel_ref[n_sel:, :] = jnp.full((SEL_PAD_ROWS, qb), -jnp.inf, F32)
    any_q = jnp.max(sel_t, axis=1, keepdims=True)
    bit = jnp.left_shift(1, lax.broadcasted_iota(jnp.int32, (n_sel, 1), 0) % FLAG_BITS).astype(F32)
    packed = any_q * bit
    for i in range(n_sel // FLAG_BITS):
        word = jnp.sum(packed[i * FLAG_BITS:(i + 1) * FLAG_BITS], axis=0, keepdims=True)
        flag_ref[i] = word.astype(jnp.int32)[0, 0]

    blocks_per_chunk = kc // SLC_BLOCK
    chunks_per_word = FLAG_BITS // blocks_per_chunk
    row = lax.broadcasted_iota(jnp.int32, (kc, qb), 0)

    def chunk_bias(first_block):
        bias = sel_ref[pl.ds(first_block, 1), :]
        for j in range(1, blocks_per_chunk):
            bias = jnp.where(row < j * SLC_BLOCK, bias, sel_ref[pl.ds(first_block + j, 1), :])
        return bias

    own = qb // kc
    first_own = t0 // kc
    off_first = jnp.where(n == 0, -jnp.inf, 0.0)
    query = lax.broadcasted_iota(jnp.int32, (kc, qb), 1)
    biases = [chunk_bias(0) + off_first]
    for u in range(own):
        causal = jnp.where(row + u * kc <= query, 0.0, -jnp.inf)
        biases.append(chunk_bias((first_own + u) * blocks_per_chunk) + causal)
    keys = jnp.concatenate([ks_ref[0:kc, :], ks_ref[pl.ds(pl.multiple_of(t0, kc), qb), :]], axis=0)
    vt_s = jnp.concatenate([vst_ref[0]] + [vst_ref[first_own + u] for u in range(own)], axis=1)
    m_s, acc_s = _softmax_pv(_nt(keys, q), jnp.concatenate(biases, axis=0), vt_s)
    ms_ref[...] = m_s
    accs_ref[...] = acc_s

    def scan_body(w, cnt):
        word = flag_ref[w]
        for j in range(chunks_per_word):
            c = w * chunks_per_word + j
            bits = (word >> (j * blocks_per_chunk)) & ((1 << blocks_per_chunk) - 1)
            list_ref[cnt] = c
            cnt = cnt + ((bits != 0) & (c >= 1) & (c < first_own)).astype(jnp.int32)
        return cnt

    cnt = lax.fori_loop(0, (first_own + chunks_per_word - 1) // chunks_per_word, scan_body, 0)
    for u in range(SLC_GROUP):
        list_ref[cnt + u] = -1

    def slc_body(gi, carry):
        keys, biases, vts = [], [], []
        for u in range(SLC_GROUP):
            c = list_ref[gi * SLC_GROUP + u]
            cc = jnp.maximum(c, 0)
            biases.append(chunk_bias(jnp.where(c >= 0, cc * blocks_per_chunk, n_sel)))
            keys.append(ks_ref[pl.ds(pl.multiple_of(cc * kc, kc), kc), :])
            vts.append(vst_ref[cc])
        _flash_step(_nt(jnp.concatenate(keys, axis=0), q), jnp.concatenate(biases, axis=0),
                    jnp.concatenate(vts, axis=1), ms_ref, accs_ref)
        return carry

    lax.fori_loop(0, (cnt + SLC_GROUP - 1) // SLC_GROUP, slc_body, 0)

    o_slc = _normalise(accs_ref[...])
    gate = gate_ref[...]
    outs = []
    for h in range(NSA_HPG):
        cols = slice(h * qb, (h + 1) * qb)
        gc = [gate[j * NSA_HPG + h:j * NSA_HPG + h + 1, :] for j in range(3)]
        outs.append(gc[0] * o_cmp[:, cols] + gc[1] * o_slc[:, cols] + gc[2] * o_win[:, cols])
    o_ref[...] = jnp.concatenate(outs, axis=0).T


def _nsa_call(qfeat, qn, kc, cvo, ks, vst, kw, vwt, gates):
    B, S, _ = qn.shape
    G = NSA_GROUPS
    qb = NSA_QB
    nb = S // qb
    n_sel = S // SLC_BLOCK
    width = NSA_HPG * qb
    per_group = lambda a: pl.BlockSpec((None, None) + a.shape[2:],
                                       lambda b, g, n: (b, g) + (0,) * (a.ndim - 2))
    return pl.pallas_call(
        _nsa_kernel,
        grid=(B, G, nb),
        in_specs=[
            pl.BlockSpec((None,) + qfeat.shape[1:], lambda b, g, n: (g, 0, 0, 0)),
            pl.BlockSpec((None, qb, NSA_HPG * HEAD_DIM), lambda b, g, n: (b, n, g)),
            per_group(kc), per_group(cvo), per_group(ks), per_group(vst), per_group(kw), per_group(vwt),
            pl.BlockSpec((None, None, GATE_ROWS, qb), lambda b, g, n: (b, g, 0, n)),
        ],
        out_specs=pl.BlockSpec((None, qb, NSA_HPG * HEAD_DIM), lambda b, g, n: (b, n, g)),
        out_shape=jax.ShapeDtypeStruct((B, S, NSA_HEADS * HEAD_DIM), F32),
        scratch_shapes=[pltpu.VMEM((n_sel + SEL_PAD_ROWS, qb), F32),
                        pltpu.VMEM((1, width), F32), pltpu.VMEM((VT_ROWS, width), F32),
                        pltpu.SMEM((n_sel // FLAG_BITS,), jnp.int32),
                        pltpu.SMEM((S // KEY_CHUNK + SLC_GROUP,), jnp.int32)],
        compiler_params=pltpu.CompilerParams(
            dimension_semantics=("parallel", "parallel", "arbitrary"), vmem_limit_bytes=VMEM_LIMIT),
        name="nsa",
    )(qfeat, qn, kc, cvo, ks, vst, kw, vwt, gates)


def _dilated_kernel(qfeat_ref, *refs):
    n_pat = len(DIL_PATTERNS)
    j = pl.program_id(1)
    for i, (window, dilation) in enumerate(DIL_PATTERNS):
        steps_per_phase = max(d for _, d in DIL_PATTERNS) // dilation
        _dilated_blocks(qfeat_ref, *refs[5 * i:5 * i + 5], *refs[5 * n_pat + 2 * i:5 * n_pat + 2 * i + 2],
                        wd=window // dilation, dilation=dilation,
                        phase=j // steps_per_phase, step=j % steps_per_phase)


def _dilated_blocks(qfeat_ref, q_ref, kp_ref, kc_ref, vp_ref, vc_ref, o_ref, lse_ref, *,
                    wd, dilation, phase, step):
    c = Q_BLOCK
    hd = HEAD_DIM
    blocks = q_ref.shape[0] // c
    lane = lax.broadcasted_iota(jnp.int32, (blocks * c, LANES), 1)
    key = lax.broadcasted_iota(jnp.int32, (2 * c, c), 0)
    qry = lax.broadcasted_iota(jnp.int32, (2 * c, c), 1)
    first = step == 0
    off_later = jnp.where(key < c, qry - (key + (wd - c)), (key - c) - qry)
    off_first = jnp.where(key < c, jnp.where(first, 2 * c, qry) - (key + (wd - c)), (key - c) - qry)
    bias_later = jnp.where(off_later <= 0, 0.0, -jnp.inf)
    bias_first = jnp.where(off_first <= 0, 0.0, -jnp.inf)
    sub = (step * blocks - 1) * c + lax.broadcasted_iota(jnp.int32, ((blocks + 1) * c, LANES), 0)
    pos_feat = _pos_features(sub * dilation + phase).astype(BF16)
    n_keys = (blocks + 1) * c
    tail = _ones_rows(n_keys, BF16)
    assert PAIR_VT_ROWS == 2 * hd + ONES_ROWS
    for p in range(DIL_HEADS // 2):
        cols = slice(p * LANES, (p + 1) * LANES)
        q_pair = q_ref[:, cols]
        zero = jnp.zeros_like(q_pair)
        feats = [jnp.broadcast_to(qfeat_ref[2 * p + i][0:1, :], (blocks * c, LANES)).astype(BF16)
                 for i in range(2)]
        q_a = jnp.concatenate([jnp.where(lane < hd, q_pair, zero), feats[0]], axis=1)
        q_b = jnp.concatenate([jnp.where(lane < hd, zero, q_pair), feats[1]], axis=1)
        k_all = jnp.concatenate([kp_ref[:, cols], kc_ref[:, cols]], axis=0)
        k_all = jnp.concatenate([k_all, pos_feat], axis=1)
        v_t = jnp.concatenate([jnp.concatenate([vp_ref[:, cols].T, vc_ref[:, cols].T], axis=1), tail],
                              axis=0)
        nb = min(DIL_BAND, blocks)
        for g0 in range(0, blocks, nb):
            q_aug = jnp.concatenate([part[j * c:(j + 1) * c] for j in range(g0, g0 + nb)
                                     for part in (q_a, q_b)], axis=0)
            s_all = _nt(k_all[g0 * c:(g0 + nb + 1) * c], q_aug)
            ms, band = [], []
            for j in range(nb):
                bias = bias_later if g0 + j else bias_first
                probs = []
                for i in range(2):
                    si = s_all[j * c:(j + 2) * c, (2 * j + i) * c:(2 * j + i + 1) * c] + bias
                    ms.append(jnp.max(si, axis=0, keepdims=True))
                    probs.append(jnp.exp2(si - ms[-1]).astype(BF16))
                pieces = [jnp.zeros((j * c, 2 * c), BF16)] if j else []
                pieces.append(jnp.concatenate(probs, axis=1))
                if j < nb - 1:
                    pieces.append(jnp.zeros(((nb - 1 - j) * c, 2 * c), BF16))
                band.append(jnp.concatenate(pieces, axis=0))
            r_all = jnp.dot(v_t[:, g0 * c:(g0 + nb + 1) * c], jnp.concatenate(band, axis=1),
                            preferred_element_type=F32)
            for j in range(nb):
                outs, lses = [], []
                for i in range(2):
                    at = slice((2 * j + i) * c, (2 * j + i + 1) * c)
                    l = r_all[2 * hd:2 * hd + 1, at]
                    outs.append(r_all[i * hd:(i + 1) * hd, at] * (1.0 / l))
                    lses.append(jnp.broadcast_to(ms[2 * j + i] + jnp.log2(l), (hd, c)))
                rows = slice((g0 + j) * c, (g0 + j + 1) * c)
                o_ref[rows, cols] = jnp.concatenate(outs, axis=0).T
                lse_ref[rows, cols] = jnp.concatenate(lses, axis=0).T


def _dilated_call(qfeat, qkv_by_pattern):
    B = qkv_by_pattern[0][0].shape[0]
    steps = max(d for _, d in DIL_PATTERNS)
    in_specs = [pl.BlockSpec(qfeat.shape, lambda b, j: (0, 0, 0))]
    out_specs, out_shape, operands = [], [], [qfeat]
    for (window, d), (q_v, k_v, v_v) in zip(DIL_PATTERNS, qkv_by_pattern):
        assert Q_BLOCK - 1 <= window // d <= Q_BLOCK and steps % d == 0
        L = q_v.shape[1]
        spp = steps // d
        per_step = L // (Q_BLOCK * spp)
        blk = pl.BlockSpec((None, Q_BLOCK * per_step, DIL_Q_W),
                           lambda b, j, spp=spp: (b, j % spp, j // spp))
        prev = pl.BlockSpec((None, Q_BLOCK, DIL_Q_W),
                            lambda b, j, spp=spp, per_step=per_step:
                            (b, jnp.maximum((j % spp) * per_step - 1, 0), j // spp))
        in_specs += [blk, prev, blk, prev, blk]
        operands += [q_v, k_v, k_v, v_v, v_v]
        out_specs += [blk, blk]
        out_shape += [jax.ShapeDtypeStruct((B, L, d * DIL_Q_W), F32)] * 2
    outs = pl.pallas_call(
        _dilated_kernel,
        grid=(B, steps),
        in_specs=in_specs,
        out_specs=tuple(out_specs),
        out_shape=tuple(out_shape),
        compiler_params=pltpu.CompilerParams(
            dimension_semantics=("parallel", "arbitrary"), vmem_limit_bytes=VMEM_LIMIT),
        name="dilated",
    )(*operands)
    return [(outs[2 * i], outs[2 * i + 1]) for i in range(len(DIL_PATTERNS))]


def _merge_kernel(x_ref, g_ref, wg_ref, onsa_ref, o1_ref, o2_ref, o3_ref, l1_ref, l2_ref, l3_ref,
                  wbn_ref, wbd_ref, wo_ref, fg_ref, out_ref, tok_scr, *, final):
    x = x_ref[...]
    tm = x.shape[0]
    xb = _rms(x, g_ref[...]).astype(BF16)
    cz = NSA_HEADS * HEAD_DIM
    d = x.shape[-1]

    def gate_seg(c0, c1):
        return jnp.dot(xb, wg_ref[:, c0:c1], preferred_element_type=F32)

    def token_major(ref, slot, dil):
        if dil == 1:
            return ref[...]
        tiles = DIL_Q_W // LANES
        for p in range(dil):
            for j in range(tiles):
                tok_scr[slot * tiles + j, pl.ds(p, tm // dil, stride=dil), :] = \
                    ref[:, p * DIL_Q_W + j * LANES:p * DIL_Q_W + (j + 1) * LANES]
        return jnp.concatenate([tok_scr[slot * tiles + j] for j in range(tiles)], axis=1)

    o_nsa = onsa_ref[...] * jax.nn.silu(gate_seg(0, cz))
    dils = [dil for _, dil in DIL_PATTERNS]
    o1, o2, o3 = (token_major(r, i, dil) for i, (r, dil) in enumerate(zip((o1_ref, o2_ref, o3_ref), dils)))
    l1, l2, l3 = (token_major(r, 3 + i, dil) for i, (r, dil) in enumerate(zip((l1_ref, l2_ref, l3_ref), dils)))
    mx = jnp.maximum(jnp.maximum(l1, l2), l3)
    e1, e2, e3 = jnp.exp2(l1 - mx), jnp.exp2(l2 - mx), jnp.exp2(l3 - mx)
    inv = 1.0 / (e1 + e2 + e3)
    o_dil = (e1 * inv) * o1 + (e2 * inv) * o2 + (e3 * inv) * o3
    o_dil = o_dil * jax.nn.silu(gate_seg(cz, 2 * cz))
    a = jnp.dot(o_nsa.astype(BF16), wbn_ref[...], preferred_element_type=F32)
    bd = jnp.dot(o_dil.astype(BF16), wbd_ref[...], preferred_element_type=F32)
    merged = (jax.nn.sigmoid(gate_seg(2 * cz, 2 * cz + d)) * a
              + jax.nn.sigmoid(gate_seg(2 * cz + d, 2 * cz + 2 * d)) * bd)
    y = jnp.dot(merged.astype(BF16), wo_ref[...], preferred_element_type=F32)
    out_ref[...] = _rms(x + y, fg_ref[...]) if final else x + y


def _merge_call(x, g, wg, o_nsa, o_d, lse_d, wbn, wbd, wo, fg, tm, final):
    B, S, D = x.shape
    cz = o_nsa.shape[-1]
    row = lambda c: pl.BlockSpec((None, tm, c), lambda b, i: (b, i, 0))
    full = lambda a: pl.BlockSpec(a.shape, lambda b, i: (0,) * a.ndim)
    phase = [pl.BlockSpec((None, tm // dil, dil * cz), lambda b, i: (b, i, 0)) for _, dil in DIL_PATTERNS]
    return pl.pallas_call(
        functools.partial(_merge_kernel, final=final),
        grid=(B, S // tm),
        in_specs=[row(D), full(g), full(wg), row(cz)] + phase + phase
                 + [full(wbn), full(wbd), full(wo), full(fg)],
        out_specs=row(D),
        out_shape=jax.ShapeDtypeStruct((B, S, D), F32),
        scratch_shapes=[pltpu.VMEM((2 * len(DIL_PATTERNS) * cz // LANES, tm, LANES), F32)],
        compiler_params=pltpu.CompilerParams(
            dimension_semantics=("parallel", "parallel"), vmem_limit_bytes=VMEM_LIMIT),
        name="merge",
    )(x, g, wg, o_nsa, *o_d, *lse_d, wbn, wbd, wo, fg)


def _proj_columns(d_model):
    hd, G, H = HEAD_DIM, NSA_GROUPS, NSA_HEADS
    nsa_w = H * hd
    o_q, o_kv = 0, nsa_w
    o_g = o_kv + 6 * G * hd
    o_zn = o_g + 3 * H
    o_qkvd = o_zn + nsa_w
    dil_w = DIL_HEADS * hd
    o_zd = o_qkvd + 3 * dil_w
    o_mg = o_zd + dil_w
    kv = lambda j, g: o_kv + (j * G + g) * hd + np.arange(hd)
    cols = [o_q + np.arange(nsa_w)]
    cols += [kv(j, g) for j in (0, 1) for g in range(G)]
    for j in (2, 4, 3, 5):
        cols += [kv(j, g) for g in range(G)]
    lanes = np.full(LANES, -1)
    for g in range(G):
        for j in range(3):
            lanes[g * GATE_ROWS + j * NSA_HPG + np.arange(NSA_HPG)] = o_g + j * H + g * NSA_HPG + np.arange(NSA_HPG)
    cols.append(lanes)
    cols.append(o_qkvd + np.arange(3 * dil_w))
    proj_cols = np.concatenate(cols)
    assert proj_cols.size == _PROJ_OFFS["vd"][1]
    gate_cols = np.concatenate([o_zn + np.arange(nsa_w), o_zd + np.arange(dil_w),
                                o_mg + np.arange(2 * d_model)])
    return proj_cols, gate_cols


def _take_columns(w, cols, dtype):
    pieces, start = [], 0
    for i in range(1, len(cols) + 1):
        if i < len(cols):
            same_run = (cols[i] < 0 and cols[i - 1] < 0) or (cols[i - 1] >= 0 and cols[i] == cols[i - 1] + 1)
        if i == len(cols) or not same_run:
            n = i - start
            pieces.append(jnp.zeros((w.shape[0], n), w.dtype) if cols[start] < 0
                          else w[:, int(cols[start]):int(cols[start]) + n])
            start = i
    return jnp.concatenate(pieces, axis=1).astype(dtype)


def _compress_weights(pos_k, w1_k, w2_k, pos_v, w1_v, w2_v):
    hd, G, half = HEAD_DIM, NSA_GROUPS, CMP_BLOCK // 2
    w1 = jnp.stack([w1_k] * G + [w1_v] * G).astype(BF16)
    pos = jnp.stack([pos_k] * G + [pos_v] * G)
    w2 = jnp.stack([w2_k] * G + [w2_v] * G).astype(BF16)
    halves = lambda a, tail: (a[:, :half].reshape((2 * G,) + tail), a[:, half:].reshape((2 * G,) + tail))
    posa, posb = halves(pos, (1, half * hd))
    wa, wb = halves(w1, (half * hd, hd))
    return posa, posb, wa, wb, w2


def _overlap_t(n_cmp_pad, n_sel):
    i = np.arange(n_cmp_pad)[None, :]
    j = np.arange(n_sel)[:, None]
    lo = np.maximum(i * CMP_STRIDE, j * SLC_BLOCK)
    hi = np.minimum(i * CMP_STRIDE + CMP_BLOCK, (j + 1) * SLC_BLOCK)
    return (np.clip(hi - lo, 0, None) / CMP_BLOCK).astype(np.float32)


def kernel(x, norm_g, w_in, cmp_pos_k, cmp_w1_k, cmp_w2_k, cmp_pos_v, cmp_w1_v, cmp_w2_v,
           w_br_nsa, w_br_dil, w_out, final_g):
    B, S, D = x.shape
    n_sel = S // SLC_BLOCK
    assert S % (Q_BLOCK * max(d for _, d in DIL_PATTERNS)) == 0 and n_sel % FLAG_BITS == 0
    n_cmp_pad = S // CMP_STRIDE
    slopes_nsa, slopes_dil = _alibi_slopes()
    proj_cols, gate_cols = _proj_columns(D)
    ovt = jnp.asarray(_overlap_t(n_cmp_pad, n_sel))
    qfeat = jnp.asarray(np.broadcast_to(
        _slope_features(slopes_nsa).reshape(NSA_GROUPS, NSA_HPG, 1, LANES),
        (NSA_GROUPS, NSA_HPG, 8, LANES)))
    qfeat_dil = jnp.asarray(np.broadcast_to(
        _slope_features(slopes_dil).reshape(DIL_HEADS, 1, LANES), (DIL_HEADS, 8, LANES)))
    tm = 512

    h = x
    for layer in range(w_in.shape[0]):
        w_proj = _take_columns(w_in[layer], proj_cols, BF16)
        w_gate = _take_columns(w_in[layer], gate_cols, BF16)
        g_l = norm_g[layer].reshape(1, D)

        qn, cmp_src, ks, kw, vst, vwt, gates, *dil = _proj_call(h, g_l, w_proj, tm)
        cw = _compress_weights(cmp_pos_k[layer], cmp_w1_k[layer], cmp_w2_k[layer],
                               cmp_pos_v[layer], cmp_w1_v[layer], cmp_w2_v[layer])
        kc, cvo = _compress_call(cmp_src, *cw, ovt)
        o_nsa = _nsa_call(qfeat, qn, kc, cvo, ks, vst, kw, vwt, gates)
        dil_out = _dilated_call(qfeat_dil, [dil[3 * i:3 * i + 3] for i in range(len(DIL_PATTERNS))])
        o_d, lse_d = [o for o, _ in dil_out], [lse for _, lse in dil_out]
        h = _merge_call(h, g_l, w_gate, o_nsa, o_d, lse_d, w_br_nsa[layer].astype(BF16),
                        w_br_dil[layer].astype(BF16), w_out[layer].astype(BF16),
                        final_g.reshape(1, D), tm, final=layer == w_in.shape[0] - 1)
    return h
```

```python
import functools
import math

import numpy as np
import jax
import jax.numpy as jnp
from jax import lax
from jax.experimental import pallas as pl
from jax.experimental.pallas import tpu as pltpu

F32 = jnp.float32
BF16 = jnp.bfloat16

HEAD_DIM = 64
LANES = 128
NSA_HEADS = 8
NSA_GROUPS = 2
NSA_HPG = NSA_HEADS // NSA_GROUPS
CMP_BLOCK = 32
CMP_STRIDE = 16
SLC_BLOCK = 64
SLC_TOPK = 16
WIN = 512
FORCE_SCORE = 1.0e4
DIL_HEADS = 8
DIL_PATTERNS = ((128, 1), (512, 4), (2048, 16))
Q_BLOCK = 128
RMS_EPS = 1e-6
KEY_CHUNK = 128
VT_ROWS = 80
PAIR_VT_ROWS = 144
GATE_ROWS = 16
FEAT_SPLIT = 3
FLAG_BITS = 16
SLC_GROUP = 4
PROJ_ROWS = 1024
MERGE_ROWS = 512
PERM_ROWS = 128
WIN_QB = 256
NSA_QB = 512
DIL_BAND = 4
N_FORCED = 3
assert FORCE_SCORE > NSA_HPG + 1
SEL_PAD_ROWS = 8
ONES_ROWS = 16
LOG2E = math.log2(math.e)
VMEM_LIMIT = 56 * 1024 * 1024


def _alibi_slopes():
    n = NSA_HEADS + DIL_HEADS
    s = 2.0 ** (-8.0 * np.arange(1, n + 1) / n)
    return s[0::2].astype(np.float32), s[1::2].astype(np.float32)


def _nt(a, b):
    return lax.dot_general(a, b, (((1,), (1,)), ((), ())), preferred_element_type=F32)


def _rms(x, g):
    return (x * lax.rsqrt(jnp.mean(x * x, axis=-1, keepdims=True) + RMS_EPS)) * g


def _head_tile(q_pairs, h):
    tile = q_pairs[:, (h // 2) * LANES:(h // 2 + 1) * LANES]
    return pltpu.roll(tile, HEAD_DIM, 1) if h % 2 else tile


def _pos_features(pos):
    lane = lax.broadcasted_iota(jnp.int32, pos.shape, 1) - HEAD_DIM
    hi = (pos // LANES).astype(F32)
    lo = (pos % LANES).astype(F32)
    return jnp.where(lane < 0, 0.0, jnp.where(lane < FEAT_SPLIT, hi, jnp.where(lane < 2 * FEAT_SPLIT, lo, 0.0)))


def _slope_features(slopes):
    out = np.zeros((len(slopes), LANES), np.float32)
    for h, s in enumerate(slopes):
        rest = np.float32(s) * np.float32(LOG2E)
        for i in range(FEAT_SPLIT):
            piece = np.float32(np.asarray(rest, np.float32).astype(BF16))
            out[h, HEAD_DIM + FEAT_SPLIT + i] = piece
            out[h, HEAD_DIM + i] = piece * LANES
            rest = np.float32(rest - piece)
    return out


def _ones_rows(width, dtype):
    row = lax.broadcasted_iota(jnp.int32, (ONES_ROWS, width), 0)
    return jnp.where(row == 0, 1.0, 0.0).astype(dtype)


_PROJ_SEGS = (("qn", 512), ("cmp", 256), ("ks", 128), ("kw", 128), ("vs", 128), ("vw", 128),
              ("gate", 128), ("qd", 512), ("kd", 512), ("vd", 512))
DIL_Q_W = DIL_HEADS * HEAD_DIM
_PROJ_OFFS = dict(zip((n for n, _ in _PROJ_SEGS),
                      zip(np.cumsum([0] + [w for _, w in _PROJ_SEGS])[:-1].tolist(),
                          np.cumsum([w for _, w in _PROJ_SEGS]).tolist())))


def _to_lane_tiles(scr, first, val):
    for j in range(val.shape[1] // LANES):
        scr[first + j] = val[:, j * LANES:(j + 1) * LANES]


def _proj_kernel(x_ref, g_ref, w_ref, perm_ref, qn_ref, cmp_ref, ks_ref, kw_ref, vst_ref, vwt_ref,
                 gate_ref, *dil_refs):
    cmp_scr = dil_refs[-1]
    tm = x_ref.shape[0]
    xb = _rms(x_ref[...], g_ref[...]).astype(BF16)
    scale = HEAD_DIM ** -0.5

    def seg(name, last=None):
        c0, c1 = _PROJ_OFFS[name][0], _PROJ_OFFS[last or name][1]
        return jnp.dot(xb, w_ref[:, c0:c1], preferred_element_type=F32)

    qn_ref[...] = (seg("qn") * (scale * LOG2E)).astype(BF16)
    _to_lane_tiles(cmp_scr, 0, seg("cmp"))
    lane_c = lax.broadcasted_iota(jnp.int32, (tm // CMP_STRIDE, LANES), 1)
    slot_w = CMP_STRIDE * HEAD_DIM
    for t in range(cmp_scr.shape[0]):
        for p in range(0, CMP_STRIDE, 2):
            even, odd = (cmp_scr[t, pl.ds(p + i, tm // CMP_STRIDE, stride=CMP_STRIDE), :] for i in range(2))
            for half in range(2):
                pair = (jnp.where(lane_c < HEAD_DIM, even, pltpu.roll(odd, HEAD_DIM, 1)) if half == 0
                        else jnp.where(lane_c < HEAD_DIM, pltpu.roll(even, HEAD_DIM, 1), odd))
                c0 = (2 * t + half) * slot_w + p * HEAD_DIM
                cmp_ref[:, c0:c0 + LANES] = pair
    feat =_pos_features(pl.program_id(1) * tm + lax.broadcasted_iota(jnp.int32, (tm, LANES), 0))
    k_sw, v_sw = seg("ks", "kw"), seg("vs", "vw")
    ks, kw = k_sw[:, 0:LANES], k_sw[:, LANES:]
    vst, vwt = v_sw[:, 0:LANES].T, v_sw[:, LANES:].T
    gate_t = jax.nn.sigmoid(seg("gate")).T
    tail = _ones_rows(tm, F32)
    lane = lax.broadcasted_iota(jnp.int32, (tm, LANES), 1)
    for g in range(NSA_GROUPS):
        ks_ref[g] = jnp.where(lane < HEAD_DIM, _head_tile(ks, g), feat).astype(BF16)
        kw_ref[g] = jnp.where(lane < HEAD_DIM, _head_tile(kw, g), feat).astype(BF16)
        gate_ref[g] = gate_t[g * GATE_ROWS:(g + 1) * GATE_ROWS]
        for src, dst in ((vst, vst_ref), (vwt, vwt_ref)):
            vt = jnp.concatenate([src[g * HEAD_DIM:(g + 1) * HEAD_DIM], tail], axis=0).astype(BF16)
            for j in range(tm // KEY_CHUNK):
                dst[g, j] = vt[:, j * KEY_CHUNK:(j + 1) * KEY_CHUNK]
    qkv = jnp.concatenate([(seg("qd") * (scale * LOG2E)).astype(BF16), seg("kd").astype(BF16),
                           seg("vd").astype(BF16)], axis=1)
    bounds = (0, DIL_Q_W, 2 * DIL_Q_W, 3 * DIL_Q_W)
    subs = tm // PERM_ROWS
    for i, (_, d) in enumerate(DIL_PATTERNS):
        if d > 1:
            by_phase = [jnp.dot(perm_ref[i], qkv[s * PERM_ROWS:(s + 1) * PERM_ROWS],
                                preferred_element_type=F32) for s in range(subs)]
        r = PERM_ROWS // d
        for p in range(d):
            rows = qkv if d == 1 else jnp.concatenate(
                [by_phase[s][p * r:(p + 1) * r] for s in range(subs)], axis=0).astype(BF16)
            for j, ref in enumerate(dil_refs[3 * i:3 * i + 3]):
                width = bounds[j + 1] - bounds[j]
                ref[:, p * width:(p + 1) * width] = rows[:, bounds[j]:bounds[j + 1]]


def _proj_call(x, g, w, tm):
    B, S, D = x.shape
    G = NSA_GROUPS
    cmp_w = _PROJ_OFFS["cmp"][1] - _PROJ_OFFS["cmp"][0]
    row = lambda c: pl.BlockSpec((None, tm, c), lambda b, i: (b, i, 0))
    k_spec = pl.BlockSpec((None, G, tm, LANES), lambda b, i: (b, 0, i, 0))
    vt_spec = pl.BlockSpec((None, G, tm // KEY_CHUNK, VT_ROWS, KEY_CHUNK), lambda b, i: (b, 0, i, 0, 0))
    out_shape = (
        jax.ShapeDtypeStruct((B, S, 512), BF16),
        jax.ShapeDtypeStruct((B, S // CMP_STRIDE, CMP_STRIDE * cmp_w), F32),
        jax.ShapeDtypeStruct((B, G, S, LANES), BF16),
        jax.ShapeDtypeStruct((B, G, S, LANES), BF16),
        jax.ShapeDtypeStruct((B, G, S // KEY_CHUNK, VT_ROWS, KEY_CHUNK), BF16),
        jax.ShapeDtypeStruct((B, G, S // KEY_CHUNK, VT_ROWS, KEY_CHUNK), BF16),
        jax.ShapeDtypeStruct((B, G, GATE_ROWS, S), F32),
    )
    dil_specs = ()
    for _, d in DIL_PATTERNS:
        for width in (DIL_Q_W,) * 3:
            out_shape += (jax.ShapeDtypeStruct((B, S // d, d * width), BF16),)
            dil_specs += (pl.BlockSpec((None, tm // d, d * width), lambda b, i: (b, i, 0)),)
    r = np.arange(PERM_ROWS)
    perm = np.zeros((len(DIL_PATTERNS), PERM_ROWS, PERM_ROWS), np.float32)
    for i, (_, d) in enumerate(DIL_PATTERNS):
        assert (PERM_ROWS // d) % 8 == 0
        perm[i, (r % d) * (PERM_ROWS // d) + r // d, r] = 1.0
    perm = jnp.asarray(perm, BF16)
    return pl.pallas_call(
        _proj_kernel,
        grid=(B, S // tm),
        in_specs=[row(D), pl.BlockSpec((1, D), lambda b, i: (0, 0)),
                  pl.BlockSpec(w.shape, lambda b, i: (0, 0)),
                  pl.BlockSpec(perm.shape, lambda b, i: (0, 0, 0))],
        out_specs=(row(512),
                   pl.BlockSpec((None, tm // CMP_STRIDE, CMP_STRIDE * cmp_w), lambda b, i: (b, i, 0)),
                   k_spec, k_spec, vt_spec, vt_spec,
                   pl.BlockSpec((None, G, GATE_ROWS, tm), lambda b, i: (b, 0, 0, i))) + dil_specs,
        out_shape=out_shape,
        scratch_shapes=[pltpu.VMEM((cmp_w // LANES, tm, LANES), F32)],
        compiler_params=pltpu.CompilerParams(
            dimension_semantics=("parallel", "parallel"), vmem_limit_bytes=VMEM_LIMIT),
        name="proj",
    )(x, g, w, perm)


def _compress_kernel(r_ref, posa_ref, posb_ref, wa_ref, wb_ref, w2_ref, ovt_ref, kc_ref, cvo_ref):
    n = r_ref.shape[0]
    slot_w = CMP_STRIDE * HEAD_DIM
    outs = []
    for s in range(2 * NSA_GROUPS):
        r = r_ref[:, s * slot_w:(s + 1) * slot_w]
        ha = jnp.dot((r + posa_ref[s]).astype(BF16), wa_ref[s], preferred_element_type=F32)
        hb = jnp.dot((r + posb_ref[s]).astype(BF16), wb_ref[s], preferred_element_type=F32)
        hid = jax.nn.gelu(ha + pltpu.roll(hb, n - 1, 0))
        outs.append(jnp.dot(hid.astype(BF16), w2_ref[s], preferred_element_type=F32))
    cmp_end = lax.broadcasted_iota(jnp.int32, (n, LANES), 0) * CMP_STRIDE + (CMP_BLOCK - 1)
    feat = _pos_features(cmp_end)
    zeros = jnp.zeros((n, LANES - HEAD_DIM), F32)
    for g in range(NSA_GROUPS):
        kc_ref[g] = jnp.concatenate([outs[g], feat[:, HEAD_DIM:]], axis=1).astype(BF16)
        v_t = jnp.concatenate([outs[NSA_GROUPS + g], zeros], axis=1).T[0:HEAD_DIM]
        cvo_ref[g] = jnp.concatenate([v_t, ovt_ref[...], _ones_rows(n, F32)], axis=0).astype(BF16)


def _compress_call(r, posa, posb, wa, wb, w2, ovt):
    B, R, C = r.shape
    G = NSA_GROUPS
    rows = HEAD_DIM + ovt.shape[0] + ONES_ROWS
    full = lambda a: pl.BlockSpec(a.shape, lambda b: (0,) * a.ndim)
    return pl.pallas_call(
        _compress_kernel,
        grid=(B,),
        in_specs=[pl.BlockSpec((None, R, C), lambda b: (b, 0, 0)),
                  full(posa), full(posb), full(wa), full(wb), full(w2), full(ovt)],
        out_specs=(pl.BlockSpec((None, G, R, LANES), lambda b: (b, 0, 0, 0)),
                   pl.BlockSpec((None, G, rows, R), lambda b: (b, 0, 0, 0))),
        out_shape=(jax.ShapeDtypeStruct((B, G, R, LANES), BF16),
                   jax.ShapeDtypeStruct((B, G, rows, R), BF16)),
        compiler_params=pltpu.CompilerParams(
            dimension_semantics=("parallel",), vmem_limit_bytes=VMEM_LIMIT),
        name="compress",
    )(r, posa, posb, wa, wb, w2, ovt)


def _flash_step(s, bias, v_t, m_ref, acc_ref):
    ps, alphas = [], []
    qb = bias.shape[1]
    for h in range(NSA_HPG):
        cols = slice(h * qb, (h + 1) * qb)
        sh = s[:, cols] + bias
        m_old = m_ref[:, cols]
        m_new = jnp.maximum(m_old, jnp.max(sh, axis=0, keepdims=True))
        alphas.append(jnp.exp2(m_old - m_new))
        ps.append(jnp.exp2(sh - m_new).astype(BF16))
        m_ref[:, cols] = m_new
    pv = jnp.dot(v_t, jnp.concatenate(ps, axis=1), preferred_element_type=F32)
    acc_ref[...] = jnp.concatenate(alphas, axis=1) * acc_ref[...] + pv


def _softmax_pv(s, bias, v_t):
    ms, ps = [], []
    qb = bias.shape[1]
    for h in range(NSA_HPG):
        sh = s[:, h * qb:(h + 1) * qb] + bias
        ms.append(jnp.max(sh, axis=0, keepdims=True))
        ps.append(jnp.exp2(sh - ms[h]).astype(BF16))
    r = jnp.dot(v_t, jnp.concatenate(ps, axis=1), preferred_element_type=F32)
    return jnp.concatenate(ms, axis=1), r


def _normalise(acc):
    return acc[0:HEAD_DIM] * (1.0 / jnp.maximum(acc[HEAD_DIM:HEAD_DIM + 1], 1e-30))


def _nsa_kernel(qfeat_ref, q_ref, kc_ref, cvo_ref, ks_ref, vst_ref, kw_ref, vwt_ref, gate_ref,
                o_ref, sel_ref, ms_ref, accs_ref, flag_ref, list_ref):
    n = pl.program_id(2)
    qb, kc = NSA_QB, KEY_CHUNK
    t0 = n * qb
    lane = lax.broadcasted_iota(jnp.int32, (qb, LANES), 1)
    qf = q_ref[...].astype(F32)
    q = jnp.concatenate(
        [jnp.where(lane < HEAD_DIM, _head_tile(qf, h), qfeat_ref[h][0:1, :]) for h in range(NSA_HPG)],
        axis=0).astype(BF16)

    wq = min(WIN_QB, qb)
    span = WIN + wq
    back_w = (lax.broadcasted_iota(jnp.int32, (span, wq), 1)
              - lax.broadcasted_iota(jnp.int32, (span, wq), 0))
    win_parts = []
    for j in range(qb // wq):
        k_lo = pl.multiple_of(jnp.maximum(t0 + j * wq - WIN, 0), kc)
        dist = back_w + (t0 + j * wq - k_lo)
        bias = jnp.where(dist >= 0, jnp.where(dist < WIN, 0.0, -jnp.inf), -jnp.inf)
        q_part = jnp.concatenate([q[h * qb + j * wq:h * qb + (j + 1) * wq] for h in range(NSA_HPG)], axis=0)
        vt_w = jnp.concatenate([vwt_ref[k_lo // kc + u] for u in range(span // kc)], axis=1)
        win_parts.append(_normalise(_softmax_pv(_nt(kw_ref[pl.ds(k_lo, span), :], q_part), bias, vt_w)[1]))
    o_win = jnp.concatenate([part[:, h * wq:(h + 1) * wq] for h in range(NSA_HPG) for part in win_parts],
                            axis=1)

    n_cmp = kc_ref.shape[0]
    n_sel = sel_ref.shape[0] - SEL_PAD_ROWS
    s = _nt(kc_ref[...], q)
    cmp_end = lax.broadcasted_iota(jnp.int32, (n_cmp, qb), 0) * CMP_STRIDE + (CMP_BLOCK - 1)
    bias_c = jnp.where(cmp_end <= t0 + lax.broadcasted_iota(jnp.int32, (n_cmp, qb), 1), 0.0, -jnp.inf)
    es = []
    for h in range(NSA_HPG):
        sh = s[:, h * qb:(h + 1) * qb] + bias_c
        m = jnp.max(sh, axis=0, keepdims=True)
        es.append(jnp.exp2(sh - jnp.where(m == -jnp.inf, 0.0, m)).astype(BF16))
    r = jnp.dot(cvo_ref[...], jnp.concatenate(es, axis=1), preferred_element_type=F32)
    inv_l = 1.0 / jnp.maximum(r[HEAD_DIM + n_sel:HEAD_DIM + n_sel + 1], 1e-30)
    o_cmp = r[0:HEAD_DIM] * inv_l
    imp = r[HEAD_DIM:HEAD_DIM + n_sel] * inv_l
    imp_t = imp[:, 0:qb]
    for h in range(1, NSA_HPG):
        imp_t = imp_t + imp[:, h * qb:(h + 1) * qb]

    blk = lax.broadcasted_iota(jnp.int32, (n_sel, qb), 0).astype(F32)
    cur = ((t0 + lax.broadcasted_iota(jnp.int32, (n_sel, qb), 1)) // SLC_BLOCK).astype(F32)
    ago = cur - blk
    forced = jnp.where(blk == 0, 1.0, jnp.where(ago == 0, 1.0, jnp.where(ago == 1, 1.0, 0.0)))
    valid = ago >= 0
    score = jnp.where(valid, jnp.where(forced > 0.5, -1.0, imp_t), -1.0)
    for _ in range(max(min(SLC_TOPK, n_sel) - N_FORCED, 0)):
        idx = jnp.argmax(score, axis=0, keepdims=True).astype(F32)
        score = jnp.where(blk == idx, -jnp.inf, score)
    sel_t = jnp.where(valid, jnp.where(score == -jnp.inf, 1.0, forced), 0.0)
    sel_ref[0:n_sel, :] = jnp.where(sel_t > 0.5, 0.0, -jnp.inf)
    sel_ref[n_sel:, :] = jnp.full((SEL_PAD_ROWS, qb), -jnp.inf, F32)
    any_q = jnp.max(sel_t, axis=1, keepdims=True)
    bit = jnp.left_shift(1, lax.broadcasted_iota(jnp.int32, (n_sel, 1), 0) % FLAG_BITS).astype(F32)
    packed = any_q * bit
    for i in range(n_sel // FLAG_BITS):
        word = jnp.sum(packed[i * FLAG_BITS:(i + 1) * FLAG_BITS], axis=0, keepdims=True)
        flag_ref[i] = word.astype(jnp.int32)[0, 0]

    blocks_per_chunk = kc // SLC_BLOCK
    chunks_per_word = FLAG_BITS // blocks_per_chunk
    row = lax.broadcasted_iota(jnp.int32, (kc, qb), 0)

    def chunk_bias(first_block):
        bias = sel_ref[pl.ds(first_block, 1), :]
        for j in range(1, blocks_per_chunk):
            bias = jnp.where(row < j * SLC_BLOCK, bias, sel_ref[pl.ds(first_block + j, 1), :])
        return bias

    own = qb // kc
    first_own = t0 // kc
    off_first = jnp.where(n == 0, -jnp.inf, 0.0)
    query = lax.broadcasted_iota(jnp.int32, (kc, qb), 1)
    biases = [chunk_bias(0) + off_first]
    for u in range(own):
        causal = jnp.where(row + u * kc <= query, 0.0, -jnp.inf)
        biases.append(chunk_bias((first_own + u) * blocks_per_chunk) + causal)
    keys = jnp.concatenate([ks_ref[0:kc, :], ks_ref[pl.ds(pl.multiple_of(t0, kc), qb), :]], axis=0)
    vt_s = jnp.concatenate([vst_ref[0]] + [vst_ref[first_own + u] for u in range(own)], axis=1)
    m_s, acc_s = _softmax_pv(_nt(keys, q), jnp.concatenate(biases, axis=0), vt_s)
    ms_ref[...] = m_s
    accs_ref[...] = acc_s

    def scan_body(w, cnt):
        word = flag_ref[w]
        for j in range(chunks_per_word):
            c = w * chunks_per_word + j
            bits = (word >> (j * blocks_per_chunk)) & ((1 << blocks_per_chunk) - 1)
            list_ref[cnt] = c
            cnt = cnt + ((bits != 0) & (c >= 1) & (c < first_own)).astype(jnp.int32)
        return cnt

    cnt = lax.fori_loop(0, (first_own + chunks_per_word - 1) // chunks_per_word, scan_body, 0)
    for u in range(SLC_GROUP):
        list_ref[cnt + u] = -1

    def slc_body(gi, carry):
        keys, biases, vts = [], [], []
        for u in range(SLC_GROUP):
            c = list_ref[gi * SLC_GROUP + u]
            cc = jnp.maximum(c, 0)
            biases.append(chunk_bias(jnp.where(c >= 0, cc * blocks_per_chunk, n_sel)))
            keys.append(ks_ref[pl.ds(pl.multiple_of(cc * kc, kc), kc), :])
            vts.append(vst_ref[cc])
        _flash_step(_nt(jnp.concatenate(keys, axis=0), q), jnp.concatenate(biases, axis=0),
                    jnp.concatenate(vts, axis=1), ms_ref, accs_ref)
        return carry

    lax.fori_loop(0, (cnt + SLC_GROUP - 1) // SLC_GROUP, slc_body, 0)

    o_slc = _normalise(accs_ref[...])
    gate = gate_ref[...]
    outs = []
    for h in range(NSA_HPG):
        cols = slice(h * qb, (h + 1) * qb)
        gc = [gate[j * NSA_HPG + h:j * NSA_HPG + h + 1, :] for j in range(3)]
        outs.append(gc[0] * o_cmp[:, cols] + gc[1] * o_slc[:, cols] + gc[2] * o_win[:, cols])
    o_ref[...] = jnp.concatenate(outs, axis=0).T


def _nsa_call(qfeat, qn, kc, cvo, ks, vst, kw, vwt, gates):
    B, S, _ = qn.shape
    G = NSA_GROUPS
    qb = NSA_QB
    nb = S // qb
    n_sel = S // SLC_BLOCK
    width = NSA_HPG * qb
    per_group = lambda a: pl.BlockSpec((None, None) + a.shape[2:],
                                       lambda b, g, n: (b, g) + (0,) * (a.ndim - 2))
    return pl.pallas_call(
        _nsa_kernel,
        grid=(B, G, nb),
        in_specs=[
            pl.BlockSpec((None,) + qfeat.shape[1:], lambda b, g, n: (g, 0, 0, 0)),
            pl.BlockSpec((None, qb, NSA_HPG * HEAD_DIM), lambda b, g, n: (b, n, g)),
            per_group(kc), per_group(cvo), per_group(ks), per_group(vst), per_group(kw), per_group(vwt),
            pl.BlockSpec((None, None, GATE_ROWS, qb), lambda b, g, n: (b, g, 0, n)),
        ],
        out_specs=pl.BlockSpec((None, qb, NSA_HPG * HEAD_DIM), lambda b, g, n: (b, n, g)),
        out_shape=jax.ShapeDtypeStruct((B, S, NSA_HEADS * HEAD_DIM), F32),
        scratch_shapes=[pltpu.VMEM((n_sel + SEL_PAD_ROWS, qb), F32),
                        pltpu.VMEM((1, width), F32), pltpu.VMEM((VT_ROWS, width), F32),
                        pltpu.SMEM((n_sel // FLAG_BITS,), jnp.int32),
                        pltpu.SMEM((S // KEY_CHUNK + SLC_GROUP,), jnp.int32)],
        compiler_params=pltpu.CompilerParams(
            dimension_semantics=("parallel", "parallel", "arbitrary"), vmem_limit_bytes=VMEM_LIMIT),
        name="nsa",
    )(qfeat, qn, kc, cvo, ks, vst, kw, vwt, gates)


def _dilated_kernel(qfeat_ref, *refs):
    n_pat = len(DIL_PATTERNS)
    j = pl.program_id(1)
    for i, (window, dilation) in enumerate(DIL_PATTERNS):
        steps_per_phase = max(d for _, d in DIL_PATTERNS) // dilation
        _dilated_blocks(qfeat_ref, *refs[5 * i:5 * i + 5], *refs[5 * n_pat + 2 * i:5 * n_pat + 2 * i + 2],
                        wd=window // dilation, dilation=dilation,
                        phase=j // steps_per_phase, step=j % steps_per_phase)


def _dilated_blocks(qfeat_ref, q_ref, kp_ref, kc_ref, vp_ref, vc_ref, o_ref, lse_ref, *,
                    wd, dilation, phase, step):
    c = Q_BLOCK
    hd = HEAD_DIM
    blocks = q_ref.shape[0] // c
    lane = lax.broadcasted_iota(jnp.int32, (blocks * c, LANES), 1)
    key = lax.broadcasted_iota(jnp.int32, (2 * c, c), 0)
    qry = lax.broadcasted_iota(jnp.int32, (2 * c, c), 1)
    first = step == 0
    off_later = jnp.where(key < c, qry - (key + (wd - c)), (key - c) - qry)
    off_first = jnp.where(key < c, jnp.where(first, 2 * c, qry) - (key + (wd - c)), (key - c) - qry)
    bias_later = jnp.where(off_later <= 0, 0.0, -jnp.inf)
    bias_first = jnp.where(off_first <= 0, 0.0, -jnp.inf)
    sub = (step * blocks - 1) * c + lax.broadcasted_iota(jnp.int32, ((blocks + 1) * c, LANES), 0)
    pos_feat = _pos_features(sub * dilation + phase).astype(BF16)
    n_keys = (blocks + 1) * c
    tail = _ones_rows(n_keys, BF16)
    assert PAIR_VT_ROWS == 2 * hd + ONES_ROWS
    for p in range(DIL_HEADS // 2):
        cols = slice(p * LANES, (p + 1) * LANES)
        q_pair = q_ref[:, cols]
        zero = jnp.zeros_like(q_pair)
        feats = [jnp.broadcast_to(qfeat_ref[2 * p + i][0:1, :], (blocks * c, LANES)).astype(BF16)
                 for i in range(2)]
        q_a = jnp.concatenate([jnp.where(lane < hd, q_pair, zero), feats[0]], axis=1)
        q_b = jnp.concatenate([jnp.where(lane < hd, zero, q_pair), feats[1]], axis=1)
        k_all = jnp.concatenate([kp_ref[:, cols], kc_ref[:, cols]], axis=0)
        k_all = jnp.concatenate([k_all, pos_feat], axis=1)
        v_t = jnp.concatenate([jnp.concatenate([vp_ref[:, cols].T, vc_ref[:, cols].T], axis=1), tail],
                              axis=0)
        nb = min(DIL_BAND, blocks)
        for g0 in range(0, blocks, nb):
            q_aug = jnp.concatenate([part[j * c:(j + 1) * c] for j in range(g0, g0 + nb)
                                     for part in (q_a, q_b)], axis=0)
            s_all = _nt(k_all[g0 * c:(g0 + nb + 1) * c], q_aug)
            ms, band = [], []
            for j in range(nb):
                bias = bias_later if g0 + j else bias_first
                probs = []
                for i in range(2):
                    si = s_all[j * c:(j + 2) * c, (2 * j + i) * c:(2 * j + i + 1) * c] + bias
                    ms.append(jnp.max(si, axis=0, keepdims=True))
                    probs.append(jnp.exp2(si - ms[-1]).astype(BF16))
                pieces = [jnp.zeros((j * c, 2 * c), BF16)] if j else []
                pieces.append(jnp.concatenate(probs, axis=1))
                if j < nb - 1:
                    pieces.append(jnp.zeros(((nb - 1 - j) * c, 2 * c), BF16))
                band.append(jnp.concatenate(pieces, axis=0))
            r_all = jnp.dot(v_t[:, g0 * c:(g0 + nb + 1) * c], jnp.concatenate(band, axis=1),
                            preferred_element_type=F32)
            for j in range(nb):
                outs, lses = [], []
                for i in range(2):
                    at = slice((2 * j + i) * c, (2 * j + i + 1) * c)
                    l = r_all[2 * hd:2 * hd + 1, at]
                    outs.append(r_all[i * hd:(i + 1) * hd, at] * (1.0 / l))
                    lses.append(jnp.broadcast_to(ms[2 * j + i] + jnp.log2(l), (hd, c)))
                rows = slice((g0 + j) * c, (g0 + j + 1) * c)
                o_ref[rows, cols] = jnp.concatenate(outs, axis=0).T
                lse_ref[rows, cols] = jnp.concatenate(lses, axis=0).T


def _dilated_call(qfeat, qkv_by_pattern):
    B = qkv_by_pattern[0][0].shape[0]
    steps = max(d for _, d in DIL_PATTERNS)
    in_specs = [pl.BlockSpec(qfeat.shape, lambda b, j: (0, 0, 0))]
    out_specs, out_shape, operands = [], [], [qfeat]
    for (window, d), (q_v, k_v, v_v) in zip(DIL_PATTERNS, qkv_by_pattern):
        assert Q_BLOCK - 1 <= window // d <= Q_BLOCK and steps % d == 0
        L = q_v.shape[1]
        spp = steps // d
        per_step = L // (Q_BLOCK * spp)
        blk = pl.BlockSpec((None, Q_BLOCK * per_step, DIL_Q_W),
                           lambda b, j, spp=spp: (b, j % spp, j // spp))
        prev = pl.BlockSpec((None, Q_BLOCK, DIL_Q_W),
                            lambda b, j, spp=spp, per_step=per_step:
                            (b, jnp.maximum((j % spp) * per_step - 1, 0), j // spp))
        in_specs += [blk, prev, blk, prev, blk]
        operands += [q_v, k_v, k_v, v_v, v_v]
        out_specs += [blk, blk]
        out_shape += [jax.ShapeDtypeStruct((B, L, d * DIL_Q_W), F32)] * 2
    outs = pl.pallas_call(
        _dilated_kernel,
        grid=(B, steps),
        in_specs=in_specs,
        out_specs=tuple(out_specs),
        out_shape=tuple(out_shape),
        compiler_params=pltpu.CompilerParams(
            dimension_semantics=("parallel", "arbitrary"), vmem_limit_bytes=VMEM_LIMIT),
        name="dilated",
    )(*operands)
    return [(outs[2 * i], outs[2 * i + 1]) for i in range(len(DIL_PATTERNS))]


def _merge_kernel(x_ref, g_ref, wg_ref, onsa_ref, o1_ref, o2_ref, o3_ref, l1_ref, l2_ref, l3_ref,
                  wbn_ref, wbd_ref, wo_ref, fg_ref, out_ref, tok_scr, *, final):
    x = x_ref[...]
    tm = x.shape[0]
    xb = _rms(x, g_ref[...]).astype(BF16)
    cz = NSA_HEADS * HEAD_DIM
    d = x.shape[-1]

    def gate_seg(c0, c1):
        return jnp.dot(xb, wg_ref[:, c0:c1], preferred_element_type=F32)

    def token_major(ref, slot, dil):
        if dil == 1:
            return ref[...]
        tiles = DIL_Q_W // LANES
        for p in range(dil):
            for j in range(tiles):
                tok_scr[slot * tiles + j, pl.ds(p, tm // dil, stride=dil), :] = \
                    ref[:, p * DIL_Q_W + j * LANES:p * DIL_Q_W + (j + 1) * LANES]
        return jnp.concatenate([tok_scr[slot * tiles + j] for j in range(tiles)], axis=1)

    o_nsa = onsa_ref[...] * jax.nn.silu(gate_seg(0, cz))
    dils = [dil for _, dil in DIL_PATTERNS]
    o1, o2, o3 = (token_major(r, i, dil) for i, (r, dil) in enumerate(zip((o1_ref, o2_ref, o3_ref), dils)))
    l1, l2, l3 = (token_major(r, 3 + i, dil) for i, (r, dil) in enumerate(zip((l1_ref, l2_ref, l3_ref), dils)))
    mx = jnp.maximum(jnp.maximum(l1, l2), l3)
    e1, e2, e3 = jnp.exp2(l1 - mx), jnp.exp2(l2 - mx), jnp.exp2(l3 - mx)
    inv = 1.0 / (e1 + e2 + e3)
    o_dil = (e1 * inv) * o1 + (e2 * inv) * o2 + (e3 * inv) * o3
    o_dil = o_dil * jax.nn.silu(gate_seg(cz, 2 * cz))
    a = jnp.dot(o_nsa.astype(BF16), wbn_ref[...], preferred_element_type=F32)
    bd = jnp.dot(o_dil.astype(BF16), wbd_ref[...], preferred_element_type=F32)
    merged = (jax.nn.sigmoid(gate_seg(2 * cz, 2 * cz + d)) * a
              + jax.nn.sigmoid(gate_seg(2 * cz + d, 2 * cz + 2 * d)) * bd)
    y = jnp.dot(merged.astype(BF16), wo_ref[...], preferred_element_type=F32)
    out_ref[...] = _rms(x + y, fg_ref[...]) if final else x + y


def _merge_call(x, g, wg, o_nsa, o_d, lse_d, wbn, wbd, wo, fg, tm, final):
    B, S, D = x.shape
    cz = o_nsa.shape[-1]
    row = lambda c: pl.BlockSpec((None, tm, c), lambda b, i: (b, i, 0))
    full = lambda a: pl.BlockSpec(a.shape, lambda b, i: (0,) * a.ndim)
    phase = [pl.BlockSpec((None, tm // dil, dil * cz), lambda b, i: (b, i, 0)) for _, dil in DIL_PATTERNS]
    return pl.pallas_call(
        functools.partial(_merge_kernel, final=final),
        grid=(B, S // tm),
        in_specs=[row(D), full(g), full(wg), row(cz)] + phase + phase
                 + [full(wbn), full(wbd), full(wo), full(fg)],
        out_specs=row(D),
        out_shape=jax.ShapeDtypeStruct((B, S, D), F32),
        scratch_shapes=[pltpu.VMEM((2 * len(DIL_PATTERNS) * cz // LANES, tm, LANES), F32)],
        compiler_params=pltpu.CompilerParams(
            dimension_semantics=("parallel", "parallel"), vmem_limit_bytes=VMEM_LIMIT),
        name="merge",
    )(x, g, wg, o_nsa, *o_d, *lse_d, wbn, wbd, wo, fg)


def _proj_columns(d_model):
    hd, G, H = HEAD_DIM, NSA_GROUPS, NSA_HEADS
    nsa_w = H * hd
    o_q, o_kv = 0, nsa_w
    o_g = o_kv + 6 * G * hd
    o_zn = o_g + 3 * H
    o_qkvd = o_zn + nsa_w
    dil_w = DIL_HEADS * hd
    o_zd = o_qkvd + 3 * dil_w
    o_mg = o_zd + dil_w
    kv = lambda j, g: o_kv + (j * G + g) * hd + np.arange(hd)
    cols = [o_q + np.arange(nsa_w)]
    cols += [kv(j, g) for j in (0, 1) for g in range(G)]
    for j in (2, 4, 3, 5):
        cols += [kv(j, g) for g in range(G)]
    lanes = np.full(LANES, -1)
    for g in range(G):
        for j in range(3):
            lanes[g * GATE_ROWS + j * NSA_HPG + np.arange(NSA_HPG)] = o_g + j * H + g * NSA_HPG + np.arange(NSA_HPG)
    cols.append(lanes)
    cols.append(o_qkvd + np.arange(3 * dil_w))
    proj_cols = np.concatenate(cols)
    assert proj_cols.size == _PROJ_OFFS["vd"][1]
    gate_cols = np.concatenate([o_zn + np.arange(nsa_w), o_zd + np.arange(dil_w),
                                o_mg + np.arange(2 * d_model)])
    return proj_cols, gate_cols


def _take_columns(w, cols, dtype):
    pieces, start = [], 0
    for i in range(1, len(cols) + 1):
        if i < len(cols):
            same_run = (cols[i] < 0 and cols[i - 1] < 0) or (cols[i - 1] >= 0 and cols[i] == cols[i - 1] + 1)
        if i == len(cols) or not same_run:
            n = i - start
            pieces.append(jnp.zeros((w.shape[0], n), w.dtype) if cols[start] < 0
                          else w[:, int(cols[start]):int(cols[start]) + n])
            start = i
    return jnp.concatenate(pieces, axis=1).astype(dtype)


def _compress_weights(pos_k, w1_k, w2_k, pos_v, w1_v, w2_v):
    hd, G, half = HEAD_DIM, NSA_GROUPS, CMP_BLOCK // 2
    w1 = jnp.stack([w1_k] * G + [w1_v] * G).astype(BF16)
    pos = jnp.stack([pos_k] * G + [pos_v] * G)
    w2 = jnp.stack([w2_k] * G + [w2_v] * G).astype(BF16)
    halves = lambda a, tail: (a[:, :half].reshape((2 * G,) + tail), a[:, half:].reshape((2 * G,) + tail))
    posa, posb = halves(pos, (1, half * hd))
    wa, wb = halves(w1, (half * hd, hd))
    return posa, posb, wa, wb, w2


def _overlap_t(n_cmp_pad, n_sel):
    i = np.arange(n_cmp_pad)[None, :]
    j = np.arange(n_sel)[:, None]
    lo = np.maximum(i * CMP_STRIDE, j * SLC_BLOCK)
    hi = np.minimum(i * CMP_STRIDE + CMP_BLOCK, (j + 1) * SLC_BLOCK)
    return (np.clip(hi - lo, 0, None) / CMP_BLOCK).astype(np.float32)


def kernel(x, norm_g, w_in, cmp_pos_k, cmp_w1_k, cmp_w2_k, cmp_pos_v, cmp_w1_v, cmp_w2_v,
           w_br_nsa, w_br_dil, w_out, final_g):
    B, S, D = x.shape
    n_sel = S // SLC_BLOCK
    assert S % (Q_BLOCK * max(d for _, d in DIL_PATTERNS)) == 0 and n_sel % FLAG_BITS == 0
    n_cmp_pad = S // CMP_STRIDE
    slopes_nsa, slopes_dil = _alibi_slopes()
    proj_cols, gate_cols = _proj_columns(D)
    ovt = jnp.asarray(_overlap_t(n_cmp_pad, n_sel))
    qfeat = jnp.asarray(np.broadcast_to(
        _slope_features(slopes_nsa).reshape(NSA_GROUPS, NSA_HPG, 1, LANES),
        (NSA_GROUPS, NSA_HPG, 8, LANES)))
    qfeat_dil = jnp.asarray(np.broadcast_to(
        _slope_features(slopes_dil).reshape(DIL_HEADS, 1, LANES), (DIL_HEADS, 8, LANES)))
    h = x
    for layer in range(w_in.shape[0]):
        w_proj = _take_columns(w_in[layer], proj_cols, BF16)
        w_gate = _take_columns(w_in[layer], gate_cols, BF16)
        g_l = norm_g[layer].reshape(1, D)

        qn, cmp_src, ks, kw, vst, vwt, gates, *dil = _proj_call(h, g_l, w_proj, min(PROJ_ROWS, S))
        cw = _compress_weights(cmp_pos_k[layer], cmp_w1_k[layer], cmp_w2_k[layer],
                               cmp_pos_v[layer], cmp_w1_v[layer], cmp_w2_v[layer])
        kc, cvo = _compress_call(cmp_src, *cw, ovt)
        o_nsa = _nsa_call(qfeat, qn, kc, cvo, ks, vst, kw, vwt, gates)
        dil_out = _dilated_call(qfeat_dil, [dil[3 * i:3 * i + 3] for i in range(len(DIL_PATTERNS))])
        o_d, lse_d = [o for o, _ in dil_out], [lse for _, lse in dil_out]
        h = _merge_call(h, g_l, w_gate, o_nsa, o_d, lse_d, w_br_nsa[layer].astype(BF16),
                        w_br_dil[layer].astype(BF16), w_out[layer].astype(BF16),
                        final_g.reshape(1, D), min(MERGE_ROWS, S), final=layer == w_in.shape[0] - 1)
    return h
```

```python
import functools
import math

import numpy as np
import jax
import jax.numpy as jnp
from jax import lax
from jax.experimental import pallas as pl
from jax.experimental.pallas import tpu as pltpu

F32 = jnp.float32
BF16 = jnp.bfloat16

HEAD_DIM = 64
LANES = 128
NSA_HEADS = 8
NSA_GROUPS = 2
NSA_HPG = NSA_HEADS // NSA_GROUPS
CMP_BLOCK = 32
CMP_STRIDE = 16
SLC_BLOCK = 64
SLC_TOPK = 16
WIN = 512
FORCE_SCORE = 1.0e4
DIL_HEADS = 8
DIL_PATTERNS = ((128, 1), (512, 4), (2048, 16))
Q_BLOCK = 128
RMS_EPS = 1e-6
KEY_CHUNK = 128
VT_ROWS = 80
PAIR_VT_ROWS = 144
GATE_ROWS = 16
FEAT_SPLIT = 3
FLAG_BITS = 16
SLC_GROUP = 4
PROJ_ROWS = 1024
MERGE_ROWS = 512
OUT_ROWS = 1024
PERM_ROWS = 128
WIN_QB = 256
NSA_QB = 512
DIL_BAND = 4
N_FORCED = 3
assert FORCE_SCORE > NSA_HPG + 1
SEL_PAD_ROWS = 8
ONES_ROWS = 16
LOG2E = math.log2(math.e)
VMEM_LIMIT = 56 * 1024 * 1024


def _alibi_slopes():
    n = NSA_HEADS + DIL_HEADS
    s = 2.0 ** (-8.0 * np.arange(1, n + 1) / n)
    return s[0::2].astype(np.float32), s[1::2].astype(np.float32)


def _nt(a, b):
    return lax.dot_general(a, b, (((1,), (1,)), ((), ())), preferred_element_type=F32)


def _rms(x, g):
    return (x * lax.rsqrt(jnp.mean(x * x, axis=-1, keepdims=True) + RMS_EPS)) * g


def _head_tile(q_pairs, h):
    tile = q_pairs[:, (h // 2) * LANES:(h // 2 + 1) * LANES]
    return pltpu.roll(tile, HEAD_DIM, 1) if h % 2 else tile


def _pos_features(pos):
    lane = lax.broadcasted_iota(jnp.int32, pos.shape, 1) - HEAD_DIM
    hi = (pos // LANES).astype(F32)
    lo = (pos % LANES).astype(F32)
    return jnp.where(lane < 0, 0.0, jnp.where(lane < FEAT_SPLIT, hi, jnp.where(lane < 2 * FEAT_SPLIT, lo, 0.0)))


def _slope_features(slopes):
    out = np.zeros((len(slopes), LANES), np.float32)
    for h, s in enumerate(slopes):
        rest = np.float32(s) * np.float32(LOG2E)
        for i in range(FEAT_SPLIT):
            piece = np.float32(np.asarray(rest, np.float32).astype(BF16))
            out[h, HEAD_DIM + FEAT_SPLIT + i] = piece
            out[h, HEAD_DIM + i] = piece * LANES
            rest = np.float32(rest - piece)
    return out


def _ones_rows(width, dtype):
    row = lax.broadcasted_iota(jnp.int32, (ONES_ROWS, width), 0)
    return jnp.where(row == 0, 1.0, 0.0).astype(dtype)


_PROJ_SEGS = (("qn", 512), ("cmp", 256), ("ks", 128), ("kw", 128), ("vs", 128), ("vw", 128),
              ("gate", 128), ("qd", 512), ("kd", 512), ("vd", 512))
DIL_Q_W = DIL_HEADS * HEAD_DIM
_PROJ_OFFS = dict(zip((n for n, _ in _PROJ_SEGS),
                      zip(np.cumsum([0] + [w for _, w in _PROJ_SEGS])[:-1].tolist(),
                          np.cumsum([w for _, w in _PROJ_SEGS]).tolist())))


def _to_lane_tiles(scr, first, val):
    for j in range(val.shape[1] // LANES):
        scr[first + j] = val[:, j * LANES:(j + 1) * LANES]


def _proj_kernel(x_ref, g_ref, w_ref, perm_ref, qn_ref, cmp_ref, ks_ref, kw_ref, vst_ref, vwt_ref,
                 gate_ref, *dil_refs):
    cmp_scr = dil_refs[-1]
    tm = x_ref.shape[0]
    xb = _rms(x_ref[...], g_ref[...]).astype(BF16)
    scale = HEAD_DIM ** -0.5

    def seg(name, last=None):
        c0, c1 = _PROJ_OFFS[name][0], _PROJ_OFFS[last or name][1]
        return jnp.dot(xb, w_ref[:, c0:c1], preferred_element_type=F32)

    qn_ref[...] = (seg("qn") * (scale * LOG2E)).astype(BF16)
    _to_lane_tiles(cmp_scr, 0, seg("cmp"))
    lane_c = lax.broadcasted_iota(jnp.int32, (tm // CMP_STRIDE, LANES), 1)
    slot_w = CMP_STRIDE * HEAD_DIM
    for t in range(cmp_scr.shape[0]):
        for p in range(0, CMP_STRIDE, 2):
            even, odd = (cmp_scr[t, pl.ds(p + i, tm // CMP_STRIDE, stride=CMP_STRIDE), :] for i in range(2))
            for half in range(2):
                pair = (jnp.where(lane_c < HEAD_DIM, even, pltpu.roll(odd, HEAD_DIM, 1)) if half == 0
                        else jnp.where(lane_c < HEAD_DIM, pltpu.roll(even, HEAD_DIM, 1), odd))
                c0 = (2 * t + half) * slot_w + p * HEAD_DIM
                cmp_ref[:, c0:c0 + LANES] = pair
    feat =_pos_features(pl.program_id(1) * tm + lax.broadcasted_iota(jnp.int32, (tm, LANES), 0))
    k_sw, v_sw = seg("ks", "kw"), seg("vs", "vw")
    ks, kw = k_sw[:, 0:LANES], k_sw[:, LANES:]
    vst, vwt = v_sw[:, 0:LANES].T, v_sw[:, LANES:].T
    gate_t = jax.nn.sigmoid(seg("gate")).T
    tail = _ones_rows(tm, F32)
    lane = lax.broadcasted_iota(jnp.int32, (tm, LANES), 1)
    for g in range(NSA_GROUPS):
        ks_ref[g] = jnp.where(lane < HEAD_DIM, _head_tile(ks, g), feat).astype(BF16)
        kw_ref[g] = jnp.where(lane < HEAD_DIM, _head_tile(kw, g), feat).astype(BF16)
        gate_ref[g] = gate_t[g * GATE_ROWS:(g + 1) * GATE_ROWS]
        for src, dst in ((vst, vst_ref), (vwt, vwt_ref)):
            vt = jnp.concatenate([src[g * HEAD_DIM:(g + 1) * HEAD_DIM], tail], axis=0).astype(BF16)
            for j in range(tm // KEY_CHUNK):
                dst[g, j] = vt[:, j * KEY_CHUNK:(j + 1) * KEY_CHUNK]
    qkv = jnp.concatenate([(seg("qd") * (scale * LOG2E)).astype(BF16), seg("kd").astype(BF16),
                           seg("vd").astype(BF16)], axis=1)
    bounds = (0, DIL_Q_W, 2 * DIL_Q_W, 3 * DIL_Q_W)
    subs = tm // PERM_ROWS
    for i, (_, d) in enumerate(DIL_PATTERNS):
        if d > 1:
            by_phase = [jnp.dot(perm_ref[i], qkv[s * PERM_ROWS:(s + 1) * PERM_ROWS],
                                preferred_element_type=F32) for s in range(subs)]
        r = PERM_ROWS // d
        for p in range(d):
            rows = qkv if d == 1 else jnp.concatenate(
                [by_phase[s][p * r:(p + 1) * r] for s in range(subs)], axis=0).astype(BF16)
            for j, ref in enumerate(dil_refs[3 * i:3 * i + 3]):
                width = bounds[j + 1] - bounds[j]
                ref[:, p * width:(p + 1) * width] = rows[:, bounds[j]:bounds[j + 1]]


def _proj_call(x, g, w, tm):
    B, S, D = x.shape
    G = NSA_GROUPS
    cmp_w = _PROJ_OFFS["cmp"][1] - _PROJ_OFFS["cmp"][0]
    row = lambda c: pl.BlockSpec((None, tm, c), lambda b, i: (b, i, 0))
    k_spec = pl.BlockSpec((None, G, tm, LANES), lambda b, i: (b, 0, i, 0))
    vt_spec = pl.BlockSpec((None, G, tm // KEY_CHUNK, VT_ROWS, KEY_CHUNK), lambda b, i: (b, 0, i, 0, 0))
    out_shape = (
        jax.ShapeDtypeStruct((B, S, 512), BF16),
        jax.ShapeDtypeStruct((B, S // CMP_STRIDE, CMP_STRIDE * cmp_w), F32),
        jax.ShapeDtypeStruct((B, G, S, LANES), BF16),
        jax.ShapeDtypeStruct((B, G, S, LANES), BF16),
        jax.ShapeDtypeStruct((B, G, S // KEY_CHUNK, VT_ROWS, KEY_CHUNK), BF16),
        jax.ShapeDtypeStruct((B, G, S // KEY_CHUNK, VT_ROWS, KEY_CHUNK), BF16),
        jax.ShapeDtypeStruct((B, G, GATE_ROWS, S), F32),
    )
    dil_specs = ()
    for _, d in DIL_PATTERNS:
        for width in (DIL_Q_W,) * 3:
            out_shape += (jax.ShapeDtypeStruct((B, S // d, d * width), BF16),)
            dil_specs += (pl.BlockSpec((None, tm // d, d * width), lambda b, i: (b, i, 0)),)
    r = np.arange(PERM_ROWS)
    perm = np.zeros((len(DIL_PATTERNS), PERM_ROWS, PERM_ROWS), np.float32)
    for i, (_, d) in enumerate(DIL_PATTERNS):
        assert (PERM_ROWS // d) % 8 == 0
        perm[i, (r % d) * (PERM_ROWS // d) + r // d, r] = 1.0
    perm = jnp.asarray(perm, BF16)
    return pl.pallas_call(
        _proj_kernel,
        grid=(B, S // tm),
        in_specs=[row(D), pl.BlockSpec((1, D), lambda b, i: (0, 0)),
                  pl.BlockSpec(w.shape, lambda b, i: (0, 0)),
                  pl.BlockSpec(perm.shape, lambda b, i: (0, 0, 0))],
        out_specs=(row(512),
                   pl.BlockSpec((None, tm // CMP_STRIDE, CMP_STRIDE * cmp_w), lambda b, i: (b, i, 0)),
                   k_spec, k_spec, vt_spec, vt_spec,
                   pl.BlockSpec((None, G, GATE_ROWS, tm), lambda b, i: (b, 0, 0, i))) + dil_specs,
        out_shape=out_shape,
        scratch_shapes=[pltpu.VMEM((cmp_w // LANES, tm, LANES), F32)],
        compiler_params=pltpu.CompilerParams(
            dimension_semantics=("parallel", "parallel"), vmem_limit_bytes=VMEM_LIMIT),
        name="proj",
    )(x, g, w, perm)


def _compress_kernel(r_ref, posa_ref, posb_ref, wa_ref, wb_ref, w2_ref, ovt_ref, kc_ref, cvo_ref):
    n = r_ref.shape[0]
    slot_w = CMP_STRIDE * HEAD_DIM
    outs = []
    for s in range(2 * NSA_GROUPS):
        r = r_ref[:, s * slot_w:(s + 1) * slot_w]
        ha = jnp.dot((r + posa_ref[s]).astype(BF16), wa_ref[s], preferred_element_type=F32)
        hb = jnp.dot((r + posb_ref[s]).astype(BF16), wb_ref[s], preferred_element_type=F32)
        hid = jax.nn.gelu(ha + pltpu.roll(hb, n - 1, 0))
        outs.append(jnp.dot(hid.astype(BF16), w2_ref[s], preferred_element_type=F32))
    cmp_end = lax.broadcasted_iota(jnp.int32, (n, LANES), 0) * CMP_STRIDE + (CMP_BLOCK - 1)
    feat = _pos_features(cmp_end)
    zeros = jnp.zeros((n, LANES - HEAD_DIM), F32)
    for g in range(NSA_GROUPS):
        kc_ref[g] = jnp.concatenate([outs[g], feat[:, HEAD_DIM:]], axis=1).astype(BF16)
        v_t = jnp.concatenate([outs[NSA_GROUPS + g], zeros], axis=1).T[0:HEAD_DIM]
        cvo_ref[g] = jnp.concatenate([v_t, ovt_ref[...], _ones_rows(n, F32)], axis=0).astype(BF16)


def _compress_call(r, posa, posb, wa, wb, w2, ovt):
    B, R, C = r.shape
    G = NSA_GROUPS
    rows = HEAD_DIM + ovt.shape[0] + ONES_ROWS
    full = lambda a: pl.BlockSpec(a.shape, lambda b: (0,) * a.ndim)
    return pl.pallas_call(
        _compress_kernel,
        grid=(B,),
        in_specs=[pl.BlockSpec((None, R, C), lambda b: (b, 0, 0)),
                  full(posa), full(posb), full(wa), full(wb), full(w2), full(ovt)],
        out_specs=(pl.BlockSpec((None, G, R, LANES), lambda b: (b, 0, 0, 0)),
                   pl.BlockSpec((None, G, rows, R), lambda b: (b, 0, 0, 0))),
        out_shape=(jax.ShapeDtypeStruct((B, G, R, LANES), BF16),
                   jax.ShapeDtypeStruct((B, G, rows, R), BF16)),
        compiler_params=pltpu.CompilerParams(
            dimension_semantics=("parallel",), vmem_limit_bytes=VMEM_LIMIT),
        name="compress",
    )(r, posa, posb, wa, wb, w2, ovt)


def _flash_step(s, bias, v_t, m_ref, acc_ref):
    ps, alphas = [], []
    qb = bias.shape[1]
    for h in range(NSA_HPG):
        cols = slice(h * qb, (h + 1) * qb)
        sh = s[:, cols] + bias
        m_old = m_ref[:, cols]
        m_new = jnp.maximum(m_old, jnp.max(sh, axis=0, keepdims=True))
        alphas.append(jnp.exp2(m_old - m_new))
        ps.append(jnp.exp2(sh - m_new).astype(BF16))
        m_ref[:, cols] = m_new
    pv = jnp.dot(v_t, jnp.concatenate(ps, axis=1), preferred_element_type=F32)
    acc_ref[...] = jnp.concatenate(alphas, axis=1) * acc_ref[...] + pv


def _softmax_pv(s, bias, v_t):
    ms, ps = [], []
    qb = bias.shape[1]
    for h in range(NSA_HPG):
        sh = s[:, h * qb:(h + 1) * qb] + bias
        ms.append(jnp.max(sh, axis=0, keepdims=True))
        ps.append(jnp.exp2(sh - ms[h]).astype(BF16))
    r = jnp.dot(v_t, jnp.concatenate(ps, axis=1), preferred_element_type=F32)
    return jnp.concatenate(ms, axis=1), r


def _normalise(acc):
    return acc[0:HEAD_DIM] * (1.0 / jnp.maximum(acc[HEAD_DIM:HEAD_DIM + 1], 1e-30))


def _nsa_kernel(qfeat_ref, q_ref, kc_ref, cvo_ref, ks_ref, vst_ref, kw_ref, vwt_ref, gate_ref,
                o_ref, sel_ref, ms_ref, accs_ref, flag_ref, list_ref):
    n = pl.program_id(2)
    qb, kc = NSA_QB, KEY_CHUNK
    t0 = n * qb
    lane = lax.broadcasted_iota(jnp.int32, (qb, LANES), 1)
    qf = q_ref[...].astype(F32)
    q = jnp.concatenate(
        [jnp.where(lane < HEAD_DIM, _head_tile(qf, h), qfeat_ref[h][0:1, :]) for h in range(NSA_HPG)],
        axis=0).astype(BF16)

    wq = min(WIN_QB, qb)
    span = WIN + wq
    back_w = (lax.broadcasted_iota(jnp.int32, (span, wq), 1)
              - lax.broadcasted_iota(jnp.int32, (span, wq), 0))
    win_parts = []
    for j in range(qb // wq):
        k_lo = pl.multiple_of(jnp.maximum(t0 + j * wq - WIN, 0), kc)
        dist = back_w + (t0 + j * wq - k_lo)
        bias = jnp.where(dist >= 0, jnp.where(dist < WIN, 0.0, -jnp.inf), -jnp.inf)
        q_part = jnp.concatenate([q[h * qb + j * wq:h * qb + (j + 1) * wq] for h in range(NSA_HPG)], axis=0)
        vt_w = jnp.concatenate([vwt_ref[k_lo // kc + u] for u in range(span // kc)], axis=1)
        win_parts.append(_normalise(_softmax_pv(_nt(kw_ref[pl.ds(k_lo, span), :], q_part), bias, vt_w)[1]))
    o_win = jnp.concatenate([part[:, h * wq:(h + 1) * wq] for h in range(NSA_HPG) for part in win_parts],
                            axis=1)

    n_cmp = kc_ref.shape[0]
    n_sel = sel_ref.shape[0] - SEL_PAD_ROWS
    s = _nt(kc_ref[...], q)
    cmp_end = lax.broadcasted_iota(jnp.int32, (n_cmp, qb), 0) * CMP_STRIDE + (CMP_BLOCK - 1)
    bias_c = jnp.where(cmp_end <= t0 + lax.broadcasted_iota(jnp.int32, (n_cmp, qb), 1), 0.0, -jnp.inf)
    es = []
    for h in range(NSA_HPG):
        sh = s[:, h * qb:(h + 1) * qb] + bias_c
        m = jnp.max(sh, axis=0, keepdims=True)
        es.append(jnp.exp2(sh - jnp.where(m == -jnp.inf, 0.0, m)).astype(BF16))
    r = jnp.dot(cvo_ref[...], jnp.concatenate(es, axis=1), preferred_element_type=F32)
    inv_l = 1.0 / jnp.maximum(r[HEAD_DIM + n_sel:HEAD_DIM + n_sel + 1], 1e-30)
    o_cmp = r[0:HEAD_DIM] * inv_l
    imp = r[HEAD_DIM:HEAD_DIM + n_sel] * inv_l
    imp_t = imp[:, 0:qb]
    for h in range(1, NSA_HPG):
        imp_t = imp_t + imp[:, h * qb:(h + 1) * qb]

    blk = lax.broadcasted_iota(jnp.int32, (n_sel, qb), 0).astype(F32)
    cur = ((t0 + lax.broadcasted_iota(jnp.int32, (n_sel, qb), 1)) // SLC_BLOCK).astype(F32)
    ago = cur - blk
    forced = jnp.where(blk == 0, 1.0, jnp.where(ago == 0, 1.0, jnp.where(ago == 1, 1.0, 0.0)))
    valid = ago >= 0
    score = jnp.where(valid, jnp.where(forced > 0.5, -1.0, imp_t), -1.0)
    for _ in range(max(min(SLC_TOPK, n_sel) - N_FORCED, 0)):
        idx = jnp.argmax(score, axis=0, keepdims=True).astype(F32)
        score = jnp.where(blk == idx, -jnp.inf, score)
    sel_t = jnp.where(valid, jnp.where(score == -jnp.inf, 1.0, forced), 0.0)
    sel_ref[0:n_sel, :] = jnp.where(sel_t > 0.5, 0.0, -jnp.inf)
    sel_ref[n_sel:, :] = jnp.full((SEL_PAD_ROWS, qb), -jnp.inf, F32)
    any_q = jnp.max(sel_t, axis=1, keepdims=True)
    bit = jnp.left_shift(1, lax.broadcasted_iota(jnp.int32, (n_sel, 1), 0) % FLAG_BITS).astype(F32)
    packed = any_q * bit
    for i in range(n_sel // FLAG_BITS):
        word = jnp.sum(packed[i * FLAG_BITS:(i + 1) * FLAG_BITS], axis=0, keepdims=True)
        flag_ref[i] = word.astype(jnp.int32)[0, 0]

    blocks_per_chunk = kc // SLC_BLOCK
    chunks_per_word = FLAG_BITS // blocks_per_chunk
    row = lax.broadcasted_iota(jnp.int32, (kc, qb), 0)

    def chunk_bias(first_block):
        bias = sel_ref[pl.ds(first_block, 1), :]
        for j in range(1, blocks_per_chunk):
            bias = jnp.where(row < j * SLC_BLOCK, bias, sel_ref[pl.ds(first_block + j, 1), :])
        return bias

    own = qb // kc
    first_own = t0 // kc
    off_first = jnp.where(n == 0, -jnp.inf, 0.0)
    query = lax.broadcasted_iota(jnp.int32, (kc, qb), 1)
    biases = [chunk_bias(0) + off_first]
    for u in range(own):
        causal = jnp.where(row + u * kc <= query, 0.0, -jnp.inf)
        biases.append(chunk_bias((first_own + u) * blocks_per_chunk) + causal)
    keys = jnp.concatenate([ks_ref[0:kc, :], ks_ref[pl.ds(pl.multiple_of(t0, kc), qb), :]], axis=0)
    vt_s = jnp.concatenate([vst_ref[0]] + [vst_ref[first_own + u] for u in range(own)], axis=1)
    m_s, acc_s = _softmax_pv(_nt(keys, q), jnp.concatenate(biases, axis=0), vt_s)
    ms_ref[...] = m_s
    accs_ref[...] = acc_s

    def scan_body(w, cnt):
        word = flag_ref[w]
        for j in range(chunks_per_word):
            c = w * chunks_per_word + j
            bits = (word >> (j * blocks_per_chunk)) & ((1 << blocks_per_chunk) - 1)
            list_ref[cnt] = c
            cnt = cnt + ((bits != 0) & (c >= 1) & (c < first_own)).astype(jnp.int32)
        return cnt

    cnt = lax.fori_loop(0, (first_own + chunks_per_word - 1) // chunks_per_word, scan_body, 0)
    for u in range(SLC_GROUP):
        list_ref[cnt + u] = -1

    def slc_body(gi, carry):
        keys, biases, vts = [], [], []
        for u in range(SLC_GROUP):
            c = list_ref[gi * SLC_GROUP + u]
            cc = jnp.maximum(c, 0)
            biases.append(chunk_bias(jnp.where(c >= 0, cc * blocks_per_chunk, n_sel)))
            keys.append(ks_ref[pl.ds(pl.multiple_of(cc * kc, kc), kc), :])
            vts.append(vst_ref[cc])
        _flash_step(_nt(jnp.concatenate(keys, axis=0), q), jnp.concatenate(biases, axis=0),
                    jnp.concatenate(vts, axis=1), ms_ref, accs_ref)
        return carry

    lax.fori_loop(0, (cnt + SLC_GROUP - 1) // SLC_GROUP, slc_body, 0)

    o_slc = _normalise(accs_ref[...])
    gate = gate_ref[...]
    outs = []
    for h in range(NSA_HPG):
        cols = slice(h * qb, (h + 1) * qb)
        gc = [gate[j * NSA_HPG + h:j * NSA_HPG + h + 1, :] for j in range(3)]
        outs.append(gc[0] * o_cmp[:, cols] + gc[1] * o_slc[:, cols] + gc[2] * o_win[:, cols])
    o_ref[...] = jnp.concatenate(outs, axis=0).T


def _nsa_call(qfeat, qn, kc, cvo, ks, vst, kw, vwt, gates):
    B, S, _ = qn.shape
    G = NSA_GROUPS
    qb = NSA_QB
    nb = S // qb
    n_sel = S // SLC_BLOCK
    width = NSA_HPG * qb
    per_group = lambda a: pl.BlockSpec((None, None) + a.shape[2:],
                                       lambda b, g, n: (b, g) + (0,) * (a.ndim - 2))
    return pl.pallas_call(
        _nsa_kernel,
        grid=(B, G, nb),
        in_specs=[
            pl.BlockSpec((None,) + qfeat.shape[1:], lambda b, g, n: (g, 0, 0, 0)),
            pl.BlockSpec((None, qb, NSA_HPG * HEAD_DIM), lambda b, g, n: (b, n, g)),
            per_group(kc), per_group(cvo), per_group(ks), per_group(vst), per_group(kw), per_group(vwt),
            pl.BlockSpec((None, None, GATE_ROWS, qb), lambda b, g, n: (b, g, 0, n)),
        ],
        out_specs=pl.BlockSpec((None, qb, NSA_HPG * HEAD_DIM), lambda b, g, n: (b, n, g)),
        out_shape=jax.ShapeDtypeStruct((B, S, NSA_HEADS * HEAD_DIM), F32),
        scratch_shapes=[pltpu.VMEM((n_sel + SEL_PAD_ROWS, qb), F32),
                        pltpu.VMEM((1, width), F32), pltpu.VMEM((VT_ROWS, width), F32),
                        pltpu.SMEM((n_sel // FLAG_BITS,), jnp.int32),
                        pltpu.SMEM((S // KEY_CHUNK + SLC_GROUP,), jnp.int32)],
        compiler_params=pltpu.CompilerParams(
            dimension_semantics=("parallel", "parallel", "arbitrary"), vmem_limit_bytes=VMEM_LIMIT),
        name="nsa",
    )(qfeat, qn, kc, cvo, ks, vst, kw, vwt, gates)


def _dilated_kernel(qfeat_ref, *refs):
    n_pat = len(DIL_PATTERNS)
    j = pl.program_id(1)
    for i, (window, dilation) in enumerate(DIL_PATTERNS):
        steps_per_phase = max(d for _, d in DIL_PATTERNS) // dilation
        _dilated_blocks(qfeat_ref, *refs[5 * i:5 * i + 5], *refs[5 * n_pat + 2 * i:5 * n_pat + 2 * i + 2],
                        wd=window // dilation, dilation=dilation,
                        phase=j // steps_per_phase, step=j % steps_per_phase)


def _dilated_blocks(qfeat_ref, q_ref, kp_ref, kc_ref, vp_ref, vc_ref, o_ref, lse_ref, *,
                    wd, dilation, phase, step):
    c = Q_BLOCK
    hd = HEAD_DIM
    blocks = q_ref.shape[0] // c
    lane = lax.broadcasted_iota(jnp.int32, (blocks * c, LANES), 1)
    key = lax.broadcasted_iota(jnp.int32, (2 * c, c), 0)
    qry = lax.broadcasted_iota(jnp.int32, (2 * c, c), 1)
    first = step == 0
    off_later = jnp.where(key < c, qry - (key + (wd - c)), (key - c) - qry)
    off_first = jnp.where(key < c, jnp.where(first, 2 * c, qry) - (key + (wd - c)), (key - c) - qry)
    bias_later = jnp.where(off_later <= 0, 0.0, -jnp.inf)
    bias_first = jnp.where(off_first <= 0, 0.0, -jnp.inf)
    sub = (step * blocks - 1) * c + lax.broadcasted_iota(jnp.int32, ((blocks + 1) * c, LANES), 0)
    pos_feat = _pos_features(sub * dilation + phase).astype(BF16)
    n_keys = (blocks + 1) * c
    tail = _ones_rows(n_keys, BF16)
    assert PAIR_VT_ROWS == 2 * hd + ONES_ROWS
    for p in range(DIL_HEADS // 2):
        cols = slice(p * LANES, (p + 1) * LANES)
        q_pair = q_ref[:, cols]
        zero = jnp.zeros_like(q_pair)
        feats = [jnp.broadcast_to(qfeat_ref[2 * p + i][0:1, :], (blocks * c, LANES)).astype(BF16)
                 for i in range(2)]
        q_a = jnp.concatenate([jnp.where(lane < hd, q_pair, zero), feats[0]], axis=1)
        q_b = jnp.concatenate([jnp.where(lane < hd, zero, q_pair), feats[1]], axis=1)
        k_all = jnp.concatenate([kp_ref[:, cols], kc_ref[:, cols]], axis=0)
        k_all = jnp.concatenate([k_all, pos_feat], axis=1)
        v_t = jnp.concatenate([jnp.concatenate([vp_ref[:, cols].T, vc_ref[:, cols].T], axis=1), tail],
                              axis=0)
        nb = min(DIL_BAND, blocks)
        for g0 in range(0, blocks, nb):
            q_aug = jnp.concatenate([part[j * c:(j + 1) * c] for j in range(g0, g0 + nb)
                                     for part in (q_a, q_b)], axis=0)
            s_all = _nt(k_all[g0 * c:(g0 + nb + 1) * c], q_aug)
            ms, band = [], []
            for j in range(nb):
                bias = bias_later if g0 + j else bias_first
                probs = []
                for i in range(2):
                    si = s_all[j * c:(j + 2) * c, (2 * j + i) * c:(2 * j + i + 1) * c] + bias
                    ms.append(jnp.max(si, axis=0, keepdims=True))
                    probs.append(jnp.exp2(si - ms[-1]).astype(BF16))
                pieces = [jnp.zeros((j * c, 2 * c), BF16)] if j else []
                pieces.append(jnp.concatenate(probs, axis=1))
                if j < nb - 1:
                    pieces.append(jnp.zeros(((nb - 1 - j) * c, 2 * c), BF16))
                band.append(jnp.concatenate(pieces, axis=0))
            r_all = jnp.dot(v_t[:, g0 * c:(g0 + nb + 1) * c], jnp.concatenate(band, axis=1),
                            preferred_element_type=F32)
            for j in range(nb):
                outs, lses = [], []
                for i in range(2):
                    at = slice((2 * j + i) * c, (2 * j + i + 1) * c)
                    l = r_all[2 * hd:2 * hd + 1, at]
                    outs.append(r_all[i * hd:(i + 1) * hd, at] * (1.0 / l))
                    lses.append(jnp.broadcast_to(ms[2 * j + i] + jnp.log2(l), (hd, c)))
                rows = slice((g0 + j) * c, (g0 + j + 1) * c)
                o_ref[rows, cols] = jnp.concatenate(outs, axis=0).T
                lse_ref[rows, cols] = jnp.concatenate(lses, axis=0).T


def _dilated_call(qfeat, qkv_by_pattern):
    B = qkv_by_pattern[0][0].shape[0]
    steps = max(d for _, d in DIL_PATTERNS)
    in_specs = [pl.BlockSpec(qfeat.shape, lambda b, j: (0, 0, 0))]
    out_specs, out_shape, operands = [], [], [qfeat]
    for (window, d), (q_v, k_v, v_v) in zip(DIL_PATTERNS, qkv_by_pattern):
        assert Q_BLOCK - 1 <= window // d <= Q_BLOCK and steps % d == 0
        L = q_v.shape[1]
        spp = steps // d
        per_step = L // (Q_BLOCK * spp)
        blk = pl.BlockSpec((None, Q_BLOCK * per_step, DIL_Q_W),
                           lambda b, j, spp=spp: (b, j % spp, j // spp))
        prev = pl.BlockSpec((None, Q_BLOCK, DIL_Q_W),
                            lambda b, j, spp=spp, per_step=per_step:
                            (b, jnp.maximum((j % spp) * per_step - 1, 0), j // spp))
        in_specs += [blk, prev, blk, prev, blk]
        operands += [q_v, k_v, k_v, v_v, v_v]
        out_specs += [blk, blk]
        out_shape += [jax.ShapeDtypeStruct((B, L, d * DIL_Q_W), F32)] * 2
    outs = pl.pallas_call(
        _dilated_kernel,
        grid=(B, steps),
        in_specs=in_specs,
        out_specs=tuple(out_specs),
        out_shape=tuple(out_shape),
        compiler_params=pltpu.CompilerParams(
            dimension_semantics=("parallel", "arbitrary"), vmem_limit_bytes=VMEM_LIMIT),
        name="dilated",
    )(*operands)
    return [(outs[2 * i], outs[2 * i + 1]) for i in range(len(DIL_PATTERNS))]


def _merge_kernel(x_ref, g_ref, wg_ref, onsa_ref, o1_ref, o2_ref, o3_ref, l1_ref, l2_ref, l3_ref,
                  wbn_ref, wbd_ref, merged_ref, tok_scr):
    x = x_ref[...]
    tm = x.shape[0]
    xb = _rms(x, g_ref[...]).astype(BF16)
    cz = NSA_HEADS * HEAD_DIM
    d = x.shape[-1]

    def gate_seg(c0, c1):
        return jnp.dot(xb, wg_ref[:, c0:c1], preferred_element_type=F32)

    def token_major(ref, slot, dil):
        if dil == 1:
            return ref[...]
        tiles = DIL_Q_W // LANES
        for p in range(dil):
            for j in range(tiles):
                tok_scr[slot * tiles + j, pl.ds(p, tm // dil, stride=dil), :] = \
                    ref[:, p * DIL_Q_W + j * LANES:p * DIL_Q_W + (j + 1) * LANES]
        return jnp.concatenate([tok_scr[slot * tiles + j] for j in range(tiles)], axis=1)

    o_nsa = onsa_ref[...] * jax.nn.silu(gate_seg(0, cz))
    dils = [dil for _, dil in DIL_PATTERNS]
    o1, o2, o3 = (token_major(r, i, dil) for i, (r, dil) in enumerate(zip((o1_ref, o2_ref, o3_ref), dils)))
    l1, l2, l3 = (token_major(r, 3 + i, dil) for i, (r, dil) in enumerate(zip((l1_ref, l2_ref, l3_ref), dils)))
    mx = jnp.maximum(jnp.maximum(l1, l2), l3)
    e1, e2, e3 = jnp.exp2(l1 - mx), jnp.exp2(l2 - mx), jnp.exp2(l3 - mx)
    inv = 1.0 / (e1 + e2 + e3)
    o_dil = (e1 * inv) * o1 + (e2 * inv) * o2 + (e3 * inv) * o3
    o_dil = o_dil * jax.nn.silu(gate_seg(cz, 2 * cz))
    a = jnp.dot(o_nsa.astype(BF16), wbn_ref[...], preferred_element_type=F32)
    bd = jnp.dot(o_dil.astype(BF16), wbd_ref[...], preferred_element_type=F32)
    merged = (jax.nn.sigmoid(gate_seg(2 * cz, 2 * cz + d)) * a
              + jax.nn.sigmoid(gate_seg(2 * cz + d, 2 * cz + 2 * d)) * bd)
    merged_ref[...] = merged.astype(BF16)


def _out_kernel(x_ref, merged_ref, wo_ref, fg_ref, out_ref, *, final):
    h = x_ref[...] + jnp.dot(merged_ref[...], wo_ref[...], preferred_element_type=F32)
    out_ref[...] = _rms(h, fg_ref[...]) if final else h


def _merge_call(x, g, wg, o_nsa, o_d, lse_d, wbn, wbd, wo, fg, tm, final):
    B, S, D = x.shape
    cz = o_nsa.shape[-1]
    row = lambda c: pl.BlockSpec((None, tm, c), lambda b, i: (b, i, 0))
    full = lambda a: pl.BlockSpec(a.shape, lambda b, i: (0,) * a.ndim)
    phase = [pl.BlockSpec((None, tm // dil, dil * cz), lambda b, i: (b, i, 0)) for _, dil in DIL_PATTERNS]
    merged = pl.pallas_call(
        _merge_kernel,
        grid=(B, S // tm),
        in_specs=[row(D), full(g), full(wg), row(cz)] + phase + phase + [full(wbn), full(wbd)],
        out_specs=row(D),
        out_shape=jax.ShapeDtypeStruct((B, S, D), BF16),
        scratch_shapes=[pltpu.VMEM((2 * len(DIL_PATTERNS) * cz // LANES, tm, LANES), F32)],
        compiler_params=pltpu.CompilerParams(
            dimension_semantics=("parallel", "parallel"), vmem_limit_bytes=VMEM_LIMIT),
        name="merge",
    )(x, g, wg, o_nsa, *o_d, *lse_d, wbn, wbd)
    to = min(OUT_ROWS, S)
    orow = lambda c: pl.BlockSpec((None, to, c), lambda b, i: (b, i, 0))
    return pl.pallas_call(
        functools.partial(_out_kernel, final=final),
        grid=(B, S // to),
        in_specs=[orow(D), orow(D), full(wo), full(fg)],
        out_specs=orow(D),
        out_shape=jax.ShapeDtypeStruct((B, S, D), F32),
        compiler_params=pltpu.CompilerParams(
            dimension_semantics=("parallel", "parallel"), vmem_limit_bytes=VMEM_LIMIT),
        name="out_proj",
    )(x, merged, wo, fg)


def _proj_columns(d_model):
    hd, G, H = HEAD_DIM, NSA_GROUPS, NSA_HEADS
    nsa_w = H * hd
    o_q, o_kv = 0, nsa_w
    o_g = o_kv + 6 * G * hd
    o_zn = o_g + 3 * H
    o_qkvd = o_zn + nsa_w
    dil_w = DIL_HEADS * hd
    o_zd = o_qkvd + 3 * dil_w
    o_mg = o_zd + dil_w
    kv = lambda j, g: o_kv + (j * G + g) * hd + np.arange(hd)
    cols = [o_q + np.arange(nsa_w)]
    cols += [kv(j, g) for j in (0, 1) for g in range(G)]
    for j in (2, 4, 3, 5):
        cols += [kv(j, g) for g in range(G)]
    lanes = np.full(LANES, -1)
    for g in range(G):
        for j in range(3):
            lanes[g * GATE_ROWS + j * NSA_HPG + np.arange(NSA_HPG)] = o_g + j * H + g * NSA_HPG + np.arange(NSA_HPG)
    cols.append(lanes)
    cols.append(o_qkvd + np.arange(3 * dil_w))
    proj_cols = np.concatenate(cols)
    assert proj_cols.size == _PROJ_OFFS["vd"][1]
    gate_cols = np.concatenate([o_zn + np.arange(nsa_w), o_zd + np.arange(dil_w),
                                o_mg + np.arange(2 * d_model)])
    return proj_cols, gate_cols


def _take_columns(w, cols, dtype):
    pieces, start = [], 0
    for i in range(1, len(cols) + 1):
        if i < len(cols):
            same_run = (cols[i] < 0 and cols[i - 1] < 0) or (cols[i - 1] >= 0 and cols[i] == cols[i - 1] + 1)
        if i == len(cols) or not same_run:
            n = i - start
            pieces.append(jnp.zeros((w.shape[0], n), w.dtype) if cols[start] < 0
                          else w[:, int(cols[start]):int(cols[start]) + n])
            start = i
    return jnp.concatenate(pieces, axis=1).astype(dtype)


def _compress_weights(pos_k, w1_k, w2_k, pos_v, w1_v, w2_v):
    hd, G, half = HEAD_DIM, NSA_GROUPS, CMP_BLOCK // 2
    w1 = jnp.stack([w1_k] * G + [w1_v] * G).astype(BF16)
    pos = jnp.stack([pos_k] * G + [pos_v] * G)
    w2 = jnp.stack([w2_k] * G + [w2_v] * G).astype(BF16)
    halves = lambda a, tail: (a[:, :half].reshape((2 * G,) + tail), a[:, half:].reshape((2 * G,) + tail))
    posa, posb = halves(pos, (1, half * hd))
    wa, wb = halves(w1, (half * hd, hd))
    return posa, posb, wa, wb, w2


def _overlap_t(n_cmp_pad, n_sel):
    i = np.arange(n_cmp_pad)[None, :]
    j = np.arange(n_sel)[:, None]
    lo = np.maximum(i * CMP_STRIDE, j * SLC_BLOCK)
    hi = np.minimum(i * CMP_STRIDE + CMP_BLOCK, (j + 1) * SLC_BLOCK)
    return (np.clip(hi - lo, 0, None) / CMP_BLOCK).astype(np.float32)


def kernel(x, norm_g, w_in, cmp_pos_k, cmp_w1_k, cmp_w2_k, cmp_pos_v, cmp_w1_v, cmp_w2_v,
           w_br_nsa, w_br_dil, w_out, final_g):
    B, S, D = x.shape
    n_sel = S // SLC_BLOCK
    assert S % (Q_BLOCK * max(d for _, d in DIL_PATTERNS)) == 0 and n_sel % FLAG_BITS == 0
    n_cmp_pad = S // CMP_STRIDE
    slopes_nsa, slopes_dil = _alibi_slopes()
    proj_cols, gate_cols = _proj_columns(D)
    ovt = jnp.asarray(_overlap_t(n_cmp_pad, n_sel))
    qfeat = jnp.asarray(np.broadcast_to(
        _slope_features(slopes_nsa).reshape(NSA_GROUPS, NSA_HPG, 1, LANES),
        (NSA_GROUPS, NSA_HPG, 8, LANES)))
    qfeat_dil = jnp.asarray(np.broadcast_to(
        _slope_features(slopes_dil).reshape(DIL_HEADS, 1, LANES), (DIL_HEADS, 8, LANES)))
    h = x
    for layer in range(w_in.shape[0]):
        w_proj = _take_columns(w_in[layer], proj_cols, BF16)
        w_gate = _take_columns(w_in[layer], gate_cols, BF16)
        g_l = norm_g[layer].reshape(1, D)

        qn, cmp_src, ks, kw, vst, vwt, gates, *dil = _proj_call(h, g_l, w_proj, min(PROJ_ROWS, S))
        cw = _compress_weights(cmp_pos_k[layer], cmp_w1_k[layer], cmp_w2_k[layer],
                               cmp_pos_v[layer], cmp_w1_v[layer], cmp_w2_v[layer])
        kc, cvo = _compress_call(cmp_src, *cw, ovt)
        o_nsa = _nsa_call(qfeat, qn, kc, cvo, ks, vst, kw, vwt, gates)
        dil_out = _dilated_call(qfeat_dil, [dil[3 * i:3 * i + 3] for i in range(len(DIL_PATTERNS))])
        o_d, lse_d = [o for o, _ in dil_out], [lse for _, lse in dil_out]
        h = _merge_call(h, g_l, w_gate, o_nsa, o_d, lse_d, w_br_nsa[layer].astype(BF16),
                        w_br_dil[layer].astype(BF16), w_out[layer].astype(BF16),
                        final_g.reshape(1, D), min(MERGE_ROWS, S), final=layer == w_in.shape[0] - 1)
    return h
```

```python
import functools
import math

import numpy as np
import jax
import jax.numpy as jnp
from jax import lax
from jax.experimental import pallas as pl
from jax.experimental.pallas import tpu as pltpu

F32 = jnp.float32
BF16 = jnp.bfloat16

HEAD_DIM = 64
LANES = 128
NSA_HEADS = 8
NSA_GROUPS = 2
NSA_HPG = NSA_HEADS // NSA_GROUPS
CMP_BLOCK = 32
CMP_STRIDE = 16
SLC_BLOCK = 64
SLC_TOPK = 16
WIN = 512
FORCE_SCORE = 1.0e4
DIL_HEADS = 8
DIL_PATTERNS = ((128, 1), (512, 4), (2048, 16))
Q_BLOCK = 128
RMS_EPS = 1e-6
KEY_CHUNK = 128
VT_ROWS = 80
PAIR_VT_ROWS = 144
GATE_ROWS = 16
FEAT_SPLIT = 3
FLAG_BITS = 16
SLC_GROUP = 4
PROJ_ROWS = 1024
MERGE_ROWS = 512
PERM_ROWS = 128
WIN_QB = 256
NSA_QB = 512
DIL_BAND = 4
N_FORCED = 3
assert FORCE_SCORE > NSA_HPG + 1
SEL_PAD_ROWS = 8
ONES_ROWS = 16
LOG2E = math.log2(math.e)
VMEM_LIMIT = 56 * 1024 * 1024


def _alibi_slopes():
    n = NSA_HEADS + DIL_HEADS
    s = 2.0 ** (-8.0 * np.arange(1, n + 1) / n)
    return s[0::2].astype(np.float32), s[1::2].astype(np.float32)


def _nt(a, b):
    return lax.dot_general(a, b, (((1,), (1,)), ((), ())), preferred_element_type=F32)


def _rms(x, g):
    return (x * lax.rsqrt(jnp.mean(x * x, axis=-1, keepdims=True) + RMS_EPS)) * g


def _sigmoid(x):
    return 0.5 + 0.5 * jnp.tanh(0.5 * x)


def _head_tile(q_pairs, h):
    tile = q_pairs[:, (h // 2) * LANES:(h // 2 + 1) * LANES]
    return pltpu.roll(tile, HEAD_DIM, 1) if h % 2 else tile


def _pos_features(pos):
    lane = lax.broadcasted_iota(jnp.int32, pos.shape, 1) - HEAD_DIM
    hi = (pos // LANES).astype(F32)
    lo = (pos % LANES).astype(F32)
    return jnp.where(lane < 0, 0.0, jnp.where(lane < FEAT_SPLIT, hi, jnp.where(lane < 2 * FEAT_SPLIT, lo, 0.0)))


def _slope_features(slopes):
    out = np.zeros((len(slopes), LANES), np.float32)
    for h, s in enumerate(slopes):
        rest = np.float32(s) * np.float32(LOG2E)
        for i in range(FEAT_SPLIT):
            piece = np.float32(np.asarray(rest, np.float32).astype(BF16))
            out[h, HEAD_DIM + FEAT_SPLIT + i] = piece
            out[h, HEAD_DIM + i] = piece * LANES
            rest = np.float32(rest - piece)
    return out


def _ones_rows(width, dtype):
    row = lax.broadcasted_iota(jnp.int32, (ONES_ROWS, width), 0)
    return jnp.where(row == 0, 1.0, 0.0).astype(dtype)


_PROJ_SEGS = (("qn", 512), ("cmp", 256), ("ks", 128), ("kw", 128), ("vs", 128), ("vw", 128),
              ("gate", 128), ("qd", 512), ("kd", 512), ("vd", 512))
DIL_Q_W = DIL_HEADS * HEAD_DIM
_PROJ_OFFS = dict(zip((n for n, _ in _PROJ_SEGS),
                      zip(np.cumsum([0] + [w for _, w in _PROJ_SEGS])[:-1].tolist(),
                          np.cumsum([w for _, w in _PROJ_SEGS]).tolist())))


def _to_lane_tiles(scr, first, val):
    for j in range(val.shape[1] // LANES):
        scr[first + j] = val[:, j * LANES:(j + 1) * LANES]


def _proj_kernel(x_ref, g_ref, w_ref, perm_ref, qn_ref, cmp_ref, ks_ref, kw_ref, vst_ref, vwt_ref,
                 gate_ref, *dil_refs):
    cmp_scr = dil_refs[-1]
    tm = x_ref.shape[0]
    xb = _rms(x_ref[...], g_ref[...]).astype(BF16)
    scale = HEAD_DIM ** -0.5

    def seg(name, last=None):
        c0, c1 = _PROJ_OFFS[name][0], _PROJ_OFFS[last or name][1]
        return jnp.dot(xb, w_ref[:, c0:c1], preferred_element_type=F32)

    qn_ref[...] = (seg("qn") * (scale * LOG2E)).astype(BF16)
    _to_lane_tiles(cmp_scr, 0, seg("cmp"))
    lane_c = lax.broadcasted_iota(jnp.int32, (tm // CMP_STRIDE, LANES), 1)
    slot_w = CMP_STRIDE * HEAD_DIM
    for t in range(cmp_scr.shape[0]):
        for p in range(0, CMP_STRIDE, 2):
            even, odd = (cmp_scr[t, pl.ds(p + i, tm // CMP_STRIDE, stride=CMP_STRIDE), :] for i in range(2))
            for half in range(2):
                pair = (jnp.where(lane_c < HEAD_DIM, even, pltpu.roll(odd, HEAD_DIM, 1)) if half == 0
                        else jnp.where(lane_c < HEAD_DIM, pltpu.roll(even, HEAD_DIM, 1), odd))
                c0 = (2 * t + half) * slot_w + p * HEAD_DIM
                cmp_ref[:, c0:c0 + LANES] = pair
    feat =_pos_features(pl.program_id(1) * tm + lax.broadcasted_iota(jnp.int32, (tm, LANES), 0))
    k_sw, v_sw = seg("ks", "kw"), seg("vs", "vw")
    ks, kw = k_sw[:, 0:LANES], k_sw[:, LANES:]
    vst, vwt = v_sw[:, 0:LANES].T, v_sw[:, LANES:].T
    gate_t = _sigmoid(seg("gate")).T
    tail = _ones_rows(tm, F32)
    lane = lax.broadcasted_iota(jnp.int32, (tm, LANES), 1)
    for g in range(NSA_GROUPS):
        ks_ref[g] = jnp.where(lane < HEAD_DIM, _head_tile(ks, g), feat).astype(BF16)
        kw_ref[g] = jnp.where(lane < HEAD_DIM, _head_tile(kw, g), feat).astype(BF16)
        gate_ref[g] = gate_t[g * GATE_ROWS:(g + 1) * GATE_ROWS]
        for src, dst in ((vst, vst_ref), (vwt, vwt_ref)):
            vt = jnp.concatenate([src[g * HEAD_DIM:(g + 1) * HEAD_DIM], tail], axis=0).astype(BF16)
            for j in range(tm // KEY_CHUNK):
                dst[g, j] = vt[:, j * KEY_CHUNK:(j + 1) * KEY_CHUNK]
    qkv = jnp.concatenate([(seg("qd") * (scale * LOG2E)).astype(BF16), seg("kd").astype(BF16),
                           seg("vd").astype(BF16)], axis=1)
    bounds = (0, DIL_Q_W, 2 * DIL_Q_W, 3 * DIL_Q_W)
    subs = tm // PERM_ROWS
    for i, (_, d) in enumerate(DIL_PATTERNS):
        if d > 1:
            by_phase = [jnp.dot(perm_ref[i], qkv[s * PERM_ROWS:(s + 1) * PERM_ROWS],
                                preferred_element_type=F32) for s in range(subs)]
        r = PERM_ROWS // d
        for p in range(d):
            rows = qkv if d == 1 else jnp.concatenate(
                [by_phase[s][p * r:(p + 1) * r] for s in range(subs)], axis=0).astype(BF16)
            for j, ref in enumerate(dil_refs[3 * i:3 * i + 3]):
                width = bounds[j + 1] - bounds[j]
                ref[:, p * width:(p + 1) * width] = rows[:, bounds[j]:bounds[j + 1]]


def _proj_call(x, g, w, tm):
    B, S, D = x.shape
    G = NSA_GROUPS
    cmp_w = _PROJ_OFFS["cmp"][1] - _PROJ_OFFS["cmp"][0]
    row = lambda c: pl.BlockSpec((None, tm, c), lambda b, i: (b, i, 0))
    k_spec = pl.BlockSpec((None, G, tm, LANES), lambda b, i: (b, 0, i, 0))
    vt_spec = pl.BlockSpec((None, G, tm // KEY_CHUNK, VT_ROWS, KEY_CHUNK), lambda b, i: (b, 0, i, 0, 0))
    out_shape = (
        jax.ShapeDtypeStruct((B, S, 512), BF16),
        jax.ShapeDtypeStruct((B, S // CMP_STRIDE, CMP_STRIDE * cmp_w), F32),
        jax.ShapeDtypeStruct((B, G, S, LANES), BF16),
        jax.ShapeDtypeStruct((B, G, S, LANES), BF16),
        jax.ShapeDtypeStruct((B, G, S // KEY_CHUNK, VT_ROWS, KEY_CHUNK), BF16),
        jax.ShapeDtypeStruct((B, G, S // KEY_CHUNK, VT_ROWS, KEY_CHUNK), BF16),
        jax.ShapeDtypeStruct((B, G, GATE_ROWS, S), F32),
    )
    dil_specs = ()
    for _, d in DIL_PATTERNS:
        for width in (DIL_Q_W,) * 3:
            out_shape += (jax.ShapeDtypeStruct((B, S // d, d * width), BF16),)
            dil_specs += (pl.BlockSpec((None, tm // d, d * width), lambda b, i: (b, i, 0)),)
    r = np.arange(PERM_ROWS)
    perm = np.zeros((len(DIL_PATTERNS), PERM_ROWS, PERM_ROWS), np.float32)
    for i, (_, d) in enumerate(DIL_PATTERNS):
        assert (PERM_ROWS // d) % 8 == 0
        perm[i, (r % d) * (PERM_ROWS // d) + r // d, r] = 1.0
    perm = jnp.asarray(perm, BF16)
    return pl.pallas_call(
        _proj_kernel,
        grid=(B, S // tm),
        in_specs=[row(D), pl.BlockSpec((1, D), lambda b, i: (0, 0)),
                  pl.BlockSpec(w.shape, lambda b, i: (0, 0)),
                  pl.BlockSpec(perm.shape, lambda b, i: (0, 0, 0))],
        out_specs=(row(512),
                   pl.BlockSpec((None, tm // CMP_STRIDE, CMP_STRIDE * cmp_w), lambda b, i: (b, i, 0)),
                   k_spec, k_spec, vt_spec, vt_spec,
                   pl.BlockSpec((None, G, GATE_ROWS, tm), lambda b, i: (b, 0, 0, i))) + dil_specs,
        out_shape=out_shape,
        scratch_shapes=[pltpu.VMEM((cmp_w // LANES, tm, LANES), F32)],
        compiler_params=pltpu.CompilerParams(
            dimension_semantics=("parallel", "parallel"), vmem_limit_bytes=VMEM_LIMIT),
        name="proj",
    )(x, g, w, perm)


def _compress_kernel(r_ref, posa_ref, posb_ref, wa_ref, wb_ref, w2_ref, ovt_ref, kc_ref, cvo_ref):
    n = r_ref.shape[0]
    slot_w = CMP_STRIDE * HEAD_DIM
    outs = []
    for s in range(2 * NSA_GROUPS):
        r = r_ref[:, s * slot_w:(s + 1) * slot_w]
        ha = jnp.dot((r + posa_ref[s]).astype(BF16), wa_ref[s], preferred_element_type=F32)
        hb = jnp.dot((r + posb_ref[s]).astype(BF16), wb_ref[s], preferred_element_type=F32)
        hid = jax.nn.gelu(ha + pltpu.roll(hb, n - 1, 0))
        outs.append(jnp.dot(hid.astype(BF16), w2_ref[s], preferred_element_type=F32))
    cmp_end = lax.broadcasted_iota(jnp.int32, (n, LANES), 0) * CMP_STRIDE + (CMP_BLOCK - 1)
    feat = _pos_features(cmp_end)
    zeros = jnp.zeros((n, LANES - HEAD_DIM), F32)
    for g in range(NSA_GROUPS):
        kc_ref[g] = jnp.concatenate([outs[g], feat[:, HEAD_DIM:]], axis=1).astype(BF16)
        v_t = jnp.concatenate([outs[NSA_GROUPS + g], zeros], axis=1).T[0:HEAD_DIM]
        cvo_ref[g] = jnp.concatenate([v_t, ovt_ref[...], _ones_rows(n, F32)], axis=0).astype(BF16)


def _compress_call(r, posa, posb, wa, wb, w2, ovt):
    B, R, C = r.shape
    G = NSA_GROUPS
    rows = HEAD_DIM + ovt.shape[0] + ONES_ROWS
    full = lambda a: pl.BlockSpec(a.shape, lambda b: (0,) * a.ndim)
    return pl.pallas_call(
        _compress_kernel,
        grid=(B,),
        in_specs=[pl.BlockSpec((None, R, C), lambda b: (b, 0, 0)),
                  full(posa), full(posb), full(wa), full(wb), full(w2), full(ovt)],
        out_specs=(pl.BlockSpec((None, G, R, LANES), lambda b: (b, 0, 0, 0)),
                   pl.BlockSpec((None, G, rows, R), lambda b: (b, 0, 0, 0))),
        out_shape=(jax.ShapeDtypeStruct((B, G, R, LANES), BF16),
                   jax.ShapeDtypeStruct((B, G, rows, R), BF16)),
        compiler_params=pltpu.CompilerParams(
            dimension_semantics=("parallel",), vmem_limit_bytes=VMEM_LIMIT),
        name="compress",
    )(r, posa, posb, wa, wb, w2, ovt)


def _flash_step(s, bias, v_t, m_ref, acc_ref):
    ps, alphas = [], []
    qb = bias.shape[1]
    for h in range(NSA_HPG):
        cols = slice(h * qb, (h + 1) * qb)
        sh = s[:, cols] + bias
        m_old = m_ref[:, cols]
        m_new = jnp.maximum(m_old, jnp.max(sh, axis=0, keepdims=True))
        alphas.append(jnp.exp2(m_old - m_new))
        ps.append(jnp.exp2(sh - m_new).astype(BF16))
        m_ref[:, cols] = m_new
    pv = jnp.dot(v_t, jnp.concatenate(ps, axis=1), preferred_element_type=F32)
    acc_ref[...] = jnp.concatenate(alphas, axis=1) * acc_ref[...] + pv


def _softmax_pv(s, bias, v_t):
    ms, ps = [], []
    qb = bias.shape[1]
    for h in range(NSA_HPG):
        sh = s[:, h * qb:(h + 1) * qb] + bias
        ms.append(jnp.max(sh, axis=0, keepdims=True))
        ps.append(jnp.exp2(sh - ms[h]).astype(BF16))
    r = jnp.dot(v_t, jnp.concatenate(ps, axis=1), preferred_element_type=F32)
    return jnp.concatenate(ms, axis=1), r


def _normalise(acc):
    return acc[0:HEAD_DIM] * (1.0 / jnp.maximum(acc[HEAD_DIM:HEAD_DIM + 1], 1e-30))


def _nsa_kernel(qfeat_ref, q_ref, kc_ref, cvo_ref, ks_ref, vst_ref, kw_ref, vwt_ref, gate_ref,
                o_ref, sel_ref, ms_ref, accs_ref, flag_ref, list_ref):
    n = pl.program_id(2)
    qb, kc = NSA_QB, KEY_CHUNK
    t0 = n * qb
    lane = lax.broadcasted_iota(jnp.int32, (qb, LANES), 1)
    qf = q_ref[...].astype(F32)
    q = jnp.concatenate(
        [jnp.where(lane < HEAD_DIM, _head_tile(qf, h), qfeat_ref[h][0:1, :]) for h in range(NSA_HPG)],
        axis=0).astype(BF16)

    wq = min(WIN_QB, qb)
    span = WIN + wq
    back_w = (lax.broadcasted_iota(jnp.int32, (span, wq), 1)
              - lax.broadcasted_iota(jnp.int32, (span, wq), 0))
    win_parts = []
    for j in range(qb // wq):
        k_lo = pl.multiple_of(jnp.maximum(t0 + j * wq - WIN, 0), kc)
        dist = back_w + (t0 + j * wq - k_lo)
        bias = jnp.where(dist >= 0, jnp.where(dist < WIN, 0.0, -jnp.inf), -jnp.inf)
        q_part = jnp.concatenate([q[h * qb + j * wq:h * qb + (j + 1) * wq] for h in range(NSA_HPG)], axis=0)
        vt_w = jnp.concatenate([vwt_ref[k_lo // kc + u] for u in range(span // kc)], axis=1)
        win_parts.append(_normalise(_softmax_pv(_nt(kw_ref[pl.ds(k_lo, span), :], q_part), bias, vt_w)[1]))
    o_win = jnp.concatenate([part[:, h * wq:(h + 1) * wq] for h in range(NSA_HPG) for part in win_parts],
                            axis=1)

    n_cmp = kc_ref.shape[0]
    n_sel = sel_ref.shape[0] - SEL_PAD_ROWS
    s = _nt(kc_ref[...], q)
    cmp_end = lax.broadcasted_iota(jnp.int32, (n_cmp, qb), 0) * CMP_STRIDE + (CMP_BLOCK - 1)
    bias_c = jnp.where(cmp_end <= t0 + lax.broadcasted_iota(jnp.int32, (n_cmp, qb), 1), 0.0, -jnp.inf)
    es = []
    for h in range(NSA_HPG):
        sh = s[:, h * qb:(h + 1) * qb] + bias_c
        m = jnp.max(sh, axis=0, keepdims=True)
        es.append(jnp.exp2(sh - jnp.where(m == -jnp.inf, 0.0, m)).astype(BF16))
    r = jnp.dot(cvo_ref[...], jnp.concatenate(es, axis=1), preferred_element_type=F32)
    inv_l = 1.0 / jnp.maximum(r[HEAD_DIM + n_sel:HEAD_DIM + n_sel + 1], 1e-30)
    o_cmp = r[0:HEAD_DIM] * inv_l
    imp = r[HEAD_DIM:HEAD_DIM + n_sel] * inv_l
    imp_t = imp[:, 0:qb]
    for h in range(1, NSA_HPG):
        imp_t = imp_t + imp[:, h * qb:(h + 1) * qb]

    blk = lax.broadcasted_iota(jnp.int32, (n_sel, qb), 0).astype(F32)
    cur = ((t0 + lax.broadcasted_iota(jnp.int32, (n_sel, qb), 1)) // SLC_BLOCK).astype(F32)
    ago = cur - blk
    forced = jnp.where(blk == 0, 1.0, jnp.where(ago == 0, 1.0, jnp.where(ago == 1, 1.0, 0.0)))
    valid = ago >= 0
    score = jnp.where(valid, jnp.where(forced > 0.5, -1.0, imp_t), -1.0)
    for _ in range(max(min(SLC_TOPK, n_sel) - N_FORCED, 0)):
        idx = jnp.argmax(score, axis=0, keepdims=True).astype(F32)
        score = jnp.where(blk == idx, -jnp.inf, score)
    sel_t = jnp.where(valid, jnp.where(score == -jnp.inf, 1.0, forced), 0.0)
    sel_ref[0:n_sel, :] = jnp.where(sel_t > 0.5, 0.0, -jnp.inf)
    sel_ref[n_sel:, :] = jnp.full((SEL_PAD_ROWS, qb), -jnp.inf, F32)
    any_q = jnp.max(sel_t, axis=1, keepdims=True)
    bit = jnp.left_shift(1, lax.broadcasted_iota(jnp.int32, (n_sel, 1), 0) % FLAG_BITS).astype(F32)
    packed = any_q * bit
    for i in range(n_sel // FLAG_BITS):
        word = jnp.sum(packed[i * FLAG_BITS:(i + 1) * FLAG_BITS], axis=0, keepdims=True)
        flag_ref[i] = word.astype(jnp.int32)[0, 0]

    blocks_per_chunk = kc // SLC_BLOCK
    chunks_per_word = FLAG_BITS // blocks_per_chunk
    row = lax.broadcasted_iota(jnp.int32, (kc, qb), 0)

    def chunk_bias(first_block):
        bias = sel_ref[pl.ds(first_block, 1), :]
        for j in range(1, blocks_per_chunk):
            bias = jnp.where(row < j * SLC_BLOCK, bias, sel_ref[pl.ds(first_block + j, 1), :])
        return bias

    own = qb // kc
    first_own = t0 // kc
    off_first = jnp.where(n == 0, -jnp.inf, 0.0)
    query = lax.broadcasted_iota(jnp.int32, (kc, qb), 1)
    biases = [chunk_bias(0) + off_first]
    for u in range(own):
        causal = jnp.where(row + u * kc <= query, 0.0, -jnp.inf)
        biases.append(chunk_bias((first_own + u) * blocks_per_chunk) + causal)
    keys = jnp.concatenate([ks_ref[0:kc, :], ks_ref[pl.ds(pl.multiple_of(t0, kc), qb), :]], axis=0)
    vt_s = jnp.concatenate([vst_ref[0]] + [vst_ref[first_own + u] for u in range(own)], axis=1)
    m_s, acc_s = _softmax_pv(_nt(keys, q), jnp.concatenate(biases, axis=0), vt_s)
    ms_ref[...] = m_s
    accs_ref[...] = acc_s

    def scan_body(w, cnt):
        word = flag_ref[w]
        for j in range(chunks_per_word):
            c = w * chunks_per_word + j
            bits = (word >> (j * blocks_per_chunk)) & ((1 << blocks_per_chunk) - 1)
            list_ref[cnt] = c
            cnt = cnt + ((bits != 0) & (c >= 1) & (c < first_own)).astype(jnp.int32)
        return cnt

    cnt = lax.fori_loop(0, (first_own + chunks_per_word - 1) // chunks_per_word, scan_body, 0)
    for u in range(SLC_GROUP):
        list_ref[cnt + u] = -1

    def slc_body(gi, carry):
        keys, biases, vts = [], [], []
        for u in range(SLC_GROUP):
            c = list_ref[gi * SLC_GROUP + u]
            cc = jnp.maximum(c, 0)
            biases.append(chunk_bias(jnp.where(c >= 0, cc * blocks_per_chunk, n_sel)))
            keys.append(ks_ref[pl.ds(pl.multiple_of(cc * kc, kc), kc), :])
            vts.append(vst_ref[cc])
        _flash_step(_nt(jnp.concatenate(keys, axis=0), q), jnp.concatenate(biases, axis=0),
                    jnp.concatenate(vts, axis=1), ms_ref, accs_ref)
        return carry

    lax.fori_loop(0, (cnt + SLC_GROUP - 1) // SLC_GROUP, slc_body, 0)

    o_slc = _normalise(accs_ref[...])
    gate = gate_ref[...]
    outs = []
    for h in range(NSA_HPG):
        cols = slice(h * qb, (h + 1) * qb)
        gc = [gate[j * NSA_HPG + h:j * NSA_HPG + h + 1, :] for j in range(3)]
        outs.append(gc[0] * o_cmp[:, cols] + gc[1] * o_slc[:, cols] + gc[2] * o_win[:, cols])
    o_ref[...] = jnp.concatenate(outs, axis=0).T


def _nsa_call(qfeat, qn, kc, cvo, ks, vst, kw, vwt, gates):
    B, S, _ = qn.shape
    G = NSA_GROUPS
    qb = NSA_QB
    nb = S // qb
    n_sel = S // SLC_BLOCK
    width = NSA_HPG * qb
    per_group = lambda a: pl.BlockSpec((None, None) + a.shape[2:],
                                       lambda b, g, n: (b, g) + (0,) * (a.ndim - 2))
    return pl.pallas_call(
        _nsa_kernel,
        grid=(B, G, nb),
        in_specs=[
            pl.BlockSpec((None,) + qfeat.shape[1:], lambda b, g, n: (g, 0, 0, 0)),
            pl.BlockSpec((None, qb, NSA_HPG * HEAD_DIM), lambda b, g, n: (b, n, g)),
            per_group(kc), per_group(cvo), per_group(ks), per_group(vst), per_group(kw), per_group(vwt),
            pl.BlockSpec((None, None, GATE_ROWS, qb), lambda b, g, n: (b, g, 0, n)),
        ],
        out_specs=pl.BlockSpec((None, qb, NSA_HPG * HEAD_DIM), lambda b, g, n: (b, n, g)),
        out_shape=jax.ShapeDtypeStruct((B, S, NSA_HEADS * HEAD_DIM), F32),
        scratch_shapes=[pltpu.VMEM((n_sel + SEL_PAD_ROWS, qb), F32),
                        pltpu.VMEM((1, width), F32), pltpu.VMEM((VT_ROWS, width), F32),
                        pltpu.SMEM((n_sel // FLAG_BITS,), jnp.int32),
                        pltpu.SMEM((S // KEY_CHUNK + SLC_GROUP,), jnp.int32)],
        compiler_params=pltpu.CompilerParams(
            dimension_semantics=("parallel", "parallel", "arbitrary"), vmem_limit_bytes=VMEM_LIMIT),
        name="nsa",
    )(qfeat, qn, kc, cvo, ks, vst, kw, vwt, gates)


def _dilated_kernel(qfeat_ref, *refs):
    n_pat = len(DIL_PATTERNS)
    j = pl.program_id(1)
    for i, (window, dilation) in enumerate(DIL_PATTERNS):
        steps_per_phase = max(d for _, d in DIL_PATTERNS) // dilation
        _dilated_blocks(qfeat_ref, *refs[5 * i:5 * i + 5], *refs[5 * n_pat + 2 * i:5 * n_pat + 2 * i + 2],
                        wd=window // dilation, dilation=dilation,
                        phase=j // steps_per_phase, step=j % steps_per_phase)


def _dilated_blocks(qfeat_ref, q_ref, kp_ref, kc_ref, vp_ref, vc_ref, o_ref, lse_ref, *,
                    wd, dilation, phase, step):
    c = Q_BLOCK
    hd = HEAD_DIM
    blocks = q_ref.shape[0] // c
    lane = lax.broadcasted_iota(jnp.int32, (blocks * c, LANES), 1)
    key = lax.broadcasted_iota(jnp.int32, (2 * c, c), 0)
    qry = lax.broadcasted_iota(jnp.int32, (2 * c, c), 1)
    first = step == 0
    off_later = jnp.where(key < c, qry - (key + (wd - c)), (key - c) - qry)
    off_first = jnp.where(key < c, jnp.where(first, 2 * c, qry) - (key + (wd - c)), (key - c) - qry)
    bias_later = jnp.where(off_later <= 0, 0.0, -jnp.inf)
    bias_first = jnp.where(off_first <= 0, 0.0, -jnp.inf)
    sub = (step * blocks - 1) * c + lax.broadcasted_iota(jnp.int32, ((blocks + 1) * c, LANES), 0)
    pos_feat = _pos_features(sub * dilation + phase).astype(BF16)
    n_keys = (blocks + 1) * c
    tail = _ones_rows(n_keys, BF16)
    assert PAIR_VT_ROWS == 2 * hd + ONES_ROWS
    for p in range(DIL_HEADS // 2):
        cols = slice(p * LANES, (p + 1) * LANES)
        q_pair = q_ref[:, cols]
        zero = jnp.zeros_like(q_pair)
        feats = [jnp.broadcast_to(qfeat_ref[2 * p + i][0:1, :], (blocks * c, LANES)).astype(BF16)
                 for i in range(2)]
        q_a = jnp.concatenate([jnp.where(lane < hd, q_pair, zero), feats[0]], axis=1)
        q_b = jnp.concatenate([jnp.where(lane < hd, zero, q_pair), feats[1]], axis=1)
        k_all = jnp.concatenate([kp_ref[:, cols], kc_ref[:, cols]], axis=0)
        k_all = jnp.concatenate([k_all, pos_feat], axis=1)
        v_t = jnp.concatenate([jnp.concatenate([vp_ref[:, cols].T, vc_ref[:, cols].T], axis=1), tail],
                              axis=0)
        nb = min(DIL_BAND, blocks)
        for g0 in range(0, blocks, nb):
            q_aug = jnp.concatenate([part[j * c:(j + 1) * c] for j in range(g0, g0 + nb)
                                     for part in (q_a, q_b)], axis=0)
            s_all = _nt(k_all[g0 * c:(g0 + nb + 1) * c], q_aug)
            ms, band = [], []
            for j in range(nb):
                bias = bias_later if g0 + j else bias_first
                probs = []
                for i in range(2):
                    si = s_all[j * c:(j + 2) * c, (2 * j + i) * c:(2 * j + i + 1) * c] + bias
                    ms.append(jnp.max(si, axis=0, keepdims=True))
                    probs.append(jnp.exp2(si - ms[-1]).astype(BF16))
                pieces = [jnp.zeros((j * c, 2 * c), BF16)] if j else []
                pieces.append(jnp.concatenate(probs, axis=1))
                if j < nb - 1:
                    pieces.append(jnp.zeros(((nb - 1 - j) * c, 2 * c), BF16))
                band.append(jnp.concatenate(pieces, axis=0))
            r_all = jnp.dot(v_t[:, g0 * c:(g0 + nb + 1) * c], jnp.concatenate(band, axis=1),
                            preferred_element_type=F32)
            for j in range(nb):
                outs, lses = [], []
                for i in range(2):
                    at = slice((2 * j + i) * c, (2 * j + i + 1) * c)
                    l = r_all[2 * hd:2 * hd + 1, at]
                    outs.append(r_all[i * hd:(i + 1) * hd, at] * (1.0 / l))
                    lses.append(jnp.broadcast_to(ms[2 * j + i] + jnp.log2(l), (hd, c)))
                rows = slice((g0 + j) * c, (g0 + j + 1) * c)
                o_ref[rows, cols] = jnp.concatenate(outs, axis=0).T
                lse_ref[rows, cols] = jnp.concatenate(lses, axis=0).T


def _dilated_call(qfeat, qkv_by_pattern):
    B = qkv_by_pattern[0][0].shape[0]
    steps = max(d for _, d in DIL_PATTERNS)
    in_specs = [pl.BlockSpec(qfeat.shape, lambda b, j: (0, 0, 0))]
    out_specs, out_shape, operands = [], [], [qfeat]
    for (window, d), (q_v, k_v, v_v) in zip(DIL_PATTERNS, qkv_by_pattern):
        assert Q_BLOCK - 1 <= window // d <= Q_BLOCK and steps % d == 0
        L = q_v.shape[1]
        spp = steps // d
        per_step = L // (Q_BLOCK * spp)
        blk = pl.BlockSpec((None, Q_BLOCK * per_step, DIL_Q_W),
                           lambda b, j, spp=spp: (b, j % spp, j // spp))
        prev = pl.BlockSpec((None, Q_BLOCK, DIL_Q_W),
                            lambda b, j, spp=spp, per_step=per_step:
                            (b, jnp.maximum((j % spp) * per_step - 1, 0), j // spp))
        in_specs += [blk, prev, blk, prev, blk]
        operands += [q_v, k_v, k_v, v_v, v_v]
        out_specs += [blk, blk]
        out_shape += [jax.ShapeDtypeStruct((B, L, d * DIL_Q_W), F32)] * 2
    outs = pl.pallas_call(
        _dilated_kernel,
        grid=(B, steps),
        in_specs=in_specs,
        out_specs=tuple(out_specs),
        out_shape=tuple(out_shape),
        compiler_params=pltpu.CompilerParams(
            dimension_semantics=("parallel", "arbitrary"), vmem_limit_bytes=VMEM_LIMIT),
        name="dilated",
    )(*operands)
    return [(outs[2 * i], outs[2 * i + 1]) for i in range(len(DIL_PATTERNS))]


def _merge_kernel(x_ref, g_ref, wg_ref, onsa_ref, o1_ref, o2_ref, o3_ref, l1_ref, l2_ref, l3_ref,
                  wbn_ref, wbd_ref, wo_ref, fg_ref, out_ref, tok_scr, *, final):
    x = x_ref[...]
    tm = x.shape[0]
    xb = _rms(x, g_ref[...]).astype(BF16)
    cz = NSA_HEADS * HEAD_DIM
    d = x.shape[-1]

    def gate_seg(c0, c1):
        return jnp.dot(xb, wg_ref[:, c0:c1], preferred_element_type=F32)

    def token_major(ref, slot, dil):
        if dil == 1:
            return ref[...]
        tiles = DIL_Q_W // LANES
        for p in range(dil):
            for j in range(tiles):
                tok_scr[slot * tiles + j, pl.ds(p, tm // dil, stride=dil), :] = \
                    ref[:, p * DIL_Q_W + j * LANES:p * DIL_Q_W + (j + 1) * LANES]
        return jnp.concatenate([tok_scr[slot * tiles + j] for j in range(tiles)], axis=1)

    z_n = gate_seg(0, cz)
    o_nsa = onsa_ref[...] * (z_n * _sigmoid(z_n))
    dils = [dil for _, dil in DIL_PATTERNS]
    o1, o2, o3 = (token_major(r, i, dil) for i, (r, dil) in enumerate(zip((o1_ref, o2_ref, o3_ref), dils)))
    l1, l2, l3 = (token_major(r, 3 + i, dil) for i, (r, dil) in enumerate(zip((l1_ref, l2_ref, l3_ref), dils)))
    mx = jnp.maximum(jnp.maximum(l1, l2), l3)
    e1, e2, e3 = jnp.exp2(l1 - mx), jnp.exp2(l2 - mx), jnp.exp2(l3 - mx)
    inv = 1.0 / (e1 + e2 + e3)
    o_dil = (e1 * inv) * o1 + (e2 * inv) * o2 + (e3 * inv) * o3
    z_d = gate_seg(cz, 2 * cz)
    o_dil = o_dil * (z_d * _sigmoid(z_d))
    a = jnp.dot(o_nsa.astype(BF16), wbn_ref[...], preferred_element_type=F32)
    bd = jnp.dot(o_dil.astype(BF16), wbd_ref[...], preferred_element_type=F32)
    merged = (_sigmoid(gate_seg(2 * cz, 2 * cz + d)) * a
              + _sigmoid(gate_seg(2 * cz + d, 2 * cz + 2 * d)) * bd)
    y = jnp.dot(merged.astype(BF16), wo_ref[...], preferred_element_type=F32)
    out_ref[...] = _rms(x + y, fg_ref[...]) if final else x + y


def _merge_call(x, g, wg, o_nsa, o_d, lse_d, wbn, wbd, wo, fg, tm, final):
    B, S, D = x.shape
    cz = o_nsa.shape[-1]
    row = lambda c: pl.BlockSpec((None, tm, c), lambda b, i: (b, i, 0))
    full = lambda a: pl.BlockSpec(a.shape, lambda b, i: (0,) * a.ndim)
    phase = [pl.BlockSpec((None, tm // dil, dil * cz), lambda b, i: (b, i, 0)) for _, dil in DIL_PATTERNS]
    return pl.pallas_call(
        functools.partial(_merge_kernel, final=final),
        grid=(B, S // tm),
        in_specs=[row(D), full(g), full(wg), row(cz)] + phase + phase
                 + [full(wbn), full(wbd), full(wo), full(fg)],
        out_specs=row(D),
        out_shape=jax.ShapeDtypeStruct((B, S, D), F32),
        scratch_shapes=[pltpu.VMEM((2 * len(DIL_PATTERNS) * cz // LANES, tm, LANES), F32)],
        compiler_params=pltpu.CompilerParams(
            dimension_semantics=("parallel", "parallel"), vmem_limit_bytes=VMEM_LIMIT),
        name="merge",
    )(x, g, wg, o_nsa, *o_d, *lse_d, wbn, wbd, wo, fg)


def _proj_columns(d_model):
    hd, G, H = HEAD_DIM, NSA_GROUPS, NSA_HEADS
    nsa_w = H * hd
    o_q, o_kv = 0, nsa_w
    o_g = o_kv + 6 * G * hd
    o_zn = o_g + 3 * H
    o_qkvd = o_zn + nsa_w
    dil_w = DIL_HEADS * hd
    o_zd = o_qkvd + 3 * dil_w
    o_mg = o_zd + dil_w
    kv = lambda j, g: o_kv + (j * G + g) * hd + np.arange(hd)
    cols = [o_q + np.arange(nsa_w)]
    cols += [kv(j, g) for j in (0, 1) for g in range(G)]
    for j in (2, 4, 3, 5):
        cols += [kv(j, g) for g in range(G)]
    lanes = np.full(LANES, -1)
    for g in range(G):
        for j in range(3):
            lanes[g * GATE_ROWS + j * NSA_HPG + np.arange(NSA_HPG)] = o_g + j * H + g * NSA_HPG + np.arange(NSA_HPG)
    cols.append(lanes)
    cols.append(o_qkvd + np.arange(3 * dil_w))
    proj_cols = np.concatenate(cols)
    assert proj_cols.size == _PROJ_OFFS["vd"][1]
    gate_cols = np.concatenate([o_zn + np.arange(nsa_w), o_zd + np.arange(dil_w),
                                o_mg + np.arange(2 * d_model)])
    return proj_cols, gate_cols


def _take_columns(w, cols, dtype):
    pieces, start = [], 0
    for i in range(1, len(cols) + 1):
        if i < len(cols):
            same_run = (cols[i] < 0 and cols[i - 1] < 0) or (cols[i - 1] >= 0 and cols[i] == cols[i - 1] + 1)
        if i == len(cols) or not same_run:
            n = i - start
            pieces.append(jnp.zeros((w.shape[0], n), w.dtype) if cols[start] < 0
                          else w[:, int(cols[start]):int(cols[start]) + n])
            start = i
    return jnp.concatenate(pieces, axis=1).astype(dtype)


def _compress_weights(pos_k, w1_k, w2_k, pos_v, w1_v, w2_v):
    hd, G, half = HEAD_DIM, NSA_GROUPS, CMP_BLOCK // 2
    w1 = jnp.stack([w1_k] * G + [w1_v] * G).astype(BF16)
    pos = jnp.stack([pos_k] * G + [pos_v] * G)
    w2 = jnp.stack([w2_k] * G + [w2_v] * G).astype(BF16)
    halves = lambda a, tail: (a[:, :half].reshape((2 * G,) + tail), a[:, half:].reshape((2 * G,) + tail))
    posa, posb = halves(pos, (1, half * hd))
    wa, wb = halves(w1, (half * hd, hd))
    return posa, posb, wa, wb, w2


def _overlap_t(n_cmp_pad, n_sel):
    i = np.arange(n_cmp_pad)[None, :]
    j = np.arange(n_sel)[:, None]
    lo = np.maximum(i * CMP_STRIDE, j * SLC_BLOCK)
    hi = np.minimum(i * CMP_STRIDE + CMP_BLOCK, (j + 1) * SLC_BLOCK)
    return (np.clip(hi - lo, 0, None) / CMP_BLOCK).astype(np.float32)


def kernel(x, norm_g, w_in, cmp_pos_k, cmp_w1_k, cmp_w2_k, cmp_pos_v, cmp_w1_v, cmp_w2_v,
           w_br_nsa, w_br_dil, w_out, final_g):
    B, S, D = x.shape
    n_sel = S // SLC_BLOCK
    assert S % (Q_BLOCK * max(d for _, d in DIL_PATTERNS)) == 0 and n_sel % FLAG_BITS == 0
    n_cmp_pad = S // CMP_STRIDE
    slopes_nsa, slopes_dil = _alibi_slopes()
    proj_cols, gate_cols = _proj_columns(D)
    ovt = jnp.asarray(_overlap_t(n_cmp_pad, n_sel))
    qfeat = jnp.asarray(np.broadcast_to(
        _slope_features(slopes_nsa).reshape(NSA_GROUPS, NSA_HPG, 1, LANES),
        (NSA_GROUPS, NSA_HPG, 8, LANES)))
    qfeat_dil = jnp.asarray(np.broadcast_to(
        _slope_features(slopes_dil).reshape(DIL_HEADS, 1, LANES), (DIL_HEADS, 8, LANES)))
    h = x
    for layer in range(w_in.shape[0]):
        w_proj = _take_columns(w_in[layer], proj_cols, BF16)
        w_gate = _take_columns(w_in[layer], gate_cols, BF16)
        g_l = norm_g[layer].reshape(1, D)

        qn, cmp_src, ks, kw, vst, vwt, gates, *dil = _proj_call(h, g_l, w_proj, min(PROJ_ROWS, S))
        cw = _compress_weights(cmp_pos_k[layer], cmp_w1_k[layer], cmp_w2_k[layer],
                               cmp_pos_v[layer], cmp_w1_v[layer], cmp_w2_v[layer])
        kc, cvo = _compress_call(cmp_src, *cw, ovt)
        o_nsa = _nsa_call(qfeat, qn, kc, cvo, ks, vst, kw, vwt, gates)
        dil_out = _dilated_call(qfeat_dil, [dil[3 * i:3 * i + 3] for i in range(len(DIL_PATTERNS))])
        o_d, lse_d = [o for o, _ in dil_out], [lse for _, lse in dil_out]
        h = _merge_call(h, g_l, w_gate, o_nsa, o_d, lse_d, w_br_nsa[layer].astype(BF16),
                        w_br_dil[layer].astype(BF16), w_out[layer].astype(BF16),
                        final_g.reshape(1, D), min(MERGE_ROWS, S), final=layer == w_in.shape[0] - 1)
    return h
```

```python
import functools
import math

import numpy as np
import jax
import jax.numpy as jnp
from jax import lax
from jax.experimental import pallas as pl
from jax.experimental.pallas import tpu as pltpu

F32 = jnp.float32
BF16 = jnp.bfloat16

HEAD_DIM = 64
LANES = 128
NSA_HEADS = 8
NSA_GROUPS = 2
NSA_HPG = NSA_HEADS // NSA_GROUPS
CMP_BLOCK = 32
CMP_STRIDE = 16
SLC_BLOCK = 64
SLC_TOPK = 16
WIN = 512
FORCE_SCORE = 1.0e4
DIL_HEADS = 8
DIL_PATTERNS = ((128, 1), (512, 4), (2048, 16))
Q_BLOCK = 128
RMS_EPS = 1e-6
KEY_CHUNK = 128
VT_ROWS = 80
PAIR_VT_ROWS = 144
GATE_ROWS = 16
FEAT_SPLIT = 3
FLAG_BITS = 16
SLC_GROUP = 4
PROJ_ROWS = 1024
MERGE_ROWS = 512
PERM_ROWS = 128
WIN_QB = 256
NSA_QB = 512
DIL_BAND = 4
N_FORCED = 3
assert FORCE_SCORE > NSA_HPG + 1
SEL_PAD_ROWS = 8
ONES_ROWS = 16
LOG2E = math.log2(math.e)
VMEM_LIMIT = 56 * 1024 * 1024


def _alibi_slopes():
    n = NSA_HEADS + DIL_HEADS
    s = 2.0 ** (-8.0 * np.arange(1, n + 1) / n)
    return s[0::2].astype(np.float32), s[1::2].astype(np.float32)


def _nt(a, b):
    return lax.dot_general(a, b, (((1,), (1,)), ((), ())), preferred_element_type=F32)


def _rms(x, g):
    return (x * lax.rsqrt(jnp.mean(x * x, axis=-1, keepdims=True) + RMS_EPS)) * g


def _sigmoid(x):
    return 0.5 + 0.5 * jnp.tanh(0.5 * x)


def _exp2_bf16(x):
    return jnp.exp2(x.astype(BF16))


def _head_tile(q_pairs, h):
    tile = q_pairs[:, (h // 2) * LANES:(h // 2 + 1) * LANES]
    return pltpu.roll(tile, HEAD_DIM, 1) if h % 2 else tile


def _pos_features(pos):
    lane = lax.broadcasted_iota(jnp.int32, pos.shape, 1) - HEAD_DIM
    hi = (pos // LANES).astype(F32)
    lo = (pos % LANES).astype(F32)
    return jnp.where(lane < 0, 0.0, jnp.where(lane < FEAT_SPLIT, hi, jnp.where(lane < 2 * FEAT_SPLIT, lo, 0.0)))


def _slope_features(slopes):
    out = np.zeros((len(slopes), LANES), np.float32)
    for h, s in enumerate(slopes):
        rest = np.float32(s) * np.float32(LOG2E)
        for i in range(FEAT_SPLIT):
            piece = np.float32(np.asarray(rest, np.float32).astype(BF16))
            out[h, HEAD_DIM + FEAT_SPLIT + i] = piece
            out[h, HEAD_DIM + i] = piece * LANES
            rest = np.float32(rest - piece)
    return out


def _ones_rows(width, dtype):
    row = lax.broadcasted_iota(jnp.int32, (ONES_ROWS, width), 0)
    return jnp.where(row == 0, 1.0, 0.0).astype(dtype)


_PROJ_SEGS = (("qn", 512), ("cmp", 256), ("ks", 128), ("kw", 128), ("vs", 128), ("vw", 128),
              ("gate", 128), ("qd", 512), ("kd", 512), ("vd", 512))
DIL_Q_W = DIL_HEADS * HEAD_DIM
_PROJ_OFFS = dict(zip((n for n, _ in _PROJ_SEGS),
                      zip(np.cumsum([0] + [w for _, w in _PROJ_SEGS])[:-1].tolist(),
                          np.cumsum([w for _, w in _PROJ_SEGS]).tolist())))


def _to_lane_tiles(scr, first, val):
    for j in range(val.shape[1] // LANES):
        scr[first + j] = val[:, j * LANES:(j + 1) * LANES]


def _proj_kernel(x_ref, g_ref, w_ref, perm_ref, qn_ref, cmp_ref, ks_ref, kw_ref, vst_ref, vwt_ref,
                 gate_ref, *dil_refs):
    cmp_scr = dil_refs[-1]
    tm = x_ref.shape[0]
    xb = _rms(x_ref[...], g_ref[...]).astype(BF16)
    scale = HEAD_DIM ** -0.5

    def seg(name, last=None):
        c0, c1 = _PROJ_OFFS[name][0], _PROJ_OFFS[last or name][1]
        return jnp.dot(xb, w_ref[:, c0:c1], preferred_element_type=F32)

    qn_ref[...] = (seg("qn") * (scale * LOG2E)).astype(BF16)
    _to_lane_tiles(cmp_scr, 0, seg("cmp"))
    lane_c = lax.broadcasted_iota(jnp.int32, (tm // CMP_STRIDE, LANES), 1)
    slot_w = CMP_STRIDE * HEAD_DIM
    for t in range(cmp_scr.shape[0]):
        for p in range(0, CMP_STRIDE, 2):
            even, odd = (cmp_scr[t, pl.ds(p + i, tm // CMP_STRIDE, stride=CMP_STRIDE), :] for i in range(2))
            for half in range(2):
                pair = (jnp.where(lane_c < HEAD_DIM, even, pltpu.roll(odd, HEAD_DIM, 1)) if half == 0
                        else jnp.where(lane_c < HEAD_DIM, pltpu.roll(even, HEAD_DIM, 1), odd))
                c0 = (2 * t + half) * slot_w + p * HEAD_DIM
                cmp_ref[:, c0:c0 + LANES] = pair
    feat =_pos_features(pl.program_id(1) * tm + lax.broadcasted_iota(jnp.int32, (tm, LANES), 0))
    k_sw, v_sw = seg("ks", "kw"), seg("vs", "vw")
    ks, kw = k_sw[:, 0:LANES], k_sw[:, LANES:]
    vst, vwt = v_sw[:, 0:LANES].T, v_sw[:, LANES:].T
    gate_t = _sigmoid(seg("gate")).T
    tail = _ones_rows(tm, F32)
    lane = lax.broadcasted_iota(jnp.int32, (tm, LANES), 1)
    for g in range(NSA_GROUPS):
        ks_ref[g] = jnp.where(lane < HEAD_DIM, _head_tile(ks, g), feat).astype(BF16)
        kw_ref[g] = jnp.where(lane < HEAD_DIM, _head_tile(kw, g), feat).astype(BF16)
        gate_ref[g] = gate_t[g * GATE_ROWS:(g + 1) * GATE_ROWS]
        for src, dst in ((vst, vst_ref), (vwt, vwt_ref)):
            vt = jnp.concatenate([src[g * HEAD_DIM:(g + 1) * HEAD_DIM], tail], axis=0).astype(BF16)
            for j in range(tm // KEY_CHUNK):
                dst[g, j] = vt[:, j * KEY_CHUNK:(j + 1) * KEY_CHUNK]
    qkv = jnp.concatenate([(seg("qd") * (scale * LOG2E)).astype(BF16), seg("kd").astype(BF16),
                           seg("vd").astype(BF16)], axis=1)
    bounds = (0, DIL_Q_W, 2 * DIL_Q_W, 3 * DIL_Q_W)
    subs = tm // PERM_ROWS
    for i, (_, d) in enumerate(DIL_PATTERNS):
        if d > 1:
            by_phase = [jnp.dot(perm_ref[i], qkv[s * PERM_ROWS:(s + 1) * PERM_ROWS],
                                preferred_element_type=F32) for s in range(subs)]
        r = PERM_ROWS // d
        for p in range(d):
            rows = qkv if d == 1 else jnp.concatenate(
                [by_phase[s][p * r:(p + 1) * r] for s in range(subs)], axis=0).astype(BF16)
            for j, ref in enumerate(dil_refs[3 * i:3 * i + 3]):
                width = bounds[j + 1] - bounds[j]
                ref[:, p * width:(p + 1) * width] = rows[:, bounds[j]:bounds[j + 1]]


def _proj_call(x, g, w, tm):
    B, S, D = x.shape
    G = NSA_GROUPS
    cmp_w = _PROJ_OFFS["cmp"][1] - _PROJ_OFFS["cmp"][0]
    row = lambda c: pl.BlockSpec((None, tm, c), lambda b, i: (b, i, 0))
    k_spec = pl.BlockSpec((None, G, tm, LANES), lambda b, i: (b, 0, i, 0))
    vt_spec = pl.BlockSpec((None, G, tm // KEY_CHUNK, VT_ROWS, KEY_CHUNK), lambda b, i: (b, 0, i, 0, 0))
    out_shape = (
        jax.ShapeDtypeStruct((B, S, 512), BF16),
        jax.ShapeDtypeStruct((B, S // CMP_STRIDE, CMP_STRIDE * cmp_w), F32),
        jax.ShapeDtypeStruct((B, G, S, LANES), BF16),
        jax.ShapeDtypeStruct((B, G, S, LANES), BF16),
        jax.ShapeDtypeStruct((B, G, S // KEY_CHUNK, VT_ROWS, KEY_CHUNK), BF16),
        jax.ShapeDtypeStruct((B, G, S // KEY_CHUNK, VT_ROWS, KEY_CHUNK), BF16),
        jax.ShapeDtypeStruct((B, G, GATE_ROWS, S), F32),
    )
    dil_specs = ()
    for _, d in DIL_PATTERNS:
        for width in (DIL_Q_W,) * 3:
            out_shape += (jax.ShapeDtypeStruct((B, S // d, d * width), BF16),)
            dil_specs += (pl.BlockSpec((None, tm // d, d * width), lambda b, i: (b, i, 0)),)
    r = np.arange(PERM_ROWS)
    perm = np.zeros((len(DIL_PATTERNS), PERM_ROWS, PERM_ROWS), np.float32)
    for i, (_, d) in enumerate(DIL_PATTERNS):
        assert (PERM_ROWS // d) % 8 == 0
        perm[i, (r % d) * (PERM_ROWS // d) + r // d, r] = 1.0
    perm = jnp.asarray(perm, BF16)
    return pl.pallas_call(
        _proj_kernel,
        grid=(B, S // tm),
        in_specs=[row(D), pl.BlockSpec((1, D), lambda b, i: (0, 0)),
                  pl.BlockSpec(w.shape, lambda b, i: (0, 0)),
                  pl.BlockSpec(perm.shape, lambda b, i: (0, 0, 0))],
        out_specs=(row(512),
                   pl.BlockSpec((None, tm // CMP_STRIDE, CMP_STRIDE * cmp_w), lambda b, i: (b, i, 0)),
                   k_spec, k_spec, vt_spec, vt_spec,
                   pl.BlockSpec((None, G, GATE_ROWS, tm), lambda b, i: (b, 0, 0, i))) + dil_specs,
        out_shape=out_shape,
        scratch_shapes=[pltpu.VMEM((cmp_w // LANES, tm, LANES), F32)],
        compiler_params=pltpu.CompilerParams(
            dimension_semantics=("parallel", "parallel"), vmem_limit_bytes=VMEM_LIMIT),
        name="proj",
    )(x, g, w, perm)


def _compress_kernel(r_ref, posa_ref, posb_ref, wa_ref, wb_ref, w2_ref, ovt_ref, kc_ref, cvo_ref):
    n = r_ref.shape[0]
    slot_w = CMP_STRIDE * HEAD_DIM
    outs = []
    for s in range(2 * NSA_GROUPS):
        r = r_ref[:, s * slot_w:(s + 1) * slot_w]
        ha = jnp.dot((r + posa_ref[s]).astype(BF16), wa_ref[s], preferred_element_type=F32)
        hb = jnp.dot((r + posb_ref[s]).astype(BF16), wb_ref[s], preferred_element_type=F32)
        hid = jax.nn.gelu(ha + pltpu.roll(hb, n - 1, 0))
        outs.append(jnp.dot(hid.astype(BF16), w2_ref[s], preferred_element_type=F32))
    cmp_end = lax.broadcasted_iota(jnp.int32, (n, LANES), 0) * CMP_STRIDE + (CMP_BLOCK - 1)
    feat = _pos_features(cmp_end)
    zeros = jnp.zeros((n, LANES - HEAD_DIM), F32)
    for g in range(NSA_GROUPS):
        kc_ref[g] = jnp.concatenate([outs[g], feat[:, HEAD_DIM:]], axis=1).astype(BF16)
        v_t = jnp.concatenate([outs[NSA_GROUPS + g], zeros], axis=1).T[0:HEAD_DIM]
        cvo_ref[g] = jnp.concatenate([v_t, ovt_ref[...], _ones_rows(n, F32)], axis=0).astype(BF16)


def _compress_call(r, posa, posb, wa, wb, w2, ovt):
    B, R, C = r.shape
    G = NSA_GROUPS
    rows = HEAD_DIM + ovt.shape[0] + ONES_ROWS
    full = lambda a: pl.BlockSpec(a.shape, lambda b: (0,) * a.ndim)
    return pl.pallas_call(
        _compress_kernel,
        grid=(B,),
        in_specs=[pl.BlockSpec((None, R, C), lambda b: (b, 0, 0)),
                  full(posa), full(posb), full(wa), full(wb), full(w2), full(ovt)],
        out_specs=(pl.BlockSpec((None, G, R, LANES), lambda b: (b, 0, 0, 0)),
                   pl.BlockSpec((None, G, rows, R), lambda b: (b, 0, 0, 0))),
        out_shape=(jax.ShapeDtypeStruct((B, G, R, LANES), BF16),
                   jax.ShapeDtypeStruct((B, G, rows, R), BF16)),
        compiler_params=pltpu.CompilerParams(
            dimension_semantics=("parallel",), vmem_limit_bytes=VMEM_LIMIT),
        name="compress",
    )(r, posa, posb, wa, wb, w2, ovt)


def _flash_step(s, bias, v_t, m_ref, acc_ref):
    ps, alphas = [], []
    qb = bias.shape[1]
    for h in range(NSA_HPG):
        cols = slice(h * qb, (h + 1) * qb)
        sh = s[:, cols] + bias
        m_old = m_ref[:, cols]
        m_new = jnp.maximum(m_old, jnp.max(sh, axis=0, keepdims=True))
        alphas.append(jnp.exp2(m_old - m_new))
        ps.append(_exp2_bf16(sh - m_new))
        m_ref[:, cols] = m_new
    pv = jnp.dot(v_t, jnp.concatenate(ps, axis=1), preferred_element_type=F32)
    acc_ref[...] = jnp.concatenate(alphas, axis=1) * acc_ref[...] + pv


def _softmax_pv(s, bias, v_t):
    ms, ps = [], []
    qb = bias.shape[1]
    for h in range(NSA_HPG):
        sh = s[:, h * qb:(h + 1) * qb] + bias
        ms.append(jnp.max(sh, axis=0, keepdims=True))
        ps.append(_exp2_bf16(sh - ms[h]))
    r = jnp.dot(v_t, jnp.concatenate(ps, axis=1), preferred_element_type=F32)
    return jnp.concatenate(ms, axis=1), r


def _normalise(acc):
    return acc[0:HEAD_DIM] * (1.0 / jnp.maximum(acc[HEAD_DIM:HEAD_DIM + 1], 1e-30))


def _nsa_kernel(qfeat_ref, q_ref, kc_ref, cvo_ref, ks_ref, vst_ref, kw_ref, vwt_ref, gate_ref,
                o_ref, sel_ref, ms_ref, accs_ref, flag_ref, list_ref):
    n = pl.program_id(2)
    qb, kc = NSA_QB, KEY_CHUNK
    t0 = n * qb
    lane = lax.broadcasted_iota(jnp.int32, (qb, LANES), 1)
    qf = q_ref[...].astype(F32)
    q = jnp.concatenate(
        [jnp.where(lane < HEAD_DIM, _head_tile(qf, h), qfeat_ref[h][0:1, :]) for h in range(NSA_HPG)],
        axis=0).astype(BF16)

    wq = min(WIN_QB, qb)
    span = WIN + wq
    back_w = (lax.broadcasted_iota(jnp.int32, (span, wq), 1)
              - lax.broadcasted_iota(jnp.int32, (span, wq), 0))
    win_parts = []
    for j in range(qb // wq):
        k_lo = pl.multiple_of(jnp.maximum(t0 + j * wq - WIN, 0), kc)
        dist = back_w + (t0 + j * wq - k_lo)
        bias = jnp.where(dist >= 0, jnp.where(dist < WIN, 0.0, -jnp.inf), -jnp.inf)
        q_part = jnp.concatenate([q[h * qb + j * wq:h * qb + (j + 1) * wq] for h in range(NSA_HPG)], axis=0)
        vt_w = jnp.concatenate([vwt_ref[k_lo // kc + u] for u in range(span // kc)], axis=1)
        win_parts.append(_normalise(_softmax_pv(_nt(kw_ref[pl.ds(k_lo, span), :], q_part), bias, vt_w)[1]))
    o_win = jnp.concatenate([part[:, h * wq:(h + 1) * wq] for h in range(NSA_HPG) for part in win_parts],
                            axis=1)

    n_cmp = kc_ref.shape[0]
    n_sel = sel_ref.shape[0] - SEL_PAD_ROWS
    s = _nt(kc_ref[...], q)
    cmp_end = lax.broadcasted_iota(jnp.int32, (n_cmp, qb), 0) * CMP_STRIDE + (CMP_BLOCK - 1)
    bias_c = jnp.where(cmp_end <= t0 + lax.broadcasted_iota(jnp.int32, (n_cmp, qb), 1), 0.0, -jnp.inf)
    es = []
    for h in range(NSA_HPG):
        sh = s[:, h * qb:(h + 1) * qb] + bias_c
        m = jnp.max(sh, axis=0, keepdims=True)
        es.append(jnp.exp2(sh - jnp.where(m == -jnp.inf, 0.0, m)).astype(BF16))
    r = jnp.dot(cvo_ref[...], jnp.concatenate(es, axis=1), preferred_element_type=F32)
    inv_l = 1.0 / jnp.maximum(r[HEAD_DIM + n_sel:HEAD_DIM + n_sel + 1], 1e-30)
    o_cmp = r[0:HEAD_DIM] * inv_l
    imp = r[HEAD_DIM:HEAD_DIM + n_sel] * inv_l
    imp_t = imp[:, 0:qb]
    for h in range(1, NSA_HPG):
        imp_t = imp_t + imp[:, h * qb:(h + 1) * qb]

    blk = lax.broadcasted_iota(jnp.int32, (n_sel, qb), 0).astype(F32)
    cur = ((t0 + lax.broadcasted_iota(jnp.int32, (n_sel, qb), 1)) // SLC_BLOCK).astype(F32)
    ago = cur - blk
    forced = jnp.where(blk == 0, 1.0, jnp.where(ago == 0, 1.0, jnp.where(ago == 1, 1.0, 0.0)))
    valid = ago >= 0
    score = jnp.where(valid, jnp.where(forced > 0.5, -1.0, imp_t), -1.0)
    for _ in range(max(min(SLC_TOPK, n_sel) - N_FORCED, 0)):
        idx = jnp.argmax(score, axis=0, keepdims=True).astype(F32)
        score = jnp.where(blk == idx, -jnp.inf, score)
    sel_t = jnp.where(valid, jnp.where(score == -jnp.inf, 1.0, forced), 0.0)
    sel_ref[0:n_sel, :] = jnp.where(sel_t > 0.5, 0.0, -jnp.inf)
    sel_ref[n_sel:, :] = jnp.full((SEL_PAD_ROWS, qb), -jnp.inf, F32)
    any_q = jnp.max(sel_t, axis=1, keepdims=True)
    bit = jnp.left_shift(1, lax.broadcasted_iota(jnp.int32, (n_sel, 1), 0) % FLAG_BITS).astype(F32)
    packed = any_q * bit
    for i in range(n_sel // FLAG_BITS):
        word = jnp.sum(packed[i * FLAG_BITS:(i + 1) * FLAG_BITS], axis=0, keepdims=True)
        flag_ref[i] = word.astype(jnp.int32)[0, 0]

    blocks_per_chunk = kc // SLC_BLOCK
    chunks_per_word = FLAG_BITS // blocks_per_chunk
    row = lax.broadcasted_iota(jnp.int32, (kc, qb), 0)

    def chunk_bias(first_block):
        bias = sel_ref[pl.ds(first_block, 1), :]
        for j in range(1, blocks_per_chunk):
            bias = jnp.where(row < j * SLC_BLOCK, bias, sel_ref[pl.ds(first_block + j, 1), :])
        return bias

    own = qb // kc
    first_own = t0 // kc
    off_first = jnp.where(n == 0, -jnp.inf, 0.0)
    query = lax.broadcasted_iota(jnp.int32, (kc, qb), 1)
    biases = [chunk_bias(0) + off_first]
    for u in range(own):
        causal = jnp.where(row + u * kc <= query, 0.0, -jnp.inf)
        biases.append(chunk_bias((first_own + u) * blocks_per_chunk) + causal)
    keys = jnp.concatenate([ks_ref[0:kc, :], ks_ref[pl.ds(pl.multiple_of(t0, kc), qb), :]], axis=0)
    vt_s = jnp.concatenate([vst_ref[0]] + [vst_ref[first_own + u] for u in range(own)], axis=1)
    m_s, acc_s = _softmax_pv(_nt(keys, q), jnp.concatenate(biases, axis=0), vt_s)
    ms_ref[...] = m_s
    accs_ref[...] = acc_s

    def scan_body(w, cnt):
        word = flag_ref[w]
        for j in range(chunks_per_word):
            c = w * chunks_per_word + j
            bits = (word >> (j * blocks_per_chunk)) & ((1 << blocks_per_chunk) - 1)
            list_ref[cnt] = c
            cnt = cnt + ((bits != 0) & (c >= 1) & (c < first_own)).astype(jnp.int32)
        return cnt

    cnt = lax.fori_loop(0, (first_own + chunks_per_word - 1) // chunks_per_word, scan_body, 0)
    for u in range(SLC_GROUP):
        list_ref[cnt + u] = -1

    def slc_body(gi, carry):
        keys, biases, vts = [], [], []
        for u in range(SLC_GROUP):
            c = list_ref[gi * SLC_GROUP + u]
            cc = jnp.maximum(c, 0)
            biases.append(chunk_bias(jnp.where(c >= 0, cc * blocks_per_chunk, n_sel)))
            keys.append(ks_ref[pl.ds(pl.multiple_of(cc * kc, kc), kc), :])
            vts.append(vst_ref[cc])
        _flash_step(_nt(jnp.concatenate(keys, axis=0), q), jnp.concatenate(biases, axis=0),
                    jnp.concatenate(vts, axis=1), ms_ref, accs_ref)
        return carry

    lax.fori_loop(0, (cnt + SLC_GROUP - 1) // SLC_GROUP, slc_body, 0)

    o_slc = _normalise(accs_ref[...])
    gate = gate_ref[...]
    outs = []
    for h in range(NSA_HPG):
        cols = slice(h * qb, (h + 1) * qb)
        gc = [gate[j * NSA_HPG + h:j * NSA_HPG + h + 1, :] for j in range(3)]
        outs.append(gc[0] * o_cmp[:, cols] + gc[1] * o_slc[:, cols] + gc[2] * o_win[:, cols])
    o_ref[...] = jnp.concatenate(outs, axis=0).T


def _nsa_call(qfeat, qn, kc, cvo, ks, vst, kw, vwt, gates):
    B, S, _ = qn.shape
    G = NSA_GROUPS
    qb = NSA_QB
    nb = S // qb
    n_sel = S // SLC_BLOCK
    width = NSA_HPG * qb
    per_group = lambda a: pl.BlockSpec((None, None) + a.shape[2:],
                                       lambda b, g, n: (b, g) + (0,) * (a.ndim - 2))
    return pl.pallas_call(
        _nsa_kernel,
        grid=(B, G, nb),
        in_specs=[
            pl.BlockSpec((None,) + qfeat.shape[1:], lambda b, g, n: (g, 0, 0, 0)),
            pl.BlockSpec((None, qb, NSA_HPG * HEAD_DIM), lambda b, g, n: (b, n, g)),
            per_group(kc), per_group(cvo), per_group(ks), per_group(vst), per_group(kw), per_group(vwt),
            pl.BlockSpec((None, None, GATE_ROWS, qb), lambda b, g, n: (b, g, 0, n)),
        ],
        out_specs=pl.BlockSpec((None, qb, NSA_HPG * HEAD_DIM), lambda b, g, n: (b, n, g)),
        out_shape=jax.ShapeDtypeStruct((B, S, NSA_HEADS * HEAD_DIM), F32),
        scratch_shapes=[pltpu.VMEM((n_sel + SEL_PAD_ROWS, qb), F32),
                        pltpu.VMEM((1, width), F32), pltpu.VMEM((VT_ROWS, width), F32),
                        pltpu.SMEM((n_sel // FLAG_BITS,), jnp.int32),
                        pltpu.SMEM((S // KEY_CHUNK + SLC_GROUP,), jnp.int32)],
        compiler_params=pltpu.CompilerParams(
            dimension_semantics=("parallel", "parallel", "arbitrary"), vmem_limit_bytes=VMEM_LIMIT),
        name="nsa",
    )(qfeat, qn, kc, cvo, ks, vst, kw, vwt, gates)


def _dilated_kernel(qfeat_ref, *refs):
    n_pat = len(DIL_PATTERNS)
    j = pl.program_id(1)
    for i, (window, dilation) in enumerate(DIL_PATTERNS):
        steps_per_phase = max(d for _, d in DIL_PATTERNS) // dilation
        _dilated_blocks(qfeat_ref, *refs[5 * i:5 * i + 5], *refs[5 * n_pat + 2 * i:5 * n_pat + 2 * i + 2],
                        wd=window // dilation, dilation=dilation,
                        phase=j // steps_per_phase, step=j % steps_per_phase)


def _dilated_blocks(qfeat_ref, q_ref, kp_ref, kc_ref, vp_ref, vc_ref, o_ref, lse_ref, *,
                    wd, dilation, phase, step):
    c = Q_BLOCK
    hd = HEAD_DIM
    blocks = q_ref.shape[0] // c
    lane = lax.broadcasted_iota(jnp.int32, (blocks * c, LANES), 1)
    key = lax.broadcasted_iota(jnp.int32, (2 * c, c), 0)
    qry = lax.broadcasted_iota(jnp.int32, (2 * c, c), 1)
    first = step == 0
    off_later = jnp.where(key < c, qry - (key + (wd - c)), (key - c) - qry)
    off_first = jnp.where(key < c, jnp.where(first, 2 * c, qry) - (key + (wd - c)), (key - c) - qry)
    bias_later = jnp.where(off_later <= 0, 0.0, -jnp.inf)
    bias_first = jnp.where(off_first <= 0, 0.0, -jnp.inf)
    sub = (step * blocks - 1) * c + lax.broadcasted_iota(jnp.int32, ((blocks + 1) * c, LANES), 0)
    pos_feat = _pos_features(sub * dilation + phase).astype(BF16)
    n_keys = (blocks + 1) * c
    tail = _ones_rows(n_keys, BF16)
    assert PAIR_VT_ROWS == 2 * hd + ONES_ROWS
    for p in range(DIL_HEADS // 2):
        cols = slice(p * LANES, (p + 1) * LANES)
        q_pair = q_ref[:, cols]
        zero = jnp.zeros_like(q_pair)
        feats = [jnp.broadcast_to(qfeat_ref[2 * p + i][0:1, :], (blocks * c, LANES)).astype(BF16)
                 for i in range(2)]
        q_a = jnp.concatenate([jnp.where(lane < hd, q_pair, zero), feats[0]], axis=1)
        q_b = jnp.concatenate([jnp.where(lane < hd, zero, q_pair), feats[1]], axis=1)
        k_all = jnp.concatenate([kp_ref[:, cols], kc_ref[:, cols]], axis=0)
        k_all = jnp.concatenate([k_all, pos_feat], axis=1)
        v_t = jnp.concatenate([jnp.concatenate([vp_ref[:, cols].T, vc_ref[:, cols].T], axis=1), tail],
                              axis=0)
        nb = min(DIL_BAND, blocks)
        for g0 in range(0, blocks, nb):
            q_aug = jnp.concatenate([part[j * c:(j + 1) * c] for j in range(g0, g0 + nb)
                                     for part in (q_a, q_b)], axis=0)
            s_all = _nt(k_all[g0 * c:(g0 + nb + 1) * c], q_aug)
            ms, band = [], []
            for j in range(nb):
                bias = bias_later if g0 + j else bias_first
                probs = []
                for i in range(2):
                    si = s_all[j * c:(j + 2) * c, (2 * j + i) * c:(2 * j + i + 1) * c] + bias
                    ms.append(jnp.max(si, axis=0, keepdims=True))
                    probs.append(_exp2_bf16(si - ms[-1]))
                pieces = [jnp.zeros((j * c, 2 * c), BF16)] if j else []
                pieces.append(jnp.concatenate(probs, axis=1))
                if j < nb - 1:
                    pieces.append(jnp.zeros(((nb - 1 - j) * c, 2 * c), BF16))
                band.append(jnp.concatenate(pieces, axis=0))
            r_all = jnp.dot(v_t[:, g0 * c:(g0 + nb + 1) * c], jnp.concatenate(band, axis=1),
                            preferred_element_type=F32)
            for j in range(nb):
                outs, lses = [], []
                for i in range(2):
                    at = slice((2 * j + i) * c, (2 * j + i + 1) * c)
                    l = r_all[2 * hd:2 * hd + 1, at]
                    outs.append(r_all[i * hd:(i + 1) * hd, at] * (1.0 / l))
                    lses.append(jnp.broadcast_to(ms[2 * j + i] + jnp.log2(l), (hd, c)))
                rows = slice((g0 + j) * c, (g0 + j + 1) * c)
                o_ref[rows, cols] = jnp.concatenate(outs, axis=0).T
                lse_ref[rows, cols] = jnp.concatenate(lses, axis=0).T


def _dilated_call(qfeat, qkv_by_pattern):
    B = qkv_by_pattern[0][0].shape[0]
    steps = max(d for _, d in DIL_PATTERNS)
    in_specs = [pl.BlockSpec(qfeat.shape, lambda b, j: (0, 0, 0))]
    out_specs, out_shape, operands = [], [], [qfeat]
    for (window, d), (q_v, k_v, v_v) in zip(DIL_PATTERNS, qkv_by_pattern):
        assert Q_BLOCK - 1 <= window // d <= Q_BLOCK and steps % d == 0
        L = q_v.shape[1]
        spp = steps // d
        per_step = L // (Q_BLOCK * spp)
        blk = pl.BlockSpec((None, Q_BLOCK * per_step, DIL_Q_W),
                           lambda b, j, spp=spp: (b, j % spp, j // spp))
        prev = pl.BlockSpec((None, Q_BLOCK, DIL_Q_W),
                            lambda b, j, spp=spp, per_step=per_step:
                            (b, jnp.maximum((j % spp) * per_step - 1, 0), j // spp))
        in_specs += [blk, prev, blk, prev, blk]
        operands += [q_v, k_v, k_v, v_v, v_v]
        out_specs += [blk, blk]
        out_shape += [jax.ShapeDtypeStruct((B, L, d * DIL_Q_W), F32)] * 2
    outs = pl.pallas_call(
        _dilated_kernel,
        grid=(B, steps),
        in_specs=in_specs,
        out_specs=tuple(out_specs),
        out_shape=tuple(out_shape),
        compiler_params=pltpu.CompilerParams(
            dimension_semantics=("parallel", "arbitrary"), vmem_limit_bytes=VMEM_LIMIT),
        name="dilated",
    )(*operands)
    return [(outs[2 * i], outs[2 * i + 1]) for i in range(len(DIL_PATTERNS))]


def _merge_kernel(x_ref, g_ref, wg_ref, onsa_ref, o1_ref, o2_ref, o3_ref, l1_ref, l2_ref, l3_ref,
                  wbn_ref, wbd_ref, wo_ref, fg_ref, out_ref, tok_scr, *, final):
    x = x_ref[...]
    tm = x.shape[0]
    xb = _rms(x, g_ref[...]).astype(BF16)
    cz = NSA_HEADS * HEAD_DIM
    d = x.shape[-1]

    def gate_seg(c0, c1):
        return jnp.dot(xb, wg_ref[:, c0:c1], preferred_element_type=F32)

    def token_major(ref, slot, dil):
        if dil == 1:
            return ref[...]
        tiles = DIL_Q_W // LANES
        for p in range(dil):
            for j in range(tiles):
                tok_scr[slot * tiles + j, pl.ds(p, tm // dil, stride=dil), :] = \
                    ref[:, p * DIL_Q_W + j * LANES:p * DIL_Q_W + (j + 1) * LANES]
        return jnp.concatenate([tok_scr[slot * tiles + j] for j in range(tiles)], axis=1)

    z_n = gate_seg(0, cz)
    o_nsa = onsa_ref[...] * (z_n * _sigmoid(z_n))
    dils = [dil for _, dil in DIL_PATTERNS]
    o1, o2, o3 = (token_major(r, i, dil) for i, (r, dil) in enumerate(zip((o1_ref, o2_ref, o3_ref), dils)))
    l1, l2, l3 = (token_major(r, 3 + i, dil) for i, (r, dil) in enumerate(zip((l1_ref, l2_ref, l3_ref), dils)))
    mx = jnp.maximum(jnp.maximum(l1, l2), l3)
    e1, e2, e3 = jnp.exp2(l1 - mx), jnp.exp2(l2 - mx), jnp.exp2(l3 - mx)
    inv = 1.0 / (e1 + e2 + e3)
    o_dil = (e1 * inv) * o1 + (e2 * inv) * o2 + (e3 * inv) * o3
    z_d = gate_seg(cz, 2 * cz)
    o_dil = o_dil * (z_d * _sigmoid(z_d))
    a = jnp.dot(o_nsa.astype(BF16), wbn_ref[...], preferred_element_type=F32)
    bd = jnp.dot(o_dil.astype(BF16), wbd_ref[...], preferred_element_type=F32)
    merged = (_sigmoid(gate_seg(2 * cz, 2 * cz + d)) * a
              + _sigmoid(gate_seg(2 * cz + d, 2 * cz + 2 * d)) * bd)
    y = jnp.dot(merged.astype(BF16), wo_ref[...], preferred_element_type=F32)
    out_ref[...] = _rms(x + y, fg_ref[...]) if final else x + y


def _merge_call(x, g, wg, o_nsa, o_d, lse_d, wbn, wbd, wo, fg, tm, final):
    B, S, D = x.shape
    cz = o_nsa.shape[-1]
    row = lambda c: pl.BlockSpec((None, tm, c), lambda b, i: (b, i, 0))
    full = lambda a: pl.BlockSpec(a.shape, lambda b, i: (0,) * a.ndim)
    phase = [pl.BlockSpec((None, tm // dil, dil * cz), lambda b, i: (b, i, 0)) for _, dil in DIL_PATTERNS]
    return pl.pallas_call(
        functools.partial(_merge_kernel, final=final),
        grid=(B, S // tm),
        in_specs=[row(D), full(g), full(wg), row(cz)] + phase + phase
                 + [full(wbn), full(wbd), full(wo), full(fg)],
        out_specs=row(D),
        out_shape=jax.ShapeDtypeStruct((B, S, D), F32),
        scratch_shapes=[pltpu.VMEM((2 * len(DIL_PATTERNS) * cz // LANES, tm, LANES), F32)],
        compiler_params=pltpu.CompilerParams(
            dimension_semantics=("parallel", "parallel"), vmem_limit_bytes=VMEM_LIMIT),
        name="merge",
    )(x, g, wg, o_nsa, *o_d, *lse_d, wbn, wbd, wo, fg)


def _proj_columns(d_model):
    hd, G, H = HEAD_DIM, NSA_GROUPS, NSA_HEADS
    nsa_w = H * hd
    o_q, o_kv = 0, nsa_w
    o_g = o_kv + 6 * G * hd
    o_zn = o_g + 3 * H
    o_qkvd = o_zn + nsa_w
    dil_w = DIL_HEADS * hd
    o_zd = o_qkvd + 3 * dil_w
    o_mg = o_zd + dil_w
    kv = lambda j, g: o_kv + (j * G + g) * hd + np.arange(hd)
    cols = [o_q + np.arange(nsa_w)]
    cols += [kv(j, g) for j in (0, 1) for g in range(G)]
    for j in (2, 4, 3, 5):
        cols += [kv(j, g) for g in range(G)]
    lanes = np.full(LANES, -1)
    for g in range(G):
        for j in range(3):
            lanes[g * GATE_ROWS + j * NSA_HPG + np.arange(NSA_HPG)] = o_g + j * H + g * NSA_HPG + np.arange(NSA_HPG)
    cols.append(lanes)
    cols.append(o_qkvd + np.arange(3 * dil_w))
    proj_cols = np.concatenate(cols)
    assert proj_cols.size == _PROJ_OFFS["vd"][1]
    gate_cols = np.concatenate([o_zn + np.arange(nsa_w), o_zd + np.arange(dil_w),
                                o_mg + np.arange(2 * d_model)])
    return proj_cols, gate_cols


def _take_columns(w, cols, dtype):
    pieces, start = [], 0
    for i in range(1, len(cols) + 1):
        if i < len(cols):
            same_run = (cols[i] < 0 and cols[i - 1] < 0) or (cols[i - 1] >= 0 and cols[i] == cols[i - 1] + 1)
        if i == len(cols) or not same_run:
            n = i - start
            pieces.append(jnp.zeros((w.shape[0], n), w.dtype) if cols[start] < 0
                          else w[:, int(cols[start]):int(cols[start]) + n])
            start = i
    return jnp.concatenate(pieces, axis=1).astype(dtype)


def _compress_weights(pos_k, w1_k, w2_k, pos_v, w1_v, w2_v):
    hd, G, half = HEAD_DIM, NSA_GROUPS, CMP_BLOCK // 2
    w1 = jnp.stack([w1_k] * G + [w1_v] * G).astype(BF16)
    pos = jnp.stack([pos_k] * G + [pos_v] * G)
    w2 = jnp.stack([w2_k] * G + [w2_v] * G).astype(BF16)
    halves = lambda a, tail: (a[:, :half].reshape((2 * G,) + tail), a[:, half:].reshape((2 * G,) + tail))
    posa, posb = halves(pos, (1, half * hd))
    wa, wb = halves(w1, (half * hd, hd))
    return posa, posb, wa, wb, w2


def _overlap_t(n_cmp_pad, n_sel):
    i = np.arange(n_cmp_pad)[None, :]
    j = np.arange(n_sel)[:, None]
    lo = np.maximum(i * CMP_STRIDE, j * SLC_BLOCK)
    hi = np.minimum(i * CMP_STRIDE + CMP_BLOCK, (j + 1) * SLC_BLOCK)
    return (np.clip(hi - lo, 0, None) / CMP_BLOCK).astype(np.float32)


def kernel(x, norm_g, w_in, cmp_pos_k, cmp_w1_k, cmp_w2_k, cmp_pos_v, cmp_w1_v, cmp_w2_v,
           w_br_nsa, w_br_dil, w_out, final_g):
    B, S, D = x.shape
    n_sel = S // SLC_BLOCK
    assert S % (Q_BLOCK * max(d for _, d in DIL_PATTERNS)) == 0 and n_sel % FLAG_BITS == 0
    n_cmp_pad = S // CMP_STRIDE
    slopes_nsa, slopes_dil = _alibi_slopes()
    proj_cols, gate_cols = _proj_columns(D)
    ovt = jnp.asarray(_overlap_t(n_cmp_pad, n_sel))
    qfeat = jnp.asarray(np.broadcast_to(
        _slope_features(slopes_nsa).reshape(NSA_GROUPS, NSA_HPG, 1, LANES),
        (NSA_GROUPS, NSA_HPG, 8, LANES)))
    qfeat_dil = jnp.asarray(np.broadcast_to(
        _slope_features(slopes_dil).reshape(DIL_HEADS, 1, LANES), (DIL_HEADS, 8, LANES)))
    h = x
    for layer in range(w_in.shape[0]):
        w_proj = _take_columns(w_in[layer], proj_cols, BF16)
        w_gate = _take_columns(w_in[layer], gate_cols, BF16)
        g_l = norm_g[layer].reshape(1, D)

        qn, cmp_src, ks, kw, vst, vwt, gates, *dil = _proj_call(h, g_l, w_proj, min(PROJ_ROWS, S))
        cw = _compress_weights(cmp_pos_k[layer], cmp_w1_k[layer], cmp_w2_k[layer],
                               cmp_pos_v[layer], cmp_w1_v[layer], cmp_w2_v[layer])
        kc, cvo = _compress_call(cmp_src, *cw, ovt)
        o_nsa = _nsa_call(qfeat, qn, kc, cvo, ks, vst, kw, vwt, gates)
        dil_out = _dilated_call(qfeat_dil, [dil[3 * i:3 * i + 3] for i in range(len(DIL_PATTERNS))])
        o_d, lse_d = [o for o, _ in dil_out], [lse for _, lse in dil_out]
        h = _merge_call(h, g_l, w_gate, o_nsa, o_d, lse_d, w_br_nsa[layer].astype(BF16),
                        w_br_dil[layer].astype(BF16), w_out[layer].astype(BF16),
                        final_g.reshape(1, D), min(MERGE_ROWS, S), final=layer == w_in.shape[0] - 1)
    return h
```

```python
import functools
import math

import numpy as np
import jax
import jax.numpy as jnp
from jax import lax
from jax.experimental import pallas as pl
from jax.experimental.pallas import tpu as pltpu

F32 = jnp.float32
BF16 = jnp.bfloat16

HEAD_DIM = 64
LANES = 128
NSA_HEADS = 8
NSA_GROUPS = 2
NSA_HPG = NSA_HEADS // NSA_GROUPS
CMP_BLOCK = 32
CMP_STRIDE = 16
SLC_BLOCK = 64
SLC_TOPK = 16
WIN = 512
FORCE_SCORE = 1.0e4
DIL_HEADS = 8
DIL_PATTERNS = ((128, 1), (512, 4), (2048, 16))
Q_BLOCK = 128
RMS_EPS = 1e-6
KEY_CHUNK = 128
VT_ROWS = 80
PAIR_VT_ROWS = 144
GATE_ROWS = 16
FEAT_SPLIT = 3
FLAG_BITS = 16
SLC_GROUP = 4
NSA_NEAR_CHUNKS = 8
PROJ_ROWS = 1024
MERGE_ROWS = 512
PERM_ROWS = 128
WIN_QB = 256
NSA_QB = 512
DIL_BAND = 4
N_FORCED = 3
assert FORCE_SCORE > NSA_HPG + 1
SEL_PAD_ROWS = 8
ONES_ROWS = 16
LOG2E = math.log2(math.e)
VMEM_LIMIT = 56 * 1024 * 1024


def _alibi_slopes():
    n = NSA_HEADS + DIL_HEADS
    s = 2.0 ** (-8.0 * np.arange(1, n + 1) / n)
    return s[0::2].astype(np.float32), s[1::2].astype(np.float32)


def _nt(a, b):
    return lax.dot_general(a, b, (((1,), (1,)), ((), ())), preferred_element_type=F32)


def _rms(x, g):
    return (x * lax.rsqrt(jnp.mean(x * x, axis=-1, keepdims=True) + RMS_EPS)) * g


def _sigmoid(x):
    return 0.5 + 0.5 * jnp.tanh(0.5 * x)


def _exp2_bf16(x):
    return jnp.exp2(x.astype(BF16))


def _head_tile(q_pairs, h):
    tile = q_pairs[:, (h // 2) * LANES:(h // 2 + 1) * LANES]
    return pltpu.roll(tile, HEAD_DIM, 1) if h % 2 else tile


def _pos_features(pos):
    lane = lax.broadcasted_iota(jnp.int32, pos.shape, 1) - HEAD_DIM
    hi = (pos // LANES).astype(F32)
    lo = (pos % LANES).astype(F32)
    return jnp.where(lane < 0, 0.0, jnp.where(lane < FEAT_SPLIT, hi, jnp.where(lane < 2 * FEAT_SPLIT, lo, 0.0)))


def _slope_features(slopes):
    out = np.zeros((len(slopes), LANES), np.float32)
    for h, s in enumerate(slopes):
        rest = np.float32(s) * np.float32(LOG2E)
        for i in range(FEAT_SPLIT):
            piece = np.float32(np.asarray(rest, np.float32).astype(BF16))
            out[h, HEAD_DIM + FEAT_SPLIT + i] = piece
            out[h, HEAD_DIM + i] = piece * LANES
            rest = np.float32(rest - piece)
    return out


def _ones_rows(width, dtype):
    row = lax.broadcasted_iota(jnp.int32, (ONES_ROWS, width), 0)
    return jnp.where(row == 0, 1.0, 0.0).astype(dtype)


_PROJ_SEGS = (("qn", 512), ("cmp", 256), ("ks", 128), ("kw", 128), ("vs", 128), ("vw", 128),
              ("gate", 128), ("qd", 512), ("kd", 512), ("vd", 512))
DIL_Q_W = DIL_HEADS * HEAD_DIM
_PROJ_OFFS = dict(zip((n for n, _ in _PROJ_SEGS),
                      zip(np.cumsum([0] + [w for _, w in _PROJ_SEGS])[:-1].tolist(),
                          np.cumsum([w for _, w in _PROJ_SEGS]).tolist())))


def _to_lane_tiles(scr, first, val):
    for j in range(val.shape[1] // LANES):
        scr[first + j] = val[:, j * LANES:(j + 1) * LANES]


def _proj_kernel(x_ref, g_ref, w_ref, perm_ref, qn_ref, cmp_ref, ks_ref, kw_ref, vst_ref, vwt_ref,
                 gate_ref, *dil_refs):
    cmp_scr = dil_refs[-1]
    tm = x_ref.shape[0]
    xb = _rms(x_ref[...], g_ref[...]).astype(BF16)
    scale = HEAD_DIM ** -0.5

    def seg(name, last=None):
        c0, c1 = _PROJ_OFFS[name][0], _PROJ_OFFS[last or name][1]
        return jnp.dot(xb, w_ref[:, c0:c1], preferred_element_type=F32)

    qn_ref[...] = (seg("qn") * (scale * LOG2E)).astype(BF16)
    _to_lane_tiles(cmp_scr, 0, seg("cmp"))
    lane_c = lax.broadcasted_iota(jnp.int32, (tm // CMP_STRIDE, LANES), 1)
    slot_w = CMP_STRIDE * HEAD_DIM
    for t in range(cmp_scr.shape[0]):
        for p in range(0, CMP_STRIDE, 2):
            even, odd = (cmp_scr[t, pl.ds(p + i, tm // CMP_STRIDE, stride=CMP_STRIDE), :] for i in range(2))
            for half in range(2):
                pair = (jnp.where(lane_c < HEAD_DIM, even, pltpu.roll(odd, HEAD_DIM, 1)) if half == 0
                        else jnp.where(lane_c < HEAD_DIM, pltpu.roll(even, HEAD_DIM, 1), odd))
                c0 = (2 * t + half) * slot_w + p * HEAD_DIM
                cmp_ref[:, c0:c0 + LANES] = pair
    feat =_pos_features(pl.program_id(1) * tm + lax.broadcasted_iota(jnp.int32, (tm, LANES), 0))
    k_sw, v_sw = seg("ks", "kw"), seg("vs", "vw")
    ks, kw = k_sw[:, 0:LANES], k_sw[:, LANES:]
    vst, vwt = v_sw[:, 0:LANES].T, v_sw[:, LANES:].T
    gate_t = _sigmoid(seg("gate")).T
    tail = _ones_rows(tm, F32)
    lane = lax.broadcasted_iota(jnp.int32, (tm, LANES), 1)
    for g in range(NSA_GROUPS):
        ks_ref[g] = jnp.where(lane < HEAD_DIM, _head_tile(ks, g), feat).astype(BF16)
        kw_ref[g] = jnp.where(lane < HEAD_DIM, _head_tile(kw, g), feat).astype(BF16)
        gate_ref[g] = gate_t[g * GATE_ROWS:(g + 1) * GATE_ROWS]
        for src, dst in ((vst, vst_ref), (vwt, vwt_ref)):
            vt = jnp.concatenate([src[g * HEAD_DIM:(g + 1) * HEAD_DIM], tail], axis=0).astype(BF16)
            for j in range(tm // KEY_CHUNK):
                dst[g, j] = vt[:, j * KEY_CHUNK:(j + 1) * KEY_CHUNK]
    qkv = jnp.concatenate([(seg("qd") * (scale * LOG2E)).astype(BF16), seg("kd").astype(BF16),
                           seg("vd").astype(BF16)], axis=1)
    bounds = (0, DIL_Q_W, 2 * DIL_Q_W, 3 * DIL_Q_W)
    subs = tm // PERM_ROWS
    for i, (_, d) in enumerate(DIL_PATTERNS):
        if d > 1:
            by_phase = [jnp.dot(perm_ref[i], qkv[s * PERM_ROWS:(s + 1) * PERM_ROWS],
                                preferred_element_type=F32) for s in range(subs)]
        r = PERM_ROWS // d
        for p in range(d):
            rows = qkv if d == 1 else jnp.concatenate(
                [by_phase[s][p * r:(p + 1) * r] for s in range(subs)], axis=0).astype(BF16)
            for j, ref in enumerate(dil_refs[3 * i:3 * i + 3]):
                width = bounds[j + 1] - bounds[j]
                ref[:, p * width:(p + 1) * width] = rows[:, bounds[j]:bounds[j + 1]]


def _proj_call(x, g, w, tm):
    B, S, D = x.shape
    G = NSA_GROUPS
    cmp_w = _PROJ_OFFS["cmp"][1] - _PROJ_OFFS["cmp"][0]
    row = lambda c: pl.BlockSpec((None, tm, c), lambda b, i: (b, i, 0))
    k_spec = pl.BlockSpec((None, G, tm, LANES), lambda b, i: (b, 0, i, 0))
    vt_spec = pl.BlockSpec((None, G, tm // KEY_CHUNK, VT_ROWS, KEY_CHUNK), lambda b, i: (b, 0, i, 0, 0))
    out_shape = (
        jax.ShapeDtypeStruct((B, S, 512), BF16),
        jax.ShapeDtypeStruct((B, S // CMP_STRIDE, CMP_STRIDE * cmp_w), F32),
        jax.ShapeDtypeStruct((B, G, S, LANES), BF16),
        jax.ShapeDtypeStruct((B, G, S, LANES), BF16),
        jax.ShapeDtypeStruct((B, G, S // KEY_CHUNK, VT_ROWS, KEY_CHUNK), BF16),
        jax.ShapeDtypeStruct((B, G, S // KEY_CHUNK, VT_ROWS, KEY_CHUNK), BF16),
        jax.ShapeDtypeStruct((B, G, GATE_ROWS, S), F32),
    )
    dil_specs = ()
    for _, d in DIL_PATTERNS:
        for width in (DIL_Q_W,) * 3:
            out_shape += (jax.ShapeDtypeStruct((B, S // d, d * width), BF16),)
            dil_specs += (pl.BlockSpec((None, tm // d, d * width), lambda b, i: (b, i, 0)),)
    r = np.arange(PERM_ROWS)
    perm = np.zeros((len(DIL_PATTERNS), PERM_ROWS, PERM_ROWS), np.float32)
    for i, (_, d) in enumerate(DIL_PATTERNS):
        assert (PERM_ROWS // d) % 8 == 0
        perm[i, (r % d) * (PERM_ROWS // d) + r // d, r] = 1.0
    perm = jnp.asarray(perm, BF16)
    return pl.pallas_call(
        _proj_kernel,
        grid=(B, S // tm),
        in_specs=[row(D), pl.BlockSpec((1, D), lambda b, i: (0, 0)),
                  pl.BlockSpec(w.shape, lambda b, i: (0, 0)),
                  pl.BlockSpec(perm.shape, lambda b, i: (0, 0, 0))],
        out_specs=(row(512),
                   pl.BlockSpec((None, tm // CMP_STRIDE, CMP_STRIDE * cmp_w), lambda b, i: (b, i, 0)),
                   k_spec, k_spec, vt_spec, vt_spec,
                   pl.BlockSpec((None, G, GATE_ROWS, tm), lambda b, i: (b, 0, 0, i))) + dil_specs,
        out_shape=out_shape,
        scratch_shapes=[pltpu.VMEM((cmp_w // LANES, tm, LANES), F32)],
        compiler_params=pltpu.CompilerParams(
            dimension_semantics=("parallel", "parallel"), vmem_limit_bytes=VMEM_LIMIT),
        name="proj",
    )(x, g, w, perm)


def _compress_kernel(r_ref, posa_ref, posb_ref, wa_ref, wb_ref, w2_ref, ovt_ref, kc_ref, cvo_ref):
    n = r_ref.shape[0]
    slot_w = CMP_STRIDE * HEAD_DIM
    outs = []
    for s in range(2 * NSA_GROUPS):
        r = r_ref[:, s * slot_w:(s + 1) * slot_w]
        ha = jnp.dot((r + posa_ref[s]).astype(BF16), wa_ref[s], preferred_element_type=F32)
        hb = jnp.dot((r + posb_ref[s]).astype(BF16), wb_ref[s], preferred_element_type=F32)
        hid = jax.nn.gelu(ha + pltpu.roll(hb, n - 1, 0))
        outs.append(jnp.dot(hid.astype(BF16), w2_ref[s], preferred_element_type=F32))
    cmp_end = lax.broadcasted_iota(jnp.int32, (n, LANES), 0) * CMP_STRIDE + (CMP_BLOCK - 1)
    feat = _pos_features(cmp_end)
    zeros = jnp.zeros((n, LANES - HEAD_DIM), F32)
    for g in range(NSA_GROUPS):
        kc_ref[g] = jnp.concatenate([outs[g], feat[:, HEAD_DIM:]], axis=1).astype(BF16)
        v_t = jnp.concatenate([outs[NSA_GROUPS + g], zeros], axis=1).T[0:HEAD_DIM]
        cvo_ref[g] = jnp.concatenate([v_t, ovt_ref[...], _ones_rows(n, F32)], axis=0).astype(BF16)


def _compress_call(r, posa, posb, wa, wb, w2, ovt):
    B, R, C = r.shape
    G = NSA_GROUPS
    rows = HEAD_DIM + ovt.shape[0] + ONES_ROWS
    full = lambda a: pl.BlockSpec(a.shape, lambda b: (0,) * a.ndim)
    return pl.pallas_call(
        _compress_kernel,
        grid=(B,),
        in_specs=[pl.BlockSpec((None, R, C), lambda b: (b, 0, 0)),
                  full(posa), full(posb), full(wa), full(wb), full(w2), full(ovt)],
        out_specs=(pl.BlockSpec((None, G, R, LANES), lambda b: (b, 0, 0, 0)),
                   pl.BlockSpec((None, G, rows, R), lambda b: (b, 0, 0, 0))),
        out_shape=(jax.ShapeDtypeStruct((B, G, R, LANES), BF16),
                   jax.ShapeDtypeStruct((B, G, rows, R), BF16)),
        compiler_params=pltpu.CompilerParams(
            dimension_semantics=("parallel",), vmem_limit_bytes=VMEM_LIMIT),
        name="compress",
    )(r, posa, posb, wa, wb, w2, ovt)


def _flash_step(s, bias, v_t, m_ref, acc_ref):
    ps, alphas = [], []
    qb = bias.shape[1]
    for h in range(NSA_HPG):
        cols = slice(h * qb, (h + 1) * qb)
        sh = s[:, cols] + bias
        m_old = m_ref[:, cols]
        m_new = jnp.maximum(m_old, jnp.max(sh, axis=0, keepdims=True))
        alphas.append(jnp.exp2(m_old - m_new))
        ps.append(_exp2_bf16(sh - m_new))
        m_ref[:, cols] = m_new
    pv = jnp.dot(v_t, jnp.concatenate(ps, axis=1), preferred_element_type=F32)
    acc_ref[...] = jnp.concatenate(alphas, axis=1) * acc_ref[...] + pv


def _softmax_pv(s, bias, v_t):
    ms, ps = [], []
    qb = bias.shape[1]
    for h in range(NSA_HPG):
        sh = s[:, h * qb:(h + 1) * qb] + bias
        ms.append(jnp.max(sh, axis=0, keepdims=True))
        ps.append(_exp2_bf16(sh - ms[h]))
    r = jnp.dot(v_t, jnp.concatenate(ps, axis=1), preferred_element_type=F32)
    return jnp.concatenate(ms, axis=1), r


def _normalise(acc):
    return acc[0:HEAD_DIM] * (1.0 / jnp.maximum(acc[HEAD_DIM:HEAD_DIM + 1], 1e-30))


def _nsa_kernel(qfeat_ref, q_ref, kc_ref, cvo_ref, ks_ref, vst_ref, kw_ref, vwt_ref, gate_ref,
                o_ref, sel_ref, ms_ref, accs_ref, flag_ref, list_ref):
    n = pl.program_id(2)
    qb, kc = NSA_QB, KEY_CHUNK
    t0 = n * qb
    lane = lax.broadcasted_iota(jnp.int32, (qb, LANES), 1)
    qf = q_ref[...].astype(F32)
    q = jnp.concatenate(
        [jnp.where(lane < HEAD_DIM, _head_tile(qf, h), qfeat_ref[h][0:1, :]) for h in range(NSA_HPG)],
        axis=0).astype(BF16)

    wq = min(WIN_QB, qb)
    span = WIN + wq
    back_w = (lax.broadcasted_iota(jnp.int32, (span, wq), 1)
              - lax.broadcasted_iota(jnp.int32, (span, wq), 0))
    win_parts = []
    for j in range(qb // wq):
        k_lo = pl.multiple_of(jnp.maximum(t0 + j * wq - WIN, 0), kc)
        dist = back_w + (t0 + j * wq - k_lo)
        bias = jnp.where(dist >= 0, jnp.where(dist < WIN, 0.0, -jnp.inf), -jnp.inf)
        q_part = jnp.concatenate([q[h * qb + j * wq:h * qb + (j + 1) * wq] for h in range(NSA_HPG)], axis=0)
        vt_w = jnp.concatenate([vwt_ref[k_lo // kc + u] for u in range(span // kc)], axis=1)
        win_parts.append(_normalise(_softmax_pv(_nt(kw_ref[pl.ds(k_lo, span), :], q_part), bias, vt_w)[1]))
    o_win = jnp.concatenate([part[:, h * wq:(h + 1) * wq] for h in range(NSA_HPG) for part in win_parts],
                            axis=1)

    n_cmp = kc_ref.shape[0]
    n_sel = sel_ref.shape[0] - SEL_PAD_ROWS
    s = _nt(kc_ref[...], q)
    cmp_end = lax.broadcasted_iota(jnp.int32, (n_cmp, qb), 0) * CMP_STRIDE + (CMP_BLOCK - 1)
    bias_c = jnp.where(cmp_end <= t0 + lax.broadcasted_iota(jnp.int32, (n_cmp, qb), 1), 0.0, -jnp.inf)
    es = []
    for h in range(NSA_HPG):
        sh = s[:, h * qb:(h + 1) * qb] + bias_c
        m = jnp.max(sh, axis=0, keepdims=True)
        es.append(jnp.exp2(sh - jnp.where(m == -jnp.inf, 0.0, m)).astype(BF16))
    r = jnp.dot(cvo_ref[...], jnp.concatenate(es, axis=1), preferred_element_type=F32)
    inv_l = 1.0 / jnp.maximum(r[HEAD_DIM + n_sel:HEAD_DIM + n_sel + 1], 1e-30)
    o_cmp = r[0:HEAD_DIM] * inv_l
    imp = r[HEAD_DIM:HEAD_DIM + n_sel] * inv_l
    imp_t = imp[:, 0:qb]
    for h in range(1, NSA_HPG):
        imp_t = imp_t + imp[:, h * qb:(h + 1) * qb]

    blk = lax.broadcasted_iota(jnp.int32, (n_sel, qb), 0).astype(F32)
    cur = ((t0 + lax.broadcasted_iota(jnp.int32, (n_sel, qb), 1)) // SLC_BLOCK).astype(F32)
    ago = cur - blk
    forced = jnp.where(blk == 0, 1.0, jnp.where(ago == 0, 1.0, jnp.where(ago == 1, 1.0, 0.0)))
    valid = ago >= 0
    score = jnp.where(valid, jnp.where(forced > 0.5, -1.0, imp_t), -1.0)
    for _ in range(max(min(SLC_TOPK, n_sel) - N_FORCED, 0)):
        idx = jnp.argmax(score, axis=0, keepdims=True).astype(F32)
        score = jnp.where(blk == idx, -jnp.inf, score)
    sel_t = jnp.where(valid, jnp.where(score == -jnp.inf, 1.0, forced), 0.0)
    sel_ref[0:n_sel, :] = jnp.where(sel_t > 0.5, 0.0, -jnp.inf)
    sel_ref[n_sel:, :] = jnp.full((SEL_PAD_ROWS, qb), -jnp.inf, F32)
    any_q = jnp.max(sel_t, axis=1, keepdims=True)
    bit = jnp.left_shift(1, lax.broadcasted_iota(jnp.int32, (n_sel, 1), 0) % FLAG_BITS).astype(F32)
    packed = any_q * bit
    for i in range(n_sel // FLAG_BITS):
        word = jnp.sum(packed[i * FLAG_BITS:(i + 1) * FLAG_BITS], axis=0, keepdims=True)
        flag_ref[i] = word.astype(jnp.int32)[0, 0]

    blocks_per_chunk = kc // SLC_BLOCK
    chunks_per_word = FLAG_BITS // blocks_per_chunk
    row = lax.broadcasted_iota(jnp.int32, (kc, qb), 0)

    def chunk_bias(first_block):
        bias = sel_ref[pl.ds(first_block, 1), :]
        for j in range(1, blocks_per_chunk):
            bias = jnp.where(row < j * SLC_BLOCK, bias, sel_ref[pl.ds(first_block + j, 1), :])
        return bias

    own = qb // kc
    first_own = t0 // kc
    near = min(NSA_NEAR_CHUNKS, ks_ref.shape[0] // kc)
    k_near = pl.multiple_of(jnp.maximum(t0 - near * kc, 0), kc)
    first_near = k_near // kc
    off_first = jnp.where(first_near >= 1, 0.0, -jnp.inf)
    query = lax.broadcasted_iota(jnp.int32, (kc, qb), 1)
    biases = [chunk_bias(0) + off_first]
    for u in range(own):
        causal = jnp.where(row + u * kc <= query, 0.0, -jnp.inf)
        biases.append(chunk_bias((first_own + u) * blocks_per_chunk) + causal)
    keys = jnp.concatenate([ks_ref[0:kc, :], ks_ref[pl.ds(pl.multiple_of(t0, kc), qb), :]], axis=0)
    vt_s = jnp.concatenate([vst_ref[0]] + [vst_ref[first_own + u] for u in range(own)], axis=1)
    m_s, acc_s = _softmax_pv(_nt(keys, q), jnp.concatenate(biases, axis=0), vt_s)
    ms_ref[...] = m_s
    accs_ref[...] = acc_s

    biases = []
    for u in range(near):
        off = jnp.where(first_near + u < first_own, 0.0, -jnp.inf)
        biases.append(chunk_bias((first_near + u) * blocks_per_chunk) + off)
    _flash_step(_nt(ks_ref[pl.ds(k_near, near * kc), :], q), jnp.concatenate(biases, axis=0),
                jnp.concatenate([vst_ref[first_near + u] for u in range(near)], axis=1), ms_ref, accs_ref)

    def scan_body(w, cnt):
        word = flag_ref[w]
        for j in range(chunks_per_word):
            c = w * chunks_per_word + j
            bits = (word >> (j * blocks_per_chunk)) & ((1 << blocks_per_chunk) - 1)
            list_ref[cnt] = c
            cnt = cnt + ((bits != 0) & (c >= 1) & (c < first_near)).astype(jnp.int32)
        return cnt

    cnt = lax.fori_loop(0, (first_near + chunks_per_word - 1) // chunks_per_word, scan_body, 0)
    for u in range(SLC_GROUP):
        list_ref[cnt + u] = -1

    def slc_body(gi, carry):
        keys, biases, vts = [], [], []
        for u in range(SLC_GROUP):
            c = list_ref[gi * SLC_GROUP + u]
            cc = jnp.maximum(c, 0)
            biases.append(chunk_bias(jnp.where(c >= 0, cc * blocks_per_chunk, n_sel)))
            keys.append(ks_ref[pl.ds(pl.multiple_of(cc * kc, kc), kc), :])
            vts.append(vst_ref[cc])
        _flash_step(_nt(jnp.concatenate(keys, axis=0), q), jnp.concatenate(biases, axis=0),
                    jnp.concatenate(vts, axis=1), ms_ref, accs_ref)
        return carry

    lax.fori_loop(0, (cnt + SLC_GROUP - 1) // SLC_GROUP, slc_body, 0)

    o_slc = _normalise(accs_ref[...])
    gate = gate_ref[...]
    outs = []
    for h in range(NSA_HPG):
        cols = slice(h * qb, (h + 1) * qb)
        gc = [gate[j * NSA_HPG + h:j * NSA_HPG + h + 1, :] for j in range(3)]
        outs.append(gc[0] * o_cmp[:, cols] + gc[1] * o_slc[:, cols] + gc[2] * o_win[:, cols])
    o_ref[...] = jnp.concatenate(outs, axis=0).T


def _nsa_call(qfeat, qn, kc, cvo, ks, vst, kw, vwt, gates):
    B, S, _ = qn.shape
    G = NSA_GROUPS
    qb = NSA_QB
    nb = S // qb
    n_sel = S // SLC_BLOCK
    width = NSA_HPG * qb
    per_group = lambda a: pl.BlockSpec((None, None) + a.shape[2:],
                                       lambda b, g, n: (b, g) + (0,) * (a.ndim - 2))
    return pl.pallas_call(
        _nsa_kernel,
        grid=(B, G, nb),
        in_specs=[
            pl.BlockSpec((None,) + qfeat.shape[1:], lambda b, g, n: (g, 0, 0, 0)),
            pl.BlockSpec((None, qb, NSA_HPG * HEAD_DIM), lambda b, g, n: (b, n, g)),
            per_group(kc), per_group(cvo), per_group(ks), per_group(vst), per_group(kw), per_group(vwt),
            pl.BlockSpec((None, None, GATE_ROWS, qb), lambda b, g, n: (b, g, 0, n)),
        ],
        out_specs=pl.BlockSpec((None, qb, NSA_HPG * HEAD_DIM), lambda b, g, n: (b, n, g)),
        out_shape=jax.ShapeDtypeStruct((B, S, NSA_HEADS * HEAD_DIM), F32),
        scratch_shapes=[pltpu.VMEM((n_sel + SEL_PAD_ROWS, qb), F32),
                        pltpu.VMEM((1, width), F32), pltpu.VMEM((VT_ROWS, width), F32),
                        pltpu.SMEM((n_sel // FLAG_BITS,), jnp.int32),
                        pltpu.SMEM((S // KEY_CHUNK + SLC_GROUP,), jnp.int32)],
        compiler_params=pltpu.CompilerParams(
            dimension_semantics=("parallel", "parallel", "arbitrary"), vmem_limit_bytes=VMEM_LIMIT),
        name="nsa",
    )(qfeat, qn, kc, cvo, ks, vst, kw, vwt, gates)


def _dilated_kernel(qfeat_ref, *refs):
    n_pat = len(DIL_PATTERNS)
    j = pl.program_id(1)
    for i, (window, dilation) in enumerate(DIL_PATTERNS):
        steps_per_phase = max(d for _, d in DIL_PATTERNS) // dilation
        _dilated_blocks(qfeat_ref, *refs[5 * i:5 * i + 5], *refs[5 * n_pat + 2 * i:5 * n_pat + 2 * i + 2],
                        wd=window // dilation, dilation=dilation,
                        phase=j // steps_per_phase, step=j % steps_per_phase)


def _dilated_blocks(qfeat_ref, q_ref, kp_ref, kc_ref, vp_ref, vc_ref, o_ref, lse_ref, *,
                    wd, dilation, phase, step):
    c = Q_BLOCK
    hd = HEAD_DIM
    blocks = q_ref.shape[0] // c
    lane = lax.broadcasted_iota(jnp.int32, (blocks * c, LANES), 1)
    key = lax.broadcasted_iota(jnp.int32, (2 * c, c), 0)
    qry = lax.broadcasted_iota(jnp.int32, (2 * c, c), 1)
    first = step == 0
    off_later = jnp.where(key < c, qry - (key + (wd - c)), (key - c) - qry)
    off_first = jnp.where(key < c, jnp.where(first, 2 * c, qry) - (key + (wd - c)), (key - c) - qry)
    bias_later = jnp.where(off_later <= 0, 0.0, -jnp.inf)
    bias_first = jnp.where(off_first <= 0, 0.0, -jnp.inf)
    sub = (step * blocks - 1) * c + lax.broadcasted_iota(jnp.int32, ((blocks + 1) * c, LANES), 0)
    pos_feat = _pos_features(sub * dilation + phase).astype(BF16)
    n_keys = (blocks + 1) * c
    tail = _ones_rows(n_keys, BF16)
    assert PAIR_VT_ROWS == 2 * hd + ONES_ROWS
    for p in range(DIL_HEADS // 2):
        cols = slice(p * LANES, (p + 1) * LANES)
        q_pair = q_ref[:, cols]
        zero = jnp.zeros_like(q_pair)
        feats = [jnp.broadcast_to(qfeat_ref[2 * p + i][0:1, :], (blocks * c, LANES)).astype(BF16)
                 for i in range(2)]
        q_a = jnp.concatenate([jnp.where(lane < hd, q_pair, zero), feats[0]], axis=1)
        q_b = jnp.concatenate([jnp.where(lane < hd, zero, q_pair), feats[1]], axis=1)
        k_all = jnp.concatenate([kp_ref[:, cols], kc_ref[:, cols]], axis=0)
        k_all = jnp.concatenate([k_all, pos_feat], axis=1)
        v_t = jnp.concatenate([jnp.concatenate([vp_ref[:, cols].T, vc_ref[:, cols].T], axis=1), tail],
                              axis=0)
        nb = min(DIL_BAND, blocks)
        for g0 in range(0, blocks, nb):
            q_aug = jnp.concatenate([part[j * c:(j + 1) * c] for j in range(g0, g0 + nb)
                                     for part in (q_a, q_b)], axis=0)
            s_all = _nt(k_all[g0 * c:(g0 + nb + 1) * c], q_aug)
            ms, band = [], []
            for j in range(nb):
                bias = bias_later if g0 + j else bias_first
                probs = []
                for i in range(2):
                    si = s_all[j * c:(j + 2) * c, (2 * j + i) * c:(2 * j + i + 1) * c] + bias
                    ms.append(jnp.max(si, axis=0, keepdims=True))
                    probs.append(_exp2_bf16(si - ms[-1]))
                pieces = [jnp.zeros((j * c, 2 * c), BF16)] if j else []
                pieces.append(jnp.concatenate(probs, axis=1))
                if j < nb - 1:
                    pieces.append(jnp.zeros(((nb - 1 - j) * c, 2 * c), BF16))
                band.append(jnp.concatenate(pieces, axis=0))
            r_all = jnp.dot(v_t[:, g0 * c:(g0 + nb + 1) * c], jnp.concatenate(band, axis=1),
                            preferred_element_type=F32)
            for j in range(nb):
                outs, lses = [], []
                for i in range(2):
                    at = slice((2 * j + i) * c, (2 * j + i + 1) * c)
                    l = r_all[2 * hd:2 * hd + 1, at]
                    outs.append(r_all[i * hd:(i + 1) * hd, at] * (1.0 / l))
                    lses.append(jnp.broadcast_to(ms[2 * j + i] + jnp.log2(l), (hd, c)))
                rows = slice((g0 + j) * c, (g0 + j + 1) * c)
                o_ref[rows, cols] = jnp.concatenate(outs, axis=0).T
                lse_ref[rows, cols] = jnp.concatenate(lses, axis=0).T


def _dilated_call(qfeat, qkv_by_pattern):
    B = qkv_by_pattern[0][0].shape[0]
    steps = max(d for _, d in DIL_PATTERNS)
    in_specs = [pl.BlockSpec(qfeat.shape, lambda b, j: (0, 0, 0))]
    out_specs, out_shape, operands = [], [], [qfeat]
    for (window, d), (q_v, k_v, v_v) in zip(DIL_PATTERNS, qkv_by_pattern):
        assert Q_BLOCK - 1 <= window // d <= Q_BLOCK and steps % d == 0
        L = q_v.shape[1]
        spp = steps // d
        per_step = L // (Q_BLOCK * spp)
        blk = pl.BlockSpec((None, Q_BLOCK * per_step, DIL_Q_W),
                           lambda b, j, spp=spp: (b, j % spp, j // spp))
        prev = pl.BlockSpec((None, Q_BLOCK, DIL_Q_W),
                            lambda b, j, spp=spp, per_step=per_step:
                            (b, jnp.maximum((j % spp) * per_step - 1, 0), j // spp))
        in_specs += [blk, prev, blk, prev, blk]
        operands += [q_v, k_v, k_v, v_v, v_v]
        out_specs += [blk, blk]
        out_shape += [jax.ShapeDtypeStruct((B, L, d * DIL_Q_W), F32)] * 2
    outs = pl.pallas_call(
        _dilated_kernel,
        grid=(B, steps),
        in_specs=in_specs,
        out_specs=tuple(out_specs),
        out_shape=tuple(out_shape),
        compiler_params=pltpu.CompilerParams(
            dimension_semantics=("parallel", "arbitrary"), vmem_limit_bytes=VMEM_LIMIT),
        name="dilated",
    )(*operands)
    return [(outs[2 * i], outs[2 * i + 1]) for i in range(len(DIL_PATTERNS))]


def _merge_kernel(x_ref, g_ref, wg_ref, onsa_ref, o1_ref, o2_ref, o3_ref, l1_ref, l2_ref, l3_ref,
                  wbn_ref, wbd_ref, wo_ref, fg_ref, out_ref, tok_scr, *, final):
    x = x_ref[...]
    tm = x.shape[0]
    xb = _rms(x, g_ref[...]).astype(BF16)
    cz = NSA_HEADS * HEAD_DIM
    d = x.shape[-1]

    def gate_seg(c0, c1):
        return jnp.dot(xb, wg_ref[:, c0:c1], preferred_element_type=F32)

    def token_major(ref, slot, dil):
        if dil == 1:
            return ref[...]
        tiles = DIL_Q_W // LANES
        for p in range(dil):
            for j in range(tiles):
                tok_scr[slot * tiles + j, pl.ds(p, tm // dil, stride=dil), :] = \
                    ref[:, p * DIL_Q_W + j * LANES:p * DIL_Q_W + (j + 1) * LANES]
        return jnp.concatenate([tok_scr[slot * tiles + j] for j in range(tiles)], axis=1)

    z_n = gate_seg(0, cz)
    o_nsa = onsa_ref[...] * (z_n * _sigmoid(z_n))
    dils = [dil for _, dil in DIL_PATTERNS]
    o1, o2, o3 = (token_major(r, i, dil) for i, (r, dil) in enumerate(zip((o1_ref, o2_ref, o3_ref), dils)))
    l1, l2, l3 = (token_major(r, 3 + i, dil) for i, (r, dil) in enumerate(zip((l1_ref, l2_ref, l3_ref), dils)))
    mx = jnp.maximum(jnp.maximum(l1, l2), l3)
    e1, e2, e3 = jnp.exp2(l1 - mx), jnp.exp2(l2 - mx), jnp.exp2(l3 - mx)
    inv = 1.0 / (e1 + e2 + e3)
    o_dil = (e1 * inv) * o1 + (e2 * inv) * o2 + (e3 * inv) * o3
    z_d = gate_seg(cz, 2 * cz)
    o_dil = o_dil * (z_d * _sigmoid(z_d))
    a = jnp.dot(o_nsa.astype(BF16), wbn_ref[...], preferred_element_type=F32)
    bd = jnp.dot(o_dil.astype(BF16), wbd_ref[...], preferred_element_type=F32)
    merged = (_sigmoid(gate_seg(2 * cz, 2 * cz + d)) * a
              + _sigmoid(gate_seg(2 * cz + d, 2 * cz + 2 * d)) * bd)
    y = jnp.dot(merged.astype(BF16), wo_ref[...], preferred_element_type=F32)
    out_ref[...] = _rms(x + y, fg_ref[...]) if final else x + y


def _merge_call(x, g, wg, o_nsa, o_d, lse_d, wbn, wbd, wo, fg, tm, final):
    B, S, D = x.shape
    cz = o_nsa.shape[-1]
    row = lambda c: pl.BlockSpec((None, tm, c), lambda b, i: (b, i, 0))
    full = lambda a: pl.BlockSpec(a.shape, lambda b, i: (0,) * a.ndim)
    phase = [pl.BlockSpec((None, tm // dil, dil * cz), lambda b, i: (b, i, 0)) for _, dil in DIL_PATTERNS]
    return pl.pallas_call(
        functools.partial(_merge_kernel, final=final),
        grid=(B, S // tm),
        in_specs=[row(D), full(g), full(wg), row(cz)] + phase + phase
                 + [full(wbn), full(wbd), full(wo), full(fg)],
        out_specs=row(D),
        out_shape=jax.ShapeDtypeStruct((B, S, D), F32),
        scratch_shapes=[pltpu.VMEM((2 * len(DIL_PATTERNS) * cz // LANES, tm, LANES), F32)],
        compiler_params=pltpu.CompilerParams(
            dimension_semantics=("parallel", "parallel"), vmem_limit_bytes=VMEM_LIMIT),
        name="merge",
    )(x, g, wg, o_nsa, *o_d, *lse_d, wbn, wbd, wo, fg)


def _proj_columns(d_model):
    hd, G, H = HEAD_DIM, NSA_GROUPS, NSA_HEADS
    nsa_w = H * hd
    o_q, o_kv = 0, nsa_w
    o_g = o_kv + 6 * G * hd
    o_zn = o_g + 3 * H
    o_qkvd = o_zn + nsa_w
    dil_w = DIL_HEADS * hd
    o_zd = o_qkvd + 3 * dil_w
    o_mg = o_zd + dil_w
    kv = lambda j, g: o_kv + (j * G + g) * hd + np.arange(hd)
    cols = [o_q + np.arange(nsa_w)]
    cols += [kv(j, g) for j in (0, 1) for g in range(G)]
    for j in (2, 4, 3, 5):
        cols += [kv(j, g) for g in range(G)]
    lanes = np.full(LANES, -1)
    for g in range(G):
        for j in range(3):
            lanes[g * GATE_ROWS + j * NSA_HPG + np.arange(NSA_HPG)] = o_g + j * H + g * NSA_HPG + np.arange(NSA_HPG)
    cols.append(lanes)
    cols.append(o_qkvd + np.arange(3 * dil_w))
    proj_cols = np.concatenate(cols)
    assert proj_cols.size == _PROJ_OFFS["vd"][1]
    gate_cols = np.concatenate([o_zn + np.arange(nsa_w), o_zd + np.arange(dil_w),
                                o_mg + np.arange(2 * d_model)])
    return proj_cols, gate_cols


def _take_columns(w, cols, dtype):
    pieces, start = [], 0
    for i in range(1, len(cols) + 1):
        if i < len(cols):
            same_run = (cols[i] < 0 and cols[i - 1] < 0) or (cols[i - 1] >= 0 and cols[i] == cols[i - 1] + 1)
        if i == len(cols) or not same_run:
            n = i - start
            pieces.append(jnp.zeros((w.shape[0], n), w.dtype) if cols[start] < 0
                          else w[:, int(cols[start]):int(cols[start]) + n])
            start = i
    return jnp.concatenate(pieces, axis=1).astype(dtype)


def _compress_weights(pos_k, w1_k, w2_k, pos_v, w1_v, w2_v):
    hd, G, half = HEAD_DIM, NSA_GROUPS, CMP_BLOCK // 2
    w1 = jnp.stack([w1_k] * G + [w1_v] * G).astype(BF16)
    pos = jnp.stack([pos_k] * G + [pos_v] * G)
    w2 = jnp.stack([w2_k] * G + [w2_v] * G).astype(BF16)
    halves = lambda a, tail: (a[:, :half].reshape((2 * G,) + tail), a[:, half:].reshape((2 * G,) + tail))
    posa, posb = halves(pos, (1, half * hd))
    wa, wb = halves(w1, (half * hd, hd))
    return posa, posb, wa, wb, w2


def _overlap_t(n_cmp_pad, n_sel):
    i = np.arange(n_cmp_pad)[None, :]
    j = np.arange(n_sel)[:, None]
    lo = np.maximum(i * CMP_STRIDE, j * SLC_BLOCK)
    hi = np.minimum(i * CMP_STRIDE + CMP_BLOCK, (j + 1) * SLC_BLOCK)
    return (np.clip(hi - lo, 0, None) / CMP_BLOCK).astype(np.float32)


def kernel(x, norm_g, w_in, cmp_pos_k, cmp_w1_k, cmp_w2_k, cmp_pos_v, cmp_w1_v, cmp_w2_v,
           w_br_nsa, w_br_dil, w_out, final_g):
    B, S, D = x.shape
    n_sel = S // SLC_BLOCK
    assert S % (Q_BLOCK * max(d for _, d in DIL_PATTERNS)) == 0 and n_sel % FLAG_BITS == 0
    n_cmp_pad = S // CMP_STRIDE
    slopes_nsa, slopes_dil = _alibi_slopes()
    proj_cols, gate_cols = _proj_columns(D)
    ovt = jnp.asarray(_overlap_t(n_cmp_pad, n_sel))
    qfeat = jnp.asarray(np.broadcast_to(
        _slope_features(slopes_nsa).reshape(NSA_GROUPS, NSA_HPG, 1, LANES),
        (NSA_GROUPS, NSA_HPG, 8, LANES)))
    qfeat_dil = jnp.asarray(np.broadcast_to(
        _slope_features(slopes_dil).reshape(DIL_HEADS, 1, LANES), (DIL_HEADS, 8, LANES)))
    h = x
    for layer in range(w_in.shape[0]):
        w_proj = _take_columns(w_in[layer], proj_cols, BF16)
        w_gate = _take_columns(w_in[layer], gate_cols, BF16)
        g_l = norm_g[layer].reshape(1, D)

        qn, cmp_src, ks, kw, vst, vwt, gates, *dil = _proj_call(h, g_l, w_proj, min(PROJ_ROWS, S))
        cw = _compress_weights(cmp_pos_k[layer], cmp_w1_k[layer], cmp_w2_k[layer],
                               cmp_pos_v[layer], cmp_w1_v[layer], cmp_w2_v[layer])
        kc, cvo = _compress_call(cmp_src, *cw, ovt)
        o_nsa = _nsa_call(qfeat, qn, kc, cvo, ks, vst, kw, vwt, gates)
        dil_out = _dilated_call(qfeat_dil, [dil[3 * i:3 * i + 3] for i in range(len(DIL_PATTERNS))])
        o_d, lse_d = [o for o, _ in dil_out], [lse for _, lse in dil_out]
        h = _merge_call(h, g_l, w_gate, o_nsa, o_d, lse_d, w_br_nsa[layer].astype(BF16),
                        w_br_dil[layer].astype(BF16), w_out[layer].astype(BF16),
                        final_g.reshape(1, D), min(MERGE_ROWS, S), final=layer == w_in.shape[0] - 1)
    return h
```

```python
import functools
import math

import numpy as np
import jax
import jax.numpy as jnp
from jax import lax
from jax.experimental import pallas as pl
from jax.experimental.pallas import tpu as pltpu

F32 = jnp.float32
BF16 = jnp.bfloat16

HEAD_DIM = 64
LANES = 128
NSA_HEADS = 8
NSA_GROUPS = 2
NSA_HPG = NSA_HEADS // NSA_GROUPS
CMP_BLOCK = 32
CMP_STRIDE = 16
SLC_BLOCK = 64
SLC_TOPK = 16
WIN = 512
FORCE_SCORE = 1.0e4
DIL_HEADS = 8
DIL_PATTERNS = ((128, 1), (512, 4), (2048, 16))
Q_BLOCK = 128
RMS_EPS = 1e-6
KEY_CHUNK = 128
VT_ROWS = 80
PAIR_VT_ROWS = 144
GATE_ROWS = 16
FEAT_SPLIT = 3
FLAG_BITS = 16
SLC_GROUP = 4
NSA_NEAR_CHUNKS = 8
PROJ_ROWS = 1024
MERGE_ROWS = 512
PERM_ROWS = 128
WIN_QB = 256
NSA_QB = 512
DIL_BAND = 4
N_FORCED = 3
assert FORCE_SCORE > NSA_HPG + 1
SEL_PAD_ROWS = 8
ONES_ROWS = 16
LOG2E = math.log2(math.e)
VMEM_LIMIT = 56 * 1024 * 1024


def _alibi_slopes():
    n = NSA_HEADS + DIL_HEADS
    s = 2.0 ** (-8.0 * np.arange(1, n + 1) / n)
    return s[0::2].astype(np.float32), s[1::2].astype(np.float32)


def _nt(a, b):
    return lax.dot_general(a, b, (((1,), (1,)), ((), ())), preferred_element_type=F32)


def _rms(x, g):
    return (x * lax.rsqrt(jnp.mean(x * x, axis=-1, keepdims=True) + RMS_EPS)) * g


def _sigmoid(x):
    return 0.5 + 0.5 * jnp.tanh(0.5 * x)


def _exp2_bf16(x):
    return jnp.exp2(x.astype(BF16))


def _head_tile(q_pairs, h):
    tile = q_pairs[:, (h // 2) * LANES:(h // 2 + 1) * LANES]
    return pltpu.roll(tile, HEAD_DIM, 1) if h % 2 else tile


def _pos_features(pos):
    lane = lax.broadcasted_iota(jnp.int32, pos.shape, 1) - HEAD_DIM
    hi = (pos // LANES).astype(F32)
    lo = (pos % LANES).astype(F32)
    return jnp.where(lane < 0, 0.0, jnp.where(lane < FEAT_SPLIT, hi, jnp.where(lane < 2 * FEAT_SPLIT, lo, 0.0)))


def _slope_features(slopes):
    out = np.zeros((len(slopes), LANES), np.float32)
    for h, s in enumerate(slopes):
        rest = np.float32(s) * np.float32(LOG2E)
        for i in range(FEAT_SPLIT):
            piece = np.float32(np.asarray(rest, np.float32).astype(BF16))
            out[h, HEAD_DIM + FEAT_SPLIT + i] = piece
            out[h, HEAD_DIM + i] = piece * LANES
            rest = np.float32(rest - piece)
    return out


def _ones_rows(width, dtype):
    row = lax.broadcasted_iota(jnp.int32, (ONES_ROWS, width), 0)
    return jnp.where(row == 0, 1.0, 0.0).astype(dtype)


_PROJ_SEGS = (("qn", 512), ("cmp", 256), ("ks", 128), ("kw", 128), ("vs", 128), ("vw", 128),
              ("gate", 128), ("qd", 512), ("kd", 512), ("vd", 512))
DIL_Q_W = DIL_HEADS * HEAD_DIM
_PROJ_OFFS = dict(zip((n for n, _ in _PROJ_SEGS),
                      zip(np.cumsum([0] + [w for _, w in _PROJ_SEGS])[:-1].tolist(),
                          np.cumsum([w for _, w in _PROJ_SEGS]).tolist())))


def _to_lane_tiles(scr, first, val):
    for j in range(val.shape[1] // LANES):
        scr[first + j] = val[:, j * LANES:(j + 1) * LANES]


def _proj_kernel(x_ref, g_ref, w_ref, perm_ref, qn_ref, cmp_ref, ks_ref, kw_ref, vst_ref, vwt_ref,
                 gate_ref, *dil_refs):
    cmp_scr = dil_refs[-1]
    tm = x_ref.shape[0]
    xb = _rms(x_ref[...], g_ref[...]).astype(BF16)
    scale = HEAD_DIM ** -0.5

    def seg(name, last=None):
        c0, c1 = _PROJ_OFFS[name][0], _PROJ_OFFS[last or name][1]
        return jnp.dot(xb, w_ref[:, c0:c1], preferred_element_type=F32)

    qn_ref[...] = (seg("qn") * (scale * LOG2E)).astype(BF16)
    _to_lane_tiles(cmp_scr, 0, seg("cmp"))
    lane_c = lax.broadcasted_iota(jnp.int32, (tm // CMP_STRIDE, LANES), 1)
    slot_w = CMP_STRIDE * HEAD_DIM
    for t in range(cmp_scr.shape[0]):
        for p in range(0, CMP_STRIDE, 2):
            even, odd = (cmp_scr[t, pl.ds(p + i, tm // CMP_STRIDE, stride=CMP_STRIDE), :] for i in range(2))
            for half in range(2):
                pair = (jnp.where(lane_c < HEAD_DIM, even, pltpu.roll(odd, HEAD_DIM, 1)) if half == 0
                        else jnp.where(lane_c < HEAD_DIM, pltpu.roll(even, HEAD_DIM, 1), odd))
                c0 = (2 * t + half) * slot_w + p * HEAD_DIM
                cmp_ref[:, c0:c0 + LANES] = pair
    feat =_pos_features(pl.program_id(1) * tm + lax.broadcasted_iota(jnp.int32, (tm, LANES), 0))
    k_sw, v_sw = seg("ks", "kw"), seg("vs", "vw")
    ks, kw = k_sw[:, 0:LANES], k_sw[:, LANES:]
    vst, vwt = v_sw[:, 0:LANES].T, v_sw[:, LANES:].T
    gate_t = _sigmoid(seg("gate")).T
    tail = _ones_rows(tm, F32)
    lane = lax.broadcasted_iota(jnp.int32, (tm, LANES), 1)
    for g in range(NSA_GROUPS):
        ks_ref[g] = jnp.where(lane < HEAD_DIM, _head_tile(ks, g), feat).astype(BF16)
        kw_ref[g] = jnp.where(lane < HEAD_DIM, _head_tile(kw, g), feat).astype(BF16)
        gate_ref[g] = gate_t[g * GATE_ROWS:(g + 1) * GATE_ROWS]
        for src, dst in ((vst, vst_ref), (vwt, vwt_ref)):
            vt = jnp.concatenate([src[g * HEAD_DIM:(g + 1) * HEAD_DIM], tail], axis=0).astype(BF16)
            for j in range(tm // KEY_CHUNK):
                dst[g, j] = vt[:, j * KEY_CHUNK:(j + 1) * KEY_CHUNK]
    qkv = jnp.concatenate([(seg("qd") * (scale * LOG2E)).astype(BF16), seg("kd").astype(BF16),
                           seg("vd").astype(BF16)], axis=1)
    bounds = (0, DIL_Q_W, 2 * DIL_Q_W, 3 * DIL_Q_W)
    subs = tm // PERM_ROWS
    for i, (_, d) in enumerate(DIL_PATTERNS):
        if d > 1:
            by_phase = [jnp.dot(perm_ref[i], qkv[s * PERM_ROWS:(s + 1) * PERM_ROWS],
                                preferred_element_type=F32) for s in range(subs)]
        r = PERM_ROWS // d
        for p in range(d):
            rows = qkv if d == 1 else jnp.concatenate(
                [by_phase[s][p * r:(p + 1) * r] for s in range(subs)], axis=0).astype(BF16)
            for j, ref in enumerate(dil_refs[3 * i:3 * i + 3]):
                width = bounds[j + 1] - bounds[j]
                ref[:, p * width:(p + 1) * width] = rows[:, bounds[j]:bounds[j + 1]]


def _proj_call(x, g, w, tm):
    B, S, D = x.shape
    G = NSA_GROUPS
    cmp_w = _PROJ_OFFS["cmp"][1] - _PROJ_OFFS["cmp"][0]
    row = lambda c: pl.BlockSpec((None, tm, c), lambda b, i: (b, i, 0))
    k_spec = pl.BlockSpec((None, G, tm, LANES), lambda b, i: (b, 0, i, 0))
    vt_spec = pl.BlockSpec((None, G, tm // KEY_CHUNK, VT_ROWS, KEY_CHUNK), lambda b, i: (b, 0, i, 0, 0))
    out_shape = (
        jax.ShapeDtypeStruct((B, S, 512), BF16),
        jax.ShapeDtypeStruct((B, S // CMP_STRIDE, CMP_STRIDE * cmp_w), F32),
        jax.ShapeDtypeStruct((B, G, S, LANES), BF16),
        jax.ShapeDtypeStruct((B, G, S, LANES), BF16),
        jax.ShapeDtypeStruct((B, G, S // KEY_CHUNK, VT_ROWS, KEY_CHUNK), BF16),
        jax.ShapeDtypeStruct((B, G, S // KEY_CHUNK, VT_ROWS, KEY_CHUNK), BF16),
        jax.ShapeDtypeStruct((B, G, GATE_ROWS, S), F32),
    )
    dil_specs = ()
    for _, d in DIL_PATTERNS:
        for width in (DIL_Q_W,) * 3:
            out_shape += (jax.ShapeDtypeStruct((B, S // d, d * width), BF16),)
            dil_specs += (pl.BlockSpec((None, tm // d, d * width), lambda b, i: (b, i, 0)),)
    r = np.arange(PERM_ROWS)
    perm = np.zeros((len(DIL_PATTERNS), PERM_ROWS, PERM_ROWS), np.float32)
    for i, (_, d) in enumerate(DIL_PATTERNS):
        assert (PERM_ROWS // d) % 8 == 0
        perm[i, (r % d) * (PERM_ROWS // d) + r // d, r] = 1.0
    perm = jnp.asarray(perm, BF16)
    return pl.pallas_call(
        _proj_kernel,
        grid=(B, S // tm),
        in_specs=[row(D), pl.BlockSpec((1, D), lambda b, i: (0, 0)),
                  pl.BlockSpec(w.shape, lambda b, i: (0, 0)),
                  pl.BlockSpec(perm.shape, lambda b, i: (0, 0, 0))],
        out_specs=(row(512),
                   pl.BlockSpec((None, tm // CMP_STRIDE, CMP_STRIDE * cmp_w), lambda b, i: (b, i, 0)),
                   k_spec, k_spec, vt_spec, vt_spec,
                   pl.BlockSpec((None, G, GATE_ROWS, tm), lambda b, i: (b, 0, 0, i))) + dil_specs,
        out_shape=out_shape,
        scratch_shapes=[pltpu.VMEM((cmp_w // LANES, tm, LANES), F32)],
        compiler_params=pltpu.CompilerParams(
            dimension_semantics=("parallel", "parallel"), vmem_limit_bytes=VMEM_LIMIT),
        name="proj",
    )(x, g, w, perm)


def _compress_kernel(r_ref, posa_ref, posb_ref, wa_ref, wb_ref, w2_ref, ovt_ref, kc_ref, cvo_ref):
    n = r_ref.shape[0]
    slot_w = CMP_STRIDE * HEAD_DIM
    outs = []
    for s in range(2 * NSA_GROUPS):
        r = r_ref[:, s * slot_w:(s + 1) * slot_w]
        ha = jnp.dot((r + posa_ref[s]).astype(BF16), wa_ref[s], preferred_element_type=F32)
        hb = jnp.dot((r + posb_ref[s]).astype(BF16), wb_ref[s], preferred_element_type=F32)
        hid = jax.nn.gelu(ha + pltpu.roll(hb, n - 1, 0))
        outs.append(jnp.dot(hid.astype(BF16), w2_ref[s], preferred_element_type=F32))
    cmp_end = lax.broadcasted_iota(jnp.int32, (n, LANES), 0) * CMP_STRIDE + (CMP_BLOCK - 1)
    feat = _pos_features(cmp_end)
    zeros = jnp.zeros((n, LANES - HEAD_DIM), F32)
    for g in range(NSA_GROUPS):
        kc_ref[g] = jnp.concatenate([outs[g], feat[:, HEAD_DIM:]], axis=1).astype(BF16)
        v_t = jnp.concatenate([outs[NSA_GROUPS + g], zeros], axis=1).T[0:HEAD_DIM]
        cvo_ref[g] = jnp.concatenate([v_t, ovt_ref[...], _ones_rows(n, F32)], axis=0).astype(BF16)


def _compress_call(r, posa, posb, wa, wb, w2, ovt):
    B, R, C = r.shape
    G = NSA_GROUPS
    rows = HEAD_DIM + ovt.shape[0] + ONES_ROWS
    full = lambda a: pl.BlockSpec(a.shape, lambda b: (0,) * a.ndim)
    return pl.pallas_call(
        _compress_kernel,
        grid=(B,),
        in_specs=[pl.BlockSpec((None, R, C), lambda b: (b, 0, 0)),
                  full(posa), full(posb), full(wa), full(wb), full(w2), full(ovt)],
        out_specs=(pl.BlockSpec((None, G, R, LANES), lambda b: (b, 0, 0, 0)),
                   pl.BlockSpec((None, G, rows, R), lambda b: (b, 0, 0, 0))),
        out_shape=(jax.ShapeDtypeStruct((B, G, R, LANES), BF16),
                   jax.ShapeDtypeStruct((B, G, rows, R), BF16)),
        compiler_params=pltpu.CompilerParams(
            dimension_semantics=("parallel",), vmem_limit_bytes=VMEM_LIMIT),
        name="compress",
    )(r, posa, posb, wa, wb, w2, ovt)


def _flash_step(s, bias, v_t, m_ref, acc_ref):
    ps, alphas = [], []
    qb = bias.shape[1]
    for h in range(NSA_HPG):
        cols = slice(h * qb, (h + 1) * qb)
        sh = s[:, cols] + bias
        m_old = m_ref[:, cols]
        m_new = jnp.maximum(m_old, jnp.max(sh, axis=0, keepdims=True))
        alphas.append(jnp.exp2(m_old - m_new))
        ps.append(_exp2_bf16(sh - m_new))
        m_ref[:, cols] = m_new
    pv = jnp.dot(v_t, jnp.concatenate(ps, axis=1), preferred_element_type=F32)
    acc_ref[...] = jnp.concatenate(alphas, axis=1) * acc_ref[...] + pv


def _softmax_pv(s, bias, v_t):
    ms, ps = [], []
    qb = bias.shape[1]
    for h in range(NSA_HPG):
        sh = s[:, h * qb:(h + 1) * qb] + bias
        ms.append(jnp.max(sh, axis=0, keepdims=True))
        ps.append(_exp2_bf16(sh - ms[h]))
    r = jnp.dot(v_t, jnp.concatenate(ps, axis=1), preferred_element_type=F32)
    return jnp.concatenate(ms, axis=1), r


def _normalise(acc):
    return acc[0:HEAD_DIM] * (1.0 / jnp.maximum(acc[HEAD_DIM:HEAD_DIM + 1], 1e-30))


def _nsa_kernel(qfeat_ref, q_ref, kc_ref, cvo_ref, ks_ref, vst_ref, kw_ref, vwt_ref, gate_ref,
                o_ref, sel_ref, ms_ref, accs_ref, flag_ref, list_ref):
    n = pl.program_id(2)
    qb, kc = NSA_QB, KEY_CHUNK
    t0 = n * qb
    lane = lax.broadcasted_iota(jnp.int32, (qb, LANES), 1)
    qf = q_ref[...].astype(F32)
    q = jnp.concatenate(
        [jnp.where(lane < HEAD_DIM, _head_tile(qf, h), qfeat_ref[h][0:1, :]) for h in range(NSA_HPG)],
        axis=0).astype(BF16)

    wq = min(WIN_QB, qb)
    span = WIN + wq
    back_w = (lax.broadcasted_iota(jnp.int32, (span, wq), 1)
              - lax.broadcasted_iota(jnp.int32, (span, wq), 0))
    win_parts = []
    for j in range(qb // wq):
        k_lo = pl.multiple_of(jnp.maximum(t0 + j * wq - WIN, 0), kc)
        dist = back_w + (t0 + j * wq - k_lo)
        bias = jnp.where(dist >= 0, jnp.where(dist < WIN, 0.0, -jnp.inf), -jnp.inf)
        q_part = jnp.concatenate([q[h * qb + j * wq:h * qb + (j + 1) * wq] for h in range(NSA_HPG)], axis=0)
        vt_w = jnp.concatenate([vwt_ref[k_lo // kc + u] for u in range(span // kc)], axis=1)
        win_parts.append(_normalise(_softmax_pv(_nt(kw_ref[pl.ds(k_lo, span), :], q_part), bias, vt_w)[1]))
    o_win = jnp.concatenate([part[:, h * wq:(h + 1) * wq] for h in range(NSA_HPG) for part in win_parts],
                            axis=1)

    n_cmp = kc_ref.shape[0]
    n_sel = sel_ref.shape[0] - SEL_PAD_ROWS
    s = _nt(kc_ref[...], q)
    cmp_end = lax.broadcasted_iota(jnp.int32, (n_cmp, qb), 0) * CMP_STRIDE + (CMP_BLOCK - 1)
    bias_c = jnp.where(cmp_end <= t0 + lax.broadcasted_iota(jnp.int32, (n_cmp, qb), 1), 0.0, -jnp.inf)
    es = []
    for h in range(NSA_HPG):
        sh = s[:, h * qb:(h + 1) * qb] + bias_c
        m = jnp.max(sh, axis=0, keepdims=True)
        es.append(_exp2_bf16(sh - jnp.where(m == -jnp.inf, 0.0, m)))
    r = jnp.dot(cvo_ref[...], jnp.concatenate(es, axis=1), preferred_element_type=F32)
    inv_l = 1.0 / jnp.maximum(r[HEAD_DIM + n_sel:HEAD_DIM + n_sel + 1], 1e-30)
    o_cmp = r[0:HEAD_DIM] * inv_l
    imp = r[HEAD_DIM:HEAD_DIM + n_sel] * inv_l
    imp_t = imp[:, 0:qb]
    for h in range(1, NSA_HPG):
        imp_t = imp_t + imp[:, h * qb:(h + 1) * qb]

    blk = lax.broadcasted_iota(jnp.int32, (n_sel, qb), 0).astype(F32)
    cur = ((t0 + lax.broadcasted_iota(jnp.int32, (n_sel, qb), 1)) // SLC_BLOCK).astype(F32)
    ago = cur - blk
    forced = jnp.where(blk == 0, 1.0, jnp.where(ago == 0, 1.0, jnp.where(ago == 1, 1.0, 0.0)))
    valid = ago >= 0
    score = jnp.where(valid, jnp.where(forced > 0.5, -1.0, imp_t), -1.0)
    for _ in range(max(min(SLC_TOPK, n_sel) - N_FORCED, 0)):
        idx = jnp.argmax(score, axis=0, keepdims=True).astype(F32)
        score = jnp.where(blk == idx, -jnp.inf, score)
    sel_t = jnp.where(valid, jnp.where(score == -jnp.inf, 1.0, forced), 0.0)
    sel_ref[0:n_sel, :] = jnp.where(sel_t > 0.5, 0.0, -jnp.inf)
    sel_ref[n_sel:, :] = jnp.full((SEL_PAD_ROWS, qb), -jnp.inf, F32)
    any_q = jnp.max(sel_t, axis=1, keepdims=True)
    bit = jnp.left_shift(1, lax.broadcasted_iota(jnp.int32, (n_sel, 1), 0) % FLAG_BITS).astype(F32)
    packed = any_q * bit
    for i in range(n_sel // FLAG_BITS):
        word = jnp.sum(packed[i * FLAG_BITS:(i + 1) * FLAG_BITS], axis=0, keepdims=True)
        flag_ref[i] = word.astype(jnp.int32)[0, 0]

    blocks_per_chunk = kc // SLC_BLOCK
    chunks_per_word = FLAG_BITS // blocks_per_chunk
    row = lax.broadcasted_iota(jnp.int32, (kc, qb), 0)

    def chunk_bias(first_block):
        bias = sel_ref[pl.ds(first_block, 1), :]
        for j in range(1, blocks_per_chunk):
            bias = jnp.where(row < j * SLC_BLOCK, bias, sel_ref[pl.ds(first_block + j, 1), :])
        return bias

    own = qb // kc
    first_own = t0 // kc
    near = min(NSA_NEAR_CHUNKS, ks_ref.shape[0] // kc)
    k_near = pl.multiple_of(jnp.maximum(t0 - near * kc, 0), kc)
    first_near = k_near // kc
    off_first = jnp.where(first_near >= 1, 0.0, -jnp.inf)
    query = lax.broadcasted_iota(jnp.int32, (kc, qb), 1)
    biases = [chunk_bias(0) + off_first]
    for u in range(own):
        causal = jnp.where(row + u * kc <= query, 0.0, -jnp.inf)
        biases.append(chunk_bias((first_own + u) * blocks_per_chunk) + causal)
    keys = jnp.concatenate([ks_ref[0:kc, :], ks_ref[pl.ds(pl.multiple_of(t0, kc), qb), :]], axis=0)
    vt_s = jnp.concatenate([vst_ref[0]] + [vst_ref[first_own + u] for u in range(own)], axis=1)
    m_s, acc_s = _softmax_pv(_nt(keys, q), jnp.concatenate(biases, axis=0), vt_s)
    ms_ref[...] = m_s
    accs_ref[...] = acc_s

    biases = []
    for u in range(near):
        off = jnp.where(first_near + u < first_own, 0.0, -jnp.inf)
        biases.append(chunk_bias((first_near + u) * blocks_per_chunk) + off)
    _flash_step(_nt(ks_ref[pl.ds(k_near, near * kc), :], q), jnp.concatenate(biases, axis=0),
                jnp.concatenate([vst_ref[first_near + u] for u in range(near)], axis=1), ms_ref, accs_ref)

    def scan_body(w, cnt):
        word = flag_ref[w]
        for j in range(chunks_per_word):
            c = w * chunks_per_word + j
            bits = (word >> (j * blocks_per_chunk)) & ((1 << blocks_per_chunk) - 1)
            list_ref[cnt] = c
            cnt = cnt + ((bits != 0) & (c >= 1) & (c < first_near)).astype(jnp.int32)
        return cnt

    cnt = lax.fori_loop(0, (first_near + chunks_per_word - 1) // chunks_per_word, scan_body, 0)
    for u in range(SLC_GROUP):
        list_ref[cnt + u] = -1

    def slc_body(gi, carry):
        keys, biases, vts = [], [], []
        for u in range(SLC_GROUP):
            c = list_ref[gi * SLC_GROUP + u]
            cc = jnp.maximum(c, 0)
            biases.append(chunk_bias(jnp.where(c >= 0, cc * blocks_per_chunk, n_sel)))
            keys.append(ks_ref[pl.ds(pl.multiple_of(cc * kc, kc), kc), :])
            vts.append(vst_ref[cc])
        _flash_step(_nt(jnp.concatenate(keys, axis=0), q), jnp.concatenate(biases, axis=0),
                    jnp.concatenate(vts, axis=1), ms_ref, accs_ref)
        return carry

    lax.fori_loop(0, (cnt + SLC_GROUP - 1) // SLC_GROUP, slc_body, 0)

    o_slc = _normalise(accs_ref[...])
    gate = gate_ref[...]
    outs = []
    for h in range(NSA_HPG):
        cols = slice(h * qb, (h + 1) * qb)
        gc = [gate[j * NSA_HPG + h:j * NSA_HPG + h + 1, :] for j in range(3)]
        outs.append(gc[0] * o_cmp[:, cols] + gc[1] * o_slc[:, cols] + gc[2] * o_win[:, cols])
    o_ref[...] = jnp.concatenate(outs, axis=0).T


def _nsa_call(qfeat, qn, kc, cvo, ks, vst, kw, vwt, gates):
    B, S, _ = qn.shape
    G = NSA_GROUPS
    qb = NSA_QB
    nb = S // qb
    n_sel = S // SLC_BLOCK
    width = NSA_HPG * qb
    per_group = lambda a: pl.BlockSpec((None, None) + a.shape[2:],
                                       lambda b, g, n: (b, g) + (0,) * (a.ndim - 2))
    return pl.pallas_call(
        _nsa_kernel,
        grid=(B, G, nb),
        in_specs=[
            pl.BlockSpec((None,) + qfeat.shape[1:], lambda b, g, n: (g, 0, 0, 0)),
            pl.BlockSpec((None, qb, NSA_HPG * HEAD_DIM), lambda b, g, n: (b, n, g)),
            per_group(kc), per_group(cvo), per_group(ks), per_group(vst), per_group(kw), per_group(vwt),
            pl.BlockSpec((None, None, GATE_ROWS, qb), lambda b, g, n: (b, g, 0, n)),
        ],
        out_specs=pl.BlockSpec((None, qb, NSA_HPG * HEAD_DIM), lambda b, g, n: (b, n, g)),
        out_shape=jax.ShapeDtypeStruct((B, S, NSA_HEADS * HEAD_DIM), F32),
        scratch_shapes=[pltpu.VMEM((n_sel + SEL_PAD_ROWS, qb), F32),
                        pltpu.VMEM((1, width), F32), pltpu.VMEM((VT_ROWS, width), F32),
                        pltpu.SMEM((n_sel // FLAG_BITS,), jnp.int32),
                        pltpu.SMEM((S // KEY_CHUNK + SLC_GROUP,), jnp.int32)],
        compiler_params=pltpu.CompilerParams(
            dimension_semantics=("parallel", "parallel", "arbitrary"), vmem_limit_bytes=VMEM_LIMIT),
        name="nsa",
    )(qfeat, qn, kc, cvo, ks, vst, kw, vwt, gates)


def _dilated_kernel(qfeat_ref, *refs):
    n_pat = len(DIL_PATTERNS)
    j = pl.program_id(1)
    for i, (window, dilation) in enumerate(DIL_PATTERNS):
        steps_per_phase = max(d for _, d in DIL_PATTERNS) // dilation
        _dilated_blocks(qfeat_ref, *refs[5 * i:5 * i + 5], *refs[5 * n_pat + 2 * i:5 * n_pat + 2 * i + 2],
                        wd=window // dilation, dilation=dilation,
                        phase=j // steps_per_phase, step=j % steps_per_phase)


def _dilated_blocks(qfeat_ref, q_ref, kp_ref, kc_ref, vp_ref, vc_ref, o_ref, lse_ref, *,
                    wd, dilation, phase, step):
    c = Q_BLOCK
    hd = HEAD_DIM
    blocks = q_ref.shape[0] // c
    lane = lax.broadcasted_iota(jnp.int32, (blocks * c, LANES), 1)
    key = lax.broadcasted_iota(jnp.int32, (2 * c, c), 0)
    qry = lax.broadcasted_iota(jnp.int32, (2 * c, c), 1)
    first = step == 0
    off_later = jnp.where(key < c, qry - (key + (wd - c)), (key - c) - qry)
    off_first = jnp.where(key < c, jnp.where(first, 2 * c, qry) - (key + (wd - c)), (key - c) - qry)
    bias_later = jnp.where(off_later <= 0, 0.0, -jnp.inf)
    bias_first = jnp.where(off_first <= 0, 0.0, -jnp.inf)
    sub = (step * blocks - 1) * c + lax.broadcasted_iota(jnp.int32, ((blocks + 1) * c, LANES), 0)
    pos_feat = _pos_features(sub * dilation + phase).astype(BF16)
    n_keys = (blocks + 1) * c
    tail = _ones_rows(n_keys, BF16)
    assert PAIR_VT_ROWS == 2 * hd + ONES_ROWS
    for p in range(DIL_HEADS // 2):
        cols = slice(p * LANES, (p + 1) * LANES)
        q_pair = q_ref[:, cols]
        zero = jnp.zeros_like(q_pair)
        feats = [jnp.broadcast_to(qfeat_ref[2 * p + i][0:1, :], (blocks * c, LANES)).astype(BF16)
                 for i in range(2)]
        q_a = jnp.concatenate([jnp.where(lane < hd, q_pair, zero), feats[0]], axis=1)
        q_b = jnp.concatenate([jnp.where(lane < hd, zero, q_pair), feats[1]], axis=1)
        k_all = jnp.concatenate([kp_ref[:, cols], kc_ref[:, cols]], axis=0)
        k_all = jnp.concatenate([k_all, pos_feat], axis=1)
        v_t = jnp.concatenate([jnp.concatenate([vp_ref[:, cols].T, vc_ref[:, cols].T], axis=1), tail],
                              axis=0)
        nb = min(DIL_BAND, blocks)
        for g0 in range(0, blocks, nb):
            q_aug = jnp.concatenate([part[j * c:(j + 1) * c] for j in range(g0, g0 + nb)
                                     for part in (q_a, q_b)], axis=0)
            s_all = _nt(k_all[g0 * c:(g0 + nb + 1) * c], q_aug)
            ms, band = [], []
            for j in range(nb):
                bias = bias_later if g0 + j else bias_first
                probs = []
                for i in range(2):
                    si = s_all[j * c:(j + 2) * c, (2 * j + i) * c:(2 * j + i + 1) * c] + bias
                    ms.append(jnp.max(si, axis=0, keepdims=True))
                    probs.append(_exp2_bf16(si - ms[-1]))
                pieces = [jnp.zeros((j * c, 2 * c), BF16)] if j else []
                pieces.append(jnp.concatenate(probs, axis=1))
                if j < nb - 1:
                    pieces.append(jnp.zeros(((nb - 1 - j) * c, 2 * c), BF16))
                band.append(jnp.concatenate(pieces, axis=0))
            r_all = jnp.dot(v_t[:, g0 * c:(g0 + nb + 1) * c], jnp.concatenate(band, axis=1),
                            preferred_element_type=F32)
            for j in range(nb):
                outs, lses = [], []
                for i in range(2):
                    at = slice((2 * j + i) * c, (2 * j + i + 1) * c)
                    l = r_all[2 * hd:2 * hd + 1, at]
                    outs.append(r_all[i * hd:(i + 1) * hd, at] * (1.0 / l))
                    lses.append(jnp.broadcast_to(ms[2 * j + i] + jnp.log2(l), (hd, c)))
                rows = slice((g0 + j) * c, (g0 + j + 1) * c)
                o_ref[rows, cols] = jnp.concatenate(outs, axis=0).T
                lse_ref[rows, cols] = jnp.concatenate(lses, axis=0).T


def _dilated_call(qfeat, qkv_by_pattern):
    B = qkv_by_pattern[0][0].shape[0]
    steps = max(d for _, d in DIL_PATTERNS)
    in_specs = [pl.BlockSpec(qfeat.shape, lambda b, j: (0, 0, 0))]
    out_specs, out_shape, operands = [], [], [qfeat]
    for (window, d), (q_v, k_v, v_v) in zip(DIL_PATTERNS, qkv_by_pattern):
        assert Q_BLOCK - 1 <= window // d <= Q_BLOCK and steps % d == 0
        L = q_v.shape[1]
        spp = steps // d
        per_step = L // (Q_BLOCK * spp)
        blk = pl.BlockSpec((None, Q_BLOCK * per_step, DIL_Q_W),
                           lambda b, j, spp=spp: (b, j % spp, j // spp))
        prev = pl.BlockSpec((None, Q_BLOCK, DIL_Q_W),
                            lambda b, j, spp=spp, per_step=per_step:
                            (b, jnp.maximum((j % spp) * per_step - 1, 0), j // spp))
        in_specs += [blk, prev, blk, prev, blk]
        operands += [q_v, k_v, k_v, v_v, v_v]
        out_specs += [blk, blk]
        out_shape += [jax.ShapeDtypeStruct((B, L, d * DIL_Q_W), F32)] * 2
    outs = pl.pallas_call(
        _dilated_kernel,
        grid=(B, steps),
        in_specs=in_specs,
        out_specs=tuple(out_specs),
        out_shape=tuple(out_shape),
        compiler_params=pltpu.CompilerParams(
            dimension_semantics=("parallel", "arbitrary"), vmem_limit_bytes=VMEM_LIMIT),
        name="dilated",
    )(*operands)
    return [(outs[2 * i], outs[2 * i + 1]) for i in range(len(DIL_PATTERNS))]


def _merge_kernel(x_ref, g_ref, wg_ref, onsa_ref, o1_ref, o2_ref, o3_ref, l1_ref, l2_ref, l3_ref,
                  wbn_ref, wbd_ref, wo_ref, fg_ref, out_ref, tok_scr, *, final):
    x = x_ref[...]
    tm = x.shape[0]
    xb = _rms(x, g_ref[...]).astype(BF16)
    cz = NSA_HEADS * HEAD_DIM
    d = x.shape[-1]

    def gate_seg(c0, c1):
        return jnp.dot(xb, wg_ref[:, c0:c1], preferred_element_type=F32)

    def token_major(ref, slot, dil):
        if dil == 1:
            return ref[...]
        tiles = DIL_Q_W // LANES
        for p in range(dil):
            for j in range(tiles):
                tok_scr[slot * tiles + j, pl.ds(p, tm // dil, stride=dil), :] = \
                    ref[:, p * DIL_Q_W + j * LANES:p * DIL_Q_W + (j + 1) * LANES]
        return jnp.concatenate([tok_scr[slot * tiles + j] for j in range(tiles)], axis=1)

    z_n = gate_seg(0, cz)
    o_nsa = onsa_ref[...] * (z_n * _sigmoid(z_n))
    dils = [dil for _, dil in DIL_PATTERNS]
    o1, o2, o3 = (token_major(r, i, dil) for i, (r, dil) in enumerate(zip((o1_ref, o2_ref, o3_ref), dils)))
    l1, l2, l3 = (token_major(r, 3 + i, dil) for i, (r, dil) in enumerate(zip((l1_ref, l2_ref, l3_ref), dils)))
    mx = jnp.maximum(jnp.maximum(l1, l2), l3)
    e1, e2, e3 = jnp.exp2(l1 - mx), jnp.exp2(l2 - mx), jnp.exp2(l3 - mx)
    inv = 1.0 / (e1 + e2 + e3)
    o_dil = (e1 * inv) * o1 + (e2 * inv) * o2 + (e3 * inv) * o3
    z_d = gate_seg(cz, 2 * cz)
    o_dil = o_dil * (z_d * _sigmoid(z_d))
    a = jnp.dot(o_nsa.astype(BF16), wbn_ref[...], preferred_element_type=F32)
    bd = jnp.dot(o_dil.astype(BF16), wbd_ref[...], preferred_element_type=F32)
    merged = (_sigmoid(gate_seg(2 * cz, 2 * cz + d)) * a
              + _sigmoid(gate_seg(2 * cz + d, 2 * cz + 2 * d)) * bd)
    y = jnp.dot(merged.astype(BF16), wo_ref[...], preferred_element_type=F32)
    out_ref[...] = _rms(x + y, fg_ref[...]) if final else x + y


def _merge_call(x, g, wg, o_nsa, o_d, lse_d, wbn, wbd, wo, fg, tm, final):
    B, S, D = x.shape
    cz = o_nsa.shape[-1]
    row = lambda c: pl.BlockSpec((None, tm, c), lambda b, i: (b, i, 0))
    full = lambda a: pl.BlockSpec(a.shape, lambda b, i: (0,) * a.ndim)
    phase = [pl.BlockSpec((None, tm // dil, dil * cz), lambda b, i: (b, i, 0)) for _, dil in DIL_PATTERNS]
    return pl.pallas_call(
        functools.partial(_merge_kernel, final=final),
        grid=(B, S // tm),
        in_specs=[row(D), full(g), full(wg), row(cz)] + phase + phase
                 + [full(wbn), full(wbd), full(wo), full(fg)],
        out_specs=row(D),
        out_shape=jax.ShapeDtypeStruct((B, S, D), F32),
        scratch_shapes=[pltpu.VMEM((2 * len(DIL_PATTERNS) * cz // LANES, tm, LANES), F32)],
        compiler_params=pltpu.CompilerParams(
            dimension_semantics=("parallel", "parallel"), vmem_limit_bytes=VMEM_LIMIT),
        name="merge",
    )(x, g, wg, o_nsa, *o_d, *lse_d, wbn, wbd, wo, fg)


def _proj_columns(d_model):
    hd, G, H = HEAD_DIM, NSA_GROUPS, NSA_HEADS
    nsa_w = H * hd
    o_q, o_kv = 0, nsa_w
    o_g = o_kv + 6 * G * hd
    o_zn = o_g + 3 * H
    o_qkvd = o_zn + nsa_w
    dil_w = DIL_HEADS * hd
    o_zd = o_qkvd + 3 * dil_w
    o_mg = o_zd + dil_w
    kv = lambda j, g: o_kv + (j * G + g) * hd + np.arange(hd)
    cols = [o_q + np.arange(nsa_w)]
    cols += [kv(j, g) for j in (0, 1) for g in range(G)]
    for j in (2, 4, 3, 5):
        cols += [kv(j, g) for g in range(G)]
    lanes = np.full(LANES, -1)
    for g in range(G):
        for j in range(3):
            lanes[g * GATE_ROWS + j * NSA_HPG + np.arange(NSA_HPG)] = o_g + j * H + g * NSA_HPG + np.arange(NSA_HPG)
    cols.append(lanes)
    cols.append(o_qkvd + np.arange(3 * dil_w))
    proj_cols = np.concatenate(cols)
    assert proj_cols.size == _PROJ_OFFS["vd"][1]
    gate_cols = np.concatenate([o_zn + np.arange(nsa_w), o_zd + np.arange(dil_w),
                                o_mg + np.arange(2 * d_model)])
    return proj_cols, gate_cols


def _take_columns(w, cols, dtype):
    pieces, start = [], 0
    for i in range(1, len(cols) + 1):
        if i < len(cols):
            same_run = (cols[i] < 0 and cols[i - 1] < 0) or (cols[i - 1] >= 0 and cols[i] == cols[i - 1] + 1)
        if i == len(cols) or not same_run:
            n = i - start
            pieces.append(jnp.zeros((w.shape[0], n), w.dtype) if cols[start] < 0
                          else w[:, int(cols[start]):int(cols[start]) + n])
            start = i
    return jnp.concatenate(pieces, axis=1).astype(dtype)


def _compress_weights(pos_k, w1_k, w2_k, pos_v, w1_v, w2_v):
    hd, G, half = HEAD_DIM, NSA_GROUPS, CMP_BLOCK // 2
    w1 = jnp.stack([w1_k] * G + [w1_v] * G).astype(BF16)
    pos = jnp.stack([pos_k] * G + [pos_v] * G)
    w2 = jnp.stack([w2_k] * G + [w2_v] * G).astype(BF16)
    halves = lambda a, tail: (a[:, :half].reshape((2 * G,) + tail), a[:, half:].reshape((2 * G,) + tail))
    posa, posb = halves(pos, (1, half * hd))
    wa, wb = halves(w1, (half * hd, hd))
    return posa, posb, wa, wb, w2


def _overlap_t(n_cmp_pad, n_sel):
    i = np.arange(n_cmp_pad)[None, :]
    j = np.arange(n_sel)[:, None]
    lo = np.maximum(i * CMP_STRIDE, j * SLC_BLOCK)
    hi = np.minimum(i * CMP_STRIDE + CMP_BLOCK, (j + 1) * SLC_BLOCK)
    return (np.clip(hi - lo, 0, None) / CMP_BLOCK).astype(np.float32)


def kernel(x, norm_g, w_in, cmp_pos_k, cmp_w1_k, cmp_w2_k, cmp_pos_v, cmp_w1_v, cmp_w2_v,
           w_br_nsa, w_br_dil, w_out, final_g):
    B, S, D = x.shape
    n_sel = S // SLC_BLOCK
    assert S % (Q_BLOCK * max(d for _, d in DIL_PATTERNS)) == 0 and n_sel % FLAG_BITS == 0
    n_cmp_pad = S // CMP_STRIDE
    slopes_nsa, slopes_dil = _alibi_slopes()
    proj_cols, gate_cols = _proj_columns(D)
    ovt = jnp.asarray(_overlap_t(n_cmp_pad, n_sel))
    qfeat = jnp.asarray(np.broadcast_to(
        _slope_features(slopes_nsa).reshape(NSA_GROUPS, NSA_HPG, 1, LANES),
        (NSA_GROUPS, NSA_HPG, 8, LANES)))
    qfeat_dil = jnp.asarray(np.broadcast_to(
        _slope_features(slopes_dil).reshape(DIL_HEADS, 1, LANES), (DIL_HEADS, 8, LANES)))
    h = x
    for layer in range(w_in.shape[0]):
        w_proj = _take_columns(w_in[layer], proj_cols, BF16)
        w_gate = _take_columns(w_in[layer], gate_cols, BF16)
        g_l = norm_g[layer].reshape(1, D)

        qn, cmp_src, ks, kw, vst, vwt, gates, *dil = _proj_call(h, g_l, w_proj, min(PROJ_ROWS, S))
        cw = _compress_weights(cmp_pos_k[layer], cmp_w1_k[layer], cmp_w2_k[layer],
                               cmp_pos_v[layer], cmp_w1_v[layer], cmp_w2_v[layer])
        kc, cvo = _compress_call(cmp_src, *cw, ovt)
        o_nsa = _nsa_call(qfeat, qn, kc, cvo, ks, vst, kw, vwt, gates)
        dil_out = _dilated_call(qfeat_dil, [dil[3 * i:3 * i + 3] for i in range(len(DIL_PATTERNS))])
        o_d, lse_d = [o for o, _ in dil_out], [lse for _, lse in dil_out]
        h = _merge_call(h, g_l, w_gate, o_nsa, o_d, lse_d, w_br_nsa[layer].astype(BF16),
                        w_br_dil[layer].astype(BF16), w_out[layer].astype(BF16),
                        final_g.reshape(1, D), min(MERGE_ROWS, S), final=layer == w_in.shape[0] - 1)
    return h
```

```python
import functools
import math

import numpy as np
import jax
import jax.numpy as jnp
from jax import lax
from jax.experimental import pallas as pl
from jax.experimental.pallas import tpu as pltpu

F32 = jnp.float32
BF16 = jnp.bfloat16

HEAD_DIM = 64
LANES = 128
NSA_HEADS = 8
NSA_GROUPS = 2
NSA_HPG = NSA_HEADS // NSA_GROUPS
CMP_BLOCK = 32
CMP_STRIDE = 16
SLC_BLOCK = 64
SLC_TOPK = 16
WIN = 512
FORCE_SCORE = 1.0e4
DIL_HEADS = 8
DIL_PATTERNS = ((128, 1), (512, 4), (2048, 16))
Q_BLOCK = 128
RMS_EPS = 1e-6
KEY_CHUNK = 128
VT_ROWS = 80
PAIR_VT_ROWS = 144
GATE_ROWS = 16
FEAT_SPLIT = 3
FLAG_BITS = 16
SLC_GROUP = 4
NSA_NEAR_CHUNKS = 8
PROJ_ROWS = 1024
MERGE_ROWS = 512
PERM_ROWS = 128
WIN_QB = 256
NSA_QB = 512
DIL_BAND = 4
N_FORCED = 3
assert FORCE_SCORE > NSA_HPG + 1
SEL_PAD_ROWS = 8
ONES_ROWS = 16
LOG2E = math.log2(math.e)
VMEM_LIMIT = 56 * 1024 * 1024


def _alibi_slopes():
    n = NSA_HEADS + DIL_HEADS
    s = 2.0 ** (-8.0 * np.arange(1, n + 1) / n)
    return s[0::2].astype(np.float32), s[1::2].astype(np.float32)


def _nt(a, b):
    return lax.dot_general(a, b, (((1,), (1,)), ((), ())), preferred_element_type=F32)


def _rms(x, g):
    return (x * lax.rsqrt(jnp.mean(x * x, axis=-1, keepdims=True) + RMS_EPS)) * g


def _sigmoid(x):
    return 0.5 + 0.5 * jnp.tanh(0.5 * x)


def _exp2_bf16(x):
    return jnp.exp2(x.astype(BF16))


def _head_tile(q_pairs, h):
    tile = q_pairs[:, (h // 2) * LANES:(h // 2 + 1) * LANES]
    return pltpu.roll(tile, HEAD_DIM, 1) if h % 2 else tile


def _pos_features(pos):
    lane = lax.broadcasted_iota(jnp.int32, pos.shape, 1) - HEAD_DIM
    hi = (pos // LANES).astype(F32)
    lo = (pos % LANES).astype(F32)
    return jnp.where(lane < 0, 0.0, jnp.where(lane < FEAT_SPLIT, hi, jnp.where(lane < 2 * FEAT_SPLIT, lo, 0.0)))


def _slope_features(slopes):
    out = np.zeros((len(slopes), LANES), np.float32)
    for h, s in enumerate(slopes):
        rest = np.float32(s) * np.float32(LOG2E)
        for i in range(FEAT_SPLIT):
            piece = np.float32(np.asarray(rest, np.float32).astype(BF16))
            out[h, HEAD_DIM + FEAT_SPLIT + i] = piece
            out[h, HEAD_DIM + i] = piece * LANES
            rest = np.float32(rest - piece)
    return out


def _ones_rows(width, dtype):
    row = lax.broadcasted_iota(jnp.int32, (ONES_ROWS, width), 0)
    return jnp.where(row == 0, 1.0, 0.0).astype(dtype)


_PROJ_SEGS = (("qn", 512), ("cmp", 256), ("ks", 128), ("kw", 128), ("vs", 128), ("vw", 128),
              ("gate", 128), ("qd", 512), ("kd", 512), ("vd", 512))
DIL_Q_W = DIL_HEADS * HEAD_DIM
_PROJ_OFFS = dict(zip((n for n, _ in _PROJ_SEGS),
                      zip(np.cumsum([0] + [w for _, w in _PROJ_SEGS])[:-1].tolist(),
                          np.cumsum([w for _, w in _PROJ_SEGS]).tolist())))


def _to_lane_tiles(scr, first, val):
    for j in range(val.shape[1] // LANES):
        scr[first + j] = val[:, j * LANES:(j + 1) * LANES]


def _proj_kernel(x_ref, g_ref, w_ref, perm_ref, qn_ref, cmp_ref, ks_ref, kw_ref, vst_ref, vwt_ref,
                 gate_ref, *dil_refs):
    cmp_scr = dil_refs[-1]
    tm = x_ref.shape[0]
    xb = _rms(x_ref[...], g_ref[...]).astype(BF16)
    scale = HEAD_DIM ** -0.5

    def seg(name, last=None):
        c0, c1 = _PROJ_OFFS[name][0], _PROJ_OFFS[last or name][1]
        return jnp.dot(xb, w_ref[:, c0:c1], preferred_element_type=F32)

    qn_ref[...] = (seg("qn") * (scale * LOG2E)).astype(BF16)
    _to_lane_tiles(cmp_scr, 0, seg("cmp"))
    lane_c = lax.broadcasted_iota(jnp.int32, (tm // CMP_STRIDE, LANES), 1)
    slot_w = CMP_STRIDE * HEAD_DIM
    for t in range(cmp_scr.shape[0]):
        for p in range(0, CMP_STRIDE, 2):
            even, odd = (cmp_scr[t, pl.ds(p + i, tm // CMP_STRIDE, stride=CMP_STRIDE), :] for i in range(2))
            for half in range(2):
                pair = (jnp.where(lane_c < HEAD_DIM, even, pltpu.roll(odd, HEAD_DIM, 1)) if half == 0
                        else jnp.where(lane_c < HEAD_DIM, pltpu.roll(even, HEAD_DIM, 1), odd))
                c0 = (2 * t + half) * slot_w + p * HEAD_DIM
                cmp_ref[:, c0:c0 + LANES] = pair
    feat =_pos_features(pl.program_id(1) * tm + lax.broadcasted_iota(jnp.int32, (tm, LANES), 0))
    k_sw, v_sw = seg("ks", "kw"), seg("vs", "vw")
    ks, kw = k_sw[:, 0:LANES], k_sw[:, LANES:]
    vst, vwt = v_sw[:, 0:LANES].T, v_sw[:, LANES:].T
    gate_t = _sigmoid(seg("gate")).T
    tail = _ones_rows(tm, F32)
    lane = lax.broadcasted_iota(jnp.int32, (tm, LANES), 1)
    for g in range(NSA_GROUPS):
        ks_ref[g] = jnp.where(lane < HEAD_DIM, _head_tile(ks, g), feat).astype(BF16)
        kw_ref[g] = jnp.where(lane < HEAD_DIM, _head_tile(kw, g), feat).astype(BF16)
        gate_ref[g] = gate_t[g * GATE_ROWS:(g + 1) * GATE_ROWS]
        for src, dst in ((vst, vst_ref), (vwt, vwt_ref)):
            vt = jnp.concatenate([src[g * HEAD_DIM:(g + 1) * HEAD_DIM], tail], axis=0).astype(BF16)
            for j in range(tm // KEY_CHUNK):
                dst[g, j] = vt[:, j * KEY_CHUNK:(j + 1) * KEY_CHUNK]
    qkv = jnp.concatenate([(seg("qd") * (scale * LOG2E)).astype(BF16), seg("kd").astype(BF16),
                           seg("vd").astype(BF16)], axis=1)
    bounds = (0, DIL_Q_W, 2 * DIL_Q_W, 3 * DIL_Q_W)
    subs = tm // PERM_ROWS
    for i, (_, d) in enumerate(DIL_PATTERNS):
        if d > 1:
            by_phase = [jnp.dot(perm_ref[i], qkv[s * PERM_ROWS:(s + 1) * PERM_ROWS],
                                preferred_element_type=F32) for s in range(subs)]
        r = PERM_ROWS // d
        for p in range(d):
            rows = qkv if d == 1 else jnp.concatenate(
                [by_phase[s][p * r:(p + 1) * r] for s in range(subs)], axis=0).astype(BF16)
            for j, ref in enumerate(dil_refs[3 * i:3 * i + 3]):
                width = bounds[j + 1] - bounds[j]
                ref[:, p * width:(p + 1) * width] = rows[:, bounds[j]:bounds[j + 1]]


def _proj_call(x, g, w, tm):
    B, S, D = x.shape
    G = NSA_GROUPS
    cmp_w = _PROJ_OFFS["cmp"][1] - _PROJ_OFFS["cmp"][0]
    row = lambda c: pl.BlockSpec((None, tm, c), lambda b, i: (b, i, 0))
    k_spec = pl.BlockSpec((None, G, tm, LANES), lambda b, i: (b, 0, i, 0))
    vt_spec = pl.BlockSpec((None, G, tm // KEY_CHUNK, VT_ROWS, KEY_CHUNK), lambda b, i: (b, 0, i, 0, 0))
    out_shape = (
        jax.ShapeDtypeStruct((B, S, 512), BF16),
        jax.ShapeDtypeStruct((B, S // CMP_STRIDE, CMP_STRIDE * cmp_w), F32),
        jax.ShapeDtypeStruct((B, G, S, LANES), BF16),
        jax.ShapeDtypeStruct((B, G, S, LANES), BF16),
        jax.ShapeDtypeStruct((B, G, S // KEY_CHUNK, VT_ROWS, KEY_CHUNK), BF16),
        jax.ShapeDtypeStruct((B, G, S // KEY_CHUNK, VT_ROWS, KEY_CHUNK), BF16),
        jax.ShapeDtypeStruct((B, G, GATE_ROWS, S), F32),
    )
    dil_specs = ()
    for _, d in DIL_PATTERNS:
        for width in (DIL_Q_W,) * 3:
            out_shape += (jax.ShapeDtypeStruct((B, S // d, d * width), BF16),)
            dil_specs += (pl.BlockSpec((None, tm // d, d * width), lambda b, i: (b, i, 0)),)
    r = np.arange(PERM_ROWS)
    perm = np.zeros((len(DIL_PATTERNS), PERM_ROWS, PERM_ROWS), np.float32)
    for i, (_, d) in enumerate(DIL_PATTERNS):
        assert (PERM_ROWS // d) % 8 == 0
        perm[i, (r % d) * (PERM_ROWS // d) + r // d, r] = 1.0
    perm = jnp.asarray(perm, BF16)
    return pl.pallas_call(
        _proj_kernel,
        grid=(B, S // tm),
        in_specs=[row(D), pl.BlockSpec((1, D), lambda b, i: (0, 0)),
                  pl.BlockSpec(w.shape, lambda b, i: (0, 0)),
                  pl.BlockSpec(perm.shape, lambda b, i: (0, 0, 0))],
        out_specs=(row(512),
                   pl.BlockSpec((None, tm // CMP_STRIDE, CMP_STRIDE * cmp_w), lambda b, i: (b, i, 0)),
                   k_spec, k_spec, vt_spec, vt_spec,
                   pl.BlockSpec((None, G, GATE_ROWS, tm), lambda b, i: (b, 0, 0, i))) + dil_specs,
        out_shape=out_shape,
        scratch_shapes=[pltpu.VMEM((cmp_w // LANES, tm, LANES), F32)],
        compiler_params=pltpu.CompilerParams(
            dimension_semantics=("parallel", "parallel"), vmem_limit_bytes=VMEM_LIMIT),
        name="proj",
    )(x, g, w, perm)


def _compress_kernel(r_ref, posa_ref, posb_ref, wa_ref, wb_ref, w2_ref, ovt_ref, kc_ref, cvo_ref):
    n = r_ref.shape[0]
    slot_w = CMP_STRIDE * HEAD_DIM
    outs = []
    for s in range(2 * NSA_GROUPS):
        r = r_ref[:, s * slot_w:(s + 1) * slot_w]
        ha = jnp.dot((r + posa_ref[s]).astype(BF16), wa_ref[s], preferred_element_type=F32)
        hb = jnp.dot((r + posb_ref[s]).astype(BF16), wb_ref[s], preferred_element_type=F32)
        hid = jax.nn.gelu(ha + pltpu.roll(hb, n - 1, 0))
        outs.append(jnp.dot(hid.astype(BF16), w2_ref[s], preferred_element_type=F32))
    cmp_end = lax.broadcasted_iota(jnp.int32, (n, LANES), 0) * CMP_STRIDE + (CMP_BLOCK - 1)
    feat = _pos_features(cmp_end)
    zeros = jnp.zeros((n, LANES - HEAD_DIM), F32)
    for g in range(NSA_GROUPS):
        kc_ref[g] = jnp.concatenate([outs[g], feat[:, HEAD_DIM:]], axis=1).astype(BF16)
        v_t = jnp.concatenate([outs[NSA_GROUPS + g], zeros], axis=1).T[0:HEAD_DIM]
        cvo_ref[g] = jnp.concatenate([v_t, ovt_ref[...], _ones_rows(n, F32)], axis=0).astype(BF16)


def _compress_call(r, posa, posb, wa, wb, w2, ovt):
    B, R, C = r.shape
    G = NSA_GROUPS
    rows = HEAD_DIM + ovt.shape[0] + ONES_ROWS
    full = lambda a: pl.BlockSpec(a.shape, lambda b: (0,) * a.ndim)
    return pl.pallas_call(
        _compress_kernel,
        grid=(B,),
        in_specs=[pl.BlockSpec((None, R, C), lambda b: (b, 0, 0)),
                  full(posa), full(posb), full(wa), full(wb), full(w2), full(ovt)],
        out_specs=(pl.BlockSpec((None, G, R, LANES), lambda b: (b, 0, 0, 0)),
                   pl.BlockSpec((None, G, rows, R), lambda b: (b, 0, 0, 0))),
        out_shape=(jax.ShapeDtypeStruct((B, G, R, LANES), BF16),
                   jax.ShapeDtypeStruct((B, G, rows, R), BF16)),
        compiler_params=pltpu.CompilerParams(
            dimension_semantics=("parallel",), vmem_limit_bytes=VMEM_LIMIT),
        name="compress",
    )(r, posa, posb, wa, wb, w2, ovt)


def _flash_step(s, bias, v_t, m_ref, acc_ref):
    ps, alphas = [], []
    qb = bias.shape[1]
    for h in range(NSA_HPG):
        cols = slice(h * qb, (h + 1) * qb)
        sh = s[:, cols] + bias
        m_old = m_ref[:, cols]
        m_new = jnp.maximum(m_old, jnp.max(sh, axis=0, keepdims=True))
        alphas.append(jnp.exp2(m_old - m_new))
        ps.append(_exp2_bf16(sh - m_new))
        m_ref[:, cols] = m_new
    pv = jnp.dot(v_t, jnp.concatenate(ps, axis=1), preferred_element_type=F32)
    acc_ref[...] = jnp.concatenate(alphas, axis=1) * acc_ref[...] + pv


def _softmax_pv(s, bias, v_t):
    ms, ps = [], []
    qb = bias.shape[1]
    for h in range(NSA_HPG):
        sh = s[:, h * qb:(h + 1) * qb] + bias
        ms.append(jnp.max(sh, axis=0, keepdims=True))
        ps.append(_exp2_bf16(sh - ms[h]))
    r = jnp.dot(v_t, jnp.concatenate(ps, axis=1), preferred_element_type=F32)
    return jnp.concatenate(ms, axis=1), r


def _normalise(acc):
    return acc[0:HEAD_DIM] * (1.0 / jnp.maximum(acc[HEAD_DIM:HEAD_DIM + 1], 1e-30))


def _nsa_kernel(qfeat_ref, q_ref, kc_ref, cvo_ref, ks_ref, vst_ref, kw_ref, vwt_ref, gate_ref,
                o_ref, sel_ref, ms_ref, accs_ref, flag_ref, list_ref):
    n = pl.program_id(2)
    qb, kc = NSA_QB, KEY_CHUNK
    t0 = n * qb
    lane = lax.broadcasted_iota(jnp.int32, (qb, LANES), 1)
    qf = q_ref[...].astype(F32)
    q = jnp.concatenate(
        [jnp.where(lane < HEAD_DIM, _head_tile(qf, h), qfeat_ref[h][0:1, :]) for h in range(NSA_HPG)],
        axis=0).astype(BF16)

    wq = min(WIN_QB, qb)
    span = WIN + wq
    back_w = (lax.broadcasted_iota(jnp.int32, (span, wq), 1)
              - lax.broadcasted_iota(jnp.int32, (span, wq), 0))
    win_parts = []
    for j in range(qb // wq):
        k_lo = pl.multiple_of(jnp.maximum(t0 + j * wq - WIN, 0), kc)
        dist = back_w + (t0 + j * wq - k_lo)
        bias = jnp.where(dist >= 0, jnp.where(dist < WIN, 0.0, -jnp.inf), -jnp.inf)
        q_part = jnp.concatenate([q[h * qb + j * wq:h * qb + (j + 1) * wq] for h in range(NSA_HPG)], axis=0)
        vt_w = jnp.concatenate([vwt_ref[k_lo // kc + u] for u in range(span // kc)], axis=1)
        win_parts.append(_normalise(_softmax_pv(_nt(kw_ref[pl.ds(k_lo, span), :], q_part), bias, vt_w)[1]))
    o_win = jnp.concatenate([part[:, h * wq:(h + 1) * wq] for h in range(NSA_HPG) for part in win_parts],
                            axis=1)

    n_cmp = kc_ref.shape[0]
    n_sel = sel_ref.shape[0] - SEL_PAD_ROWS
    s = _nt(kc_ref[...], q)
    cmp_end = lax.broadcasted_iota(jnp.int32, (n_cmp, qb), 0) * CMP_STRIDE + (CMP_BLOCK - 1)
    bias_c = jnp.where(cmp_end <= t0 + lax.broadcasted_iota(jnp.int32, (n_cmp, qb), 1), 0.0, -jnp.inf)
    es = []
    for h in range(NSA_HPG):
        sh = s[:, h * qb:(h + 1) * qb] + bias_c
        m = jnp.max(sh, axis=0, keepdims=True)
        es.append(jnp.exp2(sh - jnp.where(m == -jnp.inf, 0.0, m)).astype(BF16))
    r = jnp.dot(cvo_ref[...], jnp.concatenate(es, axis=1), preferred_element_type=F32)
    inv_l = 1.0 / jnp.maximum(r[HEAD_DIM + n_sel:HEAD_DIM + n_sel + 1], 1e-30)
    o_cmp = r[0:HEAD_DIM] * inv_l
    imp = r[HEAD_DIM:HEAD_DIM + n_sel] * inv_l
    imp_t = imp[:, 0:qb]
    for h in range(1, NSA_HPG):
        imp_t = imp_t + imp[:, h * qb:(h + 1) * qb]

    blk = lax.broadcasted_iota(jnp.int32, (n_sel, qb), 0).astype(F32)
    cur = ((t0 + lax.broadcasted_iota(jnp.int32, (n_sel, qb), 1)) // SLC_BLOCK).astype(F32)
    ago = cur - blk
    forced = jnp.where(blk == 0, 1.0, jnp.where(ago == 0, 1.0, jnp.where(ago == 1, 1.0, 0.0)))
    valid = ago >= 0
    score = jnp.where(valid, jnp.where(forced > 0.5, -1.0, imp_t), -1.0)
    for _ in range(max(min(SLC_TOPK, n_sel) - N_FORCED, 0)):
        idx = jnp.argmax(score, axis=0, keepdims=True).astype(F32)
        score = jnp.where(blk == idx, -jnp.inf, score)
    sel_t = jnp.where(valid, jnp.where(score == -jnp.inf, 1.0, forced), 0.0)
    sel_ref[0:n_sel, :] = jnp.where(sel_t > 0.5, 0.0, -jnp.inf)
    sel_ref[n_sel:, :] = jnp.full((SEL_PAD_ROWS, qb), -jnp.inf, F32)
    any_q = jnp.max(sel_t, axis=1, keepdims=True)
    bit = jnp.left_shift(1, lax.broadcasted_iota(jnp.int32, (n_sel, 1), 0) % FLAG_BITS).astype(F32)
    packed = any_q * bit
    for i in range(n_sel // FLAG_BITS):
        word = jnp.sum(packed[i * FLAG_BITS:(i + 1) * FLAG_BITS], axis=0, keepdims=True)
        flag_ref[i] = word.astype(jnp.int32)[0, 0]

    blocks_per_chunk = kc // SLC_BLOCK
    chunks_per_word = FLAG_BITS // blocks_per_chunk
    row = lax.broadcasted_iota(jnp.int32, (kc, qb), 0)

    def chunk_bias(first_block):
        bias = sel_ref[pl.ds(first_block, 1), :]
        for j in range(1, blocks_per_chunk):
            bias = jnp.where(row < j * SLC_BLOCK, bias, sel_ref[pl.ds(first_block + j, 1), :])
        return bias

    own = qb // kc
    first_own = t0 // kc
    near = min(NSA_NEAR_CHUNKS, ks_ref.shape[0] // kc)
    k_near = pl.multiple_of(jnp.maximum(t0 - near * kc, 0), kc)
    first_near = k_near // kc
    off_first = jnp.where(first_near >= 1, 0.0, -jnp.inf)
    query = lax.broadcasted_iota(jnp.int32, (kc, qb), 1)
    biases = [chunk_bias(0) + off_first]
    for u in range(own):
        causal = jnp.where(row + u * kc <= query, 0.0, -jnp.inf)
        biases.append(chunk_bias((first_own + u) * blocks_per_chunk) + causal)
    for u in range(near):
        off = jnp.where(first_near + u < first_own, 0.0, -jnp.inf)
        biases.append(chunk_bias((first_near + u) * blocks_per_chunk) + off)
    keys = jnp.concatenate([ks_ref[0:kc, :], ks_ref[pl.ds(pl.multiple_of(t0, kc), qb), :],
                            ks_ref[pl.ds(k_near, near * kc), :]], axis=0)
    vt_s = jnp.concatenate([vst_ref[0]] + [vst_ref[first_own + u] for u in range(own)]
                           + [vst_ref[first_near + u] for u in range(near)], axis=1)
    m_s, acc_s = _softmax_pv(_nt(keys, q), jnp.concatenate(biases, axis=0), vt_s)
    ms_ref[...] = m_s
    accs_ref[...] = acc_s

    def scan_body(w, cnt):
        word = flag_ref[w]
        for j in range(chunks_per_word):
            c = w * chunks_per_word + j
            bits = (word >> (j * blocks_per_chunk)) & ((1 << blocks_per_chunk) - 1)
            list_ref[cnt] = c
            cnt = cnt + ((bits != 0) & (c >= 1) & (c < first_near)).astype(jnp.int32)
        return cnt

    cnt = lax.fori_loop(0, (first_near + chunks_per_word - 1) // chunks_per_word, scan_body, 0)
    for u in range(SLC_GROUP):
        list_ref[cnt + u] = -1

    def slc_body(gi, carry):
        keys, biases, vts = [], [], []
        for u in range(SLC_GROUP):
            c = list_ref[gi * SLC_GROUP + u]
            cc = jnp.maximum(c, 0)
            biases.append(chunk_bias(jnp.where(c >= 0, cc * blocks_per_chunk, n_sel)))
            keys.append(ks_ref[pl.ds(pl.multiple_of(cc * kc, kc), kc), :])
            vts.append(vst_ref[cc])
        _flash_step(_nt(jnp.concatenate(keys, axis=0), q), jnp.concatenate(biases, axis=0),
                    jnp.concatenate(vts, axis=1), ms_ref, accs_ref)
        return carry

    lax.fori_loop(0, (cnt + SLC_GROUP - 1) // SLC_GROUP, slc_body, 0)

    o_slc = _normalise(accs_ref[...])
    gate = gate_ref[...]
    outs = []
    for h in range(NSA_HPG):
        cols = slice(h * qb, (h + 1) * qb)
        gc = [gate[j * NSA_HPG + h:j * NSA_HPG + h + 1, :] for j in range(3)]
        outs.append(gc[0] * o_cmp[:, cols] + gc[1] * o_slc[:, cols] + gc[2] * o_win[:, cols])
    o_ref[...] = jnp.concatenate(outs, axis=0).T


def _nsa_call(qfeat, qn, kc, cvo, ks, vst, kw, vwt, gates):
    B, S, _ = qn.shape
    G = NSA_GROUPS
    qb = NSA_QB
    nb = S // qb
    n_sel = S // SLC_BLOCK
    width = NSA_HPG * qb
    per_group = lambda a: pl.BlockSpec((None, None) + a.shape[2:],
                                       lambda b, g, n: (b, g) + (0,) * (a.ndim - 2))
    return pl.pallas_call(
        _nsa_kernel,
        grid=(B, G, nb),
        in_specs=[
            pl.BlockSpec((None,) + qfeat.shape[1:], lambda b, g, n: (g, 0, 0, 0)),
            pl.BlockSpec((None, qb, NSA_HPG * HEAD_DIM), lambda b, g, n: (b, n, g)),
            per_group(kc), per_group(cvo), per_group(ks), per_group(vst), per_group(kw), per_group(vwt),
            pl.BlockSpec((None, None, GATE_ROWS, qb), lambda b, g, n: (b, g, 0, n)),
        ],
        out_specs=pl.BlockSpec((None, qb, NSA_HPG * HEAD_DIM), lambda b, g, n: (b, n, g)),
        out_shape=jax.ShapeDtypeStruct((B, S, NSA_HEADS * HEAD_DIM), F32),
        scratch_shapes=[pltpu.VMEM((n_sel + SEL_PAD_ROWS, qb), F32),
                        pltpu.VMEM((1, width), F32), pltpu.VMEM((VT_ROWS, width), F32),
                        pltpu.SMEM((n_sel // FLAG_BITS,), jnp.int32),
                        pltpu.SMEM((S // KEY_CHUNK + SLC_GROUP,), jnp.int32)],
        compiler_params=pltpu.CompilerParams(
            dimension_semantics=("parallel", "parallel", "arbitrary"), vmem_limit_bytes=VMEM_LIMIT),
        name="nsa",
    )(qfeat, qn, kc, cvo, ks, vst, kw, vwt, gates)


def _dilated_kernel(qfeat_ref, *refs):
    n_pat = len(DIL_PATTERNS)
    j = pl.program_id(1)
    for i, (window, dilation) in enumerate(DIL_PATTERNS):
        steps_per_phase = max(d for _, d in DIL_PATTERNS) // dilation
        _dilated_blocks(qfeat_ref, *refs[5 * i:5 * i + 5], *refs[5 * n_pat + 2 * i:5 * n_pat + 2 * i + 2],
                        wd=window // dilation, dilation=dilation,
                        phase=j // steps_per_phase, step=j % steps_per_phase)


def _dilated_blocks(qfeat_ref, q_ref, kp_ref, kc_ref, vp_ref, vc_ref, o_ref, lse_ref, *,
                    wd, dilation, phase, step):
    c = Q_BLOCK
    hd = HEAD_DIM
    blocks = q_ref.shape[0] // c
    lane = lax.broadcasted_iota(jnp.int32, (blocks * c, LANES), 1)
    key = lax.broadcasted_iota(jnp.int32, (2 * c, c), 0)
    qry = lax.broadcasted_iota(jnp.int32, (2 * c, c), 1)
    first = step == 0
    off_later = jnp.where(key < c, qry - (key + (wd - c)), (key - c) - qry)
    off_first = jnp.where(key < c, jnp.where(first, 2 * c, qry) - (key + (wd - c)), (key - c) - qry)
    bias_later = jnp.where(off_later <= 0, 0.0, -jnp.inf)
    bias_first = jnp.where(off_first <= 0, 0.0, -jnp.inf)
    sub = (step * blocks - 1) * c + lax.broadcasted_iota(jnp.int32, ((blocks + 1) * c, LANES), 0)
    pos_feat = _pos_features(sub * dilation + phase).astype(BF16)
    n_keys = (blocks + 1) * c
    tail = _ones_rows(n_keys, BF16)
    assert PAIR_VT_ROWS == 2 * hd + ONES_ROWS
    for p in range(DIL_HEADS // 2):
        cols = slice(p * LANES, (p + 1) * LANES)
        q_pair = q_ref[:, cols]
        zero = jnp.zeros_like(q_pair)
        feats = [jnp.broadcast_to(qfeat_ref[2 * p + i][0:1, :], (blocks * c, LANES)).astype(BF16)
                 for i in range(2)]
        q_a = jnp.concatenate([jnp.where(lane < hd, q_pair, zero), feats[0]], axis=1)
        q_b = jnp.concatenate([jnp.where(lane < hd, zero, q_pair), feats[1]], axis=1)
        k_all = jnp.concatenate([kp_ref[:, cols], kc_ref[:, cols]], axis=0)
        k_all = jnp.concatenate([k_all, pos_feat], axis=1)
        v_t = jnp.concatenate([jnp.concatenate([vp_ref[:, cols].T, vc_ref[:, cols].T], axis=1), tail],
                              axis=0)
        nb = min(DIL_BAND, blocks)
        for g0 in range(0, blocks, nb):
            q_aug = jnp.concatenate([part[j * c:(j + 1) * c] for j in range(g0, g0 + nb)
                                     for part in (q_a, q_b)], axis=0)
            s_all = _nt(k_all[g0 * c:(g0 + nb + 1) * c], q_aug)
            ms, band = [], []
            for j in range(nb):
                bias = bias_later if g0 + j else bias_first
                probs = []
                for i in range(2):
                    si = s_all[j * c:(j + 2) * c, (2 * j + i) * c:(2 * j + i + 1) * c] + bias
                    ms.append(jnp.max(si, axis=0, keepdims=True))
                    probs.append(_exp2_bf16(si - ms[-1]))
                pieces = [jnp.zeros((j * c, 2 * c), BF16)] if j else []
                pieces.append(jnp.concatenate(probs, axis=1))
                if j < nb - 1:
                    pieces.append(jnp.zeros(((nb - 1 - j) * c, 2 * c), BF16))
                band.append(jnp.concatenate(pieces, axis=0))
            r_all = jnp.dot(v_t[:, g0 * c:(g0 + nb + 1) * c], jnp.concatenate(band, axis=1),
                            preferred_element_type=F32)
            for j in range(nb):
                outs, lses = [], []
                for i in range(2):
                    at = slice((2 * j + i) * c, (2 * j + i + 1) * c)
                    l = r_all[2 * hd:2 * hd + 1, at]
                    outs.append(r_all[i * hd:(i + 1) * hd, at] * (1.0 / l))
                    lses.append(jnp.broadcast_to(ms[2 * j + i] + jnp.log2(l), (hd, c)))
                rows = slice((g0 + j) * c, (g0 + j + 1) * c)
                o_ref[rows, cols] = jnp.concatenate(outs, axis=0).T
                lse_ref[rows, cols] = jnp.concatenate(lses, axis=0).T


def _dilated_call(qfeat, qkv_by_pattern):
    B = qkv_by_pattern[0][0].shape[0]
    steps = max(d for _, d in DIL_PATTERNS)
    in_specs = [pl.BlockSpec(qfeat.shape, lambda b, j: (0, 0, 0))]
    out_specs, out_shape, operands = [], [], [qfeat]
    for (window, d), (q_v, k_v, v_v) in zip(DIL_PATTERNS, qkv_by_pattern):
        assert Q_BLOCK - 1 <= window // d <= Q_BLOCK and steps % d == 0
        L = q_v.shape[1]
        spp = steps // d
        per_step = L // (Q_BLOCK * spp)
        blk = pl.BlockSpec((None, Q_BLOCK * per_step, DIL_Q_W),
                           lambda b, j, spp=spp: (b, j % spp, j // spp))
        prev = pl.BlockSpec((None, Q_BLOCK, DIL_Q_W),
                            lambda b, j, spp=spp, per_step=per_step:
                            (b, jnp.maximum((j % spp) * per_step - 1, 0), j // spp))
        in_specs += [blk, prev, blk, prev, blk]
        operands += [q_v, k_v, k_v, v_v, v_v]
        out_specs += [blk, blk]
        out_shape += [jax.ShapeDtypeStruct((B, L, d * DIL_Q_W), F32)] * 2
    outs = pl.pallas_call(
        _dilated_kernel,
        grid=(B, steps),
        in_specs=in_specs,
        out_specs=tuple(out_specs),
        out_shape=tuple(out_shape),
        compiler_params=pltpu.CompilerParams(
            dimension_semantics=("parallel", "arbitrary"), vmem_limit_bytes=VMEM_LIMIT),
        name="dilated",
    )(*operands)
    return [(outs[2 * i], outs[2 * i + 1]) for i in range(len(DIL_PATTERNS))]


def _merge_kernel(x_ref, g_ref, wg_ref, onsa_ref, o1_ref, o2_ref, o3_ref, l1_ref, l2_ref, l3_ref,
                  wbn_ref, wbd_ref, wo_ref, fg_ref, out_ref, tok_scr, *, final):
    x = x_ref[...]
    tm = x.shape[0]
    xb = _rms(x, g_ref[...]).astype(BF16)
    cz = NSA_HEADS * HEAD_DIM
    d = x.shape[-1]

    def gate_seg(c0, c1):
        return jnp.dot(xb, wg_ref[:, c0:c1], preferred_element_type=F32)

    def token_major(ref, slot, dil):
        if dil == 1:
            return ref[...]
        tiles = DIL_Q_W // LANES
        for p in range(dil):
            for j in range(tiles):
                tok_scr[slot * tiles + j, pl.ds(p, tm // dil, stride=dil), :] = \
                    ref[:, p * DIL_Q_W + j * LANES:p * DIL_Q_W + (j + 1) * LANES]
        return jnp.concatenate([tok_scr[slot * tiles + j] for j in range(tiles)], axis=1)

    z_n = gate_seg(0, cz)
    o_nsa = onsa_ref[...] * (z_n * _sigmoid(z_n))
    dils = [dil for _, dil in DIL_PATTERNS]
    o1, o2, o3 = (token_major(r, i, dil) for i, (r, dil) in enumerate(zip((o1_ref, o2_ref, o3_ref), dils)))
    l1, l2, l3 = (token_major(r, 3 + i, dil) for i, (r, dil) in enumerate(zip((l1_ref, l2_ref, l3_ref), dils)))
    mx = jnp.maximum(jnp.maximum(l1, l2), l3)
    e1, e2, e3 = jnp.exp2(l1 - mx), jnp.exp2(l2 - mx), jnp.exp2(l3 - mx)
    inv = 1.0 / (e1 + e2 + e3)
    o_dil = (e1 * inv) * o1 + (e2 * inv) * o2 + (e3 * inv) * o3
    z_d = gate_seg(cz, 2 * cz)
    o_dil = o_dil * (z_d * _sigmoid(z_d))
    a = jnp.dot(o_nsa.astype(BF16), wbn_ref[...], preferred_element_type=F32)
    bd = jnp.dot(o_dil.astype(BF16), wbd_ref[...], preferred_element_type=F32)
    merged = (_sigmoid(gate_seg(2 * cz, 2 * cz + d)) * a
              + _sigmoid(gate_seg(2 * cz + d, 2 * cz + 2 * d)) * bd)
    y = jnp.dot(merged.astype(BF16), wo_ref[...], preferred_element_type=F32)
    out_ref[...] = _rms(x + y, fg_ref[...]) if final else x + y


def _merge_call(x, g, wg, o_nsa, o_d, lse_d, wbn, wbd, wo, fg, tm, final):
    B, S, D = x.shape
    cz = o_nsa.shape[-1]
    row = lambda c: pl.BlockSpec((None, tm, c), lambda b, i: (b, i, 0))
    full = lambda a: pl.BlockSpec(a.shape, lambda b, i: (0,) * a.ndim)
    phase = [pl.BlockSpec((None, tm // dil, dil * cz), lambda b, i: (b, i, 0)) for _, dil in DIL_PATTERNS]
    return pl.pallas_call(
        functools.partial(_merge_kernel, final=final),
        grid=(B, S // tm),
        in_specs=[row(D), full(g), full(wg), row(cz)] + phase + phase
                 + [full(wbn), full(wbd), full(wo), full(fg)],
        out_specs=row(D),
        out_shape=jax.ShapeDtypeStruct((B, S, D), F32),
        scratch_shapes=[pltpu.VMEM((2 * len(DIL_PATTERNS) * cz // LANES, tm, LANES), F32)],
        compiler_params=pltpu.CompilerParams(
            dimension_semantics=("parallel", "parallel"), vmem_limit_bytes=VMEM_LIMIT),
        name="merge",
    )(x, g, wg, o_nsa, *o_d, *lse_d, wbn, wbd, wo, fg)


def _proj_columns(d_model):
    hd, G, H = HEAD_DIM, NSA_GROUPS, NSA_HEADS
    nsa_w = H * hd
    o_q, o_kv = 0, nsa_w
    o_g = o_kv + 6 * G * hd
    o_zn = o_g + 3 * H
    o_qkvd = o_zn + nsa_w
    dil_w = DIL_HEADS * hd
    o_zd = o_qkvd + 3 * dil_w
    o_mg = o_zd + dil_w
    kv = lambda j, g: o_kv + (j * G + g) * hd + np.arange(hd)
    cols = [o_q + np.arange(nsa_w)]
    cols += [kv(j, g) for j in (0, 1) for g in range(G)]
    for j in (2, 4, 3, 5):
        cols += [kv(j, g) for g in range(G)]
    lanes = np.full(LANES, -1)
    for g in range(G):
        for j in range(3):
            lanes[g * GATE_ROWS + j * NSA_HPG + np.arange(NSA_HPG)] = o_g + j * H + g * NSA_HPG + np.arange(NSA_HPG)
    cols.append(lanes)
    cols.append(o_qkvd + np.arange(3 * dil_w))
    proj_cols = np.concatenate(cols)
    assert proj_cols.size == _PROJ_OFFS["vd"][1]
    gate_cols = np.concatenate([o_zn + np.arange(nsa_w), o_zd + np.arange(dil_w),
                                o_mg + np.arange(2 * d_model)])
    return proj_cols, gate_cols


def _take_columns(w, cols, dtype):
    pieces, start = [], 0
    for i in range(1, len(cols) + 1):
        if i < len(cols):
            same_run = (cols[i] < 0 and cols[i - 1] < 0) or (cols[i - 1] >= 0 and cols[i] == cols[i - 1] + 1)
        if i == len(cols) or not same_run:
            n = i - start
            pieces.append(jnp.zeros((w.shape[0], n), w.dtype) if cols[start] < 0
                          else w[:, int(cols[start]):int(cols[start]) + n])
            start = i
    return jnp.concatenate(pieces, axis=1).astype(dtype)


def _compress_weights(pos_k, w1_k, w2_k, pos_v, w1_v, w2_v):
    hd, G, half = HEAD_DIM, NSA_GROUPS, CMP_BLOCK // 2
    w1 = jnp.stack([w1_k] * G + [w1_v] * G).astype(BF16)
    pos = jnp.stack([pos_k] * G + [pos_v] * G)
    w2 = jnp.stack([w2_k] * G + [w2_v] * G).astype(BF16)
    halves = lambda a, tail: (a[:, :half].reshape((2 * G,) + tail), a[:, half:].reshape((2 * G,) + tail))
    posa, posb = halves(pos, (1, half * hd))
    wa, wb = halves(w1, (half * hd, hd))
    return posa, posb, wa, wb, w2


def _overlap_t(n_cmp_pad, n_sel):
    i = np.arange(n_cmp_pad)[None, :]
    j = np.arange(n_sel)[:, None]
    lo = np.maximum(i * CMP_STRIDE, j * SLC_BLOCK)
    hi = np.minimum(i * CMP_STRIDE + CMP_BLOCK, (j + 1) * SLC_BLOCK)
    return (np.clip(hi - lo, 0, None) / CMP_BLOCK).astype(np.float32)


def kernel(x, norm_g, w_in, cmp_pos_k, cmp_w1_k, cmp_w2_k, cmp_pos_v, cmp_w1_v, cmp_w2_v,
           w_br_nsa, w_br_dil, w_out, final_g):
    B, S, D = x.shape
    n_sel = S // SLC_BLOCK
    assert S % (Q_BLOCK * max(d for _, d in DIL_PATTERNS)) == 0 and n_sel % FLAG_BITS == 0
    n_cmp_pad = S // CMP_STRIDE
    slopes_nsa, slopes_dil = _alibi_slopes()
    proj_cols, gate_cols = _proj_columns(D)
    ovt = jnp.asarray(_overlap_t(n_cmp_pad, n_sel))
    qfeat = jnp.asarray(np.broadcast_to(
        _slope_features(slopes_nsa).reshape(NSA_GROUPS, NSA_HPG, 1, LANES),
        (NSA_GROUPS, NSA_HPG, 8, LANES)))
    qfeat_dil = jnp.asarray(np.broadcast_to(
        _slope_features(slopes_dil).reshape(DIL_HEADS, 1, LANES), (DIL_HEADS, 8, LANES)))
    h = x
    for layer in range(w_in.shape[0]):
        w_proj = _take_columns(w_in[layer], proj_cols, BF16)
        w_gate = _take_columns(w_in[layer], gate_cols, BF16)
        g_l = norm_g[layer].reshape(1, D)

        qn, cmp_src, ks, kw, vst, vwt, gates, *dil = _proj_call(h, g_l, w_proj, min(PROJ_ROWS, S))
        cw = _compress_weights(cmp_pos_k[layer], cmp_w1_k[layer], cmp_w2_k[layer],
                               cmp_pos_v[layer], cmp_w1_v[layer], cmp_w2_v[layer])
        kc, cvo = _compress_call(cmp_src, *cw, ovt)
        o_nsa = _nsa_call(qfeat, qn, kc, cvo, ks, vst, kw, vwt, gates)
        dil_out = _dilated_call(qfeat_dil, [dil[3 * i:3 * i + 3] for i in range(len(DIL_PATTERNS))])
        o_d, lse_d = [o for o, _ in dil_out], [lse for _, lse in dil_out]
        h = _merge_call(h, g_l, w_gate, o_nsa, o_d, lse_d, w_br_nsa[layer].astype(BF16),
                        w_br_dil[layer].astype(BF16), w_out[layer].astype(BF16),
                        final_g.reshape(1, D), min(MERGE_ROWS, S), final=layer == w_in.shape[0] - 1)
    return h
```
